```python
import math
import jax, jax.numpy as jnp
from jax import lax
import numpy as np

D_MODEL = 1024
BATCH = 8
SEQ = 2048
DEPTH = 2
DEC_BATCH = 128
DEC_SEQ = 8
PAST_LEN = 16384
PAGE_SIZE = 128

N_MIXERS = 2
N_A_LAYERS = (DEPTH + 1) // 2
N_B_LAYERS = DEPTH // 2
HG_DK = 128
HG_HEADS = D_MODEL // HG_DK
HG_DV = D_MODEL // HG_HEADS
HG_CHUNK = 64
GD_DK = 128
GD_DV = 128
GD_HEADS = D_MODEL // GD_DK
GD_QKV = GD_HEADS * (2 * GD_DK + GD_DV)
GD_CONV = 4
GD_CHUNK = 64
D_FF = 7 * D_MODEL // 2
N_EXPERTS = 8
TOP_K = 2
MOE_BLOCK = 128
EPS = 1e-6

kernel_name = 'hgrn2_gated_deltanet_moe_step'


def rms_norm(x, gain):
    xf = x.astype(jnp.float32)
    xf = xf * lax.rsqrt(jnp.mean(xf * xf, axis=-1, keepdims=True) + EPS)
    return (xf * gain.astype(jnp.float32)).astype(x.dtype)


def l2_normalize(x):
    return x * lax.rsqrt(jnp.sum(x * x, axis=-1, keepdims=True) + EPS)


def pad_time(a, n_pad):
    if n_pad == 0:
        return a
    widths = [(0, 0)] * a.ndim
    widths[1] = (0, n_pad)
    return jnp.pad(a, widths)


def to_chunks(a, c):
    b, l, h = a.shape[:3]
    a = a.reshape((b, l // c, c, h) + a.shape[3:])
    return a.transpose((1, 0, 3, 2) + tuple(range(4, a.ndim)))


def from_chunks(a):
    n, b, h, c, d = a.shape
    return a.transpose(1, 0, 3, 2, 4).reshape(b, n * c, h, d)


def hgrn2_chunked(q, k, v, log_f, S0):
    T = q.shape[1]
    C = min(HG_CHUNK, T)
    n = -(-T // C)
    pad = n * C - T
    q, k, v, log_f = (to_chunks(pad_time(a, pad), C) for a in (q, k, v, log_f))
    b = jnp.cumsum(log_f, axis=3)
    mid = C // 2
    b_mid = b[:, :, :, mid:mid + 1]
    qr = q * jnp.exp(b - b_mid)
    kr = k * jnp.exp(b_mid - b)
    causal = jnp.tril(jnp.ones((C, C), dtype=bool))
    A = jnp.where(causal, jnp.einsum('nbhtd,nbhsd->nbhts', qr, kr), 0.0)
    o_intra = jnp.einsum('nbhts,nbhsv->nbhtv', A, v)
    b_last = b[:, :, :, -1:]
    q_in = q * jnp.exp(b)
    k_out = k * jnp.exp(b_last - b)
    chunk_decay = jnp.exp(b_last[:, :, :, 0])

    def step(S, xs):
        q_c, k_c, v_c, dec_c = xs
        o = jnp.einsum('bhtd,bhdv->bhtv', q_c, S)
        S = S * dec_c[..., None] + jnp.einsum('bhtd,bhtv->bhdv', k_c, v_c)
        return S, o

    S_fin, o_inter = lax.scan(step, S0, (q_in, k_out, v, chunk_decay))
    return from_chunks(o_intra + o_inter)[:, :T], S_fin


def gated_delta_chunked(q, k, v, g, beta, S0):
    T = q.shape[1]
    C = min(GD_CHUNK, T)
    n = -(-T // C)
    pad = n * C - T
    q, k, v = (to_chunks(pad_time(a, pad), C) for a in (q, k, v))
    g, beta = (to_chunks(pad_time(a, pad), C) for a in (g, beta))
    gc = jnp.cumsum(g, axis=-1)
    incl = jnp.tril(jnp.ones((C, C), dtype=bool))
    strict = jnp.tril(jnp.ones((C, C), dtype=bool), -1)
    decay = jnp.exp(jnp.where(incl, gc[..., :, None] - gc[..., None, :], -jnp.inf))
    k_beta = k * beta[..., None]
    lower = jnp.where(strict, jnp.einsum('nbhtd,nbhsd->nbhts', k_beta, k) * decay, 0.0)
    rhs = jnp.concatenate([v * beta[..., None], k_beta * jnp.exp(gc)[..., None]], axis=-1)
    sol = lax.linalg.triangular_solve(lower + jnp.eye(C, dtype=lower.dtype), rhs,
                                      left_side=True, lower=True, unit_diagonal=True)
    u, w = sol[..., :GD_DV], sol[..., GD_DV:]
    attn = jnp.where(incl, jnp.einsum('nbhtd,nbhsd->nbhts', q, k) * decay, 0.0)
    q_in = q * jnp.exp(gc)[..., None]
    g_last = gc[..., -1]
    k_out = k * jnp.exp(g_last[..., None] - gc)[..., None]

    def step(S, xs):
        u_c, w_c, attn_c, q_c, k_c, gl_c = xs
        v_new = u_c - jnp.einsum('bhtd,bhdv->bhtv', w_c, S)
        o = jnp.einsum('bhtd,bhdv->bhtv', q_c, S) + jnp.einsum('bhts,bhsv->bhtv', attn_c, v_new)
        S = S * jnp.exp(gl_c)[..., None, None] + jnp.einsum('bhtd,bhtv->bhdv', k_c, v_new)
        return S, o

    S_fin, o = lax.scan(step, S0, (u, w, attn, q_in, k_out, g_last))
    return from_chunks(o)[:, :T], S_fin


def hgrn2_mixer(h, S0, lb, w_in, out_norm, w_out):
    B, T, _ = h.shape
    W = HG_HEADS * HG_DK
    V = HG_HEADS * HG_DV
    proj = jnp.einsum('btd,de->bte', h, w_in).astype(jnp.float32)
    q_raw, f_raw, i_raw, gate = jnp.split(proj, [W, 2 * W, 2 * W + V], axis=-1)
    lb = lb.astype(jnp.float32)
    log_f = jnp.logaddexp(jnp.log(lb), jnp.log1p(-lb) + jax.nn.log_sigmoid(f_raw))
    k = (1.0 - lb) * jax.nn.sigmoid(-f_raw)
    q = jax.nn.silu(q_raw) * HG_DK ** -0.5
    o, S = hgrn2_chunked(q.reshape(B, T, HG_HEADS, HG_DK), k.reshape(B, T, HG_HEADS, HG_DK),
                         i_raw.reshape(B, T, HG_HEADS, HG_DV), log_f.reshape(B, T, HG_HEADS, HG_DK),
                         S0.astype(jnp.float32))
    o = rms_norm(o, out_norm) * jax.nn.silu(gate.reshape(B, T, HG_HEADS, HG_DV))
    y = jnp.einsum('bte,ed->btd', o.reshape(B, T, V).astype(h.dtype), w_out)
    return y, S.astype(S0.dtype)


def gdn_mixer(h, conv_buf, S0, w_in, w_conv, a_log, dt_bias, out_norm, w_out):
    B, T, _ = h.shape
    QK = GD_HEADS * GD_DK
    V = GD_HEADS * GD_DV
    proj = jnp.einsum('btd,de->bte', h, w_in)
    qkv, z, a, b = jnp.split(proj, [GD_QKV, GD_QKV + V, GD_QKV + V + GD_HEADS], axis=-1)
    full = jnp.concatenate([conv_buf.astype(qkv.dtype), qkv], axis=1)
    conv = full[:, 0:T] * w_conv[0]
    for j in range(1, GD_CONV):
        conv = conv + full[:, j:j + T] * w_conv[j]
    new_buf = full[:, T:]
    qkv = jax.nn.silu(conv.astype(jnp.float32))
    q, k, v = jnp.split(qkv, [QK, 2 * QK], axis=-1)
    q = l2_normalize(q.reshape(B, T, GD_HEADS, GD_DK)) * GD_DK ** -0.5
    k = l2_normalize(k.reshape(B, T, GD_HEADS, GD_DK))
    v = v.reshape(B, T, GD_HEADS, GD_DV)
    beta = jax.nn.sigmoid(b.astype(jnp.float32))
    g = -jnp.exp(a_log.astype(jnp.float32)) * jax.nn.softplus(a.astype(jnp.float32) + dt_bias.astype(jnp.float32))
    o, S = gated_delta_chunked(q, k, v, g, beta, S0.astype(jnp.float32))
    o = rms_norm(o, out_norm) * jax.nn.silu(z.astype(jnp.float32).reshape(B, T, GD_HEADS, GD_DV))
    y = jnp.einsum('bte,ed->btd', o.reshape(B, T, V).astype(h.dtype), w_out)
    return y, new_buf.astype(conv_buf.dtype), S.astype(S0.dtype)


def swiglu(h, w_gate, w_up, w_down):
    a = jnp.einsum('btd,df->btf', h, w_gate)
    u = jnp.einsum('btd,df->btf', h, w_up)
    return jnp.einsum('btf,fd->btd', jax.nn.silu(a) * u, w_down)


def moe_swiglu(h, w_router, w_gate, w_up, w_down):
    B, T, D = h.shape
    N = B * T
    NK = N * TOP_K
    x2 = h.reshape(N, D)
    logits = jnp.einsum('nd,de->ne', x2, w_router).astype(jnp.float32)
    top_logit, top_idx = lax.top_k(logits, TOP_K)
    gates = jax.nn.softmax(top_logit, axis=-1)
    flat_e = top_idx.reshape(NK)
    flat_tok = jnp.repeat(jnp.arange(N, dtype=jnp.int32), TOP_K)
    flat_w = gates.reshape(NK)
    order = jnp.argsort(flat_e)
    se, stok, sw = flat_e[order], flat_tok[order], flat_w[order]
    counts = jnp.bincount(flat_e, length=N_EXPERTS)
    start = jnp.cumsum(counts) - counts
    pcounts = (counts + MOE_BLOCK - 1) // MOE_BLOCK * MOE_BLOCK
    pend = jnp.cumsum(pcounts)
    pstart = pend - pcounts
    dest = pstart[se] + (jnp.arange(NK, dtype=jnp.int32) - start[se])
    n_blocks = -(-(NK + N_EXPERTS * (MOE_BLOCK - 1)) // MOE_BLOCK)
    P = n_blocks * MOE_BLOCK
    buf_tok = jnp.zeros((P,), jnp.int32).at[dest].set(stok)
    buf_w = jnp.zeros((P,), jnp.float32).at[dest].set(sw)
    block_e = jnp.minimum(jnp.searchsorted(pend, jnp.arange(n_blocks, dtype=jnp.int32) * MOE_BLOCK, side='right'),
                          N_EXPERTS - 1)
    xb = x2[buf_tok].reshape(n_blocks, MOE_BLOCK, D)

    def expert_block(args):
        xblk, e = args
        a = xblk @ w_gate[e]
        u = xblk @ w_up[e]
        return (jax.nn.silu(a) * u) @ w_down[e]

    yb = lax.map(expert_block, (xb, block_e))
    contrib = yb.reshape(P, D).astype(jnp.float32) * buf_w[:, None]
    y = jnp.zeros((N, D), jnp.float32).at[buf_tok].add(contrib)
    return y.astype(h.dtype).reshape(B, T, D)


def trunk(x, hg_state, gd_state, gd_conv_state, lb_all, norm_mix, norm_ffn, norm_out,
          hgrn_w_in, hgrn_norm, hgrn_w_out, gdn_w_in, gdn_conv, gdn_a_log, gdn_dt_bias, gdn_norm,
          gdn_w_out, ffn_w_gate, ffn_w_up, ffn_w_down, moe_router, moe_w_gate, moe_w_up, moe_w_down):
    new_hg, new_gd, new_cv = [], [], []
    for i in range(DEPTH):
        j = i // N_MIXERS
        h = rms_norm(x, norm_mix[i])
        if i % N_MIXERS == 0:
            m, s = hgrn2_mixer(h, hg_state[j], lb_all[i], hgrn_w_in[j], hgrn_norm[j], hgrn_w_out[j])
            new_hg.append(s)
        else:
            m, c, s = gdn_mixer(h, gd_conv_state[j], gd_state[j], gdn_w_in[j], gdn_conv[j], gdn_a_log[j],
                                gdn_dt_bias[j], gdn_norm[j], gdn_w_out[j])
            new_cv.append(c)
            new_gd.append(s)
        x = x + m
        h = rms_norm(x, norm_ffn[i])
        if i % 2 == 0:
            x = x + swiglu(h, ffn_w_gate[i // 2], ffn_w_up[i // 2], ffn_w_down[i // 2])
        else:
            x = x + moe_swiglu(h, moe_router[i // 2], moe_w_gate[i // 2], moe_w_up[i // 2], moe_w_down[i // 2])
    return rms_norm(x, norm_out), jnp.stack(new_hg), jnp.stack(new_gd), jnp.stack(new_cv)


def setup_inputs(seed: int = 0) -> dict:
    key = jax.random.key(seed)
    ks = jax.random.split(key, 26)

    def nrm(k, shape, scale):
        return jax.random.normal(k, shape, jnp.float32) * scale

    def gain(k, shape):
        return 1.0 + 0.02 * jax.random.normal(k, shape, jnp.float32)

    hg_in = 2 * HG_HEADS * HG_DK + 2 * HG_HEADS * HG_DV
    gd_in = GD_QKV + GD_HEADS * GD_DV + 2 * GD_HEADS
    dt = jnp.exp(jax.random.uniform(ks[12], (N_B_LAYERS, GD_HEADS), jnp.float32,
                                    minval=math.log(1e-3), maxval=math.log(0.1)))
    return {
        'x_prompt': nrm(ks[0], (BATCH, SEQ, D_MODEL), 1.0),
        'x_sample': nrm(ks[1], (DEC_BATCH, DEC_SEQ, D_MODEL), 1.0),
        'state_hgrn': nrm(ks[2], (N_A_LAYERS, DEC_BATCH, HG_HEADS, HG_DK, HG_DV), 0.5),
        'state_gdn': nrm(ks[3], (N_B_LAYERS, DEC_BATCH, GD_HEADS, GD_DK, GD_DV), 0.1),
        'state_gdn_conv': nrm(ks[4], (N_B_LAYERS, DEC_BATCH, GD_CONV - 1, GD_QKV), 1.0),
        'norm_mix': gain(ks[5], (DEPTH, D_MODEL)),
        'norm_ffn': gain(ks[6], (DEPTH, D_MODEL)),
        'norm_out': gain(ks[7], (D_MODEL,)),
        'hgrn_w_in': nrm(ks[8], (N_A_LAYERS, D_MODEL, hg_in), D_MODEL ** -0.5),
        'hgrn_lb': nrm(ks[9], (DEPTH + 1, HG_HEADS * HG_DK), 0.1),
        'hgrn_norm': gain(ks[10], (N_A_LAYERS, HG_DV)),
        'hgrn_w_out': nrm(ks[11], (N_A_LAYERS, HG_HEADS * HG_DV, D_MODEL), (HG_HEADS * HG_DV) ** -0.5),
        'gdn_w_in': nrm(ks[13], (N_B_LAYERS, D_MODEL, gd_in), D_MODEL ** -0.5),
        'gdn_conv': nrm(ks[14], (N_B_LAYERS, GD_CONV, GD_QKV), GD_CONV ** -0.5),
        'gdn_a_log': jnp.log(jax.random.uniform(ks[15], (N_B_LAYERS, GD_HEADS), jnp.float32, minval=1.0, maxval=16.0)),
        'gdn_dt_bias': dt + jnp.log(-jnp.expm1(-dt)),
        'gdn_norm': gain(ks[16], (N_B_LAYERS, GD_DV)),
        'gdn_w_out': nrm(ks[17], (N_B_LAYERS, GD_HEADS * GD_DV, D_MODEL), (GD_HEADS * GD_DV) ** -0.5),
        'ffn_w_gate': nrm(ks[18], (N_A_LAYERS, D_MODEL, D_FF), D_MODEL ** -0.5),
        'ffn_w_up': nrm(ks[19], (N_A_LAYERS, D_MODEL, D_FF), D_MODEL ** -0.5),
        'ffn_w_down': nrm(ks[20], (N_A_LAYERS, D_FF, D_MODEL), D_FF ** -0.5),
        'moe_router': nrm(ks[21], (N_B_LAYERS, D_MODEL, N_EXPERTS), D_MODEL ** -0.5),
        'moe_w_gate': nrm(ks[22], (N_B_LAYERS, N_EXPERTS, D_MODEL, D_FF), D_MODEL ** -0.5),
        'moe_w_up': nrm(ks[23], (N_B_LAYERS, N_EXPERTS, D_MODEL, D_FF), D_MODEL ** -0.5),
        'moe_w_down': nrm(ks[24], (N_B_LAYERS, N_EXPERTS, D_FF, D_MODEL), D_FF ** -0.5),
    }


def reference(x_prompt, x_sample, state_hgrn, state_gdn, state_gdn_conv, norm_mix, norm_ffn, norm_out,
              hgrn_w_in, hgrn_lb, hgrn_norm, hgrn_w_out, gdn_w_in, gdn_conv, gdn_a_log, gdn_dt_bias,
              gdn_norm, gdn_w_out, ffn_w_gate, ffn_w_up, ffn_w_down, moe_router, moe_w_gate, moe_w_up,
              moe_w_down):
    lb_all = jnp.cumsum(jax.nn.softmax(hgrn_lb.astype(jnp.float32), axis=0), axis=0)
    dt = x_prompt.dtype
    hg0 = jnp.zeros((N_A_LAYERS, BATCH, HG_HEADS, HG_DK, HG_DV), dt)
    gd0 = jnp.zeros((N_B_LAYERS, BATCH, GD_HEADS, GD_DK, GD_DV), dt)
    cv0 = jnp.zeros((N_B_LAYERS, BATCH, GD_CONV - 1, GD_QKV), dt)
    y_prompt, hgrn_p, gdn_p, conv_p = trunk(
        x_prompt, hg0, gd0, cv0, lb_all, norm_mix, norm_ffn, norm_out, hgrn_w_in, hgrn_norm, hgrn_w_out,
        gdn_w_in, gdn_conv, gdn_a_log, gdn_dt_bias, gdn_norm, gdn_w_out, ffn_w_gate, ffn_w_up, ffn_w_down,
        moe_router, moe_w_gate, moe_w_up, moe_w_down)
    y_sample, hgrn_s, gdn_s, conv_s = trunk(
        x_sample, state_hgrn, state_gdn, state_gdn_conv, lb_all, norm_mix, norm_ffn, norm_out, hgrn_w_in,
        hgrn_norm, hgrn_w_out, gdn_w_in, gdn_conv, gdn_a_log, gdn_dt_bias, gdn_norm, gdn_w_out, ffn_w_gate,
        ffn_w_up, ffn_w_down, moe_router, moe_w_gate, moe_w_up, moe_w_down)
    return (y_prompt, y_sample, hgrn_p, hgrn_s, gdn_p, gdn_s, conv_p, conv_s)
```

```python
import functools

import jax
import jax.numpy as jnp
from jax import lax
from jax.experimental import pallas as pl
from jax.experimental.pallas import tpu as pltpu

F32 = jnp.float32
BF16 = jnp.bfloat16
HIGHEST = lax.Precision.HIGHEST

D_MODEL = 1024
N_HEADS = 8
HEAD_DIM = 128
D_FF = 3584
N_EXPERTS = 8
TOP_K = 2
GD_QKV = 3 * D_MODEL
GD_CONV = 4
MAX_CHUNK = 64
EPS = 1e-6
LANES = 128
SUBLANES = 8
VMEM_LIMIT_BYTES = 52 * 1024 * 1024


def _params(*sem):
    return pltpu.CompilerParams(dimension_semantics=sem, vmem_limit_bytes=VMEM_LIMIT_BYTES)


def _rms(x, gain):
    return x * lax.rsqrt(jnp.mean(x * x, axis=-1, keepdims=True) + EPS) * gain


def _sigmoid(x):
    t = jnp.exp(-jnp.abs(x))
    r = 1.0 / (1.0 + t)
    return jnp.where(x >= 0, r, t * r)


def _silu(x):
    return x * _sigmoid(x)


def _softplus(x):
    return jnp.maximum(x, 0.0) + jnp.log1p(jnp.exp(-jnp.abs(x)))


def _dot(a, b):
    return jnp.dot(a, b, preferred_element_type=F32)


def _dot_nt(a, b):
    return lax.dot_general(a, b, (((1,), (1,)), ((), ())), preferred_element_type=F32)


def _dot_tn(a, b):
    return lax.dot_general(a, b, (((0,), (0,)), ((), ())), preferred_element_type=F32)


def _dot_hi(a, b):
    return jnp.dot(a, b, precision=HIGHEST, preferred_element_type=F32)


def _norm_proj_kernel(x_ref, g_ref, w_ref, *rest, has_small):
    if has_small:
        ws_ref, o_ref, os_ref, h_scr = rest
    else:
        o_ref, h_scr = rest

    @pl.when(pl.program_id(1) == 0)
    def _():
        hb = _rms(x_ref[...], g_ref[...]).astype(BF16)
        h_scr[...] = hb
        if has_small:
            os_ref[...] = _dot(hb, ws_ref[...])

    o_ref[...] = _dot(h_scr[...], w_ref[...])


def _norm_proj(x, gain, w, w_small=None, *, tm, tn):
    n, d = x.shape
    e = w.shape[1]
    has_small = w_small is not None
    in_specs = [pl.BlockSpec((tm, d), lambda i, j: (i, 0)),
                pl.BlockSpec((1, d), lambda i, j: (0, 0)),
                pl.BlockSpec((d, tn), lambda i, j: (0, j))]
    out_shape = [jax.ShapeDtypeStruct((n, e), F32)]
    out_specs = [pl.BlockSpec((tm, tn), lambda i, j: (i, j))]
    args = [x, gain.reshape(1, d), w]
    if has_small:
        es = w_small.shape[1]
        in_specs.append(pl.BlockSpec((d, es), lambda i, j: (0, 0)))
        out_shape.append(jax.ShapeDtypeStruct((n, es), F32))
        out_specs.append(pl.BlockSpec((tm, es), lambda i, j: (i, 0)))
        args.append(w_small)
    return pl.pallas_call(
        functools.partial(_norm_proj_kernel, has_small=has_small),
        grid=(n // tm, e // tn),
        in_specs=in_specs, out_specs=out_specs, out_shape=out_shape,
        scratch_shapes=[pltpu.VMEM((tm, d), BF16)],
        compiler_params=_params("parallel", "arbitrary"),
        name="norm_proj",
    )(*args)


def _hgrn_kernel(*refs, chunk, n_chunks, layer, has_s0):
    if has_s0:
        proj_ref, lb_ref, on_ref, s0_ref, og_ref, sf_ref, st_scr = refs
    else:
        proj_ref, lb_ref, on_ref, og_ref, sf_ref, st_scr = refs
    C = chunk
    W = N_HEADS * HEAD_DIM
    t = pl.program_id(1)

    @pl.when(t == 0)
    def _():
        for h in range(N_HEADS):
            if has_s0:
                st_scr[h] = s0_ref[0, h].T
            else:
                st_scr[h] = jnp.zeros((HEAD_DIM, HEAD_DIM), F32)

    lbraw = lb_ref[...]
    ex = jnp.exp(lbraw - jnp.max(lbraw, axis=0, keepdims=True))
    sm = ex / jnp.sum(ex, axis=0, keepdims=True)
    lb = sm[0:1]
    for j in range(1, layer + 1):
        lb = lb + sm[j:j + 1]

    row = lax.broadcasted_iota(jnp.int32, (C, C), 0)
    col = lax.broadcasted_iota(jnp.int32, (C, C), 1)
    causal = row >= col
    tril = causal.astype(F32)
    mid = C // 2
    scale = HEAD_DIM ** -0.5
    onorm = on_ref[...]

    def chunk_body(c, carry):
        r0 = pl.multiple_of(c * C, C)
        rows = pl.ds(r0, C)
        for h in range(N_HEADS):
            lo = h * HEAD_DIM
            q_raw = proj_ref[rows, lo:lo + HEAD_DIM]
            f_raw = proj_ref[rows, W + lo:W + lo + HEAD_DIM]
            v = proj_ref[rows, 2 * W + lo:2 * W + lo + HEAD_DIM]
            gate = proj_ref[rows, 3 * W + lo:3 * W + lo + HEAD_DIM]
            lbh = lb[:, lo:lo + HEAD_DIM]
            tt = jnp.exp(-jnp.abs(f_raw))
            rr = 1.0 / (1.0 + tt)
            pos = f_raw >= 0
            sig = jnp.where(pos, rr, tt * rr)
            sig_neg = jnp.where(pos, tt * rr, rr)
            log_f = jnp.log(lbh + (1.0 - lbh) * sig)
            k = (1.0 - lbh) * sig_neg
            q = _silu(q_raw) * scale
            b = _dot_hi(tril, log_f)
            b_mid = b[mid:mid + 1]
            b_last = b[C - 1:C]
            qr = q * jnp.exp(b - b_mid)
            kr = k * jnp.exp(b_mid - b)
            a = jnp.where(causal, _dot_nt(qr.astype(BF16), kr.astype(BF16)), 0.0)
            vb = v.astype(BF16)
            o = _dot(a.astype(BF16), vb)
            q_in = q * jnp.exp(b)
            k_out = k * jnp.exp(b_last - b)
            st = st_scr[h]
            o = o + _dot_nt(q_in.astype(BF16), st.astype(BF16))
            st_scr[h] = st * jnp.exp(b_last) + _dot_tn(vb, k_out.astype(BF16))
            og_ref[rows, lo:lo + HEAD_DIM] = _rms(o, onorm) * _silu(gate)
        return carry

    lax.fori_loop(0, n_chunks, chunk_body, 0)

    @pl.when(t == pl.num_programs(1) - 1)
    def _():
        for h in range(N_HEADS):
            sf_ref[0, h] = st_scr[h].T


def _hgrn_recurrence(proj, lb_raw, out_norm, s0, *, batch, seq, layer, rows_per_step):
    n = proj.shape[0]
    chunk = min(MAX_CHUNK, seq)
    assert seq % chunk == 0 and rows_per_step % chunk == 0 and seq % rows_per_step == 0
    nt = seq // rows_per_step
    has_s0 = s0 is not None
    state_spec = pl.BlockSpec((1, N_HEADS, HEAD_DIM, HEAD_DIM), lambda b, t: (b, 0, 0, 0))
    in_specs = [pl.BlockSpec((rows_per_step, proj.shape[1]), lambda b, t: (b * nt + t, 0)),
                pl.BlockSpec(lb_raw.shape, lambda b, t: (0, 0)),
                pl.BlockSpec((1, HEAD_DIM), lambda b, t: (0, 0))]
    args = [proj, lb_raw, out_norm.reshape(1, HEAD_DIM)]
    if has_s0:
        in_specs.append(state_spec)
        args.append(s0)
    return pl.pallas_call(
        functools.partial(_hgrn_kernel, chunk=chunk, n_chunks=rows_per_step // chunk, layer=layer, has_s0=has_s0),
        grid=(batch, nt),
        in_specs=in_specs,
        out_specs=[pl.BlockSpec((rows_per_step, D_MODEL), lambda b, t: (b * nt + t, 0)), state_spec],
        out_shape=[jax.ShapeDtypeStruct((n, D_MODEL), F32),
                   jax.ShapeDtypeStruct((batch, N_HEADS, HEAD_DIM, HEAD_DIM), F32)],
        scratch_shapes=[pltpu.VMEM((N_HEADS, HEAD_DIM, HEAD_DIM), F32)],
        compiler_params=_params("parallel", "arbitrary"),
        name="hgrn_recurrence",
    )(*args)


def _unit_lower_inverse(lower, c):
    row = lax.broadcasted_iota(jnp.int32, (c, c), 0)
    col = lax.broadcasted_iota(jnp.int32, (c, c), 1)
    m = -lower
    inv = jnp.where(row == col, 1.0, 0.0) + m
    p = m
    power = 2
    while power < c:
        p = _dot_hi(p, p)
        inv = inv + _dot_hi(inv, p)
        power *= 2
    return inv


def _gdn_kernel(*refs, chunk, n_chunks, has_s0):
    (main_ref, ab_ref, abt_ref, cw_ref, al_ref, dtb_ref, alt_ref, dtbt_ref, on_ref), rest = refs[:9], refs[9:]
    if has_s0:
        (cb0_ref, s0_ref), rest = rest[:2], rest[2:]
    og_ref, sf_ref, cb_ref, s_scr, xpad, act, gcc_scr, gcr_scr, beta_scr = rest
    C = chunk
    W = N_HEADS * HEAD_DIM
    tb = C * n_chunks
    t = pl.program_id(1)
    pad = SUBLANES

    @pl.when(t == 0)
    def _():
        for h in range(N_HEADS):
            s_scr[h] = s0_ref[0, h] if has_s0 else jnp.zeros((HEAD_DIM, HEAD_DIM), F32)
        xpad[0:pad, :] = jnp.zeros((pad, GD_QKV), F32)
        if has_s0:
            xpad[pad - (GD_CONV - 1):pad, :] = cb0_ref[0]

    @pl.when(t > 0)
    def _():
        xpad[0:pad, :] = xpad[tb:tb + pad, :]

    for s in range(GD_QKV // LANES):
        cs = slice(s * LANES, (s + 1) * LANES)
        xpad[pad:pad + tb, cs] = main_ref[:, cs]
    for s in range(GD_QKV // LANES):
        cs = slice(s * LANES, (s + 1) * LANES)
        conv = xpad[pad - 3:pad - 3 + tb, cs] * cw_ref[0:1, cs]
        for j in range(1, GD_CONV):
            conv = conv + xpad[pad - 3 + j:pad - 3 + j + tb, cs] * cw_ref[j:j + 1, cs]
        act[:, cs] = _silu(conv)

    @pl.when(t == pl.num_programs(1) - 1)
    def _():
        cb_ref[0] = xpad[pad + tb - (GD_CONV - 1):pad + tb, :]

    ab = ab_ref[...]
    g_col = -jnp.exp(al_ref[...]) * _softplus(ab[:, 0:N_HEADS] + dtb_ref[...])
    beta_scr[...] = _sigmoid(ab[:, N_HEADS:2 * N_HEADS])
    abt = abt_ref[0]
    g_row = -jnp.exp(alt_ref[...]) * _softplus(abt[0:N_HEADS, :] + dtbt_ref[...])

    row = lax.broadcasted_iota(jnp.int32, (C, C), 0)
    col = lax.broadcasted_iota(jnp.int32, (C, C), 1)
    incl = row >= col
    strict = row > col
    tril = incl.astype(F32)
    triu = (row <= col).astype(F32)
    for c in range(n_chunks):
        gcc_scr[c * C:(c + 1) * C, :] = _dot_hi(tril, g_col[c * C:(c + 1) * C, :])
        gcr_scr[c] = _dot_hi(g_row[:, c * C:(c + 1) * C], triu)

    scale = HEAD_DIM ** -0.5
    onorm = on_ref[...]

    def chunk_body(c, carry):
        r0 = pl.multiple_of(c * C, C)
        rows = pl.ds(r0, C)
        gcc = gcc_scr[rows, :]
        gcr = gcr_scr[c]
        beta = beta_scr[rows, :]
        for h in range(N_HEADS):
            lo = h * HEAD_DIM
            q = act[rows, lo:lo + HEAD_DIM]
            k = act[rows, W + lo:W + lo + HEAD_DIM]
            v = act[rows, 2 * W + lo:2 * W + lo + HEAD_DIM]
            z = main_ref[rows, 3 * W + lo:3 * W + lo + HEAD_DIM]
            q = q * lax.rsqrt(jnp.sum(q * q, axis=-1, keepdims=True) + EPS) * scale
            k = k * lax.rsqrt(jnp.sum(k * k, axis=-1, keepdims=True) + EPS)
            gc_c = gcc[:, h:h + 1]
            gc_r = gcr[h:h + 1, :]
            beta_c = beta[:, h:h + 1]
            decay = jnp.exp(jnp.where(incl, gc_c - gc_r, -jnp.inf))
            k_beta = k * beta_c
            kb16 = k.astype(BF16)
            lower = jnp.where(strict, _dot_nt(k_beta.astype(BF16), kb16) * decay, 0.0)
            rhs = jnp.concatenate([v * beta_c, k_beta * jnp.exp(gc_c)], axis=1)
            sol = _dot_hi(_unit_lower_inverse(lower, C), rhs)
            u = sol[:, :HEAD_DIM]
            w = sol[:, HEAD_DIM:]
            attn = jnp.where(incl, _dot_nt(q.astype(BF16), kb16) * decay, 0.0)
            g_last = gc_c[C - 1:C, :]
            q_in = q * jnp.exp(gc_c)
            k_out = k * jnp.exp(g_last - gc_c)
            s = s_scr[h]
            sb = s.astype(BF16)
            v_new = u - _dot(w.astype(BF16), sb)
            vn16 = v_new.astype(BF16)
            o = _dot(q_in.astype(BF16), sb) + _dot(attn.astype(BF16), vn16)
            s_scr[h] = s * jnp.exp(g_last) + _dot_tn(k_out.astype(BF16), vn16)
            og_ref[rows, lo:lo + HEAD_DIM] = _rms(o, onorm) * _silu(z)
        return carry

    lax.fori_loop(0, n_chunks, chunk_body, 0)

    @pl.when(t == pl.num_programs(1) - 1)
    def _():
        for h in range(N_HEADS):
            sf_ref[0, h] = s_scr[h]


def _gdn_recurrence(main, ab, conv_w, a_log, dt_bias, out_norm, cb0, s0, *, batch, seq, rows_per_step):
    n = main.shape[0]
    chunk = min(MAX_CHUNK, seq)
    assert seq % chunk == 0 and rows_per_step % chunk == 0 and seq % rows_per_step == 0
    assert seq >= GD_CONV - 1
    nt = seq // rows_per_step
    n_chunks = rows_per_step // chunk
    has_s0 = s0 is not None
    abt = ab.reshape(batch, seq, 2 * N_HEADS).transpose(0, 2, 1)
    state_spec = pl.BlockSpec((1, N_HEADS, HEAD_DIM, HEAD_DIM), lambda b, t: (b, 0, 0, 0))
    cb_spec = pl.BlockSpec((1, GD_CONV - 1, GD_QKV), lambda b, t: (b, 0, 0))
    small = lambda shape: pl.BlockSpec(shape, lambda b, t: (0,) * len(shape))
    in_specs = [pl.BlockSpec((rows_per_step, main.shape[1]), lambda b, t: (b * nt + t, 0)),
                pl.BlockSpec((rows_per_step, 2 * N_HEADS), lambda b, t: (b * nt + t, 0)),
                pl.BlockSpec((1, 2 * N_HEADS, rows_per_step), lambda b, t: (b, 0, t)),
                small((GD_CONV, GD_QKV)), small((1, N_HEADS)), small((1, N_HEADS)),
                small((N_HEADS, 1)), small((N_HEADS, 1)), small((1, HEAD_DIM))]
    args = [main, ab, abt, conv_w, a_log.reshape(1, N_HEADS), dt_bias.reshape(1, N_HEADS),
            a_log.reshape(N_HEADS, 1), dt_bias.reshape(N_HEADS, 1), out_norm.reshape(1, HEAD_DIM)]
    if has_s0:
        in_specs += [cb_spec, state_spec]
        args += [cb0, s0]
    return pl.pallas_call(
        functools.partial(_gdn_kernel, chunk=chunk, n_chunks=n_chunks, has_s0=has_s0),
        grid=(batch, nt),
        in_specs=in_specs,
        out_specs=[pl.BlockSpec((rows_per_step, D_MODEL), lambda b, t: (b * nt + t, 0)), state_spec, cb_spec],
        out_shape=[jax.ShapeDtypeStruct((n, D_MODEL), F32),
                   jax.ShapeDtypeStruct((batch, N_HEADS, HEAD_DIM, HEAD_DIM), F32),
                   jax.ShapeDtypeStruct((batch, GD_CONV - 1, GD_QKV), F32)],
        scratch_shapes=[pltpu.VMEM((N_HEADS, HEAD_DIM, HEAD_DIM), F32),
                        pltpu.VMEM((rows_per_step + SUBLANES, GD_QKV), F32),
                        pltpu.VMEM((rows_per_step, GD_QKV), F32),
                        pltpu.VMEM((rows_per_step, N_HEADS), F32),
                        pltpu.VMEM((n_chunks, N_HEADS, chunk), F32),
                        pltpu.VMEM((rows_per_step, N_HEADS), F32)],
        compiler_params=_params("parallel", "arbitrary"),
        name="gdn_recurrence",
    )(*args)


def _out_proj_kernel(og_ref, w_ref, x_ref, o_ref):
    o_ref[...] = x_ref[...] + _dot(og_ref[...].astype(BF16), w_ref[...])


def _out_proj(og, w, x, *, tm):
    n, d = x.shape
    return pl.pallas_call(
        _out_proj_kernel,
        grid=(n // tm,),
        in_specs=[pl.BlockSpec((tm, d), lambda i: (i, 0)),
                  pl.BlockSpec((d, d), lambda i: (0, 0)),
                  pl.BlockSpec((tm, d), lambda i: (i, 0))],
        out_specs=pl.BlockSpec((tm, d), lambda i: (i, 0)),
        out_shape=jax.ShapeDtypeStruct((n, d), F32),
        compiler_params=_params("parallel"),
        name="out_proj",
    )(og, w, x)


def _ffn_kernel(x_ref, g_ref, wg_ref, wu_ref, wd_ref, o_ref, h_scr):
    j = pl.program_id(1)

    @pl.when(j == 0)
    def _():
        x = x_ref[...]
        h_scr[...] = _rms(x, g_ref[...]).astype(BF16)
        o_ref[...] = x

    h = h_scr[...]
    a = _dot(h, wg_ref[...])
    u = _dot(h, wu_ref[...])
    o_ref[...] += _dot((_silu(a) * u).astype(BF16), wd_ref[...])


def _ffn(x, gain, wg, wu, wd, *, tm, tf):
    n, d = x.shape
    f = wg.shape[1]
    return pl.pallas_call(
        _ffn_kernel,
        grid=(n // tm, f // tf),
        in_specs=[pl.BlockSpec((tm, d), lambda i, j: (i, 0)),
                  pl.BlockSpec((1, d), lambda i, j: (0, 0)),
                  pl.BlockSpec((d, tf), lambda i, j: (0, j)),
                  pl.BlockSpec((d, tf), lambda i, j: (0, j)),
                  pl.BlockSpec((tf, d), lambda i, j: (j, 0))],
        out_specs=pl.BlockSpec((tm, d), lambda i, j: (i, 0)),
        out_shape=jax.ShapeDtypeStruct((n, d), F32),
        scratch_shapes=[pltpu.VMEM((tm, d), BF16)],
        compiler_params=_params("parallel", "arbitrary"),
        name="dense_ffn",
    )(x, gain.reshape(1, d), wg, wu, wd)


def _router_kernel(x_ref, g_ref, wr_ref, h_ref, idx_ref, gate_ref, rank_ref, cnt_ref, cnt_scr, *, tm):
    i = pl.program_id(0)

    @pl.when(i == 0)
    def _():
        cnt_scr[...] = jnp.zeros_like(cnt_scr)

    h = _rms(x_ref[...], g_ref[...])
    h_ref[...] = h
    logits = _dot_hi(h, wr_ref[...])
    lane = lax.broadcasted_iota(jnp.int32, logits.shape, 1)
    m1 = jnp.max(logits, axis=-1, keepdims=True)
    i1 = jnp.min(jnp.where(logits == m1, lane, N_EXPERTS), axis=-1, keepdims=True)
    rest = jnp.where(lane == i1, -jnp.inf, logits)
    m2 = jnp.max(rest, axis=-1, keepdims=True)
    i2 = jnp.min(jnp.where(rest == m2, lane, N_EXPERTS), axis=-1, keepdims=True)
    e2 = jnp.exp(m2 - m1)
    den = 1.0 + e2
    two = lax.broadcasted_iota(jnp.int32, (tm, TOP_K), 1)
    idx_ref[...] = jnp.where(two == 0, i1, i2)
    gate_ref[...] = jnp.where(two == 0, 1.0 / den, e2 / den)

    onehot = ((lane == i1) | (lane == i2)).astype(F32)
    row = lax.broadcasted_iota(jnp.int32, (tm, tm), 0)
    col = lax.broadcasted_iota(jnp.int32, (tm, tm), 1)
    before = (row > col).astype(BF16)
    excl = _dot(before, onehot.astype(BF16)) + cnt_scr[...]
    r1 = jnp.sum(jnp.where(lane == i1, excl, 0.0), axis=-1, keepdims=True)
    r2 = jnp.sum(jnp.where(lane == i2, excl, 0.0), axis=-1, keepdims=True)
    rank_ref[...] = jnp.where(two == 0, r1, r2).astype(jnp.int32)
    cnt = cnt_scr[...] + jnp.sum(onehot, axis=0, keepdims=True)
    cnt_scr[...] = cnt
    cnt_ref[...] = cnt.astype(jnp.int32)


def _router(x, gain, w_router, *, tm):
    n, d = x.shape
    two = pl.BlockSpec((tm, TOP_K), lambda i: (i, 0))
    return pl.pallas_call(
        functools.partial(_router_kernel, tm=tm),
        grid=(n // tm,),
        in_specs=[pl.BlockSpec((tm, d), lambda i: (i, 0)),
                  pl.BlockSpec((1, d), lambda i: (0, 0)),
                  pl.BlockSpec((d, N_EXPERTS), lambda i: (0, 0))],
        out_specs=[pl.BlockSpec((tm, d), lambda i: (i, 0)), two, two, two,
                   pl.BlockSpec((1, N_EXPERTS), lambda i: (0, 0))],
        out_shape=[jax.ShapeDtypeStruct((n, d), F32),
                   jax.ShapeDtypeStruct((n, TOP_K), jnp.int32),
                   jax.ShapeDtypeStruct((n, TOP_K), F32),
                   jax.ShapeDtypeStruct((n, TOP_K), jnp.int32),
                   jax.ShapeDtypeStruct((1, N_EXPERTS), jnp.int32)],
        scratch_shapes=[pltpu.VMEM((1, N_EXPERTS), F32)],
        compiler_params=_params("arbitrary"),
        name="moe_router",
    )(x, gain.reshape(1, d), w_router)


def _row_gather_copy(src_hbm, dst, sem, src_row, dst_row):
    return pltpu.make_async_copy(src_hbm.at[pl.ds(src_row, 1)], dst.at[pl.ds(dst_row, 1)], sem)


def _expert_kernel(be_ref, tok_ref, nu_ref, h_hbm, wg_ref, wu_ref, wd_ref, yb_ref, xg, xb, sem, *, tmm):
    i = pl.program_id(0)
    j = pl.program_id(1)
    active = i < nu_ref[0]

    @pl.when(jnp.logical_and(active, j == 0))
    def _():
        base = i * tmm

        def issue(r, carry):
            _row_gather_copy(h_hbm, xg, sem, tok_ref[base + r], r).start()
            return carry

        lax.fori_loop(0, tmm, issue, 0, unroll=8)
        pltpu.make_async_copy(h_hbm.at[pl.ds(0, tmm)], xg, sem).wait()
        xb[...] = xg[...].astype(BF16)

    @pl.when(jnp.logical_and(jnp.logical_not(active), j == 0))
    def _():
        yb_ref[...] = jnp.zeros_like(yb_ref)

    @pl.when(active)
    def _():
        x = xb[...]
        a = _dot(x, wg_ref[0])
        u = _dot(x, wu_ref[0])
        y = _dot((_silu(a) * u).astype(BF16), wd_ref[0])

        @pl.when(j == 0)
        def _():
            yb_ref[...] = y

        @pl.when(j > 0)
        def _():
            yb_ref[...] += y


def _experts(h, block_e, buf_tok, n_used, wg, wu, wd, *, tmm, tf):
    n, d = h.shape
    nb = block_e.shape[0]
    nf = D_FF // tf

    def wcol(i, j, be, tok, nu):
        return (be[i], 0, jnp.where(i < nu[0], j, nf - 1))

    def wrow(i, j, be, tok, nu):
        return (be[i], jnp.where(i < nu[0], j, nf - 1), 0)

    grid_spec = pltpu.PrefetchScalarGridSpec(
        num_scalar_prefetch=3,
        grid=(nb, nf),
        in_specs=[pl.BlockSpec(memory_space=pl.ANY),
                  pl.BlockSpec((1, d, tf), wcol),
                  pl.BlockSpec((1, d, tf), wcol),
                  pl.BlockSpec((1, tf, d), wrow)],
        out_specs=pl.BlockSpec((tmm, d), lambda i, j, be, tok, nu: (i, 0)),
        scratch_shapes=[pltpu.VMEM((tmm, d), F32), pltpu.VMEM((tmm, d), BF16), pltpu.SemaphoreType.DMA(())],
    )
    return pl.pallas_call(
        functools.partial(_expert_kernel, tmm=tmm),
        grid_spec=grid_spec,
        out_shape=jax.ShapeDtypeStruct((nb * tmm, d), F32),
        compiler_params=_params("arbitrary", "arbitrary"),
        name="moe_experts",
    )(block_e, buf_tok, n_used, h, wg, wu, wd)


def _combine_kernel(dest_ref, yb_hbm, x_ref, gate_ref, gain_ref, o_ref, r0, r1, sem0, sem1, *, tm):
    base = pl.program_id(0) * tm

    def issue(r, carry):
        _row_gather_copy(yb_hbm, r0, sem0, dest_ref[TOP_K * (base + r)], r).start()
        _row_gather_copy(yb_hbm, r1, sem1, dest_ref[TOP_K * (base + r) + 1], r).start()
        return carry

    lax.fori_loop(0, tm, issue, 0, unroll=8)
    pltpu.make_async_copy(yb_hbm.at[pl.ds(0, tm)], r0, sem0).wait()
    pltpu.make_async_copy(yb_hbm.at[pl.ds(0, tm)], r1, sem1).wait()
    g = gate_ref[...]
    y = r0[...] * g[:, 0:1] + r1[...] * g[:, 1:2]
    o_ref[...] = _rms(x_ref[...] + y, gain_ref[...])


def _combine(yb, dest_flat, x, gates, gain, *, tm):
    n, d = x.shape
    grid_spec = pltpu.PrefetchScalarGridSpec(
        num_scalar_prefetch=1,
        grid=(n // tm,),
        in_specs=[pl.BlockSpec(memory_space=pl.ANY),
                  pl.BlockSpec((tm, d), lambda i, dest: (i, 0)),
                  pl.BlockSpec((tm, TOP_K), lambda i, dest: (i, 0)),
                  pl.BlockSpec((1, d), lambda i, dest: (0, 0))],
        out_specs=pl.BlockSpec((tm, d), lambda i, dest: (i, 0)),
        scratch_shapes=[pltpu.VMEM((tm, d), F32), pltpu.VMEM((tm, d), F32),
                        pltpu.SemaphoreType.DMA(()), pltpu.SemaphoreType.DMA(())],
    )
    return pl.pallas_call(
        functools.partial(_combine_kernel, tm=tm),
        grid_spec=grid_spec,
        out_shape=jax.ShapeDtypeStruct((n, d), F32),
        compiler_params=_params("arbitrary"),
        name="moe_combine",
    )(dest_flat, yb, x, gates, gain.reshape(1, d))


def _moe_and_final_norm(x, gain, w_router, wg, wu, wd, final_gain, *, tiles):
    n = x.shape[0]
    tmm = tiles["moe_rows"]
    h, idx, gates, rank, counts = _router(x, gain, w_router, tm=tiles["router_rows"])
    counts = counts[0]
    pcounts = (counts + tmm - 1) // tmm * tmm
    pend = jnp.cumsum(pcounts)
    pstart = pend - pcounts
    dest = pstart[idx] + rank
    nb = -(-(n * TOP_K + N_EXPERTS * (tmm - 1)) // tmm)
    tok = jnp.broadcast_to(jnp.arange(n, dtype=jnp.int32)[:, None], (n, TOP_K))
    buf_tok = jnp.zeros((nb * tmm,), jnp.int32).at[dest.reshape(-1)].set(tok.reshape(-1))
    n_used = (pend[-1] // tmm).astype(jnp.int32)
    blk = jnp.arange(nb, dtype=jnp.int32)
    block_e = jnp.minimum(jnp.searchsorted(pend, jnp.minimum(blk, n_used - 1) * tmm, side="right"),
                          N_EXPERTS - 1).astype(jnp.int32)
    yb = _experts(h, block_e, buf_tok, n_used.reshape(1), wg, wu, wd, tmm=tmm, tf=tiles["ff_cols"])
    return _combine(yb, dest.reshape(-1).astype(jnp.int32), x, gates, final_gain, tm=tiles["combine_rows"])


def _tiles(n_rows, seq):
    big = n_rows >= 8192
    return {
        "proj_rows": 1024 if n_rows % 1024 == 0 else n_rows,
        "proj_cols": 1024,
        "rec_rows": min(seq, 256),
        "ffn_rows": 1024 if n_rows % 1024 == 0 else n_rows,
        "ff_cols": 512,
        "router_rows": 512,
        "moe_rows": 512 if big else 256,
        "combine_rows": 256,
    }


def _trunk(x3, hg_state, gd_state, gd_conv, p):
    batch, seq, d = x3.shape
    n = batch * seq
    x = x3.reshape(n, d)
    tiles = _tiles(n, seq)
    tm, tn = tiles["proj_rows"], tiles["proj_cols"]

    proj = _norm_proj(x, p["norm_mix"][0], p["hgrn_w_in"], tm=tm, tn=tn)[0]
    og, hg_new = _hgrn_recurrence(proj, p["hgrn_lb"], p["hgrn_norm"], hg_state, batch=batch, seq=seq, layer=0,
                                  rows_per_step=tiles["rec_rows"])
    x = _out_proj(og, p["hgrn_w_out"], x, tm=tm)
    x = _ffn(x, p["norm_ffn"][0], p["ffn_w_gate"], p["ffn_w_up"], p["ffn_w_down"],
             tm=tiles["ffn_rows"], tf=tiles["ff_cols"])

    main, ab = _norm_proj(x, p["norm_mix"][1], p["gdn_w_main"], p["gdn_w_ab"], tm=tm, tn=tn)
    og, gd_new, cv_new = _gdn_recurrence(main, ab, p["gdn_conv"], p["gdn_a_log"], p["gdn_dt_bias"], p["gdn_norm"],
                                         gd_conv, gd_state, batch=batch, seq=seq, rows_per_step=tiles["rec_rows"])
    x = _out_proj(og, p["gdn_w_out"], x, tm=tm)
    y = _moe_and_final_norm(x, p["norm_ffn"][1], p["moe_router"], p["moe_w_gate"], p["moe_w_up"], p["moe_w_down"],
                            p["norm_out"], tiles=tiles)
    return y.reshape(batch, seq, d), hg_new[None], gd_new[None], cv_new[None]


def kernel(x_prompt, x_sample, state_hgrn, state_gdn, state_gdn_conv, norm_mix, norm_ffn, norm_out, hgrn_w_in, hgrn_lb, hgrn_norm, hgrn_w_out, gdn_w_in, gdn_conv, gdn_a_log, gdn_dt_bias, gdn_norm, gdn_w_out, ffn_w_gate, ffn_w_up, ffn_w_down, moe_router, moe_w_gate, moe_w_up, moe_w_down):
    assert hgrn_w_in.shape[0] == 1 and gdn_w_in.shape[0] == 1, "one HGRN2 layer and one gated-DeltaNet layer"
    main_cols = GD_QKV + N_HEADS * HEAD_DIM
    p = {
        "norm_mix": norm_mix, "norm_ffn": norm_ffn, "norm_out": norm_out,
        "hgrn_w_in": hgrn_w_in[0].astype(BF16), "hgrn_lb": hgrn_lb, "hgrn_norm": hgrn_norm[0],
        "hgrn_w_out": hgrn_w_out[0].astype(BF16),
        "gdn_w_main": gdn_w_in[0, :, :main_cols].astype(BF16), "gdn_w_ab": gdn_w_in[0, :, main_cols:].astype(BF16),
        "gdn_conv": gdn_conv[0], "gdn_a_log": gdn_a_log[0], "gdn_dt_bias": gdn_dt_bias[0], "gdn_norm": gdn_norm[0],
        "gdn_w_out": gdn_w_out[0].astype(BF16),
        "ffn_w_gate": ffn_w_gate[0].astype(BF16), "ffn_w_up": ffn_w_up[0].astype(BF16),
        "ffn_w_down": ffn_w_down[0].astype(BF16),
        "moe_router": moe_router[0],
        "moe_w_gate": moe_w_gate[0].astype(BF16), "moe_w_up": moe_w_up[0].astype(BF16),
        "moe_w_down": moe_w_down[0].astype(BF16),
    }
    y_p, hg_p, gd_p, cv_p = _trunk(x_prompt, None, None, None, p)
    y_s, hg_s, gd_s, cv_s = _trunk(x_sample, state_hgrn[0], state_gdn[0], state_gdn_conv[0], p)
    return (y_p, y_s, hg_p, hg_s, gd_p, gd_s, cv_p, cv_s)
```

```python
import functools

import jax
import jax.numpy as jnp
from jax import lax
from jax.experimental import pallas as pl
from jax.experimental.pallas import tpu as pltpu

F32 = jnp.float32
BF16 = jnp.bfloat16
HIGHEST = lax.Precision.HIGHEST

D_MODEL = 1024
N_HEADS = 8
HEAD_DIM = 128
D_FF = 3584
N_EXPERTS = 8
TOP_K = 2
GD_QKV = 3 * D_MODEL
GD_CONV = 4
MAX_CHUNK = 64
EPS = 1e-6
LANES = 128
SUBLANES = 8
VMEM_LIMIT_BYTES = 52 * 1024 * 1024


def _params(*sem):
    return pltpu.CompilerParams(dimension_semantics=sem, vmem_limit_bytes=VMEM_LIMIT_BYTES)


def _rms(x, gain):
    return x * lax.rsqrt(jnp.mean(x * x, axis=-1, keepdims=True) + EPS) * gain


def _sigmoid(x):
    t = jnp.exp(-jnp.abs(x))
    r = 1.0 / (1.0 + t)
    return jnp.where(x >= 0, r, t * r)


def _silu(x):
    return x * _sigmoid(x)


def _softplus(x):
    return jnp.maximum(x, 0.0) + jnp.log1p(jnp.exp(-jnp.abs(x)))


def _dot(a, b):
    return jnp.dot(a, b, preferred_element_type=F32)


def _dot_nt(a, b):
    return lax.dot_general(a, b, (((1,), (1,)), ((), ())), preferred_element_type=F32)


def _dot_tn(a, b):
    return lax.dot_general(a, b, (((0,), (0,)), ((), ())), preferred_element_type=F32)


def _dot_hi(a, b):
    return jnp.dot(a, b, precision=HIGHEST, preferred_element_type=F32)


def _norm_proj_kernel(x_ref, g_ref, w_ref, *rest, has_small):
    if has_small:
        ws_ref, o_ref, os_ref, h_scr = rest
    else:
        o_ref, h_scr = rest

    @pl.when(pl.program_id(1) == 0)
    def _():
        hb = _rms(x_ref[...], g_ref[...]).astype(BF16)
        h_scr[...] = hb
        if has_small:
            os_ref[...] = _dot(hb, ws_ref[...])

    o_ref[...] = _dot(h_scr[...], w_ref[...])


def _norm_proj(x, gain, w, w_small=None, *, tm, tn):
    n, d = x.shape
    e = w.shape[1]
    has_small = w_small is not None
    in_specs = [pl.BlockSpec((tm, d), lambda i, j: (i, 0)),
                pl.BlockSpec((1, d), lambda i, j: (0, 0)),
                pl.BlockSpec((d, tn), lambda i, j: (0, j))]
    out_shape = [jax.ShapeDtypeStruct((n, e), F32)]
    out_specs = [pl.BlockSpec((tm, tn), lambda i, j: (i, j))]
    args = [x, gain.reshape(1, d), w]
    if has_small:
        es = w_small.shape[1]
        in_specs.append(pl.BlockSpec((d, es), lambda i, j: (0, 0)))
        out_shape.append(jax.ShapeDtypeStruct((n, es), F32))
        out_specs.append(pl.BlockSpec((tm, es), lambda i, j: (i, 0)))
        args.append(w_small)
    return pl.pallas_call(
        functools.partial(_norm_proj_kernel, has_small=has_small),
        grid=(n // tm, e // tn),
        in_specs=in_specs, out_specs=out_specs, out_shape=out_shape,
        scratch_shapes=[pltpu.VMEM((tm, d), BF16)],
        compiler_params=_params("parallel", "arbitrary"),
        name="norm_proj",
    )(*args)


def _hgrn_kernel(*refs, chunk, n_chunks, n_seq, layer, has_s0):
    if has_s0:
        proj_ref, lb_ref, on_ref, s0_ref, og_ref, sf_ref, *scr = refs
    else:
        proj_ref, lb_ref, on_ref, og_ref, sf_ref, *scr = refs
    st_scr, b_scr, q_scr, k_scr, a_scr, qin_scr, kout_scr, dec_scr, oi_scr = scr
    C = chunk
    W = N_HEADS * HEAD_DIM
    tb = C * n_chunks
    t = pl.program_id(1)
    problems = [(s, c, h) for s in range(n_seq) for c in range(n_chunks) for h in range(N_HEADS)]

    def pid(s, c, h):
        return (s * n_chunks + c) * N_HEADS + h

    def row0(s, c):
        return s * tb + c * C

    @pl.when(t == 0)
    def _():
        for s in range(n_seq):
            for h in range(N_HEADS):
                st_scr[s * N_HEADS + h] = s0_ref[s, h].T if has_s0 else jnp.zeros((HEAD_DIM, HEAD_DIM), F32)

    lbraw = lb_ref[...]
    ex = jnp.exp(lbraw - jnp.max(lbraw, axis=0, keepdims=True))
    sm = ex / jnp.sum(ex, axis=0, keepdims=True)
    lb = sm[0:1]
    for j in range(1, layer + 1):
        lb = lb + sm[j:j + 1]

    row = lax.broadcasted_iota(jnp.int32, (C, C), 0)
    col = lax.broadcasted_iota(jnp.int32, (C, C), 1)
    causal = row >= col
    tril = causal.astype(BF16)
    mid = C // 2
    scale = HEAD_DIM ** -0.5
    onorm = on_ref[...]

    for s in range(n_seq):
        for c in range(n_chunks):
            rows = slice(row0(s, c), row0(s, c) + C)
            f_raw = proj_ref[rows, W:2 * W]
            tt = jnp.exp(-jnp.abs(f_raw))
            rr = 1.0 / (1.0 + tt)
            pos = f_raw >= 0
            sig = jnp.where(pos, rr, tt * rr)
            sig_neg = jnp.where(pos, tt * rr, rr)
            log_f = jnp.log(lb + (1.0 - lb) * sig)
            k_scr[rows, :] = (1.0 - lb) * sig_neg
            q_scr[rows, :] = _silu(proj_ref[rows, 0:W]) * scale
            hi = log_f.astype(BF16)
            rest = log_f - hi.astype(F32)
            md = rest.astype(BF16)
            lw = (rest - md.astype(F32)).astype(BF16)
            b_scr[rows, :] = (_dot(tril, lw) + _dot(tril, md)) + _dot(tril, hi)

    for s, c, h in problems:
        p, lo = pid(s, c, h), h * HEAD_DIM
        rows = slice(row0(s, c), row0(s, c) + C)
        b = b_scr[rows, lo:lo + HEAD_DIM]
        q = q_scr[rows, lo:lo + HEAD_DIM]
        k = k_scr[rows, lo:lo + HEAD_DIM]
        b_mid = b[mid:mid + 1]
        b_last = b[C - 1:C]
        a_scr[p] = _dot_nt((q * jnp.exp(b - b_mid)).astype(BF16), (k * jnp.exp(b_mid - b)).astype(BF16))
        qin_scr[p] = q * jnp.exp(b)
        kout_scr[p] = k * jnp.exp(b_last - b)
        dec_scr[p] = jnp.exp(b_last)

    for s, c, h in problems:
        p, lo = pid(s, c, h), h * HEAD_DIM
        rows = slice(row0(s, c), row0(s, c) + C)
        a = jnp.where(causal, a_scr[p], 0.0)
        oi_scr[p] = _dot(a.astype(BF16), proj_ref[rows, 2 * W + lo:2 * W + lo + HEAD_DIM].astype(BF16))

    for c in range(n_chunks):
        for s in range(n_seq):
            for h in range(N_HEADS):
                p, lo, sh = pid(s, c, h), h * HEAD_DIM, s * N_HEADS + h
                rows = slice(row0(s, c), row0(s, c) + C)
                st = st_scr[sh]
                o = oi_scr[p] + _dot_nt(qin_scr[p].astype(BF16), st.astype(BF16))
                vb = proj_ref[rows, 2 * W + lo:2 * W + lo + HEAD_DIM].astype(BF16)
                st_scr[sh] = st * dec_scr[p] + _dot_tn(vb, kout_scr[p].astype(BF16))
                gate = proj_ref[rows, 3 * W + lo:3 * W + lo + HEAD_DIM]
                og_ref[rows, lo:lo + HEAD_DIM] = _rms(o, onorm) * _silu(gate)

    @pl.when(t == pl.num_programs(1) - 1)
    def _():
        for s in range(n_seq):
            for h in range(N_HEADS):
                sf_ref[s, h] = st_scr[s * N_HEADS + h].T


def _hgrn_recurrence(proj, lb_raw, out_norm, s0, *, batch, seq, layer, n_chunks, n_seq):
    n = proj.shape[0]
    chunk = min(MAX_CHUNK, seq)
    tb = chunk * n_chunks
    assert seq % tb == 0 and batch % n_seq == 0
    assert n_seq == 1 or tb == seq, "several sequences per step only when a step covers whole sequences"
    nt = seq // tb
    rows = n_seq * tb
    n_prob = n_seq * n_chunks * N_HEADS
    width = N_HEADS * HEAD_DIM
    has_s0 = s0 is not None
    state_spec = pl.BlockSpec((n_seq, N_HEADS, HEAD_DIM, HEAD_DIM), lambda b, t: (b, 0, 0, 0))
    in_specs = [pl.BlockSpec((rows, proj.shape[1]), lambda b, t: (b * nt + t, 0)),
                pl.BlockSpec(lb_raw.shape, lambda b, t: (0, 0)),
                pl.BlockSpec((1, HEAD_DIM), lambda b, t: (0, 0))]
    args = [proj, lb_raw, out_norm.reshape(1, HEAD_DIM)]
    if has_s0:
        in_specs.append(state_spec)
        args.append(s0)
    vm = lambda *shape: pltpu.VMEM(shape, F32)
    return pl.pallas_call(
        functools.partial(_hgrn_kernel, chunk=chunk, n_chunks=n_chunks, n_seq=n_seq, layer=layer, has_s0=has_s0),
        grid=(batch // n_seq, nt),
        in_specs=in_specs,
        out_specs=[pl.BlockSpec((rows, D_MODEL), lambda b, t: (b * nt + t, 0)), state_spec],
        out_shape=[jax.ShapeDtypeStruct((n, D_MODEL), F32),
                   jax.ShapeDtypeStruct((batch, N_HEADS, HEAD_DIM, HEAD_DIM), F32)],
        scratch_shapes=[vm(n_seq * N_HEADS, HEAD_DIM, HEAD_DIM),
                        vm(rows, width),
                        vm(rows, width),
                        vm(rows, width),
                        vm(n_prob, chunk, chunk),
                        vm(n_prob, chunk, HEAD_DIM),
                        vm(n_prob, chunk, HEAD_DIM),
                        vm(n_prob, 1, HEAD_DIM),
                        vm(n_prob, chunk, HEAD_DIM)],
        compiler_params=_params("parallel", "arbitrary"),
        name="hgrn_recurrence",
    )(*args)


def _split_bf16(x):
    hi = x.astype(BF16)
    return hi, (x - hi.astype(F32)).astype(BF16)


def _gdn_kernel(*refs, chunk, n_chunks, n_seq, has_s0):
    (main_ref, ab_ref, abt_ref, cw_ref, al_ref, dtb_ref, alt_ref, dtbt_ref, on_ref), rest = refs[:9], refs[9:]
    if has_s0:
        (cb0_ref, s0_ref), rest = rest[:2], rest[2:]
    (og_ref, sf_ref, cb_ref, s_scr, xpad, act, gcc_scr, gcr_scr, beta_scr, gram_scr, rhs_scr, wq_scr, kout_scr,
     inv_scr, pow_scr, low_scr, attn_scr, x0_scr, u_scr, ws_scr) = rest
    C = chunk
    W = N_HEADS * HEAD_DIM
    tb = C * n_chunks
    t = pl.program_id(1)
    pad = SUBLANES
    problems = [(s, c, h) for s in range(n_seq) for c in range(n_chunks) for h in range(N_HEADS)]

    def pid(s, c, h):
        return (s * n_chunks + c) * N_HEADS + h

    def row0(s, c):
        return s * tb + c * C

    @pl.when(t == 0)
    def _():
        for s in range(n_seq):
            for h in range(N_HEADS):
                s_scr[s * N_HEADS + h] = s0_ref[s, h] if has_s0 else jnp.zeros((HEAD_DIM, HEAD_DIM), F32)
            xpad[s, 0:pad, :] = jnp.zeros((pad, GD_QKV), F32)
            if has_s0:
                xpad[s, pad - (GD_CONV - 1):pad, :] = cb0_ref[s]

    @pl.when(t > 0)
    def _():
        for s in range(n_seq):
            xpad[s, 0:pad, :] = xpad[s, tb:tb + pad, :]

    for s in range(n_seq):
        for j in range(GD_QKV // LANES):
            cs = slice(j * LANES, (j + 1) * LANES)
            xpad[s, pad:pad + tb, cs] = main_ref[s * tb:(s + 1) * tb, cs]
    for s in range(n_seq):
        for j in range(GD_QKV // LANES):
            cs = slice(j * LANES, (j + 1) * LANES)
            conv = xpad[s, pad - 3:pad - 3 + tb, cs] * cw_ref[0:1, cs]
            for tap in range(1, GD_CONV):
                conv = conv + xpad[s, pad - 3 + tap:pad - 3 + tap + tb, cs] * cw_ref[tap:tap + 1, cs]
            act[s * tb:(s + 1) * tb, cs] = _silu(conv)

    @pl.when(t == pl.num_programs(1) - 1)
    def _():
        for s in range(n_seq):
            cb_ref[s] = xpad[s, pad + tb - (GD_CONV - 1):pad + tb, :]

    row = lax.broadcasted_iota(jnp.int32, (C, C), 0)
    col = lax.broadcasted_iota(jnp.int32, (C, C), 1)
    incl = row >= col
    strict = row > col
    eye = jnp.where(row == col, 1.0, 0.0)
    tril = incl.astype(F32)
    triu = (row <= col).astype(F32)
    ab = ab_ref[...]
    g_col = -jnp.exp(al_ref[...]) * _softplus(ab[:, 0:N_HEADS] + dtb_ref[...])
    beta_scr[...] = _sigmoid(ab[:, N_HEADS:2 * N_HEADS])
    for s in range(n_seq):
        g_row = -jnp.exp(alt_ref[...]) * _softplus(abt_ref[s][0:N_HEADS, :] + dtbt_ref[...])
        for c in range(n_chunks):
            r = row0(s, c)
            gcc_scr[r:r + C, :] = _dot_hi(tril, g_col[r:r + C, :])
            gcr_scr[s * n_chunks + c] = _dot_hi(g_row[:, c * C:(c + 1) * C], triu)

    scale = HEAD_DIM ** -0.5
    onorm = on_ref[...]

    for s, c, h in problems:
        p, r, lo = pid(s, c, h), row0(s, c), h * HEAD_DIM
        rows = slice(r, r + C)
        q = act[rows, lo:lo + HEAD_DIM]
        k = act[rows, W + lo:W + lo + HEAD_DIM]
        v = act[rows, 2 * W + lo:2 * W + lo + HEAD_DIM]
        q = q * lax.rsqrt(jnp.sum(q * q, axis=-1, keepdims=True) + EPS) * scale
        k = k * lax.rsqrt(jnp.sum(k * k, axis=-1, keepdims=True) + EPS)
        gc_c = gcc_scr[rows, h:h + 1]
        beta_c = beta_scr[rows, h:h + 1]
        k_beta = k * beta_c
        gram_scr[p] = _dot_nt(jnp.concatenate([k_beta, q], axis=0).astype(BF16), k.astype(BF16))
        e_gc = jnp.exp(gc_c)
        rhs_scr[p] = jnp.concatenate([v * beta_c, k_beta * e_gc], axis=1)
        wq_scr[p, C:2 * C, :] = q * e_gc
        kout_scr[p] = k * jnp.exp(gc_c[C - 1:C, :] - gc_c)

    for s, c, h in problems:
        p, r = pid(s, c, h), row0(s, c)
        gc_c = gcc_scr[r:r + C, h:h + 1]
        gc_r = gcr_scr[s * n_chunks + c, h:h + 1, :]
        decay = jnp.exp(jnp.where(incl, gc_c - gc_r, -jnp.inf))
        gram = gram_scr[p]
        lower = jnp.where(strict, gram[:C] * decay, 0.0)
        low_scr[p] = lower
        attn_scr[p] = gram[C:] * decay
        inv_scr[p] = eye - lower
        mb = (-lower).astype(BF16)
        pow_scr[p] = _dot(mb, mb)

    power = 2
    while power < C:
        last = 2 * power >= C
        for s, c, h in problems:
            p = pid(s, c, h)
            pw = pow_scr[p]
            inv = inv_scr[p]
            pb = pw.astype(BF16)
            if last:
                inv_scr[p] = inv + _dot(inv.astype(BF16), pb)
            else:
                both = _dot(jnp.concatenate([inv, pw], axis=0).astype(BF16), pb)
                inv_scr[p] = inv + both[:C]
                pow_scr[p] = both[C:]
        power *= 2

    for s, c, h in problems:
        p = pid(s, c, h)
        x0_scr[p] = _dot(inv_scr[p].astype(BF16), rhs_scr[p].astype(BF16))
    for s, c, h in problems:
        p = pid(s, c, h)
        x0 = x0_scr[p]
        x_hi, x_lo = _split_bf16(x0)
        l_hi, l_lo = _split_bf16(low_scr[p])
        lx = _dot(l_hi, jnp.concatenate([x_hi, x_lo], axis=1))
        lx = lx[:, :2 * HEAD_DIM] + (lx[:, 2 * HEAD_DIM:] + _dot(l_lo, x_hi))
        rhs_scr[p] = rhs_scr[p] - x0 - lx
    for s, c, h in problems:
        p = pid(s, c, h)
        sol = x0_scr[p] + _dot(inv_scr[p].astype(BF16), rhs_scr[p].astype(BF16))
        u_scr[p] = sol[:, :HEAD_DIM]
        wq_scr[p, 0:C, :] = sol[:, HEAD_DIM:]

    for c in range(n_chunks):
        for s in range(n_seq):
            for h in range(N_HEADS):
                sh = s * N_HEADS + h
                ws_scr[sh] = _dot(wq_scr[pid(s, c, h)].astype(BF16), s_scr[sh].astype(BF16))
        for s in range(n_seq):
            for h in range(N_HEADS):
                p, r, lo, sh = pid(s, c, h), row0(s, c), h * HEAD_DIM, s * N_HEADS + h
                rows = slice(r, r + C)
                ws = ws_scr[sh]
                vn16 = (u_scr[p] - ws[:C]).astype(BF16)
                o = ws[C:] + _dot(attn_scr[p].astype(BF16), vn16)
                g_last = gcc_scr[r + C - 1:r + C, h:h + 1]
                s_scr[sh] = s_scr[sh] * jnp.exp(g_last) + _dot_tn(kout_scr[p].astype(BF16), vn16)
                z = main_ref[rows, 3 * W + lo:3 * W + lo + HEAD_DIM]
                og_ref[rows, lo:lo + HEAD_DIM] = _rms(o, onorm) * _silu(z)

    @pl.when(t == pl.num_programs(1) - 1)
    def _():
        for s in range(n_seq):
            for h in range(N_HEADS):
                sf_ref[s, h] = s_scr[s * N_HEADS + h]


def _gdn_recurrence(main, ab, conv_w, a_log, dt_bias, out_norm, cb0, s0, *, batch, seq, n_chunks, n_seq):
    n = main.shape[0]
    chunk = min(MAX_CHUNK, seq)
    tb = chunk * n_chunks
    assert seq % tb == 0 and batch % n_seq == 0 and seq >= GD_CONV - 1
    assert n_seq == 1 or tb == seq, "several sequences per step only when a step covers whole sequences"
    nt = seq // tb
    rows = n_seq * tb
    n_prob = n_seq * n_chunks * N_HEADS
    has_s0 = s0 is not None
    abt = ab.reshape(batch, seq, 2 * N_HEADS).transpose(0, 2, 1)
    state_spec = pl.BlockSpec((n_seq, N_HEADS, HEAD_DIM, HEAD_DIM), lambda b, t: (b, 0, 0, 0))
    cb_spec = pl.BlockSpec((n_seq, GD_CONV - 1, GD_QKV), lambda b, t: (b, 0, 0))
    small = lambda shape: pl.BlockSpec(shape, lambda b, t: (0,) * len(shape))
    in_specs = [pl.BlockSpec((rows, main.shape[1]), lambda b, t: (b * nt + t, 0)),
                pl.BlockSpec((rows, 2 * N_HEADS), lambda b, t: (b * nt + t, 0)),
                pl.BlockSpec((n_seq, 2 * N_HEADS, tb), lambda b, t: (b, 0, t)),
                small((GD_CONV, GD_QKV)), small((1, N_HEADS)), small((1, N_HEADS)),
                small((N_HEADS, 1)), small((N_HEADS, 1)), small((1, HEAD_DIM))]
    args = [main, ab, abt, conv_w, a_log.reshape(1, N_HEADS), dt_bias.reshape(1, N_HEADS),
            a_log.reshape(N_HEADS, 1), dt_bias.reshape(N_HEADS, 1), out_norm.reshape(1, HEAD_DIM)]
    if has_s0:
        in_specs += [cb_spec, state_spec]
        args += [cb0, s0]
    vm = lambda *shape: pltpu.VMEM(shape, F32)
    return pl.pallas_call(
        functools.partial(_gdn_kernel, chunk=chunk, n_chunks=n_chunks, n_seq=n_seq, has_s0=has_s0),
        grid=(batch // n_seq, nt),
        in_specs=in_specs,
        out_specs=[pl.BlockSpec((rows, D_MODEL), lambda b, t: (b * nt + t, 0)), state_spec, cb_spec],
        out_shape=[jax.ShapeDtypeStruct((n, D_MODEL), F32),
                   jax.ShapeDtypeStruct((batch, N_HEADS, HEAD_DIM, HEAD_DIM), F32),
                   jax.ShapeDtypeStruct((batch, GD_CONV - 1, GD_QKV), F32)],
        scratch_shapes=[vm(n_seq * N_HEADS, HEAD_DIM, HEAD_DIM),
                        vm(n_seq, tb + SUBLANES, GD_QKV),
                        vm(rows, GD_QKV),
                        vm(rows, N_HEADS),
                        vm(n_seq * n_chunks, N_HEADS, chunk),
                        vm(rows, N_HEADS),
                        vm(n_prob, 2 * chunk, chunk),
                        vm(n_prob, chunk, 2 * HEAD_DIM),
                        vm(n_prob, 2 * chunk, HEAD_DIM),
                        vm(n_prob, chunk, HEAD_DIM),
                        vm(n_prob, chunk, chunk),
                        vm(n_prob, chunk, chunk),
                        vm(n_prob, chunk, chunk),
                        vm(n_prob, chunk, chunk),
                        vm(n_prob, chunk, 2 * HEAD_DIM),
                        vm(n_prob, chunk, HEAD_DIM),
                        vm(n_seq * N_HEADS, 2 * chunk, HEAD_DIM)],
        compiler_params=_params("parallel", "arbitrary"),
        name="gdn_recurrence",
    )(*args)


def _out_proj_kernel(og_ref, w_ref, x_ref, o_ref):
    o_ref[...] = x_ref[...] + _dot(og_ref[...].astype(BF16), w_ref[...])


def _out_proj(og, w, x, *, tm):
    n, d = x.shape
    return pl.pallas_call(
        _out_proj_kernel,
        grid=(n // tm,),
        in_specs=[pl.BlockSpec((tm, d), lambda i: (i, 0)),
                  pl.BlockSpec((d, d), lambda i: (0, 0)),
                  pl.BlockSpec((tm, d), lambda i: (i, 0))],
        out_specs=pl.BlockSpec((tm, d), lambda i: (i, 0)),
        out_shape=jax.ShapeDtypeStruct((n, d), F32),
        compiler_params=_params("parallel"),
        name="out_proj",
    )(og, w, x)


def _ffn_kernel(x_ref, g_ref, wg_ref, wu_ref, wd_ref, o_ref, h_scr):
    j = pl.program_id(1)

    @pl.when(j == 0)
    def _():
        x = x_ref[...]
        h_scr[...] = _rms(x, g_ref[...]).astype(BF16)
        o_ref[...] = x

    h = h_scr[...]
    a = _dot(h, wg_ref[...])
    u = _dot(h, wu_ref[...])
    o_ref[...] += _dot((_silu(a) * u).astype(BF16), wd_ref[...])


def _ffn(x, gain, wg, wu, wd, *, tm, tf):
    n, d = x.shape
    f = wg.shape[1]
    return pl.pallas_call(
        _ffn_kernel,
        grid=(n // tm, f // tf),
        in_specs=[pl.BlockSpec((tm, d), lambda i, j: (i, 0)),
                  pl.BlockSpec((1, d), lambda i, j: (0, 0)),
                  pl.BlockSpec((d, tf), lambda i, j: (0, j)),
                  pl.BlockSpec((d, tf), lambda i, j: (0, j)),
                  pl.BlockSpec((tf, d), lambda i, j: (j, 0))],
        out_specs=pl.BlockSpec((tm, d), lambda i, j: (i, 0)),
        out_shape=jax.ShapeDtypeStruct((n, d), F32),
        scratch_shapes=[pltpu.VMEM((tm, d), BF16)],
        compiler_params=_params("parallel", "arbitrary"),
        name="dense_ffn",
    )(x, gain.reshape(1, d), wg, wu, wd)


def _router_kernel(x_ref, g_ref, wr_ref, h_ref, idx_ref, gate_ref, rank_ref, cnt_ref, cnt_scr, *, tm):
    i = pl.program_id(0)

    @pl.when(i == 0)
    def _():
        cnt_scr[...] = jnp.zeros_like(cnt_scr)

    h = _rms(x_ref[...], g_ref[...])
    h_ref[...] = h
    logits = _dot_hi(h, wr_ref[...])
    lane = lax.broadcasted_iota(jnp.int32, logits.shape, 1)
    m1 = jnp.max(logits, axis=-1, keepdims=True)
    i1 = jnp.min(jnp.where(logits == m1, lane, N_EXPERTS), axis=-1, keepdims=True)
    rest = jnp.where(lane == i1, -jnp.inf, logits)
    m2 = jnp.max(rest, axis=-1, keepdims=True)
    i2 = jnp.min(jnp.where(rest == m2, lane, N_EXPERTS), axis=-1, keepdims=True)
    e2 = jnp.exp(m2 - m1)
    den = 1.0 + e2
    two = lax.broadcasted_iota(jnp.int32, (tm, TOP_K), 1)
    idx_ref[...] = jnp.where(two == 0, i1, i2)
    gate_ref[...] = jnp.where(two == 0, 1.0 / den, e2 / den)

    onehot = ((lane == i1) | (lane == i2)).astype(F32)
    row = lax.broadcasted_iota(jnp.int32, (tm, tm), 0)
    col = lax.broadcasted_iota(jnp.int32, (tm, tm), 1)
    before = (row > col).astype(BF16)
    excl = _dot(before, onehot.astype(BF16)) + cnt_scr[...]
    r1 = jnp.sum(jnp.where(lane == i1, excl, 0.0), axis=-1, keepdims=True)
    r2 = jnp.sum(jnp.where(lane == i2, excl, 0.0), axis=-1, keepdims=True)
    rank_ref[...] = jnp.where(two == 0, r1, r2).astype(jnp.int32)
    cnt = cnt_scr[...] + jnp.sum(onehot, axis=0, keepdims=True)
    cnt_scr[...] = cnt
    cnt_ref[...] = cnt.astype(jnp.int32)


def _router(x, gain, w_router, *, tm):
    n, d = x.shape
    two = pl.BlockSpec((tm, TOP_K), lambda i: (i, 0))
    return pl.pallas_call(
        functools.partial(_router_kernel, tm=tm),
        grid=(n // tm,),
        in_specs=[pl.BlockSpec((tm, d), lambda i: (i, 0)),
                  pl.BlockSpec((1, d), lambda i: (0, 0)),
                  pl.BlockSpec((d, N_EXPERTS), lambda i: (0, 0))],
        out_specs=[pl.BlockSpec((tm, d), lambda i: (i, 0)), two, two, two,
                   pl.BlockSpec((1, N_EXPERTS), lambda i: (0, 0))],
        out_shape=[jax.ShapeDtypeStruct((n, d), F32),
                   jax.ShapeDtypeStruct((n, TOP_K), jnp.int32),
                   jax.ShapeDtypeStruct((n, TOP_K), F32),
                   jax.ShapeDtypeStruct((n, TOP_K), jnp.int32),
                   jax.ShapeDtypeStruct((1, N_EXPERTS), jnp.int32)],
        scratch_shapes=[pltpu.VMEM((1, N_EXPERTS), F32)],
        compiler_params=_params("arbitrary"),
        name="moe_router",
    )(x, gain.reshape(1, d), w_router)


def _row_gather_copy(src_hbm, dst, sem, src_row, dst_row):
    return pltpu.make_async_copy(src_hbm.at[pl.ds(src_row, 1)], dst.at[pl.ds(dst_row, 1)], sem)


def _expert_kernel(be_ref, tok_ref, nu_ref, h_hbm, wg_ref, wu_ref, wd_ref, yb_ref, xg, xb, sem, *, tmm):
    i = pl.program_id(0)
    j = pl.program_id(1)
    active = i < nu_ref[0]

    @pl.when(jnp.logical_and(active, j == 0))
    def _():
        base = i * tmm

        def issue(r, carry):
            _row_gather_copy(h_hbm, xg, sem, tok_ref[base + r], r).start()
            return carry

        lax.fori_loop(0, tmm, issue, 0, unroll=8)
        pltpu.make_async_copy(h_hbm.at[pl.ds(0, tmm)], xg, sem).wait()
        xb[...] = xg[...].astype(BF16)

    @pl.when(jnp.logical_and(jnp.logical_not(active), j == 0))
    def _():
        yb_ref[...] = jnp.zeros_like(yb_ref)

    @pl.when(active)
    def _():
        x = xb[...]
        a = _dot(x, wg_ref[0])
        u = _dot(x, wu_ref[0])
        y = _dot((_silu(a) * u).astype(BF16), wd_ref[0])

        @pl.when(j == 0)
        def _():
            yb_ref[...] = y

        @pl.when(j > 0)
        def _():
            yb_ref[...] += y


def _experts(h, block_e, buf_tok, n_used, wg, wu, wd, *, tmm, tf):
    n, d = h.shape
    nb = block_e.shape[0]
    nf = D_FF // tf

    def wcol(i, j, be, tok, nu):
        return (be[i], 0, jnp.where(i < nu[0], j, nf - 1))

    def wrow(i, j, be, tok, nu):
        return (be[i], jnp.where(i < nu[0], j, nf - 1), 0)

    grid_spec = pltpu.PrefetchScalarGridSpec(
        num_scalar_prefetch=3,
        grid=(nb, nf),
        in_specs=[pl.BlockSpec(memory_space=pl.ANY),
                  pl.BlockSpec((1, d, tf), wcol),
                  pl.BlockSpec((1, d, tf), wcol),
                  pl.BlockSpec((1, tf, d), wrow)],
        out_specs=pl.BlockSpec((tmm, d), lambda i, j, be, tok, nu: (i, 0)),
        scratch_shapes=[pltpu.VMEM((tmm, d), F32), pltpu.VMEM((tmm, d), BF16), pltpu.SemaphoreType.DMA(())],
    )
    return pl.pallas_call(
        functools.partial(_expert_kernel, tmm=tmm),
        grid_spec=grid_spec,
        out_shape=jax.ShapeDtypeStruct((nb * tmm, d), F32),
        compiler_params=_params("arbitrary", "arbitrary"),
        name="moe_experts",
    )(block_e, buf_tok, n_used, h, wg, wu, wd)


def _combine_kernel(dest_ref, yb_hbm, x_ref, gate_ref, gain_ref, o_ref, r0, r1, sem0, sem1, *, tm):
    base = pl.program_id(0) * tm

    def issue(r, carry):
        _row_gather_copy(yb_hbm, r0, sem0, dest_ref[TOP_K * (base + r)], r).start()
        _row_gather_copy(yb_hbm, r1, sem1, dest_ref[TOP_K * (base + r) + 1], r).start()
        return carry

    lax.fori_loop(0, tm, issue, 0, unroll=8)
    pltpu.make_async_copy(yb_hbm.at[pl.ds(0, tm)], r0, sem0).wait()
    pltpu.make_async_copy(yb_hbm.at[pl.ds(0, tm)], r1, sem1).wait()
    g = gate_ref[...]
    y = r0[...] * g[:, 0:1] + r1[...] * g[:, 1:2]
    o_ref[...] = _rms(x_ref[...] + y, gain_ref[...])


def _combine(yb, dest_flat, x, gates, gain, *, tm):
    n, d = x.shape
    grid_spec = pltpu.PrefetchScalarGridSpec(
        num_scalar_prefetch=1,
        grid=(n // tm,),
        in_specs=[pl.BlockSpec(memory_space=pl.ANY),
                  pl.BlockSpec((tm, d), lambda i, dest: (i, 0)),
                  pl.BlockSpec((tm, TOP_K), lambda i, dest: (i, 0)),
                  pl.BlockSpec((1, d), lambda i, dest: (0, 0))],
        out_specs=pl.BlockSpec((tm, d), lambda i, dest: (i, 0)),
        scratch_shapes=[pltpu.VMEM((tm, d), F32), pltpu.VMEM((tm, d), F32),
                        pltpu.SemaphoreType.DMA(()), pltpu.SemaphoreType.DMA(())],
    )
    return pl.pallas_call(
        functools.partial(_combine_kernel, tm=tm),
        grid_spec=grid_spec,
        out_shape=jax.ShapeDtypeStruct((n, d), F32),
        compiler_params=_params("arbitrary"),
        name="moe_combine",
    )(dest_flat, yb, x, gates, gain.reshape(1, d))


def _moe_and_final_norm(x, gain, w_router, wg, wu, wd, final_gain, *, tiles):
    n = x.shape[0]
    tmm = tiles["moe_rows"]
    h, idx, gates, rank, counts = _router(x, gain, w_router, tm=tiles["router_rows"])
    counts = counts[0]
    pcounts = (counts + tmm - 1) // tmm * tmm
    pend = jnp.cumsum(pcounts)
    pstart = pend - pcounts
    dest = pstart[idx] + rank
    nb = -(-(n * TOP_K + N_EXPERTS * (tmm - 1)) // tmm)
    tok = jnp.broadcast_to(jnp.arange(n, dtype=jnp.int32)[:, None], (n, TOP_K))
    buf_tok = jnp.zeros((nb * tmm,), jnp.int32).at[dest.reshape(-1)].set(tok.reshape(-1))
    n_used = (pend[-1] // tmm).astype(jnp.int32)
    blk = jnp.arange(nb, dtype=jnp.int32)
    block_e = jnp.minimum(jnp.searchsorted(pend, jnp.minimum(blk, n_used - 1) * tmm, side="right"),
                          N_EXPERTS - 1).astype(jnp.int32)
    yb = _experts(h, block_e, buf_tok, n_used.reshape(1), wg, wu, wd, tmm=tmm, tf=tiles["ff_cols"])
    return _combine(yb, dest.reshape(-1).astype(jnp.int32), x, gates, final_gain, tm=tiles["combine_rows"])


def _tiles(n_rows, seq):
    big = n_rows >= 8192
    return {
        "proj_rows": 1024 if n_rows % 1024 == 0 else n_rows,
        "proj_cols": 1024,
        "rec_chunks": min(4, seq // min(MAX_CHUNK, seq)),
        "rec_seqs": 1 if seq > MAX_CHUNK else 4,
        "ffn_rows": 1024 if n_rows % 1024 == 0 else n_rows,
        "ff_cols": 512,
        "router_rows": 512,
        "moe_rows": 512 if big else 256,
        "combine_rows": 256,
    }


def _trunk(x3, hg_state, gd_state, gd_conv, p):
    batch, seq, d = x3.shape
    n = batch * seq
    x = x3.reshape(n, d)
    tiles = _tiles(n, seq)
    tm, tn = tiles["proj_rows"], tiles["proj_cols"]

    proj = _norm_proj(x, p["norm_mix"][0], p["hgrn_w_in"], tm=tm, tn=tn)[0]
    og, hg_new = _hgrn_recurrence(proj, p["hgrn_lb"], p["hgrn_norm"], hg_state, batch=batch, seq=seq, layer=0,
                                  n_chunks=tiles["rec_chunks"], n_seq=tiles["rec_seqs"])
    x = _out_proj(og, p["hgrn_w_out"], x, tm=tm)
    x = _ffn(x, p["norm_ffn"][0], p["ffn_w_gate"], p["ffn_w_up"], p["ffn_w_down"],
             tm=tiles["ffn_rows"], tf=tiles["ff_cols"])

    main, ab = _norm_proj(x, p["norm_mix"][1], p["gdn_w_main"], p["gdn_w_ab"], tm=tm, tn=tn)
    og, gd_new, cv_new = _gdn_recurrence(main, ab, p["gdn_conv"], p["gdn_a_log"], p["gdn_dt_bias"], p["gdn_norm"],
                                         gd_conv, gd_state, batch=batch, seq=seq, n_chunks=tiles["rec_chunks"],
                                         n_seq=tiles["rec_seqs"])
    x = _out_proj(og, p["gdn_w_out"], x, tm=tm)
    y = _moe_and_final_norm(x, p["norm_ffn"][1], p["moe_router"], p["moe_w_gate"], p["moe_w_up"], p["moe_w_down"],
                            p["norm_out"], tiles=tiles)
    return y.reshape(batch, seq, d), hg_new[None], gd_new[None], cv_new[None]


def kernel(x_prompt, x_sample, state_hgrn, state_gdn, state_gdn_conv, norm_mix, norm_ffn, norm_out, hgrn_w_in, hgrn_lb, hgrn_norm, hgrn_w_out, gdn_w_in, gdn_conv, gdn_a_log, gdn_dt_bias, gdn_norm, gdn_w_out, ffn_w_gate, ffn_w_up, ffn_w_down, moe_router, moe_w_gate, moe_w_up, moe_w_down):
    assert hgrn_w_in.shape[0] == 1 and gdn_w_in.shape[0] == 1, "one HGRN2 layer and one gated-DeltaNet layer"
    main_cols = GD_QKV + N_HEADS * HEAD_DIM
    p = {
        "norm_mix": norm_mix, "norm_ffn": norm_ffn, "norm_out": norm_out,
        "hgrn_w_in": hgrn_w_in[0].astype(BF16), "hgrn_lb": hgrn_lb, "hgrn_norm": hgrn_norm[0],
        "hgrn_w_out": hgrn_w_out[0].astype(BF16),
        "gdn_w_main": gdn_w_in[0, :, :main_cols].astype(BF16), "gdn_w_ab": gdn_w_in[0, :, main_cols:].astype(BF16),
        "gdn_conv": gdn_conv[0], "gdn_a_log": gdn_a_log[0], "gdn_dt_bias": gdn_dt_bias[0], "gdn_norm": gdn_norm[0],
        "gdn_w_out": gdn_w_out[0].astype(BF16),
        "ffn_w_gate": ffn_w_gate[0].astype(BF16), "ffn_w_up": ffn_w_up[0].astype(BF16),
        "ffn_w_down": ffn_w_down[0].astype(BF16),
        "moe_router": moe_router[0],
        "moe_w_gate": moe_w_gate[0].astype(BF16), "moe_w_up": moe_w_up[0].astype(BF16),
        "moe_w_down": moe_w_down[0].astype(BF16),
    }
    y_p, hg_p, gd_p, cv_p = _trunk(x_prompt, None, None, None, p)
    y_s, hg_s, gd_s, cv_s = _trunk(x_sample, state_hgrn[0], state_gdn[0], state_gdn_conv[0], p)
    return (y_p, y_s, hg_p, hg_s, gd_p, gd_s, cv_p, cv_s)
```

```python
import functools

import jax
import jax.numpy as jnp
from jax import lax
from jax.experimental import pallas as pl
from jax.experimental.pallas import tpu as pltpu

F32 = jnp.float32
BF16 = jnp.bfloat16
HIGHEST = lax.Precision.HIGHEST

D_MODEL = 1024
N_HEADS = 8
HEAD_DIM = 128
D_FF = 3584
N_EXPERTS = 8
TOP_K = 2
GD_QKV = 3 * D_MODEL
GD_CONV = 4
MAX_CHUNK = 64
EPS = 1e-6
LANES = 128
SUBLANES = 8
VMEM_LIMIT_BYTES = 52 * 1024 * 1024


def _params(*sem):
    return pltpu.CompilerParams(dimension_semantics=sem, vmem_limit_bytes=VMEM_LIMIT_BYTES)


def _rms(x, gain):
    return x * lax.rsqrt(jnp.mean(x * x, axis=-1, keepdims=True) + EPS) * gain


def _sigmoid(x):
    t = jnp.exp(-jnp.abs(x))
    r = 1.0 / (1.0 + t)
    return jnp.where(x >= 0, r, t * r)


def _silu(x):
    return x * _sigmoid(x)


def _softplus(x):
    return jnp.maximum(x, 0.0) + jnp.log1p(jnp.exp(-jnp.abs(x)))


def _dot(a, b):
    return jnp.dot(a, b, preferred_element_type=F32)


def _dot_nt(a, b):
    return lax.dot_general(a, b, (((1,), (1,)), ((), ())), preferred_element_type=F32)


def _dot_tn(a, b):
    return lax.dot_general(a, b, (((0,), (0,)), ((), ())), preferred_element_type=F32)


def _dot_hi(a, b):
    return jnp.dot(a, b, precision=HIGHEST, preferred_element_type=F32)


def _norm_proj_kernel(x_ref, g_ref, w_ref, *rest, has_small):
    if has_small:
        ws_ref, o_ref, os_ref, h_scr = rest
    else:
        o_ref, h_scr = rest

    @pl.when(pl.program_id(1) == 0)
    def _():
        hb = _rms(x_ref[...], g_ref[...]).astype(BF16)
        h_scr[...] = hb
        if has_small:
            os_ref[...] = _dot(hb, ws_ref[...])

    o_ref[...] = _dot(h_scr[...], w_ref[...])


def _norm_proj(x, gain, w, w_small=None, *, tm, tn):
    n, d = x.shape
    e = w.shape[1]
    has_small = w_small is not None
    in_specs = [pl.BlockSpec((tm, d), lambda i, j: (i, 0)),
                pl.BlockSpec((1, d), lambda i, j: (0, 0)),
                pl.BlockSpec((d, tn), lambda i, j: (0, j))]
    out_shape = [jax.ShapeDtypeStruct((n, e), F32)]
    out_specs = [pl.BlockSpec((tm, tn), lambda i, j: (i, j))]
    args = [x, gain.reshape(1, d), w]
    if has_small:
        es = w_small.shape[1]
        in_specs.append(pl.BlockSpec((d, es), lambda i, j: (0, 0)))
        out_shape.append(jax.ShapeDtypeStruct((n, es), F32))
        out_specs.append(pl.BlockSpec((tm, es), lambda i, j: (i, 0)))
        args.append(w_small)
    return pl.pallas_call(
        functools.partial(_norm_proj_kernel, has_small=has_small),
        grid=(n // tm, e // tn),
        in_specs=in_specs, out_specs=out_specs, out_shape=out_shape,
        scratch_shapes=[pltpu.VMEM((tm, d), BF16)],
        compiler_params=_params("parallel", "arbitrary"),
        name="norm_proj",
    )(*args)


def _hgrn_kernel(*refs, chunk, n_chunks, n_seq, layer, has_s0):
    if has_s0:
        proj_ref, lb_ref, on_ref, s0_ref, og_ref, sf_ref, *scr = refs
    else:
        proj_ref, lb_ref, on_ref, og_ref, sf_ref, *scr = refs
    st_scr, b_scr, q_scr, k_scr, a_scr, qin_scr, kout_scr, dec_scr, oi_scr = scr
    C = chunk
    W = N_HEADS * HEAD_DIM
    tb = C * n_chunks
    t = pl.program_id(1)
    problems = [(s, c, h) for s in range(n_seq) for c in range(n_chunks) for h in range(N_HEADS)]

    def pid(s, c, h):
        return (s * n_chunks + c) * N_HEADS + h

    def row0(s, c):
        return s * tb + c * C

    @pl.when(t == 0)
    def _():
        for s in range(n_seq):
            for h in range(N_HEADS):
                st_scr[s * N_HEADS + h] = s0_ref[s, h].T if has_s0 else jnp.zeros((HEAD_DIM, HEAD_DIM), F32)

    lbraw = lb_ref[...]
    ex = jnp.exp(lbraw - jnp.max(lbraw, axis=0, keepdims=True))
    sm = ex / jnp.sum(ex, axis=0, keepdims=True)
    lb = sm[0:1]
    for j in range(1, layer + 1):
        lb = lb + sm[j:j + 1]

    row = lax.broadcasted_iota(jnp.int32, (C, C), 0)
    col = lax.broadcasted_iota(jnp.int32, (C, C), 1)
    causal = row >= col
    tril = causal.astype(BF16)
    mid = C // 2
    scale = HEAD_DIM ** -0.5
    onorm = on_ref[...]

    for s in range(n_seq):
        for c in range(n_chunks):
            rows = slice(row0(s, c), row0(s, c) + C)
            f_raw = proj_ref[rows, W:2 * W]
            tt = jnp.exp(-jnp.abs(f_raw))
            rr = 1.0 / (1.0 + tt)
            pos = f_raw >= 0
            sig = jnp.where(pos, rr, tt * rr)
            sig_neg = jnp.where(pos, tt * rr, rr)
            log_f = jnp.log(lb + (1.0 - lb) * sig)
            k_scr[rows, :] = (1.0 - lb) * sig_neg
            q_scr[rows, :] = _silu(proj_ref[rows, 0:W]) * scale
            hi = log_f.astype(BF16)
            rest = log_f - hi.astype(F32)
            md = rest.astype(BF16)
            lw = (rest - md.astype(F32)).astype(BF16)
            b_scr[rows, :] = (_dot(tril, lw) + _dot(tril, md)) + _dot(tril, hi)

    for s, c, h in problems:
        p, lo = pid(s, c, h), h * HEAD_DIM
        rows = slice(row0(s, c), row0(s, c) + C)
        b = b_scr[rows, lo:lo + HEAD_DIM]
        q = q_scr[rows, lo:lo + HEAD_DIM]
        k = k_scr[rows, lo:lo + HEAD_DIM]
        b_mid = b[mid:mid + 1]
        b_last = b[C - 1:C]
        a_scr[p] = _dot_nt((q * jnp.exp(b - b_mid)).astype(BF16), (k * jnp.exp(b_mid - b)).astype(BF16))
        qin_scr[p] = q * jnp.exp(b)
        kout_scr[p] = k * jnp.exp(b_last - b)
        dec_scr[p] = jnp.exp(b_last)

    for s, c, h in problems:
        p, lo = pid(s, c, h), h * HEAD_DIM
        rows = slice(row0(s, c), row0(s, c) + C)
        a = jnp.where(causal, a_scr[p], 0.0)
        oi_scr[p] = _dot(a.astype(BF16), proj_ref[rows, 2 * W + lo:2 * W + lo + HEAD_DIM].astype(BF16))

    for c in range(n_chunks):
        for s in range(n_seq):
            for h in range(N_HEADS):
                p, lo, sh = pid(s, c, h), h * HEAD_DIM, s * N_HEADS + h
                rows = slice(row0(s, c), row0(s, c) + C)
                st = st_scr[sh]
                o = oi_scr[p] + _dot_nt(qin_scr[p].astype(BF16), st.astype(BF16))
                vb = proj_ref[rows, 2 * W + lo:2 * W + lo + HEAD_DIM].astype(BF16)
                st_scr[sh] = st * dec_scr[p] + _dot_tn(vb, kout_scr[p].astype(BF16))
                gate = proj_ref[rows, 3 * W + lo:3 * W + lo + HEAD_DIM]
                og_ref[rows, lo:lo + HEAD_DIM] = _rms(o, onorm) * _silu(gate)

    @pl.when(t == pl.num_programs(1) - 1)
    def _():
        for s in range(n_seq):
            for h in range(N_HEADS):
                sf_ref[s, h] = st_scr[s * N_HEADS + h].T


def _hgrn_recurrence(proj, lb_raw, out_norm, s0, *, batch, seq, layer, n_chunks, n_seq):
    n = proj.shape[0]
    chunk = min(MAX_CHUNK, seq)
    tb = chunk * n_chunks
    assert seq % tb == 0 and batch % n_seq == 0
    assert n_seq == 1 or tb == seq, "several sequences per step only when a step covers whole sequences"
    nt = seq // tb
    rows = n_seq * tb
    n_prob = n_seq * n_chunks * N_HEADS
    width = N_HEADS * HEAD_DIM
    has_s0 = s0 is not None
    state_spec = pl.BlockSpec((n_seq, N_HEADS, HEAD_DIM, HEAD_DIM), lambda b, t: (b, 0, 0, 0))
    in_specs = [pl.BlockSpec((rows, proj.shape[1]), lambda b, t: (b * nt + t, 0)),
                pl.BlockSpec(lb_raw.shape, lambda b, t: (0, 0)),
                pl.BlockSpec((1, HEAD_DIM), lambda b, t: (0, 0))]
    args = [proj, lb_raw, out_norm.reshape(1, HEAD_DIM)]
    if has_s0:
        in_specs.append(state_spec)
        args.append(s0)
    vm = lambda *shape: pltpu.VMEM(shape, F32)
    return pl.pallas_call(
        functools.partial(_hgrn_kernel, chunk=chunk, n_chunks=n_chunks, n_seq=n_seq, layer=layer, has_s0=has_s0),
        grid=(batch // n_seq, nt),
        in_specs=in_specs,
        out_specs=[pl.BlockSpec((rows, D_MODEL), lambda b, t: (b * nt + t, 0)), state_spec],
        out_shape=[jax.ShapeDtypeStruct((n, D_MODEL), F32),
                   jax.ShapeDtypeStruct((batch, N_HEADS, HEAD_DIM, HEAD_DIM), F32)],
        scratch_shapes=[vm(n_seq * N_HEADS, HEAD_DIM, HEAD_DIM),
                        vm(rows, width),
                        vm(rows, width),
                        vm(rows, width),
                        vm(n_prob, chunk, chunk),
                        vm(n_prob, chunk, HEAD_DIM),
                        vm(n_prob, chunk, HEAD_DIM),
                        vm(n_prob, 1, HEAD_DIM),
                        vm(n_prob, chunk, HEAD_DIM)],
        compiler_params=_params("parallel", "arbitrary"),
        name="hgrn_recurrence",
    )(*args)


def _split_bf16(x):
    hi = x.astype(BF16)
    return hi, (x - hi.astype(F32)).astype(BF16)


def _gdn_kernel(*refs, chunk, n_chunks, n_seq, has_s0):
    (main_ref, ab_ref, abt_ref, cw_ref, al_ref, dtb_ref, alt_ref, dtbt_ref, on_ref), rest = refs[:9], refs[9:]
    if has_s0:
        (cb0_ref, s0_ref), rest = rest[:2], rest[2:]
    (og_ref, sf_ref, cb_ref, s_scr, xpad, act, gcc_scr, gcr_scr, beta_scr, gram_scr, rhs_scr, wq_scr, kout_scr,
     inv_scr, pow_scr, low_scr, attn_scr, x0_scr, u_scr, ws_scr) = rest
    C = chunk
    W = N_HEADS * HEAD_DIM
    tb = C * n_chunks
    t = pl.program_id(1)
    pad = SUBLANES
    problems = [(s, c, h) for s in range(n_seq) for c in range(n_chunks) for h in range(N_HEADS)]

    def pid(s, c, h):
        return (s * n_chunks + c) * N_HEADS + h

    def row0(s, c):
        return s * tb + c * C

    @pl.when(t == 0)
    def _():
        for s in range(n_seq):
            for h in range(N_HEADS):
                s_scr[s * N_HEADS + h] = s0_ref[s, h] if has_s0 else jnp.zeros((HEAD_DIM, HEAD_DIM), F32)
            xpad[s, 0:pad, :] = jnp.zeros((pad, GD_QKV), F32)
            if has_s0:
                xpad[s, pad - (GD_CONV - 1):pad, :] = cb0_ref[s]

    @pl.when(t > 0)
    def _():
        for s in range(n_seq):
            xpad[s, 0:pad, :] = xpad[s, tb:tb + pad, :]

    for s in range(n_seq):
        for j in range(GD_QKV // LANES):
            cs = slice(j * LANES, (j + 1) * LANES)
            xpad[s, pad:pad + tb, cs] = main_ref[s * tb:(s + 1) * tb, cs]
    for s in range(n_seq):
        for j in range(GD_QKV // LANES):
            cs = slice(j * LANES, (j + 1) * LANES)
            conv = xpad[s, pad - 3:pad - 3 + tb, cs] * cw_ref[0:1, cs]
            for tap in range(1, GD_CONV):
                conv = conv + xpad[s, pad - 3 + tap:pad - 3 + tap + tb, cs] * cw_ref[tap:tap + 1, cs]
            act[s * tb:(s + 1) * tb, cs] = _silu(conv)

    @pl.when(t == pl.num_programs(1) - 1)
    def _():
        for s in range(n_seq):
            cb_ref[s] = xpad[s, pad + tb - (GD_CONV - 1):pad + tb, :]

    row = lax.broadcasted_iota(jnp.int32, (C, C), 0)
    col = lax.broadcasted_iota(jnp.int32, (C, C), 1)
    incl = row >= col
    strict = row > col
    eye = jnp.where(row == col, 1.0, 0.0)
    tril = incl.astype(F32)
    triu = (row <= col).astype(F32)
    ab = ab_ref[...]
    g_col = -jnp.exp(al_ref[...]) * _softplus(ab[:, 0:N_HEADS] + dtb_ref[...])
    beta_scr[...] = _sigmoid(ab[:, N_HEADS:2 * N_HEADS])
    for s in range(n_seq):
        g_row = -jnp.exp(alt_ref[...]) * _softplus(abt_ref[s][0:N_HEADS, :] + dtbt_ref[...])
        for c in range(n_chunks):
            r = row0(s, c)
            gcc_scr[r:r + C, :] = _dot_hi(tril, g_col[r:r + C, :])
            gcr_scr[s * n_chunks + c] = _dot_hi(g_row[:, c * C:(c + 1) * C], triu)

    scale = HEAD_DIM ** -0.5
    onorm = on_ref[...]

    for s, c, h in problems:
        p, r, lo = pid(s, c, h), row0(s, c), h * HEAD_DIM
        rows = slice(r, r + C)
        q = act[rows, lo:lo + HEAD_DIM]
        k = act[rows, W + lo:W + lo + HEAD_DIM]
        v = act[rows, 2 * W + lo:2 * W + lo + HEAD_DIM]
        q = q * lax.rsqrt(jnp.sum(q * q, axis=-1, keepdims=True) + EPS) * scale
        k = k * lax.rsqrt(jnp.sum(k * k, axis=-1, keepdims=True) + EPS)
        gc_c = gcc_scr[rows, h:h + 1]
        beta_c = beta_scr[rows, h:h + 1]
        k_beta = k * beta_c
        gram_scr[p] = _dot_nt(jnp.concatenate([k_beta, q], axis=0).astype(BF16), k.astype(BF16))
        e_gc = jnp.exp(gc_c)
        rhs_scr[p] = jnp.concatenate([v * beta_c, k_beta * e_gc], axis=1)
        wq_scr[p, C:2 * C, :] = q * e_gc
        kout_scr[p] = k * jnp.exp(gc_c[C - 1:C, :] - gc_c)

    for s, c, h in problems:
        p, r = pid(s, c, h), row0(s, c)
        gc_c = gcc_scr[r:r + C, h:h + 1]
        gc_r = gcr_scr[s * n_chunks + c, h:h + 1, :]
        decay = jnp.exp(jnp.where(incl, gc_c - gc_r, -jnp.inf))
        gram = gram_scr[p]
        lower = jnp.where(strict, gram[:C] * decay, 0.0)
        low_scr[p] = lower
        attn_scr[p] = gram[C:] * decay
        inv_scr[p] = eye - lower
        mb = (-lower).astype(BF16)
        pow_scr[p] = _dot(mb, mb)

    power = 2
    while power < C:
        last = 2 * power >= C
        for s, c, h in problems:
            p = pid(s, c, h)
            pw = pow_scr[p]
            inv = inv_scr[p]
            pb = pw.astype(BF16)
            if last:
                inv_scr[p] = inv + _dot(inv.astype(BF16), pb)
            else:
                both = _dot(jnp.concatenate([inv, pw], axis=0).astype(BF16), pb)
                inv_scr[p] = inv + both[:C]
                pow_scr[p] = both[C:]
        power *= 2

    for s, c, h in problems:
        p = pid(s, c, h)
        x0_scr[p] = _dot(inv_scr[p].astype(BF16), rhs_scr[p].astype(BF16))
    for s, c, h in problems:
        p = pid(s, c, h)
        x0 = x0_scr[p]
        x_hi, x_lo = _split_bf16(x0)
        l_hi, l_lo = _split_bf16(low_scr[p])
        lx = _dot(l_hi, jnp.concatenate([x_hi, x_lo], axis=1))
        lx = lx[:, :2 * HEAD_DIM] + (lx[:, 2 * HEAD_DIM:] + _dot(l_lo, x_hi))
        rhs_scr[p] = rhs_scr[p] - x0 - lx
    for s, c, h in problems:
        p = pid(s, c, h)
        sol = x0_scr[p] + _dot(inv_scr[p].astype(BF16), rhs_scr[p].astype(BF16))
        u_scr[p] = sol[:, :HEAD_DIM]
        wq_scr[p, 0:C, :] = sol[:, HEAD_DIM:]

    for c in range(n_chunks):
        for s in range(n_seq):
            for h in range(N_HEADS):
                sh = s * N_HEADS + h
                ws_scr[sh] = _dot(wq_scr[pid(s, c, h)].astype(BF16), s_scr[sh].astype(BF16))
        for s in range(n_seq):
            for h in range(N_HEADS):
                p, r, lo, sh = pid(s, c, h), row0(s, c), h * HEAD_DIM, s * N_HEADS + h
                rows = slice(r, r + C)
                ws = ws_scr[sh]
                vn16 = (u_scr[p] - ws[:C]).astype(BF16)
                o = ws[C:] + _dot(attn_scr[p].astype(BF16), vn16)
                g_last = gcc_scr[r + C - 1:r + C, h:h + 1]
                s_scr[sh] = s_scr[sh] * jnp.exp(g_last) + _dot_tn(kout_scr[p].astype(BF16), vn16)
                z = main_ref[rows, 3 * W + lo:3 * W + lo + HEAD_DIM]
                og_ref[rows, lo:lo + HEAD_DIM] = _rms(o, onorm) * _silu(z)

    @pl.when(t == pl.num_programs(1) - 1)
    def _():
        for s in range(n_seq):
            for h in range(N_HEADS):
                sf_ref[s, h] = s_scr[s * N_HEADS + h]


def _gdn_recurrence(main, ab, conv_w, a_log, dt_bias, out_norm, cb0, s0, *, batch, seq, n_chunks, n_seq):
    n = main.shape[0]
    chunk = min(MAX_CHUNK, seq)
    tb = chunk * n_chunks
    assert seq % tb == 0 and batch % n_seq == 0 and seq >= GD_CONV - 1
    assert n_seq == 1 or tb == seq, "several sequences per step only when a step covers whole sequences"
    nt = seq // tb
    rows = n_seq * tb
    n_prob = n_seq * n_chunks * N_HEADS
    has_s0 = s0 is not None
    abt = ab.reshape(batch, seq, 2 * N_HEADS).transpose(0, 2, 1)
    state_spec = pl.BlockSpec((n_seq, N_HEADS, HEAD_DIM, HEAD_DIM), lambda b, t: (b, 0, 0, 0))
    cb_spec = pl.BlockSpec((n_seq, GD_CONV - 1, GD_QKV), lambda b, t: (b, 0, 0))
    small = lambda shape: pl.BlockSpec(shape, lambda b, t: (0,) * len(shape))
    in_specs = [pl.BlockSpec((rows, main.shape[1]), lambda b, t: (b * nt + t, 0)),
                pl.BlockSpec((rows, 2 * N_HEADS), lambda b, t: (b * nt + t, 0)),
                pl.BlockSpec((n_seq, 2 * N_HEADS, tb), lambda b, t: (b, 0, t)),
                small((GD_CONV, GD_QKV)), small((1, N_HEADS)), small((1, N_HEADS)),
                small((N_HEADS, 1)), small((N_HEADS, 1)), small((1, HEAD_DIM))]
    args = [main, ab, abt, conv_w, a_log.reshape(1, N_HEADS), dt_bias.reshape(1, N_HEADS),
            a_log.reshape(N_HEADS, 1), dt_bias.reshape(N_HEADS, 1), out_norm.reshape(1, HEAD_DIM)]
    if has_s0:
        in_specs += [cb_spec, state_spec]
        args += [cb0, s0]
    vm = lambda *shape: pltpu.VMEM(shape, F32)
    return pl.pallas_call(
        functools.partial(_gdn_kernel, chunk=chunk, n_chunks=n_chunks, n_seq=n_seq, has_s0=has_s0),
        grid=(batch // n_seq, nt),
        in_specs=in_specs,
        out_specs=[pl.BlockSpec((rows, D_MODEL), lambda b, t: (b * nt + t, 0)), state_spec, cb_spec],
        out_shape=[jax.ShapeDtypeStruct((n, D_MODEL), F32),
                   jax.ShapeDtypeStruct((batch, N_HEADS, HEAD_DIM, HEAD_DIM), F32),
                   jax.ShapeDtypeStruct((batch, GD_CONV - 1, GD_QKV), F32)],
        scratch_shapes=[vm(n_seq * N_HEADS, HEAD_DIM, HEAD_DIM),
                        vm(n_seq, tb + SUBLANES, GD_QKV),
                        vm(rows, GD_QKV),
                        vm(rows, N_HEADS),
                        vm(n_seq * n_chunks, N_HEADS, chunk),
                        vm(rows, N_HEADS),
                        vm(n_prob, 2 * chunk, chunk),
                        vm(n_prob, chunk, 2 * HEAD_DIM),
                        vm(n_prob, 2 * chunk, HEAD_DIM),
                        vm(n_prob, chunk, HEAD_DIM),
                        vm(n_prob, chunk, chunk),
                        vm(n_prob, chunk, chunk),
                        vm(n_prob, chunk, chunk),
                        vm(n_prob, chunk, chunk),
                        vm(n_prob, chunk, 2 * HEAD_DIM),
                        vm(n_prob, chunk, HEAD_DIM),
                        vm(n_seq * N_HEADS, 2 * chunk, HEAD_DIM)],
        compiler_params=_params("parallel", "arbitrary"),
        name="gdn_recurrence",
    )(*args)


def _out_proj_kernel(og_ref, w_ref, x_ref, o_ref):
    o_ref[...] = x_ref[...] + _dot(og_ref[...].astype(BF16), w_ref[...])


def _out_proj(og, w, x, *, tm):
    n, d = x.shape
    return pl.pallas_call(
        _out_proj_kernel,
        grid=(n // tm,),
        in_specs=[pl.BlockSpec((tm, d), lambda i: (i, 0)),
                  pl.BlockSpec((d, d), lambda i: (0, 0)),
                  pl.BlockSpec((tm, d), lambda i: (i, 0))],
        out_specs=pl.BlockSpec((tm, d), lambda i: (i, 0)),
        out_shape=jax.ShapeDtypeStruct((n, d), F32),
        compiler_params=_params("parallel"),
        name="out_proj",
    )(og, w, x)


def _ffn_kernel(x_ref, g_ref, wg_ref, wu_ref, wd_ref, o_ref, h_scr):
    j = pl.program_id(1)

    @pl.when(j == 0)
    def _():
        x = x_ref[...]
        h_scr[...] = _rms(x, g_ref[...]).astype(BF16)
        o_ref[...] = x

    h = h_scr[...]
    a = _dot(h, wg_ref[...])
    u = _dot(h, wu_ref[...])
    o_ref[...] += _dot((_silu(a) * u).astype(BF16), wd_ref[...])


def _ffn(x, gain, wg, wu, wd, *, tm, tf):
    n, d = x.shape
    f = wg.shape[1]
    return pl.pallas_call(
        _ffn_kernel,
        grid=(n // tm, f // tf),
        in_specs=[pl.BlockSpec((tm, d), lambda i, j: (i, 0)),
                  pl.BlockSpec((1, d), lambda i, j: (0, 0)),
                  pl.BlockSpec((d, tf), lambda i, j: (0, j)),
                  pl.BlockSpec((d, tf), lambda i, j: (0, j)),
                  pl.BlockSpec((tf, d), lambda i, j: (j, 0))],
        out_specs=pl.BlockSpec((tm, d), lambda i, j: (i, 0)),
        out_shape=jax.ShapeDtypeStruct((n, d), F32),
        scratch_shapes=[pltpu.VMEM((tm, d), BF16)],
        compiler_params=_params("parallel", "arbitrary"),
        name="dense_ffn",
    )(x, gain.reshape(1, d), wg, wu, wd)


def _router_kernel(x_ref, g_ref, wr_ref, h_ref, idx_ref, gate_ref, rank_ref, cnt_ref, cnt_scr, *, tm):
    i = pl.program_id(0)

    @pl.when(i == 0)
    def _():
        cnt_scr[...] = jnp.zeros_like(cnt_scr)

    h = _rms(x_ref[...], g_ref[...])
    _store_token_tiles(h_ref, 0, h)
    logits = _dot_hi(h, wr_ref[...])
    lane = lax.broadcasted_iota(jnp.int32, logits.shape, 1)
    m1 = jnp.max(logits, axis=-1, keepdims=True)
    i1 = jnp.min(jnp.where(logits == m1, lane, N_EXPERTS), axis=-1, keepdims=True)
    rest = jnp.where(lane == i1, -jnp.inf, logits)
    m2 = jnp.max(rest, axis=-1, keepdims=True)
    i2 = jnp.min(jnp.where(rest == m2, lane, N_EXPERTS), axis=-1, keepdims=True)
    e2 = jnp.exp(m2 - m1)
    den = 1.0 + e2
    two = lax.broadcasted_iota(jnp.int32, (tm, TOP_K), 1)
    idx_ref[...] = jnp.where(two == 0, i1, i2)
    gate_ref[...] = jnp.where(two == 0, 1.0 / den, e2 / den)

    onehot = ((lane == i1) | (lane == i2)).astype(F32)
    row = lax.broadcasted_iota(jnp.int32, (tm, tm), 0)
    col = lax.broadcasted_iota(jnp.int32, (tm, tm), 1)
    before = (row > col).astype(BF16)
    excl = _dot(before, onehot.astype(BF16)) + cnt_scr[...]
    r1 = jnp.sum(jnp.where(lane == i1, excl, 0.0), axis=-1, keepdims=True)
    r2 = jnp.sum(jnp.where(lane == i2, excl, 0.0), axis=-1, keepdims=True)
    rank_ref[...] = jnp.where(two == 0, r1, r2).astype(jnp.int32)
    cnt = cnt_scr[...] + jnp.sum(onehot, axis=0, keepdims=True)
    cnt_scr[...] = cnt
    cnt_ref[...] = cnt.astype(jnp.int32)


def _router(x, gain, w_router, *, tm):
    n, d = x.shape
    two = pl.BlockSpec((tm, TOP_K), lambda i: (i, 0))
    return pl.pallas_call(
        functools.partial(_router_kernel, tm=tm),
        grid=(n // tm,),
        in_specs=[pl.BlockSpec((tm, d), lambda i: (i, 0)),
                  pl.BlockSpec((1, d), lambda i: (0, 0)),
                  pl.BlockSpec((d, N_EXPERTS), lambda i: (0, 0))],
        out_specs=[pl.BlockSpec((tm * SUBLANES, LANES), lambda i: (i, 0)), two, two, two,
                   pl.BlockSpec((1, N_EXPERTS), lambda i: (0, 0))],
        out_shape=[jax.ShapeDtypeStruct((n * SUBLANES, LANES), F32),
                   jax.ShapeDtypeStruct((n, TOP_K), jnp.int32),
                   jax.ShapeDtypeStruct((n, TOP_K), F32),
                   jax.ShapeDtypeStruct((n, TOP_K), jnp.int32),
                   jax.ShapeDtypeStruct((1, N_EXPERTS), jnp.int32)],
        scratch_shapes=[pltpu.VMEM((1, N_EXPERTS), F32)],
        compiler_params=_params("arbitrary"),
        name="moe_router",
    )(x, gain.reshape(1, d), w_router)


def _tile_gather_copy(src_hbm, dst, sem, src_row, dst_row):
    return pltpu.make_async_copy(src_hbm.at[pl.ds(pl.multiple_of(src_row * SUBLANES, SUBLANES), SUBLANES)],
                                 dst.at[pl.ds(pl.multiple_of(dst_row * SUBLANES, SUBLANES), SUBLANES)], sem)


def _load_token_tiles(ref, first_row, n_rows):
    return jnp.concatenate([ref[pl.ds(first_row * SUBLANES + s, n_rows, stride=SUBLANES), :]
                            for s in range(D_MODEL // LANES)], axis=1)


def _store_token_tiles(ref, first_row, x):
    for s in range(D_MODEL // LANES):
        ref[pl.ds(first_row * SUBLANES + s, x.shape[0], stride=SUBLANES), :] = x[:, s * LANES:(s + 1) * LANES]


def _expert_kernel(be_ref, tok_ref, nv_ref, nu_ref, h_hbm, wg_ref, wu_ref, wd_ref, yb_ref, xg, xb, acc, sems,
                   *, sub_rows, n_sub):
    i = pl.program_id(0)
    j = pl.program_id(1)
    n_valid = nv_ref[i]
    base = i * (sub_rows * n_sub)

    @pl.when(j == 0)
    def _():
        for sb in range(n_sub):
            @pl.when(sb * sub_rows < n_valid)
            def _():
                def issue(r, carry):
                    _tile_gather_copy(h_hbm, xg, sems.at[sb], tok_ref[base + r], r).start()
                    return carry

                lax.fori_loop(sb * sub_rows, (sb + 1) * sub_rows, issue, 0, unroll=8)
        for sb in range(n_sub):
            @pl.when(sb * sub_rows < n_valid)
            def _():
                tiles = pl.ds(sb * sub_rows * SUBLANES, sub_rows * SUBLANES)
                pltpu.make_async_copy(h_hbm.at[tiles], xg.at[tiles], sems.at[sb]).wait()
                xb[sb * sub_rows:(sb + 1) * sub_rows, :] = _load_token_tiles(xg, sb * sub_rows, sub_rows).astype(BF16)

    for sb in range(n_sub):
        rows = slice(sb * sub_rows, (sb + 1) * sub_rows)

        @pl.when(sb * sub_rows < n_valid)
        def _():
            x = xb[rows, :]
            a = _dot(x, wg_ref[0])
            u = _dot(x, wu_ref[0])
            y = _dot((_silu(a) * u).astype(BF16), wd_ref[0])

            @pl.when(j == 0)
            def _():
                acc[rows, :] = y

            @pl.when(j > 0)
            def _():
                acc[rows, :] += y

    @pl.when(j == pl.num_programs(1) - 1)
    def _():
        for sb in range(n_sub):
            rows = slice(sb * sub_rows, (sb + 1) * sub_rows)

            @pl.when(sb * sub_rows < n_valid)
            def _():
                _store_token_tiles(yb_ref, sb * sub_rows, acc[rows, :])

            @pl.when(sb * sub_rows >= n_valid)
            def _():
                tiles = slice(sb * sub_rows * SUBLANES, (sb + 1) * sub_rows * SUBLANES)
                yb_ref[tiles, :] = jnp.zeros((sub_rows * SUBLANES, LANES), F32)


def _experts(h_tiles, block_e, buf_tok, n_valid, n_used, wg, wu, wd, *, sub_rows, n_sub, tf):
    d = D_MODEL
    tmm = sub_rows * n_sub
    nb = block_e.shape[0]
    nf = D_FF // tf

    def wcol(i, j, be, tok, nv, nu):
        return (be[i], 0, jnp.where(i < nu[0], j, nf - 1))

    def wrow(i, j, be, tok, nv, nu):
        return (be[i], jnp.where(i < nu[0], j, nf - 1), 0)

    grid_spec = pltpu.PrefetchScalarGridSpec(
        num_scalar_prefetch=4,
        grid=(nb, nf),
        in_specs=[pl.BlockSpec(memory_space=pl.ANY),
                  pl.BlockSpec((1, d, tf), wcol),
                  pl.BlockSpec((1, d, tf), wcol),
                  pl.BlockSpec((1, tf, d), wrow)],
        out_specs=pl.BlockSpec((tmm * SUBLANES, LANES), lambda i, j, be, tok, nv, nu: (i, 0)),
        scratch_shapes=[pltpu.VMEM((tmm * SUBLANES, LANES), F32), pltpu.VMEM((tmm, d), BF16),
                        pltpu.VMEM((tmm, d), F32), pltpu.SemaphoreType.DMA((n_sub,))],
    )
    return pl.pallas_call(
        functools.partial(_expert_kernel, sub_rows=sub_rows, n_sub=n_sub),
        grid_spec=grid_spec,
        out_shape=jax.ShapeDtypeStruct((nb * tmm * SUBLANES, LANES), F32),
        compiler_params=_params("arbitrary", "arbitrary"),
        name="moe_experts",
    )(block_e, buf_tok, n_valid, n_used, h_tiles, wg, wu, wd)


def _combine_kernel(dest_ref, yb_hbm, x_ref, gate_ref, gain_ref, o_ref, r0, r1, sem0, sem1, *, tm):
    base = pl.program_id(0) * tm

    def issue(r, carry):
        _tile_gather_copy(yb_hbm, r0, sem0, dest_ref[TOP_K * (base + r)], r).start()
        _tile_gather_copy(yb_hbm, r1, sem1, dest_ref[TOP_K * (base + r) + 1], r).start()
        return carry

    lax.fori_loop(0, tm, issue, 0, unroll=8)
    pltpu.make_async_copy(yb_hbm.at[pl.ds(0, tm * SUBLANES)], r0, sem0).wait()
    pltpu.make_async_copy(yb_hbm.at[pl.ds(0, tm * SUBLANES)], r1, sem1).wait()
    g = gate_ref[...]
    y = _load_token_tiles(r0, 0, tm) * g[:, 0:1] + _load_token_tiles(r1, 0, tm) * g[:, 1:2]
    o_ref[...] = _rms(x_ref[...] + y, gain_ref[...])


def _combine(yb, dest_flat, x, gates, gain, *, tm):
    n, d = x.shape
    grid_spec = pltpu.PrefetchScalarGridSpec(
        num_scalar_prefetch=1,
        grid=(n // tm,),
        in_specs=[pl.BlockSpec(memory_space=pl.ANY),
                  pl.BlockSpec((tm, d), lambda i, dest: (i, 0)),
                  pl.BlockSpec((tm, TOP_K), lambda i, dest: (i, 0)),
                  pl.BlockSpec((1, d), lambda i, dest: (0, 0))],
        out_specs=pl.BlockSpec((tm, d), lambda i, dest: (i, 0)),
        scratch_shapes=[pltpu.VMEM((tm * SUBLANES, LANES), F32), pltpu.VMEM((tm * SUBLANES, LANES), F32),
                        pltpu.SemaphoreType.DMA(()), pltpu.SemaphoreType.DMA(())],
    )
    return pl.pallas_call(
        functools.partial(_combine_kernel, tm=tm),
        grid_spec=grid_spec,
        out_shape=jax.ShapeDtypeStruct((n, d), F32),
        compiler_params=_params("arbitrary"),
        name="moe_combine",
    )(dest_flat, yb, x, gates, gain.reshape(1, d))


def _moe_and_final_norm(x, gain, w_router, wg, wu, wd, final_gain, *, tiles):
    n = x.shape[0]
    sub_rows, n_sub = tiles["moe_sub_rows"], tiles["moe_subs"]
    tmm = sub_rows * n_sub
    h, idx, gates, rank, counts = _router(x, gain, w_router, tm=tiles["router_rows"])
    counts = counts[0]
    pcounts = (counts + tmm - 1) // tmm * tmm
    pend = jnp.cumsum(pcounts)
    pstart = pend - pcounts
    dest = pstart[idx] + rank
    nb = -(-(n * TOP_K + N_EXPERTS * (tmm - 1)) // tmm)
    tok = jnp.broadcast_to(jnp.arange(n, dtype=jnp.int32)[:, None], (n, TOP_K))
    buf_tok = jnp.zeros((nb * tmm,), jnp.int32).at[dest.reshape(-1)].set(tok.reshape(-1))
    n_used = (pend[-1] // tmm).astype(jnp.int32)
    blk = jnp.arange(nb, dtype=jnp.int32)
    block_e = jnp.minimum(jnp.searchsorted(pend, jnp.minimum(blk, n_used - 1) * tmm, side="right"),
                          N_EXPERTS - 1).astype(jnp.int32)
    n_valid = jnp.where(blk < n_used, jnp.clip(counts[block_e] - (blk * tmm - pstart[block_e]), 0, tmm), 0)
    yb = _experts(h, block_e, buf_tok, n_valid.astype(jnp.int32), n_used.reshape(1), wg, wu, wd,
                  sub_rows=sub_rows, n_sub=n_sub, tf=tiles["moe_ff_cols"])
    return _combine(yb, dest.reshape(-1).astype(jnp.int32), x, gates, final_gain, tm=tiles["combine_rows"])


def _tiles(n_rows, seq):
    big = n_rows >= 8192
    return {
        "proj_rows": 1024 if n_rows % 1024 == 0 else n_rows,
        "proj_cols": 1024,
        "rec_chunks": min(4, seq // min(MAX_CHUNK, seq)),
        "rec_seqs": 1 if seq > MAX_CHUNK else 4,
        "ffn_rows": 1024 if n_rows % 1024 == 0 else n_rows,
        "ff_cols": 512,
        "router_rows": 512,
        "moe_sub_rows": 256,
        "moe_subs": 4 if big else 2,
        "moe_ff_cols": 1792 if big else 512,
        "combine_rows": 256,
    }


def _trunk(x3, hg_state, gd_state, gd_conv, p):
    batch, seq, d = x3.shape
    n = batch * seq
    x = x3.reshape(n, d)
    tiles = _tiles(n, seq)
    tm, tn = tiles["proj_rows"], tiles["proj_cols"]

    proj = _norm_proj(x, p["norm_mix"][0], p["hgrn_w_in"], tm=tm, tn=tn)[0]
    og, hg_new = _hgrn_recurrence(proj, p["hgrn_lb"], p["hgrn_norm"], hg_state, batch=batch, seq=seq, layer=0,
                                  n_chunks=tiles["rec_chunks"], n_seq=tiles["rec_seqs"])
    x = _out_proj(og, p["hgrn_w_out"], x, tm=tm)
    x = _ffn(x, p["norm_ffn"][0], p["ffn_w_gate"], p["ffn_w_up"], p["ffn_w_down"],
             tm=tiles["ffn_rows"], tf=tiles["ff_cols"])

    main, ab = _norm_proj(x, p["norm_mix"][1], p["gdn_w_main"], p["gdn_w_ab"], tm=tm, tn=tn)
    og, gd_new, cv_new = _gdn_recurrence(main, ab, p["gdn_conv"], p["gdn_a_log"], p["gdn_dt_bias"], p["gdn_norm"],
                                         gd_conv, gd_state, batch=batch, seq=seq, n_chunks=tiles["rec_chunks"],
                                         n_seq=tiles["rec_seqs"])
    x = _out_proj(og, p["gdn_w_out"], x, tm=tm)
    y = _moe_and_final_norm(x, p["norm_ffn"][1], p["moe_router"], p["moe_w_gate"], p["moe_w_up"], p["moe_w_down"],
                            p["norm_out"], tiles=tiles)
    return y.reshape(batch, seq, d), hg_new[None], gd_new[None], cv_new[None]


def kernel(x_prompt, x_sample, state_hgrn, state_gdn, state_gdn_conv, norm_mix, norm_ffn, norm_out, hgrn_w_in, hgrn_lb, hgrn_norm, hgrn_w_out, gdn_w_in, gdn_conv, gdn_a_log, gdn_dt_bias, gdn_norm, gdn_w_out, ffn_w_gate, ffn_w_up, ffn_w_down, moe_router, moe_w_gate, moe_w_up, moe_w_down):
    assert hgrn_w_in.shape[0] == 1 and gdn_w_in.shape[0] == 1, "one HGRN2 layer and one gated-DeltaNet layer"
    main_cols = GD_QKV + N_HEADS * HEAD_DIM
    p = {
        "norm_mix": norm_mix, "norm_ffn": norm_ffn, "norm_out": norm_out,
        "hgrn_w_in": hgrn_w_in[0].astype(BF16), "hgrn_lb": hgrn_lb, "hgrn_norm": hgrn_norm[0],
        "hgrn_w_out": hgrn_w_out[0].astype(BF16),
        "gdn_w_main": gdn_w_in[0, :, :main_cols].astype(BF16), "gdn_w_ab": gdn_w_in[0, :, main_cols:].astype(BF16),
        "gdn_conv": gdn_conv[0], "gdn_a_log": gdn_a_log[0], "gdn_dt_bias": gdn_dt_bias[0], "gdn_norm": gdn_norm[0],
        "gdn_w_out": gdn_w_out[0].astype(BF16),
        "ffn_w_gate": ffn_w_gate[0].astype(BF16), "ffn_w_up": ffn_w_up[0].astype(BF16),
        "ffn_w_down": ffn_w_down[0].astype(BF16),
        "moe_router": moe_router[0],
        "moe_w_gate": moe_w_gate[0].astype(BF16), "moe_w_up": moe_w_up[0].astype(BF16),
        "moe_w_down": moe_w_down[0].astype(BF16),
    }
    y_p, hg_p, gd_p, cv_p = _trunk(x_prompt, None, None, None, p)
    y_s, hg_s, gd_s, cv_s = _trunk(x_sample, state_hgrn[0], state_gdn[0], state_gdn_conv[0], p)
    return (y_p, y_s, hg_p, hg_s, gd_p, gd_s, cv_p, cv_s)
```

```python
import functools

import jax
import jax.numpy as jnp
from jax import lax
from jax.experimental import pallas as pl
from jax.experimental.pallas import tpu as pltpu

F32 = jnp.float32
BF16 = jnp.bfloat16
HIGHEST = lax.Precision.HIGHEST

D_MODEL = 1024
N_HEADS = 8
HEAD_DIM = 128
D_FF = 3584
N_EXPERTS = 8
TOP_K = 2
GD_QKV = 3 * D_MODEL
GD_CONV = 4
MAX_CHUNK = 64
EPS = 1e-6
LANES = 128
SUBLANES = 8
VMEM_LIMIT_BYTES = 52 * 1024 * 1024


def _params(*sem):
    return pltpu.CompilerParams(dimension_semantics=sem, vmem_limit_bytes=VMEM_LIMIT_BYTES)


def _rms(x, gain):
    return x * lax.rsqrt(jnp.mean(x * x, axis=-1, keepdims=True) + EPS) * gain


def _sigmoid(x):
    t = jnp.exp(-jnp.abs(x))
    r = 1.0 / (1.0 + t)
    return jnp.where(x >= 0, r, t * r)


def _silu(x):
    return x * (0.5 * jnp.tanh(0.5 * x) + 0.5)


def _softplus(x):
    return jnp.maximum(x, 0.0) + jnp.log1p(jnp.exp(-jnp.abs(x)))


def _dot(a, b):
    return jnp.dot(a, b, preferred_element_type=F32)


def _dot_nt(a, b):
    return lax.dot_general(a, b, (((1,), (1,)), ((), ())), preferred_element_type=F32)


def _dot_tn(a, b):
    return lax.dot_general(a, b, (((0,), (0,)), ((), ())), preferred_element_type=F32)


def _dot_hi(a, b):
    return jnp.dot(a, b, precision=HIGHEST, preferred_element_type=F32)


def _norm_proj_kernel(x_ref, g_ref, w_ref, *rest, has_small):
    if has_small:
        ws_ref, o_ref, os_ref, h_scr = rest
    else:
        o_ref, h_scr = rest

    @pl.when(pl.program_id(1) == 0)
    def _():
        hb = _rms(x_ref[...], g_ref[...]).astype(BF16)
        h_scr[...] = hb
        if has_small:
            os_ref[...] = _dot(hb, ws_ref[...])

    o_ref[...] = _dot(h_scr[...], w_ref[...])


def _norm_proj(x, gain, w, w_small=None, *, tm, tn):
    n, d = x.shape
    e = w.shape[1]
    has_small = w_small is not None
    in_specs = [pl.BlockSpec((tm, d), lambda i, j: (i, 0)),
                pl.BlockSpec((1, d), lambda i, j: (0, 0)),
                pl.BlockSpec((d, tn), lambda i, j: (0, j))]
    out_shape = [jax.ShapeDtypeStruct((n, e), F32)]
    out_specs = [pl.BlockSpec((tm, tn), lambda i, j: (i, j))]
    args = [x, gain.reshape(1, d), w]
    if has_small:
        es = w_small.shape[1]
        in_specs.append(pl.BlockSpec((d, es), lambda i, j: (0, 0)))
        out_shape.append(jax.ShapeDtypeStruct((n, es), F32))
        out_specs.append(pl.BlockSpec((tm, es), lambda i, j: (i, 0)))
        args.append(w_small)
    return pl.pallas_call(
        functools.partial(_norm_proj_kernel, has_small=has_small),
        grid=(n // tm, e // tn),
        in_specs=in_specs, out_specs=out_specs, out_shape=out_shape,
        scratch_shapes=[pltpu.VMEM((tm, d), BF16)],
        compiler_params=_params("parallel", "arbitrary"),
        name="norm_proj",
    )(*args)


def _hgrn_kernel(*refs, chunk, n_chunks, n_seq, layer, has_s0):
    if has_s0:
        proj_ref, lb_ref, on_ref, s0_ref, og_ref, sf_ref, *scr = refs
    else:
        proj_ref, lb_ref, on_ref, og_ref, sf_ref, *scr = refs
    st_scr, b_scr, q_scr, k_scr, a_scr, qin_scr, kout_scr, dec_scr, oi_scr = scr
    C = chunk
    W = N_HEADS * HEAD_DIM
    tb = C * n_chunks
    t = pl.program_id(1)
    problems = [(s, c, h) for s in range(n_seq) for c in range(n_chunks) for h in range(N_HEADS)]

    def pid(s, c, h):
        return (s * n_chunks + c) * N_HEADS + h

    def row0(s, c):
        return s * tb + c * C

    @pl.when(t == 0)
    def _():
        for s in range(n_seq):
            for h in range(N_HEADS):
                st_scr[s * N_HEADS + h] = s0_ref[s, h].T if has_s0 else jnp.zeros((HEAD_DIM, HEAD_DIM), F32)

    lbraw = lb_ref[...]
    ex = jnp.exp(lbraw - jnp.max(lbraw, axis=0, keepdims=True))
    sm = ex / jnp.sum(ex, axis=0, keepdims=True)
    lb = sm[0:1]
    for j in range(1, layer + 1):
        lb = lb + sm[j:j + 1]

    row = lax.broadcasted_iota(jnp.int32, (C, C), 0)
    col = lax.broadcasted_iota(jnp.int32, (C, C), 1)
    causal = row >= col
    tril = causal.astype(BF16)
    mid = C // 2
    scale = HEAD_DIM ** -0.5
    onorm = on_ref[...]

    for s in range(n_seq):
        for c in range(n_chunks):
            rows = slice(row0(s, c), row0(s, c) + C)
            f_raw = proj_ref[rows, W:2 * W]
            tt = jnp.exp(-jnp.abs(f_raw))
            rr = 1.0 / (1.0 + tt)
            pos = f_raw >= 0
            sig = jnp.where(pos, rr, tt * rr)
            sig_neg = jnp.where(pos, tt * rr, rr)
            log_f = jnp.log(lb + (1.0 - lb) * sig)
            k_scr[rows, :] = (1.0 - lb) * sig_neg
            q_scr[rows, :] = _silu(proj_ref[rows, 0:W]) * scale
            hi = log_f.astype(BF16)
            rest = log_f - hi.astype(F32)
            md = rest.astype(BF16)
            lw = (rest - md.astype(F32)).astype(BF16)
            b_scr[rows, :] = (_dot(tril, lw) + _dot(tril, md)) + _dot(tril, hi)

    for s, c, h in problems:
        p, lo = pid(s, c, h), h * HEAD_DIM
        rows = slice(row0(s, c), row0(s, c) + C)
        b = b_scr[rows, lo:lo + HEAD_DIM]
        q = q_scr[rows, lo:lo + HEAD_DIM]
        k = k_scr[rows, lo:lo + HEAD_DIM]
        b_mid = b[mid:mid + 1]
        b_last = b[C - 1:C]
        a_scr[p] = _dot_nt((q * jnp.exp(b - b_mid)).astype(BF16), (k * jnp.exp(b_mid - b)).astype(BF16))
        qin_scr[p] = q * jnp.exp(b)
        kout_scr[p] = k * jnp.exp(b_last - b)
        dec_scr[p] = jnp.exp(b_last)

    for s, c, h in problems:
        p, lo = pid(s, c, h), h * HEAD_DIM
        rows = slice(row0(s, c), row0(s, c) + C)
        a = jnp.where(causal, a_scr[p], 0.0)
        oi_scr[p] = _dot(a.astype(BF16), proj_ref[rows, 2 * W + lo:2 * W + lo + HEAD_DIM].astype(BF16))

    for c in range(n_chunks):
        for s in range(n_seq):
            for h in range(N_HEADS):
                p, lo, sh = pid(s, c, h), h * HEAD_DIM, s * N_HEADS + h
                rows = slice(row0(s, c), row0(s, c) + C)
                st = st_scr[sh]
                o = oi_scr[p] + _dot_nt(qin_scr[p].astype(BF16), st.astype(BF16))
                vb = proj_ref[rows, 2 * W + lo:2 * W + lo + HEAD_DIM].astype(BF16)
                st_scr[sh] = st * dec_scr[p] + _dot_tn(vb, kout_scr[p].astype(BF16))
                gate = proj_ref[rows, 3 * W + lo:3 * W + lo + HEAD_DIM]
                og_ref[rows, lo:lo + HEAD_DIM] = _rms(o, onorm) * _silu(gate)

    @pl.when(t == pl.num_programs(1) - 1)
    def _():
        for s in range(n_seq):
            for h in range(N_HEADS):
                sf_ref[s, h] = st_scr[s * N_HEADS + h].T


def _hgrn_recurrence(proj, lb_raw, out_norm, s0, *, batch, seq, layer, n_chunks, n_seq):
    n = proj.shape[0]
    chunk = min(MAX_CHUNK, seq)
    tb = chunk * n_chunks
    assert seq % tb == 0 and batch % n_seq == 0
    assert n_seq == 1 or tb == seq, "several sequences per step only when a step covers whole sequences"
    nt = seq // tb
    rows = n_seq * tb
    n_prob = n_seq * n_chunks * N_HEADS
    width = N_HEADS * HEAD_DIM
    has_s0 = s0 is not None
    state_spec = pl.BlockSpec((n_seq, N_HEADS, HEAD_DIM, HEAD_DIM), lambda b, t: (b, 0, 0, 0))
    in_specs = [pl.BlockSpec((rows, proj.shape[1]), lambda b, t: (b * nt + t, 0)),
                pl.BlockSpec(lb_raw.shape, lambda b, t: (0, 0)),
                pl.BlockSpec((1, HEAD_DIM), lambda b, t: (0, 0))]
    args = [proj, lb_raw, out_norm.reshape(1, HEAD_DIM)]
    if has_s0:
        in_specs.append(state_spec)
        args.append(s0)
    vm = lambda *shape: pltpu.VMEM(shape, F32)
    return pl.pallas_call(
        functools.partial(_hgrn_kernel, chunk=chunk, n_chunks=n_chunks, n_seq=n_seq, layer=layer, has_s0=has_s0),
        grid=(batch // n_seq, nt),
        in_specs=in_specs,
        out_specs=[pl.BlockSpec((rows, D_MODEL), lambda b, t: (b * nt + t, 0)), state_spec],
        out_shape=[jax.ShapeDtypeStruct((n, D_MODEL), F32),
                   jax.ShapeDtypeStruct((batch, N_HEADS, HEAD_DIM, HEAD_DIM), F32)],
        scratch_shapes=[vm(n_seq * N_HEADS, HEAD_DIM, HEAD_DIM),
                        vm(rows, width),
                        vm(rows, width),
                        vm(rows, width),
                        vm(n_prob, chunk, chunk),
                        vm(n_prob, chunk, HEAD_DIM),
                        vm(n_prob, chunk, HEAD_DIM),
                        vm(n_prob, 1, HEAD_DIM),
                        vm(n_prob, chunk, HEAD_DIM)],
        compiler_params=_params("parallel", "arbitrary"),
        name="hgrn_recurrence",
    )(*args)


def _split_bf16(x):
    hi = x.astype(BF16)
    return hi, (x - hi.astype(F32)).astype(BF16)


def _gdn_kernel(*refs, chunk, n_chunks, n_seq, has_s0):
    (main_ref, ab_ref, abt_ref, cw_ref, al_ref, dtb_ref, alt_ref, dtbt_ref, on_ref), rest = refs[:9], refs[9:]
    if has_s0:
        (cb0_ref, s0_ref), rest = rest[:2], rest[2:]
    (og_ref, sf_ref, cb_ref, s_scr, xpad, act, gcc_scr, gcr_scr, beta_scr, gram_scr, rhs_scr, wq_scr, kout_scr,
     inv_scr, pow_scr, low_scr, attn_scr, x0_scr, u_scr, ws_scr) = rest
    C = chunk
    W = N_HEADS * HEAD_DIM
    tb = C * n_chunks
    t = pl.program_id(1)
    pad = SUBLANES
    problems = [(s, c, h) for s in range(n_seq) for c in range(n_chunks) for h in range(N_HEADS)]

    def pid(s, c, h):
        return (s * n_chunks + c) * N_HEADS + h

    def row0(s, c):
        return s * tb + c * C

    @pl.when(t == 0)
    def _():
        for s in range(n_seq):
            for h in range(N_HEADS):
                s_scr[s * N_HEADS + h] = s0_ref[s, h] if has_s0 else jnp.zeros((HEAD_DIM, HEAD_DIM), F32)
            xpad[s, 0:pad, :] = jnp.zeros((pad, GD_QKV), F32)
            if has_s0:
                xpad[s, pad - (GD_CONV - 1):pad, :] = cb0_ref[s]

    @pl.when(t > 0)
    def _():
        for s in range(n_seq):
            xpad[s, 0:pad, :] = xpad[s, tb:tb + pad, :]

    for s in range(n_seq):
        for j in range(GD_QKV // LANES):
            cs = slice(j * LANES, (j + 1) * LANES)
            xpad[s, pad:pad + tb, cs] = main_ref[s * tb:(s + 1) * tb, cs]
    for s in range(n_seq):
        for j in range(GD_QKV // LANES):
            cs = slice(j * LANES, (j + 1) * LANES)
            conv = xpad[s, pad - 3:pad - 3 + tb, cs] * cw_ref[0:1, cs]
            for tap in range(1, GD_CONV):
                conv = conv + xpad[s, pad - 3 + tap:pad - 3 + tap + tb, cs] * cw_ref[tap:tap + 1, cs]
            act[s * tb:(s + 1) * tb, cs] = _silu(conv)

    @pl.when(t == pl.num_programs(1) - 1)
    def _():
        for s in range(n_seq):
            cb_ref[s] = xpad[s, pad + tb - (GD_CONV - 1):pad + tb, :]

    row = lax.broadcasted_iota(jnp.int32, (C, C), 0)
    col = lax.broadcasted_iota(jnp.int32, (C, C), 1)
    incl = row >= col
    strict = row > col
    eye = jnp.where(row == col, 1.0, 0.0)
    tril = incl.astype(F32)
    triu = (row <= col).astype(F32)
    ab = ab_ref[...]
    g_col = -jnp.exp(al_ref[...]) * _softplus(ab[:, 0:N_HEADS] + dtb_ref[...])
    beta_scr[...] = _sigmoid(ab[:, N_HEADS:2 * N_HEADS])
    for s in range(n_seq):
        g_row = -jnp.exp(alt_ref[...]) * _softplus(abt_ref[s][0:N_HEADS, :] + dtbt_ref[...])
        for c in range(n_chunks):
            r = row0(s, c)
            gcc_scr[r:r + C, :] = _dot_hi(tril, g_col[r:r + C, :])
            gcr_scr[s * n_chunks + c] = _dot_hi(g_row[:, c * C:(c + 1) * C], triu)

    scale = HEAD_DIM ** -0.5
    onorm = on_ref[...]

    for s, c, h in problems:
        p, r, lo = pid(s, c, h), row0(s, c), h * HEAD_DIM
        rows = slice(r, r + C)
        q = act[rows, lo:lo + HEAD_DIM]
        k = act[rows, W + lo:W + lo + HEAD_DIM]
        v = act[rows, 2 * W + lo:2 * W + lo + HEAD_DIM]
        q = q * lax.rsqrt(jnp.sum(q * q, axis=-1, keepdims=True) + EPS) * scale
        k = k * lax.rsqrt(jnp.sum(k * k, axis=-1, keepdims=True) + EPS)
        gc_c = gcc_scr[rows, h:h + 1]
        beta_c = beta_scr[rows, h:h + 1]
        k_beta = k * beta_c
        gram_scr[p] = _dot_nt(jnp.concatenate([k_beta, q], axis=0).astype(BF16), k.astype(BF16))
        e_gc = jnp.exp(gc_c)
        rhs_scr[p] = jnp.concatenate([v * beta_c, k_beta * e_gc], axis=1)
        wq_scr[p, C:2 * C, :] = q * e_gc
        kout_scr[p] = k * jnp.exp(gc_c[C - 1:C, :] - gc_c)

    for s, c, h in problems:
        p, r = pid(s, c, h), row0(s, c)
        gc_c = gcc_scr[r:r + C, h:h + 1]
        gc_r = gcr_scr[s * n_chunks + c, h:h + 1, :]
        decay = jnp.exp(jnp.where(incl, gc_c - gc_r, -jnp.inf))
        gram = gram_scr[p]
        lower = jnp.where(strict, gram[:C] * decay, 0.0)
        low_scr[p] = lower
        attn_scr[p] = gram[C:] * decay
        inv_scr[p] = eye - lower
        mb = (-lower).astype(BF16)
        pow_scr[p] = _dot(mb, mb)

    power = 2
    while power < C:
        last = 2 * power >= C
        for s, c, h in problems:
            p = pid(s, c, h)
            pw = pow_scr[p]
            inv = inv_scr[p]
            pb = pw.astype(BF16)
            if last:
                inv_scr[p] = inv + _dot(inv.astype(BF16), pb)
            else:
                both = _dot(jnp.concatenate([inv, pw], axis=0).astype(BF16), pb)
                inv_scr[p] = inv + both[:C]
                pow_scr[p] = both[C:]
        power *= 2

    for s, c, h in problems:
        p = pid(s, c, h)
        x0_scr[p] = _dot(inv_scr[p].astype(BF16), rhs_scr[p].astype(BF16))
    for s, c, h in problems:
        p = pid(s, c, h)
        x0 = x0_scr[p]
        x_hi, x_lo = _split_bf16(x0)
        l_hi, l_lo = _split_bf16(low_scr[p])
        lx = _dot(l_hi, jnp.concatenate([x_hi, x_lo], axis=1))
        lx = lx[:, :2 * HEAD_DIM] + (lx[:, 2 * HEAD_DIM:] + _dot(l_lo, x_hi))
        rhs_scr[p] = rhs_scr[p] - x0 - lx
    for s, c, h in problems:
        p = pid(s, c, h)
        sol = x0_scr[p] + _dot(inv_scr[p].astype(BF16), rhs_scr[p].astype(BF16))
        u_scr[p] = sol[:, :HEAD_DIM]
        wq_scr[p, 0:C, :] = sol[:, HEAD_DIM:]

    for c in range(n_chunks):
        for s in range(n_seq):
            for h in range(N_HEADS):
                sh = s * N_HEADS + h
                ws_scr[sh] = _dot(wq_scr[pid(s, c, h)].astype(BF16), s_scr[sh].astype(BF16))
        for s in range(n_seq):
            for h in range(N_HEADS):
                p, r, lo, sh = pid(s, c, h), row0(s, c), h * HEAD_DIM, s * N_HEADS + h
                rows = slice(r, r + C)
                ws = ws_scr[sh]
                vn16 = (u_scr[p] - ws[:C]).astype(BF16)
                o = ws[C:] + _dot(attn_scr[p].astype(BF16), vn16)
                g_last = gcc_scr[r + C - 1:r + C, h:h + 1]
                s_scr[sh] = s_scr[sh] * jnp.exp(g_last) + _dot_tn(kout_scr[p].astype(BF16), vn16)
                z = main_ref[rows, 3 * W + lo:3 * W + lo + HEAD_DIM]
                og_ref[rows, lo:lo + HEAD_DIM] = _rms(o, onorm) * _silu(z)

    @pl.when(t == pl.num_programs(1) - 1)
    def _():
        for s in range(n_seq):
            for h in range(N_HEADS):
                sf_ref[s, h] = s_scr[s * N_HEADS + h]


def _gdn_recurrence(main, ab, conv_w, a_log, dt_bias, out_norm, cb0, s0, *, batch, seq, n_chunks, n_seq):
    n = main.shape[0]
    chunk = min(MAX_CHUNK, seq)
    tb = chunk * n_chunks
    assert seq % tb == 0 and batch % n_seq == 0 and seq >= GD_CONV - 1
    assert n_seq == 1 or tb == seq, "several sequences per step only when a step covers whole sequences"
    nt = seq // tb
    rows = n_seq * tb
    n_prob = n_seq * n_chunks * N_HEADS
    has_s0 = s0 is not None
    abt = ab.reshape(batch, seq, 2 * N_HEADS).transpose(0, 2, 1)
    state_spec = pl.BlockSpec((n_seq, N_HEADS, HEAD_DIM, HEAD_DIM), lambda b, t: (b, 0, 0, 0))
    cb_spec = pl.BlockSpec((n_seq, GD_CONV - 1, GD_QKV), lambda b, t: (b, 0, 0))
    small = lambda shape: pl.BlockSpec(shape, lambda b, t: (0,) * len(shape))
    in_specs = [pl.BlockSpec((rows, main.shape[1]), lambda b, t: (b * nt + t, 0)),
                pl.BlockSpec((rows, 2 * N_HEADS), lambda b, t: (b * nt + t, 0)),
                pl.BlockSpec((n_seq, 2 * N_HEADS, tb), lambda b, t: (b, 0, t)),
                small((GD_CONV, GD_QKV)), small((1, N_HEADS)), small((1, N_HEADS)),
                small((N_HEADS, 1)), small((N_HEADS, 1)), small((1, HEAD_DIM))]
    args = [main, ab, abt, conv_w, a_log.reshape(1, N_HEADS), dt_bias.reshape(1, N_HEADS),
            a_log.reshape(N_HEADS, 1), dt_bias.reshape(N_HEADS, 1), out_norm.reshape(1, HEAD_DIM)]
    if has_s0:
        in_specs += [cb_spec, state_spec]
        args += [cb0, s0]
    vm = lambda *shape: pltpu.VMEM(shape, F32)
    return pl.pallas_call(
        functools.partial(_gdn_kernel, chunk=chunk, n_chunks=n_chunks, n_seq=n_seq, has_s0=has_s0),
        grid=(batch // n_seq, nt),
        in_specs=in_specs,
        out_specs=[pl.BlockSpec((rows, D_MODEL), lambda b, t: (b * nt + t, 0)), state_spec, cb_spec],
        out_shape=[jax.ShapeDtypeStruct((n, D_MODEL), F32),
                   jax.ShapeDtypeStruct((batch, N_HEADS, HEAD_DIM, HEAD_DIM), F32),
                   jax.ShapeDtypeStruct((batch, GD_CONV - 1, GD_QKV), F32)],
        scratch_shapes=[vm(n_seq * N_HEADS, HEAD_DIM, HEAD_DIM),
                        vm(n_seq, tb + SUBLANES, GD_QKV),
                        vm(rows, GD_QKV),
                        vm(rows, N_HEADS),
                        vm(n_seq * n_chunks, N_HEADS, chunk),
                        vm(rows, N_HEADS),
                        vm(n_prob, 2 * chunk, chunk),
                        vm(n_prob, chunk, 2 * HEAD_DIM),
                        vm(n_prob, 2 * chunk, HEAD_DIM),
                        vm(n_prob, chunk, HEAD_DIM),
                        vm(n_prob, chunk, chunk),
                        vm(n_prob, chunk, chunk),
                        vm(n_prob, chunk, chunk),
                        vm(n_prob, chunk, chunk),
                        vm(n_prob, chunk, 2 * HEAD_DIM),
                        vm(n_prob, chunk, HEAD_DIM),
                        vm(n_seq * N_HEADS, 2 * chunk, HEAD_DIM)],
        compiler_params=_params("parallel", "arbitrary"),
        name="gdn_recurrence",
    )(*args)


def _out_proj_kernel(og_ref, w_ref, x_ref, o_ref):
    o_ref[...] = x_ref[...] + _dot(og_ref[...].astype(BF16), w_ref[...])


def _out_proj(og, w, x, *, tm):
    n, d = x.shape
    return pl.pallas_call(
        _out_proj_kernel,
        grid=(n // tm,),
        in_specs=[pl.BlockSpec((tm, d), lambda i: (i, 0)),
                  pl.BlockSpec((d, d), lambda i: (0, 0)),
                  pl.BlockSpec((tm, d), lambda i: (i, 0))],
        out_specs=pl.BlockSpec((tm, d), lambda i: (i, 0)),
        out_shape=jax.ShapeDtypeStruct((n, d), F32),
        compiler_params=_params("parallel"),
        name="out_proj",
    )(og, w, x)


def _ffn_kernel(x_ref, g_ref, wg_ref, wu_ref, wd_ref, o_ref, h_scr):
    j = pl.program_id(1)

    @pl.when(j == 0)
    def _():
        x = x_ref[...]
        h_scr[...] = _rms(x, g_ref[...]).astype(BF16)
        o_ref[...] = x

    h = h_scr[...]
    a = _dot(h, wg_ref[...])
    u = _dot(h, wu_ref[...])
    o_ref[...] += _dot((_silu(a) * u).astype(BF16), wd_ref[...])


def _ffn(x, gain, wg, wu, wd, *, tm, tf):
    n, d = x.shape
    f = wg.shape[1]
    return pl.pallas_call(
        _ffn_kernel,
        grid=(n // tm, f // tf),
        in_specs=[pl.BlockSpec((tm, d), lambda i, j: (i, 0)),
                  pl.BlockSpec((1, d), lambda i, j: (0, 0)),
                  pl.BlockSpec((d, tf), lambda i, j: (0, j)),
                  pl.BlockSpec((d, tf), lambda i, j: (0, j)),
                  pl.BlockSpec((tf, d), lambda i, j: (j, 0))],
        out_specs=pl.BlockSpec((tm, d), lambda i, j: (i, 0)),
        out_shape=jax.ShapeDtypeStruct((n, d), F32),
        scratch_shapes=[pltpu.VMEM((tm, d), BF16)],
        compiler_params=_params("parallel", "arbitrary"),
        name="dense_ffn",
    )(x, gain.reshape(1, d), wg, wu, wd)


def _router_kernel(x_ref, g_ref, wr_ref, h_ref, idx_ref, gate_ref, rank_ref, cnt_ref, cnt_scr, *, tm):
    i = pl.program_id(0)

    @pl.when(i == 0)
    def _():
        cnt_scr[...] = jnp.zeros_like(cnt_scr)

    h = _rms(x_ref[...], g_ref[...])
    _store_token_tiles(h_ref, 0, h)
    logits = _dot_hi(h, wr_ref[...])
    lane = lax.broadcasted_iota(jnp.int32, logits.shape, 1)
    m1 = jnp.max(logits, axis=-1, keepdims=True)
    i1 = jnp.min(jnp.where(logits == m1, lane, N_EXPERTS), axis=-1, keepdims=True)
    rest = jnp.where(lane == i1, -jnp.inf, logits)
    m2 = jnp.max(rest, axis=-1, keepdims=True)
    i2 = jnp.min(jnp.where(rest == m2, lane, N_EXPERTS), axis=-1, keepdims=True)
    e2 = jnp.exp(m2 - m1)
    den = 1.0 + e2
    two = lax.broadcasted_iota(jnp.int32, (tm, TOP_K), 1)
    idx_ref[...] = jnp.where(two == 0, i1, i2)
    gate_ref[...] = jnp.where(two == 0, 1.0 / den, e2 / den)

    onehot = ((lane == i1) | (lane == i2)).astype(F32)
    row = lax.broadcasted_iota(jnp.int32, (tm, tm), 0)
    col = lax.broadcasted_iota(jnp.int32, (tm, tm), 1)
    before = (row > col).astype(BF16)
    excl = _dot(before, onehot.astype(BF16)) + cnt_scr[...]
    r1 = jnp.sum(jnp.where(lane == i1, excl, 0.0), axis=-1, keepdims=True)
    r2 = jnp.sum(jnp.where(lane == i2, excl, 0.0), axis=-1, keepdims=True)
    rank_ref[...] = jnp.where(two == 0, r1, r2).astype(jnp.int32)
    cnt = cnt_scr[...] + jnp.sum(onehot, axis=0, keepdims=True)
    cnt_scr[...] = cnt
    cnt_ref[...] = cnt.astype(jnp.int32)


def _router(x, gain, w_router, *, tm):
    n, d = x.shape
    two = pl.BlockSpec((tm, TOP_K), lambda i: (i, 0))
    return pl.pallas_call(
        functools.partial(_router_kernel, tm=tm),
        grid=(n // tm,),
        in_specs=[pl.BlockSpec((tm, d), lambda i: (i, 0)),
                  pl.BlockSpec((1, d), lambda i: (0, 0)),
                  pl.BlockSpec((d, N_EXPERTS), lambda i: (0, 0))],
        out_specs=[pl.BlockSpec((tm * SUBLANES, LANES), lambda i: (i, 0)), two, two, two,
                   pl.BlockSpec((1, N_EXPERTS), lambda i: (0, 0))],
        out_shape=[jax.ShapeDtypeStruct((n * SUBLANES, LANES), F32),
                   jax.ShapeDtypeStruct((n, TOP_K), jnp.int32),
                   jax.ShapeDtypeStruct((n, TOP_K), F32),
                   jax.ShapeDtypeStruct((n, TOP_K), jnp.int32),
                   jax.ShapeDtypeStruct((1, N_EXPERTS), jnp.int32)],
        scratch_shapes=[pltpu.VMEM((1, N_EXPERTS), F32)],
        compiler_params=_params("arbitrary"),
        name="moe_router",
    )(x, gain.reshape(1, d), w_router)


def _tile_gather_copy(src_hbm, dst, sem, src_row, dst_row):
    return pltpu.make_async_copy(src_hbm.at[pl.ds(pl.multiple_of(src_row * SUBLANES, SUBLANES), SUBLANES)],
                                 dst.at[pl.ds(pl.multiple_of(dst_row * SUBLANES, SUBLANES), SUBLANES)], sem)


def _load_token_tiles(ref, first_row, n_rows):
    return jnp.concatenate([ref[pl.ds(first_row * SUBLANES + s, n_rows, stride=SUBLANES), :]
                            for s in range(D_MODEL // LANES)], axis=1)


def _store_token_tiles(ref, first_row, x):
    for s in range(D_MODEL // LANES):
        ref[pl.ds(first_row * SUBLANES + s, x.shape[0], stride=SUBLANES), :] = x[:, s * LANES:(s + 1) * LANES]


def _expert_kernel(be_ref, tok_ref, nv_ref, nu_ref, h_hbm, wg_ref, wu_ref, wd_ref, yb_ref, xg, xb, acc, sems,
                   *, sub_rows, n_sub):
    i = pl.program_id(0)
    j = pl.program_id(1)
    tmm = sub_rows * n_sub
    n_valid = nv_ref[i]
    slot = lax.rem(i, 2)

    def start_gather(block, dst_slot):
        nv = nv_ref[block]
        for sb in range(n_sub):
            @pl.when(sb * sub_rows < nv)
            def _():
                def issue(r2, carry):
                    for k in range(2):
                        r = 2 * r2 + k
                        _tile_gather_copy(h_hbm, xg.at[dst_slot], sems.at[dst_slot, sb], tok_ref[block * tmm + r],
                                          r).start(priority=k)
                    return carry

                lax.fori_loop(sb * sub_rows // 2, (sb + 1) * sub_rows // 2, issue, 0, unroll=4)

    @pl.when(jnp.logical_and(i == 0, j == 0))
    def _():
        start_gather(0, 0)

    @pl.when(j == 0)
    def _():
        for sb in range(n_sub):
            @pl.when(sb * sub_rows < n_valid)
            def _():
                tiles = pl.ds(sb * sub_rows * SUBLANES, sub_rows * SUBLANES)
                pltpu.make_async_copy(h_hbm.at[tiles], xg.at[slot, tiles], sems.at[slot, sb]).wait()
                xb[sb * sub_rows:(sb + 1) * sub_rows, :] = _load_token_tiles(
                    xg.at[slot], sb * sub_rows, sub_rows).astype(BF16)

        @pl.when(i + 1 < pl.num_programs(0))
        def _():
            start_gather(i + 1, 1 - slot)

    for sb in range(n_sub):
        rows = slice(sb * sub_rows, (sb + 1) * sub_rows)

        @pl.when(sb * sub_rows < n_valid)
        def _():
            x = xb[rows, :]
            a = _dot(x, wg_ref[0])
            u = _dot(x, wu_ref[0])
            y = _dot((_silu(a) * u).astype(BF16), wd_ref[0])

            @pl.when(j == 0)
            def _():
                acc[rows, :] = y

            @pl.when(j > 0)
            def _():
                acc[rows, :] += y

    @pl.when(j == pl.num_programs(1) - 1)
    def _():
        for sb in range(n_sub):
            rows = slice(sb * sub_rows, (sb + 1) * sub_rows)

            @pl.when(sb * sub_rows < n_valid)
            def _():
                _store_token_tiles(yb_ref, sb * sub_rows, acc[rows, :])

            @pl.when(sb * sub_rows >= n_valid)
            def _():
                tiles = slice(sb * sub_rows * SUBLANES, (sb + 1) * sub_rows * SUBLANES)
                yb_ref[tiles, :] = jnp.zeros((sub_rows * SUBLANES, LANES), F32)


def _experts(h_tiles, block_e, buf_tok, n_valid, n_used, wg, wu, wd, *, sub_rows, n_sub, tf):
    d = D_MODEL
    tmm = sub_rows * n_sub
    nb = block_e.shape[0]
    nf = D_FF // tf

    def wcol(i, j, be, tok, nv, nu):
        return (be[i], 0, jnp.where(i < nu[0], j, nf - 1))

    def wrow(i, j, be, tok, nv, nu):
        return (be[i], jnp.where(i < nu[0], j, nf - 1), 0)

    grid_spec = pltpu.PrefetchScalarGridSpec(
        num_scalar_prefetch=4,
        grid=(nb, nf),
        in_specs=[pl.BlockSpec(memory_space=pl.ANY),
                  pl.BlockSpec((1, d, tf), wcol),
                  pl.BlockSpec((1, d, tf), wcol),
                  pl.BlockSpec((1, tf, d), wrow)],
        out_specs=pl.BlockSpec((tmm * SUBLANES, LANES), lambda i, j, be, tok, nv, nu: (i, 0)),
        scratch_shapes=[pltpu.VMEM((2, tmm * SUBLANES, LANES), F32), pltpu.VMEM((tmm, d), BF16),
                        pltpu.VMEM((tmm, d), F32), pltpu.SemaphoreType.DMA((2, n_sub))],
    )
    return pl.pallas_call(
        functools.partial(_expert_kernel, sub_rows=sub_rows, n_sub=n_sub),
        grid_spec=grid_spec,
        out_shape=jax.ShapeDtypeStruct((nb * tmm * SUBLANES, LANES), F32),
        compiler_params=_params("arbitrary", "arbitrary"),
        name="moe_experts",
    )(block_e, buf_tok, n_valid, n_used, h_tiles, wg, wu, wd)


def _combine_kernel(dest_ref, yb_hbm, x_ref, gate_ref, gain_ref, o_ref, r0, r1, sems, *, tm):
    i = pl.program_id(0)
    slot = lax.rem(i, 2)

    def start_gather(tile, dst_slot):
        def issue(r, carry):
            first = TOP_K * (tile * tm + r)
            _tile_gather_copy(yb_hbm, r0.at[dst_slot], sems.at[dst_slot, 0], dest_ref[first], r).start(priority=0)
            _tile_gather_copy(yb_hbm, r1.at[dst_slot], sems.at[dst_slot, 1], dest_ref[first + 1], r).start(priority=1)
            return carry

        lax.fori_loop(0, tm, issue, 0, unroll=8)

    @pl.when(i == 0)
    def _():
        start_gather(0, 0)

    @pl.when(i + 1 < pl.num_programs(0))
    def _():
        start_gather(i + 1, 1 - slot)

    whole = pl.ds(0, tm * SUBLANES)
    pltpu.make_async_copy(yb_hbm.at[whole], r0.at[slot], sems.at[slot, 0]).wait()
    pltpu.make_async_copy(yb_hbm.at[whole], r1.at[slot], sems.at[slot, 1]).wait()
    g = gate_ref[...]
    y = _load_token_tiles(r0.at[slot], 0, tm) * g[:, 0:1] + _load_token_tiles(r1.at[slot], 0, tm) * g[:, 1:2]
    o_ref[...] = _rms(x_ref[...] + y, gain_ref[...])


def _combine(yb, dest_flat, x, gates, gain, *, tm):
    n, d = x.shape
    grid_spec = pltpu.PrefetchScalarGridSpec(
        num_scalar_prefetch=1,
        grid=(n // tm,),
        in_specs=[pl.BlockSpec(memory_space=pl.ANY),
                  pl.BlockSpec((tm, d), lambda i, dest: (i, 0)),
                  pl.BlockSpec((tm, TOP_K), lambda i, dest: (i, 0)),
                  pl.BlockSpec((1, d), lambda i, dest: (0, 0))],
        out_specs=pl.BlockSpec((tm, d), lambda i, dest: (i, 0)),
        scratch_shapes=[pltpu.VMEM((2, tm * SUBLANES, LANES), F32), pltpu.VMEM((2, tm * SUBLANES, LANES), F32),
                        pltpu.SemaphoreType.DMA((2, TOP_K))],
    )
    return pl.pallas_call(
        functools.partial(_combine_kernel, tm=tm),
        grid_spec=grid_spec,
        out_shape=jax.ShapeDtypeStruct((n, d), F32),
        compiler_params=_params("arbitrary"),
        name="moe_combine",
    )(dest_flat, yb, x, gates, gain.reshape(1, d))


def _moe_and_final_norm(x, gain, w_router, wg, wu, wd, final_gain, *, tiles):
    n = x.shape[0]
    sub_rows, n_sub = tiles["moe_sub_rows"], tiles["moe_subs"]
    tmm = sub_rows * n_sub
    h, idx, gates, rank, counts = _router(x, gain, w_router, tm=tiles["router_rows"])
    counts = counts[0]
    pcounts = (counts + tmm - 1) // tmm * tmm
    pend = jnp.cumsum(pcounts)
    pstart = pend - pcounts
    dest = pstart[idx] + rank
    nb = -(-(n * TOP_K + N_EXPERTS * (tmm - 1)) // tmm)
    tok = jnp.broadcast_to(jnp.arange(n, dtype=jnp.int32)[:, None], (n, TOP_K))
    buf_tok = jnp.zeros((nb * tmm,), jnp.int32).at[dest.reshape(-1)].set(tok.reshape(-1), unique_indices=True)
    n_used = (pend[-1] // tmm).astype(jnp.int32)
    blk = jnp.arange(nb, dtype=jnp.int32)
    block_e = jnp.minimum(jnp.searchsorted(pend, jnp.minimum(blk, n_used - 1) * tmm, side="right"),
                          N_EXPERTS - 1).astype(jnp.int32)
    n_valid = jnp.where(blk < n_used, jnp.clip(counts[block_e] - (blk * tmm - pstart[block_e]), 0, tmm), 0)
    yb = _experts(h, block_e, buf_tok, n_valid.astype(jnp.int32), n_used.reshape(1), wg, wu, wd,
                  sub_rows=sub_rows, n_sub=n_sub, tf=tiles["moe_ff_cols"])
    return _combine(yb, dest.reshape(-1).astype(jnp.int32), x, gates, final_gain, tm=tiles["combine_rows"])


def _tiles(n_rows, seq):
    big = n_rows >= 8192
    return {
        "proj_rows": 1024 if n_rows % 1024 == 0 else n_rows,
        "proj_cols": 1024,
        "rec_chunks": min(4, seq // min(MAX_CHUNK, seq)),
        "rec_seqs": 1 if seq > MAX_CHUNK else 4,
        "ffn_rows": 1024 if n_rows % 1024 == 0 else n_rows,
        "ff_cols": 512,
        "router_rows": 512,
        "moe_sub_rows": 256,
        "moe_subs": 4 if big else 2,
        "moe_ff_cols": 1792 if big else 512,
        "combine_rows": 256,
    }


def _trunk(x3, hg_state, gd_state, gd_conv, p):
    batch, seq, d = x3.shape
    n = batch * seq
    x = x3.reshape(n, d)
    tiles = _tiles(n, seq)
    tm, tn = tiles["proj_rows"], tiles["proj_cols"]

    proj = _norm_proj(x, p["norm_mix"][0], p["hgrn_w_in"], tm=tm, tn=tn)[0]
    og, hg_new = _hgrn_recurrence(proj, p["hgrn_lb"], p["hgrn_norm"], hg_state, batch=batch, seq=seq, layer=0,
                                  n_chunks=tiles["rec_chunks"], n_seq=tiles["rec_seqs"])
    x = _out_proj(og, p["hgrn_w_out"], x, tm=tm)
    x = _ffn(x, p["norm_ffn"][0], p["ffn_w_gate"], p["ffn_w_up"], p["ffn_w_down"],
             tm=tiles["ffn_rows"], tf=tiles["ff_cols"])

    main, ab = _norm_proj(x, p["norm_mix"][1], p["gdn_w_main"], p["gdn_w_ab"], tm=tm, tn=tn)
    og, gd_new, cv_new = _gdn_recurrence(main, ab, p["gdn_conv"], p["gdn_a_log"], p["gdn_dt_bias"], p["gdn_norm"],
                                         gd_conv, gd_state, batch=batch, seq=seq, n_chunks=tiles["rec_chunks"],
                                         n_seq=tiles["rec_seqs"])
    x = _out_proj(og, p["gdn_w_out"], x, tm=tm)
    y = _moe_and_final_norm(x, p["norm_ffn"][1], p["moe_router"], p["moe_w_gate"], p["moe_w_up"], p["moe_w_down"],
                            p["norm_out"], tiles=tiles)
    return y.reshape(batch, seq, d), hg_new[None], gd_new[None], cv_new[None]


def kernel(x_prompt, x_sample, state_hgrn, state_gdn, state_gdn_conv, norm_mix, norm_ffn, norm_out, hgrn_w_in, hgrn_lb, hgrn_norm, hgrn_w_out, gdn_w_in, gdn_conv, gdn_a_log, gdn_dt_bias, gdn_norm, gdn_w_out, ffn_w_gate, ffn_w_up, ffn_w_down, moe_router, moe_w_gate, moe_w_up, moe_w_down):
    assert hgrn_w_in.shape[0] == 1 and gdn_w_in.shape[0] == 1, "one HGRN2 layer and one gated-DeltaNet layer"
    main_cols = GD_QKV + N_HEADS * HEAD_DIM
    p = {
        "norm_mix": norm_mix, "norm_ffn": norm_ffn, "norm_out": norm_out,
        "hgrn_w_in": hgrn_w_in[0].astype(BF16), "hgrn_lb": hgrn_lb, "hgrn_norm": hgrn_norm[0],
        "hgrn_w_out": hgrn_w_out[0].astype(BF16),
        "gdn_w_main": gdn_w_in[0, :, :main_cols].astype(BF16), "gdn_w_ab": gdn_w_in[0, :, main_cols:].astype(BF16),
        "gdn_conv": gdn_conv[0], "gdn_a_log": gdn_a_log[0], "gdn_dt_bias": gdn_dt_bias[0], "gdn_norm": gdn_norm[0],
        "gdn_w_out": gdn_w_out[0].astype(BF16),
        "ffn_w_gate": ffn_w_gate[0].astype(BF16), "ffn_w_up": ffn_w_up[0].astype(BF16),
        "ffn_w_down": ffn_w_down[0].astype(BF16),
        "moe_router": moe_router[0],
        "moe_w_gate": moe_w_gate[0].astype(BF16), "moe_w_up": moe_w_up[0].astype(BF16),
        "moe_w_down": moe_w_down[0].astype(BF16),
    }
    y_p, hg_p, gd_p, cv_p = _trunk(x_prompt, None, None, None, p)
    y_s, hg_s, gd_s, cv_s = _trunk(x_sample, state_hgrn[0], state_gdn[0], state_gdn_conv[0], p)
    return (y_p, y_s, hg_p, hg_s, gd_p, gd_s, cv_p, cv_s)
```

```python
import functools

import jax
import jax.numpy as jnp
from jax import lax
from jax.experimental import pallas as pl
from jax.experimental.pallas import tpu as pltpu

F32 = jnp.float32
BF16 = jnp.bfloat16
HIGHEST = lax.Precision.HIGHEST

D_MODEL = 1024
N_HEADS = 8
HEAD_DIM = 128
D_FF = 3584
N_EXPERTS = 8
TOP_K = 2
GD_QKV = 3 * D_MODEL
GD_CONV = 4
MAX_CHUNK = 64
EPS = 1e-6
LANES = 128
SUBLANES = 8
VMEM_LIMIT_BYTES = 52 * 1024 * 1024


def _params(*sem):
    return pltpu.CompilerParams(dimension_semantics=sem, vmem_limit_bytes=VMEM_LIMIT_BYTES)


def _rms(x, gain):
    return x * lax.rsqrt(jnp.mean(x * x, axis=-1, keepdims=True) + EPS) * gain


def _sigmoid(x):
    t = jnp.exp(-jnp.abs(x))
    r = 1.0 / (1.0 + t)
    return jnp.where(x >= 0, r, t * r)


def _silu(x):
    return x * (0.5 * jnp.tanh(0.5 * x) + 0.5)


def _softplus(x):
    return jnp.maximum(x, 0.0) + jnp.log1p(jnp.exp(-jnp.abs(x)))


def _dot(a, b):
    return jnp.dot(a, b, preferred_element_type=F32)


def _dot_nt(a, b):
    return lax.dot_general(a, b, (((1,), (1,)), ((), ())), preferred_element_type=F32)


def _dot_tn(a, b):
    return lax.dot_general(a, b, (((0,), (0,)), ((), ())), preferred_element_type=F32)


def _dot_hi(a, b):
    return jnp.dot(a, b, precision=HIGHEST, preferred_element_type=F32)


def _norm_proj_kernel(x_ref, g_ref, w_ref, *rest, has_small):
    if has_small:
        ws_ref, o_ref, os_ref, h_scr = rest
    else:
        o_ref, h_scr = rest

    @pl.when(pl.program_id(1) == 0)
    def _():
        hb = _rms(x_ref[...], g_ref[...]).astype(BF16)
        h_scr[...] = hb
        if has_small:
            os_ref[...] = _dot(hb, ws_ref[...])

    o_ref[...] = _dot(h_scr[...], w_ref[...])


def _norm_proj(x, gain, w, w_small=None, *, tm, tn):
    n, d = x.shape
    e = w.shape[1]
    has_small = w_small is not None
    in_specs = [pl.BlockSpec((tm, d), lambda i, j: (i, 0)),
                pl.BlockSpec((1, d), lambda i, j: (0, 0)),
                pl.BlockSpec((d, tn), lambda i, j: (0, j))]
    out_shape = [jax.ShapeDtypeStruct((n, e), F32)]
    out_specs = [pl.BlockSpec((tm, tn), lambda i, j: (i, j))]
    args = [x, gain.reshape(1, d), w]
    if has_small:
        es = w_small.shape[1]
        in_specs.append(pl.BlockSpec((d, es), lambda i, j: (0, 0)))
        out_shape.append(jax.ShapeDtypeStruct((n, es), F32))
        out_specs.append(pl.BlockSpec((tm, es), lambda i, j: (i, 0)))
        args.append(w_small)
    return pl.pallas_call(
        functools.partial(_norm_proj_kernel, has_small=has_small),
        grid=(n // tm, e // tn),
        in_specs=in_specs, out_specs=out_specs, out_shape=out_shape,
        scratch_shapes=[pltpu.VMEM((tm, d), BF16)],
        compiler_params=_params("parallel", "arbitrary"),
        name="norm_proj",
    )(*args)


def _hgrn_kernel(*refs, chunk, n_chunks, n_seq, layer, has_s0):
    if has_s0:
        proj_ref, lb_ref, on_ref, s0_ref, og_ref, sf_ref, *scr = refs
    else:
        proj_ref, lb_ref, on_ref, og_ref, sf_ref, *scr = refs
    st_scr, b_scr, q_scr, k_scr, a_scr, qin_scr, kout_scr, dec_scr, oi_scr = scr
    C = chunk
    W = N_HEADS * HEAD_DIM
    tb = C * n_chunks
    t = pl.program_id(1)
    problems = [(s, c, h) for s in range(n_seq) for c in range(n_chunks) for h in range(N_HEADS)]

    def pid(s, c, h):
        return (s * n_chunks + c) * N_HEADS + h

    def row0(s, c):
        return s * tb + c * C

    @pl.when(t == 0)
    def _():
        for s in range(n_seq):
            for h in range(N_HEADS):
                st_scr[s * N_HEADS + h] = s0_ref[s, h].T if has_s0 else jnp.zeros((HEAD_DIM, HEAD_DIM), F32)

    lbraw = lb_ref[...]
    ex = jnp.exp(lbraw - jnp.max(lbraw, axis=0, keepdims=True))
    sm = ex / jnp.sum(ex, axis=0, keepdims=True)
    lb = sm[0:1]
    for j in range(1, layer + 1):
        lb = lb + sm[j:j + 1]

    row = lax.broadcasted_iota(jnp.int32, (C, C), 0)
    col = lax.broadcasted_iota(jnp.int32, (C, C), 1)
    causal = row >= col
    tril = causal.astype(BF16)
    mid = C // 2
    scale = HEAD_DIM ** -0.5
    onorm = on_ref[...]

    for s in range(n_seq):
        for c in range(n_chunks):
            rows = slice(row0(s, c), row0(s, c) + C)
            f_raw = proj_ref[rows, W:2 * W]
            tt = jnp.exp(-jnp.abs(f_raw))
            rr = 1.0 / (1.0 + tt)
            pos = f_raw >= 0
            sig = jnp.where(pos, rr, tt * rr)
            sig_neg = jnp.where(pos, tt * rr, rr)
            log_f = jnp.log(lb + (1.0 - lb) * sig)
            k_scr[rows, :] = (1.0 - lb) * sig_neg
            q_scr[rows, :] = _silu(proj_ref[rows, 0:W]) * scale
            hi = log_f.astype(BF16)
            rest = log_f - hi.astype(F32)
            md = rest.astype(BF16)
            lw = (rest - md.astype(F32)).astype(BF16)
            b_scr[rows, :] = (_dot(tril, lw) + _dot(tril, md)) + _dot(tril, hi)

    for s, c, h in problems:
        p, lo = pid(s, c, h), h * HEAD_DIM
        rows = slice(row0(s, c), row0(s, c) + C)
        b = b_scr[rows, lo:lo + HEAD_DIM]
        q = q_scr[rows, lo:lo + HEAD_DIM]
        k = k_scr[rows, lo:lo + HEAD_DIM]
        b_mid = b[mid:mid + 1]
        b_last = b[C - 1:C]
        a_scr[p] = _dot_nt((q * jnp.exp(b - b_mid)).astype(BF16), (k * jnp.exp(b_mid - b)).astype(BF16))
        qin_scr[p] = q * jnp.exp(b)
        kout_scr[p] = k * jnp.exp(b_last - b)
        dec_scr[p] = jnp.exp(b_last)

    for s, c, h in problems:
        p, lo = pid(s, c, h), h * HEAD_DIM
        rows = slice(row0(s, c), row0(s, c) + C)
        a = jnp.where(causal, a_scr[p], 0.0)
        oi_scr[p] = _dot(a.astype(BF16), proj_ref[rows, 2 * W + lo:2 * W + lo + HEAD_DIM].astype(BF16))

    for c in range(n_chunks):
        for s in range(n_seq):
            for h in range(N_HEADS):
                p, lo, sh = pid(s, c, h), h * HEAD_DIM, s * N_HEADS + h
                rows = slice(row0(s, c), row0(s, c) + C)
                st = st_scr[sh]
                o = oi_scr[p] + _dot_nt(qin_scr[p].astype(BF16), st.astype(BF16))
                vb = proj_ref[rows, 2 * W + lo:2 * W + lo + HEAD_DIM].astype(BF16)
                st_scr[sh] = st * dec_scr[p] + _dot_tn(vb, kout_scr[p].astype(BF16))
                gate = proj_ref[rows, 3 * W + lo:3 * W + lo + HEAD_DIM]
                og_ref[rows, lo:lo + HEAD_DIM] = _rms(o, onorm) * _silu(gate)

    @pl.when(t == pl.num_programs(1) - 1)
    def _():
        for s in range(n_seq):
            for h in range(N_HEADS):
                sf_ref[s, h] = st_scr[s * N_HEADS + h].T


def _hgrn_recurrence(proj, lb_raw, out_norm, s0, *, batch, seq, layer, n_chunks, n_seq):
    n = proj.shape[0]
    chunk = min(MAX_CHUNK, seq)
    tb = chunk * n_chunks
    assert seq % tb == 0 and batch % n_seq == 0
    assert n_seq == 1 or tb == seq, "several sequences per step only when a step covers whole sequences"
    nt = seq // tb
    rows = n_seq * tb
    n_prob = n_seq * n_chunks * N_HEADS
    width = N_HEADS * HEAD_DIM
    has_s0 = s0 is not None
    state_spec = pl.BlockSpec((n_seq, N_HEADS, HEAD_DIM, HEAD_DIM), lambda b, t: (b, 0, 0, 0))
    in_specs = [pl.BlockSpec((rows, proj.shape[1]), lambda b, t: (b * nt + t, 0)),
                pl.BlockSpec(lb_raw.shape, lambda b, t: (0, 0)),
                pl.BlockSpec((1, HEAD_DIM), lambda b, t: (0, 0))]
    args = [proj, lb_raw, out_norm.reshape(1, HEAD_DIM)]
    if has_s0:
        in_specs.append(state_spec)
        args.append(s0)
    vm = lambda *shape: pltpu.VMEM(shape, F32)
    return pl.pallas_call(
        functools.partial(_hgrn_kernel, chunk=chunk, n_chunks=n_chunks, n_seq=n_seq, layer=layer, has_s0=has_s0),
        grid=(batch // n_seq, nt),
        in_specs=in_specs,
        out_specs=[pl.BlockSpec((rows, D_MODEL), lambda b, t: (b * nt + t, 0)), state_spec],
        out_shape=[jax.ShapeDtypeStruct((n, D_MODEL), F32),
                   jax.ShapeDtypeStruct((batch, N_HEADS, HEAD_DIM, HEAD_DIM), F32)],
        scratch_shapes=[vm(n_seq * N_HEADS, HEAD_DIM, HEAD_DIM),
                        vm(rows, width),
                        vm(rows, width),
                        vm(rows, width),
                        vm(n_prob, chunk, chunk),
                        vm(n_prob, chunk, HEAD_DIM),
                        vm(n_prob, chunk, HEAD_DIM),
                        vm(n_prob, 1, HEAD_DIM),
                        vm(n_prob, chunk, HEAD_DIM)],
        compiler_params=_params("parallel", "arbitrary"),
        name="hgrn_recurrence",
    )(*args)


def _split_bf16(x):
    hi = x.astype(BF16)
    return hi, (x - hi.astype(F32)).astype(BF16)


def _gdn_kernel(*refs, chunk, n_chunks, n_seq, has_s0):
    (main_ref, ab_ref, abt_ref, cw_ref, al_ref, dtb_ref, alt_ref, dtbt_ref, on_ref), rest = refs[:9], refs[9:]
    if has_s0:
        (cb0_ref, s0_ref), rest = rest[:2], rest[2:]
    (og_ref, sf_ref, cb_ref, s_scr, xpad, act, gcc_scr, gcr_scr, beta_scr, gram_scr, rhs_scr, wq_scr, kout_scr,
     inv_scr, pow_scr, low_scr, attn_scr, x0_scr, u_scr, ws_scr) = rest
    C = chunk
    W = N_HEADS * HEAD_DIM
    tb = C * n_chunks
    t = pl.program_id(1)
    pad = SUBLANES
    problems = [(s, c, h) for s in range(n_seq) for c in range(n_chunks) for h in range(N_HEADS)]

    def pid(s, c, h):
        return (s * n_chunks + c) * N_HEADS + h

    def row0(s, c):
        return s * tb + c * C

    @pl.when(t == 0)
    def _():
        for s in range(n_seq):
            for h in range(N_HEADS):
                s_scr[s * N_HEADS + h] = s0_ref[s, h] if has_s0 else jnp.zeros((HEAD_DIM, HEAD_DIM), F32)
            xpad[s, 0:pad, :] = jnp.zeros((pad, GD_QKV), F32)
            if has_s0:
                xpad[s, pad - (GD_CONV - 1):pad, :] = cb0_ref[s]

    @pl.when(t > 0)
    def _():
        for s in range(n_seq):
            xpad[s, 0:pad, :] = xpad[s, tb:tb + pad, :]

    for s in range(n_seq):
        for j in range(GD_QKV // LANES):
            cs = slice(j * LANES, (j + 1) * LANES)
            xpad[s, pad:pad + tb, cs] = main_ref[s * tb:(s + 1) * tb, cs]
    for s in range(n_seq):
        for j in range(GD_QKV // LANES):
            cs = slice(j * LANES, (j + 1) * LANES)
            conv = xpad[s, pad - 3:pad - 3 + tb, cs] * cw_ref[0:1, cs]
            for tap in range(1, GD_CONV):
                conv = conv + xpad[s, pad - 3 + tap:pad - 3 + tap + tb, cs] * cw_ref[tap:tap + 1, cs]
            act[s * tb:(s + 1) * tb, cs] = _silu(conv)

    @pl.when(t == pl.num_programs(1) - 1)
    def _():
        for s in range(n_seq):
            cb_ref[s] = xpad[s, pad + tb - (GD_CONV - 1):pad + tb, :]

    row = lax.broadcasted_iota(jnp.int32, (C, C), 0)
    col = lax.broadcasted_iota(jnp.int32, (C, C), 1)
    incl = row >= col
    strict = row > col
    eye = jnp.where(row == col, 1.0, 0.0)
    tril = incl.astype(F32)
    triu = (row <= col).astype(F32)
    ab = ab_ref[...]
    g_col = -jnp.exp(al_ref[...]) * _softplus(ab[:, 0:N_HEADS] + dtb_ref[...])
    beta_scr[...] = _sigmoid(ab[:, N_HEADS:2 * N_HEADS])
    for s in range(n_seq):
        g_row = -jnp.exp(alt_ref[...]) * _softplus(abt_ref[s][0:N_HEADS, :] + dtbt_ref[...])
        for c in range(n_chunks):
            r = row0(s, c)
            gcc_scr[r:r + C, :] = _dot_hi(tril, g_col[r:r + C, :])
            gcr_scr[s * n_chunks + c] = _dot_hi(g_row[:, c * C:(c + 1) * C], triu)

    scale = HEAD_DIM ** -0.5
    onorm = on_ref[...]

    for s, c, h in problems:
        p, r, lo = pid(s, c, h), row0(s, c), h * HEAD_DIM
        rows = slice(r, r + C)
        q = act[rows, lo:lo + HEAD_DIM]
        k = act[rows, W + lo:W + lo + HEAD_DIM]
        v = act[rows, 2 * W + lo:2 * W + lo + HEAD_DIM]
        q = q * lax.rsqrt(jnp.sum(q * q, axis=-1, keepdims=True) + EPS) * scale
        k = k * lax.rsqrt(jnp.sum(k * k, axis=-1, keepdims=True) + EPS)
        gc_c = gcc_scr[rows, h:h + 1]
        beta_c = beta_scr[rows, h:h + 1]
        k_beta = k * beta_c
        gram_scr[p] = _dot_nt(jnp.concatenate([k_beta, q], axis=0).astype(BF16), k.astype(BF16))
        e_gc = jnp.exp(gc_c)
        rhs_scr[p] = jnp.concatenate([v * beta_c, k_beta * e_gc], axis=1)
        wq_scr[p, C:2 * C, :] = q * e_gc
        kout_scr[p] = k * jnp.exp(gc_c[C - 1:C, :] - gc_c)

    for s, c, h in problems:
        p, r = pid(s, c, h), row0(s, c)
        gc_c = gcc_scr[r:r + C, h:h + 1]
        gc_r = gcr_scr[s * n_chunks + c, h:h + 1, :]
        decay = jnp.exp(jnp.where(incl, gc_c - gc_r, -jnp.inf))
        gram = gram_scr[p]
        lower = jnp.where(strict, gram[:C] * decay, 0.0)
        low_scr[p] = lower
        attn_scr[p] = gram[C:] * decay
        inv_scr[p] = eye - lower
        mb = (-lower).astype(BF16)
        pow_scr[p] = _dot(mb, mb)

    power = 2
    while power < C:
        last = 2 * power >= C
        for s, c, h in problems:
            p = pid(s, c, h)
            pw = pow_scr[p]
            inv = inv_scr[p]
            pb = pw.astype(BF16)
            if last:
                inv_scr[p] = inv + _dot(inv.astype(BF16), pb)
            else:
                both = _dot(jnp.concatenate([inv, pw], axis=0).astype(BF16), pb)
                inv_scr[p] = inv + both[:C]
                pow_scr[p] = both[C:]
        power *= 2

    for s, c, h in problems:
        p = pid(s, c, h)
        x0_scr[p] = _dot(inv_scr[p].astype(BF16), rhs_scr[p].astype(BF16))
    for s, c, h in problems:
        p = pid(s, c, h)
        x0 = x0_scr[p]
        x_hi, x_lo = _split_bf16(x0)
        l_hi, l_lo = _split_bf16(low_scr[p])
        lx = _dot(l_hi, jnp.concatenate([x_hi, x_lo], axis=1))
        lx = lx[:, :2 * HEAD_DIM] + (lx[:, 2 * HEAD_DIM:] + _dot(l_lo, x_hi))
        rhs_scr[p] = rhs_scr[p] - x0 - lx
    for s, c, h in problems:
        p = pid(s, c, h)
        sol = x0_scr[p] + _dot(inv_scr[p].astype(BF16), rhs_scr[p].astype(BF16))
        u_scr[p] = sol[:, :HEAD_DIM]
        wq_scr[p, 0:C, :] = sol[:, HEAD_DIM:]

    for c in range(n_chunks):
        for s in range(n_seq):
            for h in range(N_HEADS):
                sh = s * N_HEADS + h
                ws_scr[sh] = _dot(wq_scr[pid(s, c, h)].astype(BF16), s_scr[sh].astype(BF16))
        for s in range(n_seq):
            for h in range(N_HEADS):
                p, r, lo, sh = pid(s, c, h), row0(s, c), h * HEAD_DIM, s * N_HEADS + h
                rows = slice(r, r + C)
                ws = ws_scr[sh]
                vn16 = (u_scr[p] - ws[:C]).astype(BF16)
                o = ws[C:] + _dot(attn_scr[p].astype(BF16), vn16)
                g_last = gcc_scr[r + C - 1:r + C, h:h + 1]
                s_scr[sh] = s_scr[sh] * jnp.exp(g_last) + _dot_tn(kout_scr[p].astype(BF16), vn16)
                z = main_ref[rows, 3 * W + lo:3 * W + lo + HEAD_DIM]
                og_ref[rows, lo:lo + HEAD_DIM] = _rms(o, onorm) * _silu(z)

    @pl.when(t == pl.num_programs(1) - 1)
    def _():
        for s in range(n_seq):
            for h in range(N_HEADS):
                sf_ref[s, h] = s_scr[s * N_HEADS + h]


def _gdn_recurrence(main, ab, conv_w, a_log, dt_bias, out_norm, cb0, s0, *, batch, seq, n_chunks, n_seq):
    n = main.shape[0]
    chunk = min(MAX_CHUNK, seq)
    tb = chunk * n_chunks
    assert seq % tb == 0 and batch % n_seq == 0 and seq >= GD_CONV - 1
    assert n_seq == 1 or tb == seq, "several sequences per step only when a step covers whole sequences"
    nt = seq // tb
    rows = n_seq * tb
    n_prob = n_seq * n_chunks * N_HEADS
    has_s0 = s0 is not None
    abt = ab.reshape(batch, seq, 2 * N_HEADS).transpose(0, 2, 1)
    state_spec = pl.BlockSpec((n_seq, N_HEADS, HEAD_DIM, HEAD_DIM), lambda b, t: (b, 0, 0, 0))
    cb_spec = pl.BlockSpec((n_seq, GD_CONV - 1, GD_QKV), lambda b, t: (b, 0, 0))
    small = lambda shape: pl.BlockSpec(shape, lambda b, t: (0,) * len(shape))
    in_specs = [pl.BlockSpec((rows, main.shape[1]), lambda b, t: (b * nt + t, 0)),
                pl.BlockSpec((rows, 2 * N_HEADS), lambda b, t: (b * nt + t, 0)),
                pl.BlockSpec((n_seq, 2 * N_HEADS, tb), lambda b, t: (b, 0, t)),
                small((GD_CONV, GD_QKV)), small((1, N_HEADS)), small((1, N_HEADS)),
                small((N_HEADS, 1)), small((N_HEADS, 1)), small((1, HEAD_DIM))]
    args = [main, ab, abt, conv_w, a_log.reshape(1, N_HEADS), dt_bias.reshape(1, N_HEADS),
            a_log.reshape(N_HEADS, 1), dt_bias.reshape(N_HEADS, 1), out_norm.reshape(1, HEAD_DIM)]
    if has_s0:
        in_specs += [cb_spec, state_spec]
        args += [cb0, s0]
    vm = lambda *shape: pltpu.VMEM(shape, F32)
    return pl.pallas_call(
        functools.partial(_gdn_kernel, chunk=chunk, n_chunks=n_chunks, n_seq=n_seq, has_s0=has_s0),
        grid=(batch // n_seq, nt),
        in_specs=in_specs,
        out_specs=[pl.BlockSpec((rows, D_MODEL), lambda b, t: (b * nt + t, 0)), state_spec, cb_spec],
        out_shape=[jax.ShapeDtypeStruct((n, D_MODEL), F32),
                   jax.ShapeDtypeStruct((batch, N_HEADS, HEAD_DIM, HEAD_DIM), F32),
                   jax.ShapeDtypeStruct((batch, GD_CONV - 1, GD_QKV), F32)],
        scratch_shapes=[vm(n_seq * N_HEADS, HEAD_DIM, HEAD_DIM),
                        vm(n_seq, tb + SUBLANES, GD_QKV),
                        vm(rows, GD_QKV),
                        vm(rows, N_HEADS),
                        vm(n_seq * n_chunks, N_HEADS, chunk),
                        vm(rows, N_HEADS),
                        vm(n_prob, 2 * chunk, chunk),
                        vm(n_prob, chunk, 2 * HEAD_DIM),
                        vm(n_prob, 2 * chunk, HEAD_DIM),
                        vm(n_prob, chunk, HEAD_DIM),
                        vm(n_prob, chunk, chunk),
                        vm(n_prob, chunk, chunk),
                        vm(n_prob, chunk, chunk),
                        vm(n_prob, chunk, chunk),
                        vm(n_prob, chunk, 2 * HEAD_DIM),
                        vm(n_prob, chunk, HEAD_DIM),
                        vm(n_seq * N_HEADS, 2 * chunk, HEAD_DIM)],
        compiler_params=_params("parallel", "arbitrary"),
        name="gdn_recurrence",
    )(*args)


def _swiglu_rows(x_of, wg, wu, wd, n_sub, emit):
    pending = None
    for sb in range(n_sub + 1):
        if sb < n_sub:
            x = x_of(sb)
            nxt = (_dot(x, wg), _dot(x, wu))
        if pending is not None:
            a, u = pending
            emit(sb - 1, _dot((_silu(a) * u).astype(BF16), wd))
        pending = nxt


def _ffn_kernel(x_ref, og_ref, wo_ref, g_ref, wg_ref, wu_ref, wd_ref, o_ref, h_scr, *, n_sub):
    j = pl.program_id(1)
    sub = x_ref.shape[0] // n_sub

    @pl.when(j == 0)
    def _():
        x = x_ref[...] + _dot(og_ref[...].astype(BF16), wo_ref[...])
        h_scr[...] = _rms(x, g_ref[...]).astype(BF16)
        o_ref[...] = x

    def emit(sb, y):
        o_ref[sb * sub:(sb + 1) * sub, :] += y

    _swiglu_rows(lambda sb: h_scr[sb * sub:(sb + 1) * sub, :], wg_ref[...], wu_ref[...], wd_ref[...], n_sub, emit)


def _ffn(x, og, w_out, gain, wg, wu, wd, *, tm, tf, n_sub):
    n, d = x.shape
    f = wg.shape[1]
    rows = pl.BlockSpec((tm, d), lambda i, j: (i, 0))
    return pl.pallas_call(
        functools.partial(_ffn_kernel, n_sub=n_sub),
        grid=(n // tm, f // tf),
        in_specs=[rows, rows,
                  pl.BlockSpec((d, d), lambda i, j: (0, 0)),
                  pl.BlockSpec((1, d), lambda i, j: (0, 0)),
                  pl.BlockSpec((d, tf), lambda i, j: (0, j)),
                  pl.BlockSpec((d, tf), lambda i, j: (0, j)),
                  pl.BlockSpec((tf, d), lambda i, j: (j, 0))],
        out_specs=rows,
        out_shape=jax.ShapeDtypeStruct((n, d), F32),
        scratch_shapes=[pltpu.VMEM((tm, d), BF16)],
        compiler_params=_params("parallel", "arbitrary"),
        name="dense_ffn",
    )(x, og, w_out, gain.reshape(1, d), wg, wu, wd)


def _router_kernel(x_ref, og_ref, wo_ref, g_ref, wr_ref, xo_ref, h_ref, idx_ref, gate_ref, rank_ref, cnt_ref, cnt_scr,
                   *, tm):
    i = pl.program_id(0)

    @pl.when(i == 0)
    def _():
        cnt_scr[...] = jnp.zeros_like(cnt_scr)

    x = x_ref[...] + _dot(og_ref[...].astype(BF16), wo_ref[...])
    xo_ref[...] = x
    h = _rms(x, g_ref[...])
    _store_token_tiles(h_ref, 0, h)
    logits = _dot_hi(h, wr_ref[...])
    lane = lax.broadcasted_iota(jnp.int32, logits.shape, 1)
    m1 = jnp.max(logits, axis=-1, keepdims=True)
    i1 = jnp.min(jnp.where(logits == m1, lane, N_EXPERTS), axis=-1, keepdims=True)
    rest = jnp.where(lane == i1, -jnp.inf, logits)
    m2 = jnp.max(rest, axis=-1, keepdims=True)
    i2 = jnp.min(jnp.where(rest == m2, lane, N_EXPERTS), axis=-1, keepdims=True)
    e2 = jnp.exp(m2 - m1)
    den = 1.0 + e2
    two = lax.broadcasted_iota(jnp.int32, (tm, TOP_K), 1)
    idx_ref[...] = jnp.where(two == 0, i1, i2)
    gate_ref[...] = jnp.where(two == 0, 1.0 / den, e2 / den)

    onehot = ((lane == i1) | (lane == i2)).astype(F32)
    row = lax.broadcasted_iota(jnp.int32, (tm, tm), 0)
    col = lax.broadcasted_iota(jnp.int32, (tm, tm), 1)
    before = (row > col).astype(BF16)
    excl = _dot(before, onehot.astype(BF16)) + cnt_scr[...]
    r1 = jnp.sum(jnp.where(lane == i1, excl, 0.0), axis=-1, keepdims=True)
    r2 = jnp.sum(jnp.where(lane == i2, excl, 0.0), axis=-1, keepdims=True)
    rank_ref[...] = jnp.where(two == 0, r1, r2).astype(jnp.int32)
    cnt = cnt_scr[...] + jnp.sum(onehot, axis=0, keepdims=True)
    cnt_scr[...] = cnt
    cnt_ref[...] = cnt.astype(jnp.int32)


def _router(x, og, w_out, gain, w_router, *, tm):
    n, d = x.shape
    two = pl.BlockSpec((tm, TOP_K), lambda i: (i, 0))
    rows = pl.BlockSpec((tm, d), lambda i: (i, 0))
    return pl.pallas_call(
        functools.partial(_router_kernel, tm=tm),
        grid=(n // tm,),
        in_specs=[rows, rows,
                  pl.BlockSpec((d, d), lambda i: (0, 0)),
                  pl.BlockSpec((1, d), lambda i: (0, 0)),
                  pl.BlockSpec((d, N_EXPERTS), lambda i: (0, 0))],
        out_specs=[rows, pl.BlockSpec((tm * SUBLANES, LANES), lambda i: (i, 0)), two, two, two,
                   pl.BlockSpec((1, N_EXPERTS), lambda i: (0, 0))],
        out_shape=[jax.ShapeDtypeStruct((n, d), F32),
                   jax.ShapeDtypeStruct((n * SUBLANES, LANES), F32),
                   jax.ShapeDtypeStruct((n, TOP_K), jnp.int32),
                   jax.ShapeDtypeStruct((n, TOP_K), F32),
                   jax.ShapeDtypeStruct((n, TOP_K), jnp.int32),
                   jax.ShapeDtypeStruct((1, N_EXPERTS), jnp.int32)],
        scratch_shapes=[pltpu.VMEM((1, N_EXPERTS), F32)],
        compiler_params=_params("arbitrary"),
        name="moe_router",
    )(x, og, w_out, gain.reshape(1, d), w_router)


def _tile_gather_copy(src_hbm, dst, sem, src_row, dst_row):
    return pltpu.make_async_copy(src_hbm.at[pl.ds(pl.multiple_of(src_row * SUBLANES, SUBLANES), SUBLANES)],
                                 dst.at[pl.ds(pl.multiple_of(dst_row * SUBLANES, SUBLANES), SUBLANES)], sem)


def _load_token_tiles(ref, first_row, n_rows):
    return jnp.concatenate([ref[pl.ds(first_row * SUBLANES + s, n_rows, stride=SUBLANES), :]
                            for s in range(D_MODEL // LANES)], axis=1)


def _store_token_tiles(ref, first_row, x):
    for s in range(D_MODEL // LANES):
        ref[pl.ds(first_row * SUBLANES + s, x.shape[0], stride=SUBLANES), :] = x[:, s * LANES:(s + 1) * LANES]


def _expert_kernel(be_ref, tok_ref, nv_ref, nu_ref, h_hbm, wg_ref, wu_ref, wd_ref, yb_ref, xg, xb, acc, sems,
                   *, sub_rows, n_sub):
    i = pl.program_id(0)
    j = pl.program_id(1)
    tmm = sub_rows * n_sub
    n_valid = nv_ref[i]
    slot = lax.rem(i, 2)

    def start_gather(block, dst_slot):
        nv = nv_ref[block]
        for sb in range(n_sub):
            @pl.when(sb * sub_rows < nv)
            def _():
                def issue(r2, carry):
                    for k in range(2):
                        r = 2 * r2 + k
                        _tile_gather_copy(h_hbm, xg.at[dst_slot], sems.at[dst_slot, sb], tok_ref[block * tmm + r],
                                          r).start(priority=k)
                    return carry

                lax.fori_loop(sb * sub_rows // 2, (sb + 1) * sub_rows // 2, issue, 0, unroll=4)

    @pl.when(jnp.logical_and(i == 0, j == 0))
    def _():
        start_gather(0, 0)

    @pl.when(j == 0)
    def _():
        acc[...] = jnp.zeros_like(acc)
        for sb in range(n_sub):
            @pl.when(sb * sub_rows < n_valid)
            def _():
                tiles = pl.ds(sb * sub_rows * SUBLANES, sub_rows * SUBLANES)
                pltpu.make_async_copy(h_hbm.at[tiles], xg.at[slot, tiles], sems.at[slot, sb]).wait()
                xb[sb * sub_rows:(sb + 1) * sub_rows, :] = _load_token_tiles(
                    xg.at[slot], sb * sub_rows, sub_rows).astype(BF16)

        @pl.when(i + 1 < pl.num_programs(0))
        def _():
            start_gather(i + 1, 1 - slot)

    def emit(sb, y):
        acc[sb * sub_rows:(sb + 1) * sub_rows, :] += y

    def x_of(sb):
        return xb[sb * sub_rows:(sb + 1) * sub_rows, :]

    full = n_valid > (n_sub - 1) * sub_rows

    @pl.when(full)
    def _():
        _swiglu_rows(x_of, wg_ref[0], wu_ref[0], wd_ref[0], n_sub, emit)

    for sb in range(n_sub - 1):
        @pl.when(jnp.logical_and(jnp.logical_not(full), sb * sub_rows < n_valid))
        def _():
            _swiglu_rows(lambda _, sb=sb: x_of(sb), wg_ref[0], wu_ref[0], wd_ref[0], 1, lambda _, y, sb=sb: emit(sb, y))

    @pl.when(j == pl.num_programs(1) - 1)
    def _():
        for sb in range(n_sub):
            rows = slice(sb * sub_rows, (sb + 1) * sub_rows)

            @pl.when(sb * sub_rows < n_valid)
            def _():
                _store_token_tiles(yb_ref, sb * sub_rows, acc[rows, :])

            @pl.when(sb * sub_rows >= n_valid)
            def _():
                tiles = slice(sb * sub_rows * SUBLANES, (sb + 1) * sub_rows * SUBLANES)
                yb_ref[tiles, :] = jnp.zeros((sub_rows * SUBLANES, LANES), F32)


def _experts(h_tiles, block_e, buf_tok, n_valid, n_used, wg, wu, wd, *, sub_rows, n_sub, tf):
    d = D_MODEL
    tmm = sub_rows * n_sub
    nb = block_e.shape[0]
    nf = D_FF // tf

    def wcol(i, j, be, tok, nv, nu):
        return (be[i], 0, jnp.where(i < nu[0], j, nf - 1))

    def wrow(i, j, be, tok, nv, nu):
        return (be[i], jnp.where(i < nu[0], j, nf - 1), 0)

    grid_spec = pltpu.PrefetchScalarGridSpec(
        num_scalar_prefetch=4,
        grid=(nb, nf),
        in_specs=[pl.BlockSpec(memory_space=pl.ANY),
                  pl.BlockSpec((1, d, tf), wcol),
                  pl.BlockSpec((1, d, tf), wcol),
                  pl.BlockSpec((1, tf, d), wrow)],
        out_specs=pl.BlockSpec((tmm * SUBLANES, LANES), lambda i, j, be, tok, nv, nu: (i, 0)),
        scratch_shapes=[pltpu.VMEM((2, tmm * SUBLANES, LANES), F32), pltpu.VMEM((tmm, d), BF16),
                        pltpu.VMEM((tmm, d), F32), pltpu.SemaphoreType.DMA((2, n_sub))],
    )
    return pl.pallas_call(
        functools.partial(_expert_kernel, sub_rows=sub_rows, n_sub=n_sub),
        grid_spec=grid_spec,
        out_shape=jax.ShapeDtypeStruct((nb * tmm * SUBLANES, LANES), F32),
        compiler_params=_params("arbitrary", "arbitrary"),
        name="moe_experts",
    )(block_e, buf_tok, n_valid, n_used, h_tiles, wg, wu, wd)


def _combine_kernel(dest_ref, yb_hbm, x_ref, gate_ref, gain_ref, o_ref, r0, r1, sems, *, tm):
    i = pl.program_id(0)
    slot = lax.rem(i, 2)

    def start_gather(tile, dst_slot):
        def issue(r, carry):
            first = TOP_K * (tile * tm + r)
            _tile_gather_copy(yb_hbm, r0.at[dst_slot], sems.at[dst_slot, 0], dest_ref[first], r).start(priority=0)
            _tile_gather_copy(yb_hbm, r1.at[dst_slot], sems.at[dst_slot, 1], dest_ref[first + 1], r).start(priority=1)
            return carry

        lax.fori_loop(0, tm, issue, 0, unroll=8)

    @pl.when(i == 0)
    def _():
        start_gather(0, 0)

    @pl.when(i + 1 < pl.num_programs(0))
    def _():
        start_gather(i + 1, 1 - slot)

    whole = pl.ds(0, tm * SUBLANES)
    pltpu.make_async_copy(yb_hbm.at[whole], r0.at[slot], sems.at[slot, 0]).wait()
    pltpu.make_async_copy(yb_hbm.at[whole], r1.at[slot], sems.at[slot, 1]).wait()
    g = gate_ref[...]
    y = _load_token_tiles(r0.at[slot], 0, tm) * g[:, 0:1] + _load_token_tiles(r1.at[slot], 0, tm) * g[:, 1:2]
    o_ref[...] = _rms(x_ref[...] + y, gain_ref[...])


def _combine(yb, dest_flat, x, gates, gain, *, tm):
    n, d = x.shape
    grid_spec = pltpu.PrefetchScalarGridSpec(
        num_scalar_prefetch=1,
        grid=(n // tm,),
        in_specs=[pl.BlockSpec(memory_space=pl.ANY),
                  pl.BlockSpec((tm, d), lambda i, dest: (i, 0)),
                  pl.BlockSpec((tm, TOP_K), lambda i, dest: (i, 0)),
                  pl.BlockSpec((1, d), lambda i, dest: (0, 0))],
        out_specs=pl.BlockSpec((tm, d), lambda i, dest: (i, 0)),
        scratch_shapes=[pltpu.VMEM((2, tm * SUBLANES, LANES), F32), pltpu.VMEM((2, tm * SUBLANES, LANES), F32),
                        pltpu.SemaphoreType.DMA((2, TOP_K))],
    )
    return pl.pallas_call(
        functools.partial(_combine_kernel, tm=tm),
        grid_spec=grid_spec,
        out_shape=jax.ShapeDtypeStruct((n, d), F32),
        compiler_params=_params("arbitrary"),
        name="moe_combine",
    )(dest_flat, yb, x, gates, gain.reshape(1, d))


def _moe_and_final_norm(x, og, w_out, gain, w_router, wg, wu, wd, final_gain, *, tiles):
    n = x.shape[0]
    sub_rows, n_sub = tiles["moe_sub_rows"], tiles["moe_subs"]
    tmm = sub_rows * n_sub
    x, h, idx, gates, rank, counts = _router(x, og, w_out, gain, w_router, tm=tiles["router_rows"])
    counts = counts[0]
    pcounts = (counts + tmm - 1) // tmm * tmm
    pend = jnp.cumsum(pcounts)
    pstart = pend - pcounts
    dest = pstart[idx] + rank
    nb = -(-(n * TOP_K + N_EXPERTS * (tmm - 1)) // tmm)
    tok = jnp.broadcast_to(jnp.arange(n, dtype=jnp.int32)[:, None], (n, TOP_K))
    buf_tok = jnp.zeros((nb * tmm,), jnp.int32).at[dest.reshape(-1)].set(tok.reshape(-1), unique_indices=True)
    n_used = (pend[-1] // tmm).astype(jnp.int32)
    blk = jnp.arange(nb, dtype=jnp.int32)
    block_e = jnp.minimum(jnp.searchsorted(pend, jnp.minimum(blk, n_used - 1) * tmm, side="right"),
                          N_EXPERTS - 1).astype(jnp.int32)
    n_valid = jnp.where(blk < n_used, jnp.clip(counts[block_e] - (blk * tmm - pstart[block_e]), 0, tmm), 0)
    yb = _experts(h, block_e, buf_tok, n_valid.astype(jnp.int32), n_used.reshape(1), wg, wu, wd,
                  sub_rows=sub_rows, n_sub=n_sub, tf=tiles["moe_ff_cols"])
    return _combine(yb, dest.reshape(-1).astype(jnp.int32), x, gates, final_gain, tm=tiles["combine_rows"])


def _tiles(n_rows, seq):
    big = n_rows >= 8192
    return {
        "proj_rows": 1024 if n_rows % 1024 == 0 else n_rows,
        "proj_cols": 1024,
        "rec_chunks": min(4, seq // min(MAX_CHUNK, seq)),
        "rec_seqs": 1 if seq > MAX_CHUNK else 4,
        "ffn_rows": 1024 if n_rows % 1024 == 0 else n_rows,
        "ff_cols": 512,
        "router_rows": 512,
        "moe_sub_rows": 256,
        "moe_subs": 4 if big else 2,
        "moe_ff_cols": 1792,
        "ffn_subs": 4 if big else 2,
        "combine_rows": 256,
    }


def _trunk(x3, hg_state, gd_state, gd_conv, p):
    batch, seq, d = x3.shape
    n = batch * seq
    x = x3.reshape(n, d)
    tiles = _tiles(n, seq)
    tm, tn = tiles["proj_rows"], tiles["proj_cols"]

    proj = _norm_proj(x, p["norm_mix"][0], p["hgrn_w_in"], tm=tm, tn=tn)[0]
    og, hg_new = _hgrn_recurrence(proj, p["hgrn_lb"], p["hgrn_norm"], hg_state, batch=batch, seq=seq, layer=0,
                                  n_chunks=tiles["rec_chunks"], n_seq=tiles["rec_seqs"])
    x = _ffn(x, og, p["hgrn_w_out"], p["norm_ffn"][0], p["ffn_w_gate"], p["ffn_w_up"], p["ffn_w_down"],
             tm=tiles["ffn_rows"], tf=tiles["ff_cols"], n_sub=tiles["ffn_subs"])

    main, ab = _norm_proj(x, p["norm_mix"][1], p["gdn_w_main"], p["gdn_w_ab"], tm=tm, tn=tn)
    og, gd_new, cv_new = _gdn_recurrence(main, ab, p["gdn_conv"], p["gdn_a_log"], p["gdn_dt_bias"], p["gdn_norm"],
                                         gd_conv, gd_state, batch=batch, seq=seq, n_chunks=tiles["rec_chunks"],
                                         n_seq=tiles["rec_seqs"])
    y = _moe_and_final_norm(x, og, p["gdn_w_out"], p["norm_ffn"][1], p["moe_router"], p["moe_w_gate"], p["moe_w_up"], p["moe_w_down"],
                            p["norm_out"], tiles=tiles)
    return y.reshape(batch, seq, d), hg_new[None], gd_new[None], cv_new[None]


def kernel(x_prompt, x_sample, state_hgrn, state_gdn, state_gdn_conv, norm_mix, norm_ffn, norm_out, hgrn_w_in, hgrn_lb, hgrn_norm, hgrn_w_out, gdn_w_in, gdn_conv, gdn_a_log, gdn_dt_bias, gdn_norm, gdn_w_out, ffn_w_gate, ffn_w_up, ffn_w_down, moe_router, moe_w_gate, moe_w_up, moe_w_down):
    assert hgrn_w_in.shape[0] == 1 and gdn_w_in.shape[0] == 1, "one HGRN2 layer and one gated-DeltaNet layer"
    main_cols = GD_QKV + N_HEADS * HEAD_DIM
    p = {
        "norm_mix": norm_mix, "norm_ffn": norm_ffn, "norm_out": norm_out,
        "hgrn_w_in": hgrn_w_in[0].astype(BF16), "hgrn_lb": hgrn_lb, "hgrn_norm": hgrn_norm[0],
        "hgrn_w_out": hgrn_w_out[0].astype(BF16),
        "gdn_w_main": gdn_w_in[0, :, :main_cols].astype(BF16), "gdn_w_ab": gdn_w_in[0, :, main_cols:].astype(BF16),
        "gdn_conv": gdn_conv[0], "gdn_a_log": gdn_a_log[0], "gdn_dt_bias": gdn_dt_bias[0], "gdn_norm": gdn_norm[0],
        "gdn_w_out": gdn_w_out[0].astype(BF16),
        "ffn_w_gate": ffn_w_gate[0].astype(BF16), "ffn_w_up": ffn_w_up[0].astype(BF16),
        "ffn_w_down": ffn_w_down[0].astype(BF16),
        "moe_router": moe_router[0],
        "moe_w_gate": moe_w_gate[0].astype(BF16), "moe_w_up": moe_w_up[0].astype(BF16),
        "moe_w_down": moe_w_down[0].astype(BF16),
    }
    y_p, hg_p, gd_p, cv_p = _trunk(x_prompt, None, None, None, p)
    y_s, hg_s, gd_s, cv_s = _trunk(x_sample, state_hgrn[0], state_gdn[0], state_gdn_conv[0], p)
    return (y_p, y_s, hg_p, hg_s, gd_p, gd_s, cv_p, cv_s)
```

```python
import functools

import jax
import jax.numpy as jnp
from jax import lax
from jax.experimental import pallas as pl
from jax.experimental.pallas import tpu as pltpu

F32 = jnp.float32
BF16 = jnp.bfloat16
HIGHEST = lax.Precision.HIGHEST

D_MODEL = 1024
N_HEADS = 8
HEAD_DIM = 128
D_FF = 3584
N_EXPERTS = 8
TOP_K = 2
GD_QKV = 3 * D_MODEL
GD_CONV = 4
MAX_CHUNK = 64
EPS = 1e-6
LANES = 128
SUBLANES = 8
VMEM_LIMIT_BYTES = 52 * 1024 * 1024


def _params(*sem):
    return pltpu.CompilerParams(dimension_semantics=sem, vmem_limit_bytes=VMEM_LIMIT_BYTES)


def _rms(x, gain):
    return x * lax.rsqrt(jnp.mean(x * x, axis=-1, keepdims=True) + EPS) * gain


def _sigmoid(x):
    t = jnp.exp(-jnp.abs(x))
    r = 1.0 / (1.0 + t)
    return jnp.where(x >= 0, r, t * r)


def _silu(x):
    return x * (0.5 * jnp.tanh(0.5 * x) + 0.5)


def _softplus(x):
    return jnp.maximum(x, 0.0) + jnp.log1p(jnp.exp(-jnp.abs(x)))


def _dot(a, b):
    return jnp.dot(a, b, preferred_element_type=F32)


def _dot_nt(a, b):
    return lax.dot_general(a, b, (((1,), (1,)), ((), ())), preferred_element_type=F32)


def _dot_tn(a, b):
    return lax.dot_general(a, b, (((0,), (0,)), ((), ())), preferred_element_type=F32)


def _dot_hi(a, b):
    return jnp.dot(a, b, precision=HIGHEST, preferred_element_type=F32)


def _norm_proj_kernel(x_ref, g_ref, w_ref, *rest, has_small):
    if has_small:
        ws_ref, o_ref, os_ref, h_scr = rest
    else:
        o_ref, h_scr = rest

    @pl.when(pl.program_id(1) == 0)
    def _():
        hb = _rms(x_ref[...], g_ref[...]).astype(BF16)
        h_scr[...] = hb
        if has_small:
            os_ref[...] = _dot(hb, ws_ref[...])

    o_ref[...] = _dot(h_scr[...], w_ref[...])


def _norm_proj(x, gain, w, w_small=None, *, tm, tn):
    n, d = x.shape
    e = w.shape[1]
    has_small = w_small is not None
    in_specs = [pl.BlockSpec((tm, d), lambda i, j: (i, 0)),
                pl.BlockSpec((1, d), lambda i, j: (0, 0)),
                pl.BlockSpec((d, tn), lambda i, j: (0, j))]
    out_shape = [jax.ShapeDtypeStruct((n, e), F32)]
    out_specs = [pl.BlockSpec((tm, tn), lambda i, j: (i, j))]
    args = [x, gain.reshape(1, d), w]
    if has_small:
        es = w_small.shape[1]
        in_specs.append(pl.BlockSpec((d, es), lambda i, j: (0, 0)))
        out_shape.append(jax.ShapeDtypeStruct((n, es), F32))
        out_specs.append(pl.BlockSpec((tm, es), lambda i, j: (i, 0)))
        args.append(w_small)
    return pl.pallas_call(
        functools.partial(_norm_proj_kernel, has_small=has_small),
        grid=(n // tm, e // tn),
        in_specs=in_specs, out_specs=out_specs, out_shape=out_shape,
        scratch_shapes=[pltpu.VMEM((tm, d), BF16)],
        compiler_params=_params("parallel", "arbitrary"),
        name="norm_proj",
    )(*args)


def _hgrn_kernel(*refs, chunk, n_chunks, n_seq, layer, has_s0):
    if has_s0:
        proj_ref, lb_ref, on_ref, s0_ref, og_ref, sf_ref, *scr = refs
    else:
        proj_ref, lb_ref, on_ref, og_ref, sf_ref, *scr = refs
    st_scr, b_scr, q_scr, k_scr, a_scr, qin_scr, kout_scr, dec_scr, oi_scr = scr
    C = chunk
    W = N_HEADS * HEAD_DIM
    tb = C * n_chunks
    t = pl.program_id(1)
    problems = [(s, c, h) for s in range(n_seq) for c in range(n_chunks) for h in range(N_HEADS)]

    def pid(s, c, h):
        return (s * n_chunks + c) * N_HEADS + h

    def row0(s, c):
        return s * tb + c * C

    @pl.when(t == 0)
    def _():
        for s in range(n_seq):
            for h in range(N_HEADS):
                st_scr[s * N_HEADS + h] = s0_ref[s, h].T if has_s0 else jnp.zeros((HEAD_DIM, HEAD_DIM), F32)

    lbraw = lb_ref[...]
    ex = jnp.exp(lbraw - jnp.max(lbraw, axis=0, keepdims=True))
    sm = ex / jnp.sum(ex, axis=0, keepdims=True)
    lb = sm[0:1]
    for j in range(1, layer + 1):
        lb = lb + sm[j:j + 1]

    row = lax.broadcasted_iota(jnp.int32, (C, C), 0)
    col = lax.broadcasted_iota(jnp.int32, (C, C), 1)
    causal = row >= col
    tril = causal.astype(BF16)
    mid = C // 2
    scale = HEAD_DIM ** -0.5
    onorm = on_ref[...]

    for s in range(n_seq):
        for c in range(n_chunks):
            rows = slice(row0(s, c), row0(s, c) + C)
            f_raw = proj_ref[rows, W:2 * W]
            tt = jnp.exp(-jnp.abs(f_raw))
            rr = 1.0 / (1.0 + tt)
            pos = f_raw >= 0
            sig = jnp.where(pos, rr, tt * rr)
            sig_neg = jnp.where(pos, tt * rr, rr)
            log_f = jnp.log(lb + (1.0 - lb) * sig)
            k_scr[rows, :] = (1.0 - lb) * sig_neg
            q_scr[rows, :] = _silu(proj_ref[rows, 0:W]) * scale
            hi = log_f.astype(BF16)
            rest = log_f - hi.astype(F32)
            md = rest.astype(BF16)
            lw = (rest - md.astype(F32)).astype(BF16)
            b_scr[rows, :] = (_dot(tril, lw) + _dot(tril, md)) + _dot(tril, hi)

    for s, c, h in problems:
        p, lo = pid(s, c, h), h * HEAD_DIM
        rows = slice(row0(s, c), row0(s, c) + C)
        b = b_scr[rows, lo:lo + HEAD_DIM]
        q = q_scr[rows, lo:lo + HEAD_DIM]
        k = k_scr[rows, lo:lo + HEAD_DIM]
        b_mid = b[mid:mid + 1]
        b_last = b[C - 1:C]
        a_scr[p] = _dot_nt((q * jnp.exp(b - b_mid)).astype(BF16), (k * jnp.exp(b_mid - b)).astype(BF16))
        qin_scr[p] = q * jnp.exp(b)
        kout_scr[p] = k * jnp.exp(b_last - b)
        dec_scr[p] = jnp.exp(b_last)

    for s, c, h in problems:
        p, lo = pid(s, c, h), h * HEAD_DIM
        rows = slice(row0(s, c), row0(s, c) + C)
        a = jnp.where(causal, a_scr[p], 0.0)
        oi_scr[p] = _dot(a.astype(BF16), proj_ref[rows, 2 * W + lo:2 * W + lo + HEAD_DIM].astype(BF16))

    for c in range(n_chunks):
        for s in range(n_seq):
            for h in range(N_HEADS):
                p, lo, sh = pid(s, c, h), h * HEAD_DIM, s * N_HEADS + h
                rows = slice(row0(s, c), row0(s, c) + C)
                st = st_scr[sh]
                o = oi_scr[p] + _dot_nt(qin_scr[p].astype(BF16), st.astype(BF16))
                vb = proj_ref[rows, 2 * W + lo:2 * W + lo + HEAD_DIM].astype(BF16)
                st_scr[sh] = st * dec_scr[p] + _dot_tn(vb, kout_scr[p].astype(BF16))
                gate = proj_ref[rows, 3 * W + lo:3 * W + lo + HEAD_DIM]
                og_ref[rows, lo:lo + HEAD_DIM] = _rms(o, onorm) * _silu(gate)

    @pl.when(t == pl.num_programs(1) - 1)
    def _():
        for s in range(n_seq):
            for h in range(N_HEADS):
                sf_ref[s, h] = st_scr[s * N_HEADS + h].T


def _hgrn_recurrence(proj, lb_raw, out_norm, s0, *, batch, seq, layer, n_chunks, n_seq):
    n = proj.shape[0]
    chunk = min(MAX_CHUNK, seq)
    tb = chunk * n_chunks
    assert seq % tb == 0 and batch % n_seq == 0
    assert n_seq == 1 or tb == seq, "several sequences per step only when a step covers whole sequences"
    nt = seq // tb
    rows = n_seq * tb
    n_prob = n_seq * n_chunks * N_HEADS
    width = N_HEADS * HEAD_DIM
    has_s0 = s0 is not None
    state_spec = pl.BlockSpec((n_seq, N_HEADS, HEAD_DIM, HEAD_DIM), lambda b, t: (b, 0, 0, 0))
    in_specs = [pl.BlockSpec((rows, proj.shape[1]), lambda b, t: (b * nt + t, 0)),
                pl.BlockSpec(lb_raw.shape, lambda b, t: (0, 0)),
                pl.BlockSpec((1, HEAD_DIM), lambda b, t: (0, 0))]
    args = [proj, lb_raw, out_norm.reshape(1, HEAD_DIM)]
    if has_s0:
        in_specs.append(state_spec)
        args.append(s0)
    vm = lambda *shape: pltpu.VMEM(shape, F32)
    return pl.pallas_call(
        functools.partial(_hgrn_kernel, chunk=chunk, n_chunks=n_chunks, n_seq=n_seq, layer=layer, has_s0=has_s0),
        grid=(batch // n_seq, nt),
        in_specs=in_specs,
        out_specs=[pl.BlockSpec((rows, D_MODEL), lambda b, t: (b * nt + t, 0)), state_spec],
        out_shape=[jax.ShapeDtypeStruct((n, D_MODEL), F32),
                   jax.ShapeDtypeStruct((batch, N_HEADS, HEAD_DIM, HEAD_DIM), F32)],
        scratch_shapes=[vm(n_seq * N_HEADS, HEAD_DIM, HEAD_DIM),
                        vm(rows, width),
                        vm(rows, width),
                        vm(rows, width),
                        vm(n_prob, chunk, chunk),
                        vm(n_prob, chunk, HEAD_DIM),
                        vm(n_prob, chunk, HEAD_DIM),
                        vm(n_prob, 1, HEAD_DIM),
                        vm(n_prob, chunk, HEAD_DIM)],
        compiler_params=_params("parallel", "arbitrary"),
        name="hgrn_recurrence",
    )(*args)


def _split_bf16(x):
    hi = x.astype(BF16)
    return hi, (x - hi.astype(F32)).astype(BF16)


def _gdn_kernel(*refs, chunk, n_chunks, n_seq, has_s0):
    (main_ref, ab_ref, abt_ref, cw_ref, al_ref, dtb_ref, alt_ref, dtbt_ref, on_ref), rest = refs[:9], refs[9:]
    if has_s0:
        (cb0_ref, s0_ref), rest = rest[:2], rest[2:]
    (og_ref, sf_ref, cb_ref, s_scr, xpad, act, gcc_scr, gcr_scr, beta_scr, gram_scr, rhs_scr, wq_scr, kout_scr,
     inv_scr, pow_scr, low_scr, attn_scr, x0_scr, u_scr, ws_scr) = rest
    C = chunk
    W = N_HEADS * HEAD_DIM
    tb = C * n_chunks
    t = pl.program_id(1)
    pad = SUBLANES
    problems = [(s, c, h) for s in range(n_seq) for c in range(n_chunks) for h in range(N_HEADS)]

    def pid(s, c, h):
        return (s * n_chunks + c) * N_HEADS + h

    def row0(s, c):
        return s * tb + c * C

    @pl.when(t == 0)
    def _():
        for s in range(n_seq):
            for h in range(N_HEADS):
                s_scr[s * N_HEADS + h] = s0_ref[s, h] if has_s0 else jnp.zeros((HEAD_DIM, HEAD_DIM), F32)
            xpad[s, 0:pad, :] = jnp.zeros((pad, GD_QKV), F32)
            if has_s0:
                xpad[s, pad - (GD_CONV - 1):pad, :] = cb0_ref[s]

    @pl.when(t > 0)
    def _():
        for s in range(n_seq):
            xpad[s, 0:pad, :] = xpad[s, tb:tb + pad, :]

    for s in range(n_seq):
        for j in range(GD_QKV // LANES):
            cs = slice(j * LANES, (j + 1) * LANES)
            xpad[s, pad:pad + tb, cs] = main_ref[s * tb:(s + 1) * tb, cs]
    for s in range(n_seq):
        for j in range(GD_QKV // LANES):
            cs = slice(j * LANES, (j + 1) * LANES)
            conv = xpad[s, pad - 3:pad - 3 + tb, cs] * cw_ref[0:1, cs]
            for tap in range(1, GD_CONV):
                conv = conv + xpad[s, pad - 3 + tap:pad - 3 + tap + tb, cs] * cw_ref[tap:tap + 1, cs]
            act[s * tb:(s + 1) * tb, cs] = _silu(conv)

    @pl.when(t == pl.num_programs(1) - 1)
    def _():
        for s in range(n_seq):
            cb_ref[s] = xpad[s, pad + tb - (GD_CONV - 1):pad + tb, :]

    row = lax.broadcasted_iota(jnp.int32, (C, C), 0)
    col = lax.broadcasted_iota(jnp.int32, (C, C), 1)
    incl = row >= col
    strict = row > col
    eye = jnp.where(row == col, 1.0, 0.0)
    tril = incl.astype(F32)
    triu = (row <= col).astype(F32)
    ab = ab_ref[...]
    g_col = -jnp.exp(al_ref[...]) * _softplus(ab[:, 0:N_HEADS] + dtb_ref[...])
    beta_scr[...] = _sigmoid(ab[:, N_HEADS:2 * N_HEADS])
    for s in range(n_seq):
        g_row = -jnp.exp(alt_ref[...]) * _softplus(abt_ref[s][0:N_HEADS, :] + dtbt_ref[...])
        for c in range(n_chunks):
            r = row0(s, c)
            gcc_scr[r:r + C, :] = _dot_hi(tril, g_col[r:r + C, :])
            gcr_scr[s * n_chunks + c] = _dot_hi(g_row[:, c * C:(c + 1) * C], triu)

    scale = HEAD_DIM ** -0.5
    onorm = on_ref[...]

    for s, c, h in problems:
        p, r, lo = pid(s, c, h), row0(s, c), h * HEAD_DIM
        rows = slice(r, r + C)
        q = act[rows, lo:lo + HEAD_DIM]
        k = act[rows, W + lo:W + lo + HEAD_DIM]
        v = act[rows, 2 * W + lo:2 * W + lo + HEAD_DIM]
        q = q * lax.rsqrt(jnp.sum(q * q, axis=-1, keepdims=True) + EPS) * scale
        k = k * lax.rsqrt(jnp.sum(k * k, axis=-1, keepdims=True) + EPS)
        gc_c = gcc_scr[rows, h:h + 1]
        beta_c = beta_scr[rows, h:h + 1]
        k_beta = k * beta_c
        gram_scr[p] = _dot_nt(jnp.concatenate([k_beta, q], axis=0).astype(BF16), k.astype(BF16))
        e_gc = jnp.exp(gc_c)
        rhs_scr[p] = jnp.concatenate([v * beta_c, k_beta * e_gc], axis=1)
        wq_scr[p, C:2 * C, :] = q * e_gc
        kout_scr[p] = k * jnp.exp(gc_c[C - 1:C, :] - gc_c)

    for s, c, h in problems:
        p, r = pid(s, c, h), row0(s, c)
        gc_c = gcc_scr[r:r + C, h:h + 1]
        gc_r = gcr_scr[s * n_chunks + c, h:h + 1, :]
        decay = jnp.exp(jnp.where(incl, gc_c - gc_r, -jnp.inf))
        gram = gram_scr[p]
        lower = jnp.where(strict, gram[:C] * decay, 0.0)
        low_scr[p] = lower
        attn_scr[p] = gram[C:] * decay
        inv_scr[p] = eye - lower
        mb = (-lower).astype(BF16)
        pow_scr[p] = _dot(mb, mb)

    power = 2
    while power < C:
        last = 2 * power >= C
        for s, c, h in problems:
            p = pid(s, c, h)
            pw = pow_scr[p]
            inv = inv_scr[p]
            pb = pw.astype(BF16)
            if last:
                inv_scr[p] = inv + _dot(inv.astype(BF16), pb)
            else:
                both = _dot(jnp.concatenate([inv, pw], axis=0).astype(BF16), pb)
                inv_scr[p] = inv + both[:C]
                pow_scr[p] = both[C:]
        power *= 2

    for s, c, h in problems:
        p = pid(s, c, h)
        x0_scr[p] = _dot(inv_scr[p].astype(BF16), rhs_scr[p].astype(BF16))
    for s, c, h in problems:
        p = pid(s, c, h)
        x0 = x0_scr[p]
        x_hi, x_lo = _split_bf16(x0)
        l_hi, l_lo = _split_bf16(low_scr[p])
        lx = _dot(l_hi, jnp.concatenate([x_hi, x_lo], axis=1))
        lx = lx[:, :2 * HEAD_DIM] + (lx[:, 2 * HEAD_DIM:] + _dot(l_lo, x_hi))
        rhs_scr[p] = rhs_scr[p] - x0 - lx
    for s, c, h in problems:
        p = pid(s, c, h)
        sol = x0_scr[p] + _dot(inv_scr[p].astype(BF16), rhs_scr[p].astype(BF16))
        u_scr[p] = sol[:, :HEAD_DIM]
        wq_scr[p, 0:C, :] = sol[:, HEAD_DIM:]

    for c in range(n_chunks):
        for s in range(n_seq):
            for h in range(N_HEADS):
                sh = s * N_HEADS + h
                ws_scr[sh] = _dot(wq_scr[pid(s, c, h)].astype(BF16), s_scr[sh].astype(BF16))
        for s in range(n_seq):
            for h in range(N_HEADS):
                p, r, lo, sh = pid(s, c, h), row0(s, c), h * HEAD_DIM, s * N_HEADS + h
                rows = slice(r, r + C)
                ws = ws_scr[sh]
                vn16 = (u_scr[p] - ws[:C]).astype(BF16)
                o = ws[C:] + _dot(attn_scr[p].astype(BF16), vn16)
                g_last = gcc_scr[r + C - 1:r + C, h:h + 1]
                s_scr[sh] = s_scr[sh] * jnp.exp(g_last) + _dot_tn(kout_scr[p].astype(BF16), vn16)
                z = main_ref[rows, 3 * W + lo:3 * W + lo + HEAD_DIM]
                og_ref[rows, lo:lo + HEAD_DIM] = _rms(o, onorm) * _silu(z)

    @pl.when(t == pl.num_programs(1) - 1)
    def _():
        for s in range(n_seq):
            for h in range(N_HEADS):
                sf_ref[s, h] = s_scr[s * N_HEADS + h]


def _gdn_recurrence(main, ab, conv_w, a_log, dt_bias, out_norm, cb0, s0, *, batch, seq, n_chunks, n_seq):
    n = main.shape[0]
    chunk = min(MAX_CHUNK, seq)
    tb = chunk * n_chunks
    assert seq % tb == 0 and batch % n_seq == 0 and seq >= GD_CONV - 1
    assert n_seq == 1 or tb == seq, "several sequences per step only when a step covers whole sequences"
    nt = seq // tb
    rows = n_seq * tb
    n_prob = n_seq * n_chunks * N_HEADS
    has_s0 = s0 is not None
    abt = ab.reshape(batch, seq, 2 * N_HEADS).transpose(0, 2, 1)
    state_spec = pl.BlockSpec((n_seq, N_HEADS, HEAD_DIM, HEAD_DIM), lambda b, t: (b, 0, 0, 0))
    cb_spec = pl.BlockSpec((n_seq, GD_CONV - 1, GD_QKV), lambda b, t: (b, 0, 0))
    small = lambda shape: pl.BlockSpec(shape, lambda b, t: (0,) * len(shape))
    in_specs = [pl.BlockSpec((rows, main.shape[1]), lambda b, t: (b * nt + t, 0)),
                pl.BlockSpec((rows, 2 * N_HEADS), lambda b, t: (b * nt + t, 0)),
                pl.BlockSpec((n_seq, 2 * N_HEADS, tb), lambda b, t: (b, 0, t)),
                small((GD_CONV, GD_QKV)), small((1, N_HEADS)), small((1, N_HEADS)),
                small((N_HEADS, 1)), small((N_HEADS, 1)), small((1, HEAD_DIM))]
    args = [main, ab, abt, conv_w, a_log.reshape(1, N_HEADS), dt_bias.reshape(1, N_HEADS),
            a_log.reshape(N_HEADS, 1), dt_bias.reshape(N_HEADS, 1), out_norm.reshape(1, HEAD_DIM)]
    if has_s0:
        in_specs += [cb_spec, state_spec]
        args += [cb0, s0]
    vm = lambda *shape: pltpu.VMEM(shape, F32)
    return pl.pallas_call(
        functools.partial(_gdn_kernel, chunk=chunk, n_chunks=n_chunks, n_seq=n_seq, has_s0=has_s0),
        grid=(batch // n_seq, nt),
        in_specs=in_specs,
        out_specs=[pl.BlockSpec((rows, D_MODEL), lambda b, t: (b * nt + t, 0)), state_spec, cb_spec],
        out_shape=[jax.ShapeDtypeStruct((n, D_MODEL), F32),
                   jax.ShapeDtypeStruct((batch, N_HEADS, HEAD_DIM, HEAD_DIM), F32),
                   jax.ShapeDtypeStruct((batch, GD_CONV - 1, GD_QKV), F32)],
        scratch_shapes=[vm(n_seq * N_HEADS, HEAD_DIM, HEAD_DIM),
                        vm(n_seq, tb + SUBLANES, GD_QKV),
                        vm(rows, GD_QKV),
                        vm(rows, N_HEADS),
                        vm(n_seq * n_chunks, N_HEADS, chunk),
                        vm(rows, N_HEADS),
                        vm(n_prob, 2 * chunk, chunk),
                        vm(n_prob, chunk, 2 * HEAD_DIM),
                        vm(n_prob, 2 * chunk, HEAD_DIM),
                        vm(n_prob, chunk, HEAD_DIM),
                        vm(n_prob, chunk, chunk),
                        vm(n_prob, chunk, chunk),
                        vm(n_prob, chunk, chunk),
                        vm(n_prob, chunk, chunk),
                        vm(n_prob, chunk, 2 * HEAD_DIM),
                        vm(n_prob, chunk, HEAD_DIM),
                        vm(n_seq * N_HEADS, 2 * chunk, HEAD_DIM)],
        compiler_params=_params("parallel", "arbitrary"),
        name="gdn_recurrence",
    )(*args)


def _swiglu_rows(x_of, wg, wu, wd, n_sub, emit):
    pending = None
    for sb in range(n_sub + 1):
        if sb < n_sub:
            x = x_of(sb)
            nxt = (_dot(x, wg), _dot(x, wu))
        if pending is not None:
            a, u = pending
            emit(sb - 1, _dot((_silu(a) * u).astype(BF16), wd))
        pending = nxt


def _ffn_kernel(x_ref, og_ref, wo_ref, g_ref, wg_ref, wu_ref, wd_ref, o_ref, h_scr, *, n_sub):
    j = pl.program_id(1)
    sub = x_ref.shape[0] // n_sub

    @pl.when(j == 0)
    def _():
        x = x_ref[...] + _dot(og_ref[...].astype(BF16), wo_ref[...])
        h_scr[...] = _rms(x, g_ref[...]).astype(BF16)
        o_ref[...] = x

    def emit(sb, y):
        o_ref[sb * sub:(sb + 1) * sub, :] += y

    _swiglu_rows(lambda sb: h_scr[sb * sub:(sb + 1) * sub, :], wg_ref[...], wu_ref[...], wd_ref[...], n_sub, emit)


def _ffn(x, og, w_out, gain, wg, wu, wd, *, tm, tf, n_sub):
    n, d = x.shape
    f = wg.shape[1]
    rows = pl.BlockSpec((tm, d), lambda i, j: (i, 0))
    return pl.pallas_call(
        functools.partial(_ffn_kernel, n_sub=n_sub),
        grid=(n // tm, f // tf),
        in_specs=[rows, rows,
                  pl.BlockSpec((d, d), lambda i, j: (0, 0)),
                  pl.BlockSpec((1, d), lambda i, j: (0, 0)),
                  pl.BlockSpec((d, tf), lambda i, j: (0, j)),
                  pl.BlockSpec((d, tf), lambda i, j: (0, j)),
                  pl.BlockSpec((tf, d), lambda i, j: (j, 0))],
        out_specs=rows,
        out_shape=jax.ShapeDtypeStruct((n, d), F32),
        scratch_shapes=[pltpu.VMEM((tm, d), BF16)],
        compiler_params=_params("parallel", "arbitrary"),
        name="dense_ffn",
    )(x, og, w_out, gain.reshape(1, d), wg, wu, wd)


ROUTER_SLOTS = 128
PAD_ROWS = 256


def _run_copies(n_rows, max_rows, make_copy):
    pieces = []
    off = jnp.int32(0)
    bit = max_rows
    while bit >= 1:
        has = (n_rows & bit) != 0
        pieces.append((has, make_copy(off, bit)))
        off = off + (n_rows & bit)
        bit //= 2
    return pieces


def _router_kernel(x_ref, og_ref, wo_ref, g_ref, wr_ref, xo_ref, dest_ref, gate_ref, cnt_ref, xs_hbm,
                   cnt_scr, stage, zeros_buf, n_smem, base_smem, sem, *, tm, cap):
    i = pl.program_id(0)
    last = pl.num_programs(0) - 1

    @pl.when(i == 0)
    def _():
        cnt_scr[...] = jnp.zeros_like(cnt_scr)
        zeros_buf[...] = jnp.zeros_like(zeros_buf)
        for e in range(N_EXPERTS):
            n_smem[e] = 0
            base_smem[e] = e * cap

    x = x_ref[...] + _dot(og_ref[...].astype(BF16), wo_ref[...])
    xo_ref[...] = x
    h = _rms(x, g_ref[...])
    h16 = h.astype(BF16)
    h_lo = (h - h16.astype(F32)).astype(BF16)
    w_hi, w_lo = _split_bf16(wr_ref[...])
    logits = _dot(h16, w_hi) + (_dot(h16, w_lo) + _dot(h_lo, w_hi))
    lane = lax.broadcasted_iota(jnp.int32, logits.shape, 1)
    m1 = jnp.max(logits, axis=-1, keepdims=True)
    i1 = jnp.min(jnp.where(logits == m1, lane, N_EXPERTS), axis=-1, keepdims=True)
    rest = jnp.where(lane == i1, -jnp.inf, logits)
    m2 = jnp.max(rest, axis=-1, keepdims=True)
    i2 = jnp.min(jnp.where(rest == m2, lane, N_EXPERTS), axis=-1, keepdims=True)
    e2 = jnp.exp(m2 - m1)
    den = 1.0 + e2
    two = lax.broadcasted_iota(jnp.int32, (tm, TOP_K), 1)
    gate_ref[...] = jnp.where(two == 0, 1.0 / den, e2 / den)

    onehot = ((lane == i1) | (lane == i2)).astype(F32)
    row = lax.broadcasted_iota(jnp.int32, (tm, tm), 0)
    col = lax.broadcasted_iota(jnp.int32, (tm, tm), 1)
    local = _dot((row > col).astype(BF16), onehot.astype(BF16))
    rank = local + cnt_scr[...]
    r1 = jnp.sum(jnp.where(lane == i1, rank, 0.0), axis=-1, keepdims=True).astype(jnp.int32)
    r2 = jnp.sum(jnp.where(lane == i2, rank, 0.0), axis=-1, keepdims=True).astype(jnp.int32)
    dest_ref[...] = jnp.where(two == 0, i1 * cap + r1, i2 * cap + r2)
    cnt_tile = jnp.sum(onehot, axis=0, keepdims=True)
    cnt = cnt_scr[...] + cnt_tile
    cnt_scr[...] = cnt
    cnt_ref[...] = cnt.astype(jnp.int32)

    def wait_runs():
        for e in range(N_EXPERTS):
            for has, copy in _run_copies(n_smem[e], tm, lambda off, size, e=e: pltpu.make_async_copy(
                    stage.at[e, pl.ds(0, size * SUBLANES)], xs_hbm.at[pl.ds(0, size * SUBLANES)], sem)):
                @pl.when(has)
                def _():
                    copy.wait()

    wait_runs()

    code = (onehot * (local + 1.0)).astype(BF16)
    code_t = _dot_tn(code, (row == col).astype(BF16))
    slot_id = lax.broadcasted_iota(jnp.int32, (ROUTER_SLOTS, tm), 0)
    for e in range(N_EXPERTS):
        n_e = cnt_tile[0, e].astype(jnp.int32)
        code_e = code_t[e:e + 1, :].astype(jnp.int32)
        for q in range(tm // ROUTER_SLOTS):
            @pl.when(q * ROUTER_SLOTS < n_e)
            def _():
                sel = jnp.where(code_e == slot_id + (q * ROUTER_SLOTS + 1), 1.0, 0.0).astype(BF16)
                _store_token_tiles(stage.at[e], q * ROUTER_SLOTS, _dot(sel, h16))
        base = base_smem[e]
        for has, copy in _run_copies(n_e, tm, lambda off, size, e=e, base=base: pltpu.make_async_copy(
                stage.at[e, pl.ds(pl.multiple_of(off * SUBLANES, SUBLANES), size * SUBLANES)],
                xs_hbm.at[pl.ds(pl.multiple_of((base + off) * SUBLANES, SUBLANES), size * SUBLANES)], sem)):
            @pl.when(has)
            def _():
                copy.start()
        n_smem[e] = n_e
        base_smem[e] = base + n_e

    @pl.when(i == last)
    def _():
        wait_runs()
        pads = [pltpu.make_async_copy(
            zeros_buf, xs_hbm.at[pl.ds(pl.multiple_of(base_smem[e] * SUBLANES, SUBLANES), PAD_ROWS * SUBLANES)], sem)
            for e in range(N_EXPERTS)]
        for pad in pads:
            pad.start()
        for pad in pads:
            pad.wait()


def _router(x, og, w_out, gain, w_router, *, tm, cap):
    n, d = x.shape
    two = pl.BlockSpec((tm, TOP_K), lambda i: (i, 0))
    rows = pl.BlockSpec((tm, d), lambda i: (i, 0))
    return pl.pallas_call(
        functools.partial(_router_kernel, tm=tm, cap=cap),
        grid=(n // tm,),
        in_specs=[rows, rows,
                  pl.BlockSpec((d, d), lambda i: (0, 0)),
                  pl.BlockSpec((1, d), lambda i: (0, 0)),
                  pl.BlockSpec((d, N_EXPERTS), lambda i: (0, 0))],
        out_specs=[rows, two, two, pl.BlockSpec((1, N_EXPERTS), lambda i: (0, 0)),
                   pl.BlockSpec(memory_space=pl.ANY)],
        out_shape=[jax.ShapeDtypeStruct((n, d), F32),
                   jax.ShapeDtypeStruct((n, TOP_K), jnp.int32),
                   jax.ShapeDtypeStruct((n, TOP_K), F32),
                   jax.ShapeDtypeStruct((1, N_EXPERTS), jnp.int32),
                   jax.ShapeDtypeStruct((N_EXPERTS * cap * SUBLANES, LANES), F32)],
        scratch_shapes=[pltpu.VMEM((1, N_EXPERTS), F32),
                        pltpu.VMEM((N_EXPERTS, tm * SUBLANES, LANES), F32),
                        pltpu.VMEM((PAD_ROWS * SUBLANES, LANES), F32),
                        pltpu.SMEM((N_EXPERTS,), jnp.int32), pltpu.SMEM((N_EXPERTS,), jnp.int32),
                        pltpu.SemaphoreType.DMA(())],
        compiler_params=_params("arbitrary"),
        name="moe_router",
    )(x, og, w_out, gain.reshape(1, d), w_router)


def _tile_gather_copy(src_hbm, dst, sem, src_row, dst_row):
    return pltpu.make_async_copy(src_hbm.at[pl.ds(pl.multiple_of(src_row * SUBLANES, SUBLANES), SUBLANES)],
                                 dst.at[pl.ds(pl.multiple_of(dst_row * SUBLANES, SUBLANES), SUBLANES)], sem)


def _load_token_tiles(ref, first_row, n_rows):
    return jnp.concatenate([ref[pl.ds(first_row * SUBLANES + s, n_rows, stride=SUBLANES), :]
                            for s in range(D_MODEL // LANES)], axis=1)


def _store_token_tiles(ref, first_row, x):
    for s in range(D_MODEL // LANES):
        ref[pl.ds(first_row * SUBLANES + s, x.shape[0], stride=SUBLANES), :] = x[:, s * LANES:(s + 1) * LANES]


def _expert_kernel(be_ref, xblk_ref, oblk_ref, nv_ref, nu_ref, xs_ref, wg_ref, wu_ref, wd_ref, yb_ref, xb, acc,
                   *, sub_rows, n_sub):
    i = pl.program_id(0)
    j = pl.program_id(1)
    n_valid = nv_ref[i]

    @pl.when(j == 0)
    def _():
        acc[...] = jnp.zeros_like(acc)
        for sb in range(n_sub):
            @pl.when(sb * sub_rows < n_valid)
            def _():
                xb[sb * sub_rows:(sb + 1) * sub_rows, :] = _load_token_tiles(
                    xs_ref, sb * sub_rows, sub_rows).astype(BF16)

    def emit(sb, y):
        acc[sb * sub_rows:(sb + 1) * sub_rows, :] += y

    def x_of(sb):
        return xb[sb * sub_rows:(sb + 1) * sub_rows, :]

    full = n_valid > (n_sub - 1) * sub_rows

    @pl.when(full)
    def _():
        _swiglu_rows(x_of, wg_ref[0], wu_ref[0], wd_ref[0], n_sub, emit)

    for sb in range(n_sub - 1):
        @pl.when(jnp.logical_and(jnp.logical_not(full), sb * sub_rows < n_valid))
        def _():
            _swiglu_rows(lambda _, sb=sb: x_of(sb), wg_ref[0], wu_ref[0], wd_ref[0], 1, lambda _, y, sb=sb: emit(sb, y))

    @pl.when(j == pl.num_programs(1) - 1)
    def _():
        for sb in range(n_sub):
            rows = slice(sb * sub_rows, (sb + 1) * sub_rows)

            @pl.when(sb * sub_rows < n_valid)
            def _():
                _store_token_tiles(yb_ref, sb * sub_rows, acc[rows, :])

            @pl.when(sb * sub_rows >= n_valid)
            def _():
                tiles = slice(sb * sub_rows * SUBLANES, (sb + 1) * sub_rows * SUBLANES)
                yb_ref[tiles, :] = jnp.zeros((sub_rows * SUBLANES, LANES), F32)


def _experts(xs, block_e, x_block, out_block, n_valid, n_used, wg, wu, wd, *, sub_rows, n_sub, tf, out_blocks):
    d = D_MODEL
    tmm = sub_rows * n_sub
    nb = block_e.shape[0]
    nf = D_FF // tf

    def wcol(i, j, be, xblk, oblk, nv, nu):
        return (be[i], 0, jnp.where(i < nu[0], j, nf - 1))

    def wrow(i, j, be, xblk, oblk, nv, nu):
        return (be[i], jnp.where(i < nu[0], j, nf - 1), 0)

    grid_spec = pltpu.PrefetchScalarGridSpec(
        num_scalar_prefetch=5,
        grid=(nb, nf),
        in_specs=[pl.BlockSpec((tmm * SUBLANES, LANES), lambda i, j, be, xblk, oblk, nv, nu: (xblk[i], 0)),
                  pl.BlockSpec((1, d, tf), wcol),
                  pl.BlockSpec((1, d, tf), wcol),
                  pl.BlockSpec((1, tf, d), wrow)],
        out_specs=pl.BlockSpec((tmm * SUBLANES, LANES), lambda i, j, be, xblk, oblk, nv, nu: (oblk[i], 0)),
        scratch_shapes=[pltpu.VMEM((tmm, d), BF16), pltpu.VMEM((tmm, d), F32)],
    )
    return pl.pallas_call(
        functools.partial(_expert_kernel, sub_rows=sub_rows, n_sub=n_sub),
        grid_spec=grid_spec,
        out_shape=jax.ShapeDtypeStruct((out_blocks * tmm * SUBLANES, LANES), F32),
        compiler_params=_params("arbitrary", "arbitrary"),
        name="moe_experts",
    )(block_e, x_block, out_block, n_valid, n_used, xs, wg, wu, wd)


def _combine_kernel(dest_ref, yb_hbm, x_ref, gate_ref, gain_ref, o_ref, r0, r1, sems, *, tm):
    i = pl.program_id(0)
    slot = lax.rem(i, 2)

    def start_gather(tile, dst_slot):
        def issue(r, carry):
            first = TOP_K * (tile * tm + r)
            _tile_gather_copy(yb_hbm, r0.at[dst_slot], sems.at[dst_slot, 0], dest_ref[first], r).start(priority=0)
            _tile_gather_copy(yb_hbm, r1.at[dst_slot], sems.at[dst_slot, 1], dest_ref[first + 1], r).start(priority=1)
            return carry

        lax.fori_loop(0, tm, issue, 0, unroll=8)

    @pl.when(i == 0)
    def _():
        start_gather(0, 0)

    @pl.when(i + 1 < pl.num_programs(0))
    def _():
        start_gather(i + 1, 1 - slot)

    whole = pl.ds(0, tm * SUBLANES)
    pltpu.make_async_copy(yb_hbm.at[whole], r0.at[slot], sems.at[slot, 0]).wait()
    pltpu.make_async_copy(yb_hbm.at[whole], r1.at[slot], sems.at[slot, 1]).wait()
    g = gate_ref[...]
    y = _load_token_tiles(r0.at[slot], 0, tm) * g[:, 0:1] + _load_token_tiles(r1.at[slot], 0, tm) * g[:, 1:2]
    o_ref[...] = _rms(x_ref[...] + y, gain_ref[...])


def _combine(yb, dest_flat, x, gates, gain, *, tm):
    n, d = x.shape
    grid_spec = pltpu.PrefetchScalarGridSpec(
        num_scalar_prefetch=1,
        grid=(n // tm,),
        in_specs=[pl.BlockSpec(memory_space=pl.ANY),
                  pl.BlockSpec((tm, d), lambda i, dest: (i, 0)),
                  pl.BlockSpec((tm, TOP_K), lambda i, dest: (i, 0)),
                  pl.BlockSpec((1, d), lambda i, dest: (0, 0))],
        out_specs=pl.BlockSpec((tm, d), lambda i, dest: (i, 0)),
        scratch_shapes=[pltpu.VMEM((2, tm * SUBLANES, LANES), F32), pltpu.VMEM((2, tm * SUBLANES, LANES), F32),
                        pltpu.SemaphoreType.DMA((2, TOP_K))],
    )
    return pl.pallas_call(
        functools.partial(_combine_kernel, tm=tm),
        grid_spec=grid_spec,
        out_shape=jax.ShapeDtypeStruct((n, d), F32),
        compiler_params=_params("arbitrary"),
        name="moe_combine",
    )(dest_flat, yb, x, gates, gain.reshape(1, d))


def _moe_and_final_norm(x, og, w_out, gain, w_router, wg, wu, wd, final_gain, *, tiles):
    n = x.shape[0]
    sub_rows, n_sub = tiles["moe_sub_rows"], tiles["moe_subs"]
    tmm = sub_rows * n_sub
    assert sub_rows == PAD_ROWS and n % tmm == 0
    cap = n + tmm
    x, dest, gates, counts, xs = _router(x, og, w_out, gain, w_router, tm=tiles["router_rows"], cap=cap)
    counts = counts[0]
    blocks_per_e = (counts + tmm - 1) // tmm
    blocks_end = jnp.cumsum(blocks_per_e)
    n_used = blocks_end[-1].astype(jnp.int32)
    nb = -(-(n * TOP_K + N_EXPERTS * (tmm - 1)) // tmm)
    blk = jnp.minimum(jnp.arange(nb, dtype=jnp.int32), n_used - 1)
    block_e = jnp.minimum(jnp.searchsorted(blocks_end, blk, side="right"), N_EXPERTS - 1).astype(jnp.int32)
    k_in_e = blk - (blocks_end - blocks_per_e)[block_e]
    x_block = (block_e * (cap // tmm) + k_in_e).astype(jnp.int32)
    used = jnp.arange(nb, dtype=jnp.int32) < n_used
    n_valid = jnp.where(used, jnp.clip(counts[block_e] - k_in_e * tmm, 0, tmm), 0).astype(jnp.int32)
    dump_block = N_EXPERTS * (cap // tmm)
    out_block = jnp.where(used, x_block, dump_block).astype(jnp.int32)
    yb = _experts(xs, block_e, x_block, out_block, n_valid, n_used.reshape(1), wg, wu, wd,
                  sub_rows=sub_rows, n_sub=n_sub, tf=tiles["moe_ff_cols"], out_blocks=dump_block + 1)
    return _combine(yb, dest.reshape(-1), x, gates, final_gain, tm=tiles["combine_rows"])


def _tiles(n_rows, seq):
    big = n_rows >= 8192
    return {
        "proj_rows": 1024 if n_rows % 1024 == 0 else n_rows,
        "proj_cols": 1024,
        "rec_chunks": min(4, seq // min(MAX_CHUNK, seq)),
        "rec_seqs": 1 if seq > MAX_CHUNK else 4,
        "ffn_rows": 1024 if n_rows % 1024 == 0 else n_rows,
        "ff_cols": 512,
        "router_rows": 256,
        "moe_sub_rows": 256,
        "moe_subs": 4 if big else 2,
        "moe_ff_cols": 1792,
        "ffn_subs": 4 if big else 2,
        "combine_rows": 256,
    }


def _trunk(x3, hg_state, gd_state, gd_conv, p):
    batch, seq, d = x3.shape
    n = batch * seq
    x = x3.reshape(n, d)
    tiles = _tiles(n, seq)
    tm, tn = tiles["proj_rows"], tiles["proj_cols"]

    proj = _norm_proj(x, p["norm_mix"][0], p["hgrn_w_in"], tm=tm, tn=tn)[0]
    og, hg_new = _hgrn_recurrence(proj, p["hgrn_lb"], p["hgrn_norm"], hg_state, batch=batch, seq=seq, layer=0,
                                  n_chunks=tiles["rec_chunks"], n_seq=tiles["rec_seqs"])
    x = _ffn(x, og, p["hgrn_w_out"], p["norm_ffn"][0], p["ffn_w_gate"], p["ffn_w_up"], p["ffn_w_down"],
             tm=tiles["ffn_rows"], tf=tiles["ff_cols"], n_sub=tiles["ffn_subs"])

    main, ab = _norm_proj(x, p["norm_mix"][1], p["gdn_w_main"], p["gdn_w_ab"], tm=tm, tn=tn)
    og, gd_new, cv_new = _gdn_recurrence(main, ab, p["gdn_conv"], p["gdn_a_log"], p["gdn_dt_bias"], p["gdn_norm"],
                                         gd_conv, gd_state, batch=batch, seq=seq, n_chunks=tiles["rec_chunks"],
                                         n_seq=tiles["rec_seqs"])
    y = _moe_and_final_norm(x, og, p["gdn_w_out"], p["norm_ffn"][1], p["moe_router"], p["moe_w_gate"], p["moe_w_up"], p["moe_w_down"],
                            p["norm_out"], tiles=tiles)
    return y.reshape(batch, seq, d), hg_new[None], gd_new[None], cv_new[None]


def kernel(x_prompt, x_sample, state_hgrn, state_gdn, state_gdn_conv, norm_mix, norm_ffn, norm_out, hgrn_w_in, hgrn_lb, hgrn_norm, hgrn_w_out, gdn_w_in, gdn_conv, gdn_a_log, gdn_dt_bias, gdn_norm, gdn_w_out, ffn_w_gate, ffn_w_up, ffn_w_down, moe_router, moe_w_gate, moe_w_up, moe_w_down):
    assert hgrn_w_in.shape[0] == 1 and gdn_w_in.shape[0] == 1, "one HGRN2 layer and one gated-DeltaNet layer"
    main_cols = GD_QKV + N_HEADS * HEAD_DIM
    p = {
        "norm_mix": norm_mix, "norm_ffn": norm_ffn, "norm_out": norm_out,
        "hgrn_w_in": hgrn_w_in[0].astype(BF16), "hgrn_lb": hgrn_lb, "hgrn_norm": hgrn_norm[0],
        "hgrn_w_out": hgrn_w_out[0].astype(BF16),
        "gdn_w_main": gdn_w_in[0, :, :main_cols].astype(BF16), "gdn_w_ab": gdn_w_in[0, :, main_cols:].astype(BF16),
        "gdn_conv": gdn_conv[0], "gdn_a_log": gdn_a_log[0], "gdn_dt_bias": gdn_dt_bias[0], "gdn_norm": gdn_norm[0],
        "gdn_w_out": gdn_w_out[0].astype(BF16),
        "ffn_w_gate": ffn_w_gate[0].astype(BF16), "ffn_w_up": ffn_w_up[0].astype(BF16),
        "ffn_w_down": ffn_w_down[0].astype(BF16),
        "moe_router": moe_router[0],
        "moe_w_gate": moe_w_gate[0].astype(BF16), "moe_w_up": moe_w_up[0].astype(BF16),
        "moe_w_down": moe_w_down[0].astype(BF16),
    }
    y_p, hg_p, gd_p, cv_p = _trunk(x_prompt, None, None, None, p)
    y_s, hg_s, gd_s, cv_s = _trunk(x_sample, state_hgrn[0], state_gdn[0], state_gdn_conv[0], p)
    return (y_p, y_s, hg_p, hg_s, gd_p, gd_s, cv_p, cv_s)
```

```python
import functools

import jax
import jax.numpy as jnp
from jax import lax
from jax.experimental import pallas as pl
from jax.experimental.pallas import tpu as pltpu

F32 = jnp.float32
BF16 = jnp.bfloat16
HIGHEST = lax.Precision.HIGHEST

D_MODEL = 1024
N_HEADS = 8
HEAD_DIM = 128
D_FF = 3584
N_EXPERTS = 8
TOP_K = 2
GD_QKV = 3 * D_MODEL
GD_CONV = 4
MAX_CHUNK = 64
EPS = 1e-6
LANES = 128
SUBLANES = 8
VMEM_LIMIT_BYTES = 52 * 1024 * 1024


def _params(*sem):
    return pltpu.CompilerParams(dimension_semantics=sem, vmem_limit_bytes=VMEM_LIMIT_BYTES)


def _rms(x, gain):
    return x * lax.rsqrt(jnp.mean(x * x, axis=-1, keepdims=True) + EPS) * gain


def _sigmoid(x):
    t = jnp.exp(-jnp.abs(x))
    r = 1.0 / (1.0 + t)
    return jnp.where(x >= 0, r, t * r)


def _silu(x):
    return x * (0.5 * jnp.tanh(0.5 * x) + 0.5)


def _softplus(x):
    return jnp.maximum(x, 0.0) + jnp.log1p(jnp.exp(-jnp.abs(x)))


def _dot(a, b):
    return jnp.dot(a, b, preferred_element_type=F32)


def _dot_nt(a, b):
    return lax.dot_general(a, b, (((1,), (1,)), ((), ())), preferred_element_type=F32)


def _dot_tn(a, b):
    return lax.dot_general(a, b, (((0,), (0,)), ((), ())), preferred_element_type=F32)


def _dot_hi(a, b):
    return jnp.dot(a, b, precision=HIGHEST, preferred_element_type=F32)


def _norm_proj_kernel(x_ref, g_ref, w_ref, *rest, has_small):
    if has_small:
        ws_ref, o_ref, os_ref, h_scr = rest
    else:
        o_ref, h_scr = rest

    @pl.when(pl.program_id(1) == 0)
    def _():
        hb = _rms(x_ref[...], g_ref[...]).astype(BF16)
        h_scr[...] = hb
        if has_small:
            os_ref[...] = _dot(hb, ws_ref[...])

    o_ref[...] = _dot(h_scr[...], w_ref[...])


def _norm_proj(x, gain, w, w_small=None, *, tm, tn):
    n, d = x.shape
    e = w.shape[1]
    has_small = w_small is not None
    in_specs = [pl.BlockSpec((tm, d), lambda i, j: (i, 0)),
                pl.BlockSpec((1, d), lambda i, j: (0, 0)),
                pl.BlockSpec((d, tn), lambda i, j: (0, j))]
    out_shape = [jax.ShapeDtypeStruct((n, e), F32)]
    out_specs = [pl.BlockSpec((tm, tn), lambda i, j: (i, j))]
    args = [x, gain.reshape(1, d), w]
    if has_small:
        es = w_small.shape[1]
        in_specs.append(pl.BlockSpec((d, es), lambda i, j: (0, 0)))
        out_shape.append(jax.ShapeDtypeStruct((n, es), F32))
        out_specs.append(pl.BlockSpec((tm, es), lambda i, j: (i, 0)))
        args.append(w_small)
    return pl.pallas_call(
        functools.partial(_norm_proj_kernel, has_small=has_small),
        grid=(n // tm, e // tn),
        in_specs=in_specs, out_specs=out_specs, out_shape=out_shape,
        scratch_shapes=[pltpu.VMEM((tm, d), BF16)],
        compiler_params=_params("parallel", "arbitrary"),
        name="norm_proj",
    )(*args)


def _hgrn_kernel(*refs, chunk, n_chunks, n_seq, layer, has_s0):
    if has_s0:
        proj_ref, lb_ref, on_ref, s0_ref, og_ref, sf_ref, *scr = refs
    else:
        proj_ref, lb_ref, on_ref, og_ref, sf_ref, *scr = refs
    st_scr, b_scr, q_scr, k_scr, a_scr, qin_scr, kout_scr, dec_scr, oi_scr = scr
    C = chunk
    W = N_HEADS * HEAD_DIM
    tb = C * n_chunks
    t = pl.program_id(1)
    problems = [(s, c, h) for s in range(n_seq) for c in range(n_chunks) for h in range(N_HEADS)]

    def pid(s, c, h):
        return (s * n_chunks + c) * N_HEADS + h

    def row0(s, c):
        return s * tb + c * C

    @pl.when(t == 0)
    def _():
        for s in range(n_seq):
            for h in range(N_HEADS):
                st_scr[s * N_HEADS + h] = s0_ref[s, h].T if has_s0 else jnp.zeros((HEAD_DIM, HEAD_DIM), F32)

    lbraw = lb_ref[...]
    ex = jnp.exp(lbraw - jnp.max(lbraw, axis=0, keepdims=True))
    sm = ex / jnp.sum(ex, axis=0, keepdims=True)
    lb = sm[0:1]
    for j in range(1, layer + 1):
        lb = lb + sm[j:j + 1]

    row = lax.broadcasted_iota(jnp.int32, (C, C), 0)
    col = lax.broadcasted_iota(jnp.int32, (C, C), 1)
    causal = row >= col
    tril = causal.astype(BF16)
    mid = C // 2
    scale = HEAD_DIM ** -0.5
    onorm = on_ref[...]

    for s in range(n_seq):
        for c in range(n_chunks):
            rows = slice(row0(s, c), row0(s, c) + C)
            f_raw = proj_ref[rows, W:2 * W]
            tt = jnp.exp(-jnp.abs(f_raw))
            rr = 1.0 / (1.0 + tt)
            pos = f_raw >= 0
            sig = jnp.where(pos, rr, tt * rr)
            sig_neg = jnp.where(pos, tt * rr, rr)
            log_f = jnp.log(lb + (1.0 - lb) * sig)
            k_scr[rows, :] = (1.0 - lb) * sig_neg
            q_scr[rows, :] = _silu(proj_ref[rows, 0:W]) * scale
            hi = log_f.astype(BF16)
            rest = log_f - hi.astype(F32)
            md = rest.astype(BF16)
            lw = (rest - md.astype(F32)).astype(BF16)
            b_scr[rows, :] = (_dot(tril, lw) + _dot(tril, md)) + _dot(tril, hi)

    for s, c, h in problems:
        p, lo = pid(s, c, h), h * HEAD_DIM
        rows = slice(row0(s, c), row0(s, c) + C)
        b = b_scr[rows, lo:lo + HEAD_DIM]
        q = q_scr[rows, lo:lo + HEAD_DIM]
        k = k_scr[rows, lo:lo + HEAD_DIM]
        b_mid = b[mid:mid + 1]
        b_last = b[C - 1:C]
        a_scr[p] = _dot_nt((q * jnp.exp(b - b_mid)).astype(BF16), (k * jnp.exp(b_mid - b)).astype(BF16))
        qin_scr[p] = q * jnp.exp(b)
        kout_scr[p] = k * jnp.exp(b_last - b)
        dec_scr[p] = jnp.exp(b_last)

    for s, c, h in problems:
        p, lo = pid(s, c, h), h * HEAD_DIM
        rows = slice(row0(s, c), row0(s, c) + C)
        a = jnp.where(causal, a_scr[p], 0.0)
        oi_scr[p] = _dot(a.astype(BF16), proj_ref[rows, 2 * W + lo:2 * W + lo + HEAD_DIM].astype(BF16))

    for c in range(n_chunks):
        for s in range(n_seq):
            for h in range(N_HEADS):
                p, lo, sh = pid(s, c, h), h * HEAD_DIM, s * N_HEADS + h
                rows = slice(row0(s, c), row0(s, c) + C)
                st = st_scr[sh]
                o = oi_scr[p] + _dot_nt(qin_scr[p].astype(BF16), st.astype(BF16))
                vb = proj_ref[rows, 2 * W + lo:2 * W + lo + HEAD_DIM].astype(BF16)
                st_scr[sh] = st * dec_scr[p] + _dot_tn(vb, kout_scr[p].astype(BF16))
                gate = proj_ref[rows, 3 * W + lo:3 * W + lo + HEAD_DIM]
                og_ref[rows, lo:lo + HEAD_DIM] = _rms(o, onorm) * _silu(gate)

    @pl.when(t == pl.num_programs(1) - 1)
    def _():
        for s in range(n_seq):
            for h in range(N_HEADS):
                sf_ref[s, h] = st_scr[s * N_HEADS + h].T


def _hgrn_recurrence(proj, lb_raw, out_norm, s0, *, batch, seq, layer, n_chunks, n_seq):
    n = proj.shape[0]
    chunk = min(MAX_CHUNK, seq)
    tb = chunk * n_chunks
    assert seq % tb == 0 and batch % n_seq == 0
    assert n_seq == 1 or tb == seq, "several sequences per step only when a step covers whole sequences"
    nt = seq // tb
    rows = n_seq * tb
    n_prob = n_seq * n_chunks * N_HEADS
    width = N_HEADS * HEAD_DIM
    has_s0 = s0 is not None
    state_spec = pl.BlockSpec((n_seq, N_HEADS, HEAD_DIM, HEAD_DIM), lambda b, t: (b, 0, 0, 0))
    in_specs = [pl.BlockSpec((rows, proj.shape[1]), lambda b, t: (b * nt + t, 0)),
                pl.BlockSpec(lb_raw.shape, lambda b, t: (0, 0)),
                pl.BlockSpec((1, HEAD_DIM), lambda b, t: (0, 0))]
    args = [proj, lb_raw, out_norm.reshape(1, HEAD_DIM)]
    if has_s0:
        in_specs.append(state_spec)
        args.append(s0)
    vm = lambda *shape: pltpu.VMEM(shape, F32)
    return pl.pallas_call(
        functools.partial(_hgrn_kernel, chunk=chunk, n_chunks=n_chunks, n_seq=n_seq, layer=layer, has_s0=has_s0),
        grid=(batch // n_seq, nt),
        in_specs=in_specs,
        out_specs=[pl.BlockSpec((rows, D_MODEL), lambda b, t: (b * nt + t, 0)), state_spec],
        out_shape=[jax.ShapeDtypeStruct((n, D_MODEL), F32),
                   jax.ShapeDtypeStruct((batch, N_HEADS, HEAD_DIM, HEAD_DIM), F32)],
        scratch_shapes=[vm(n_seq * N_HEADS, HEAD_DIM, HEAD_DIM),
                        vm(rows, width),
                        vm(rows, width),
                        vm(rows, width),
                        vm(n_prob, chunk, chunk),
                        vm(n_prob, chunk, HEAD_DIM),
                        vm(n_prob, chunk, HEAD_DIM),
                        vm(n_prob, 1, HEAD_DIM),
                        vm(n_prob, chunk, HEAD_DIM)],
        compiler_params=_params("parallel", "arbitrary"),
        name="hgrn_recurrence",
    )(*args)


def _split_bf16(x):
    hi = x.astype(BF16)
    return hi, (x - hi.astype(F32)).astype(BF16)


def _gdn_kernel(*refs, chunk, n_chunks, n_seq, has_s0):
    (main_ref, ab_ref, abt_ref, cw_ref, al_ref, dtb_ref, alt_ref, dtbt_ref, on_ref), rest = refs[:9], refs[9:]
    if has_s0:
        (cb0_ref, s0_ref), rest = rest[:2], rest[2:]
    (og_ref, sf_ref, cb_ref, s_scr, xpad, act, gcc_scr, gcr_scr, beta_scr, gram_scr, rhs_scr, wq_scr, kout_scr,
     inv_scr, pow_scr, low_scr, attn_scr, x0_scr, u_scr, ws_scr) = rest
    C = chunk
    W = N_HEADS * HEAD_DIM
    tb = C * n_chunks
    t = pl.program_id(1)
    pad = SUBLANES
    problems = [(s, c, h) for s in range(n_seq) for c in range(n_chunks) for h in range(N_HEADS)]

    def pid(s, c, h):
        return (s * n_chunks + c) * N_HEADS + h

    def row0(s, c):
        return s * tb + c * C

    @pl.when(t == 0)
    def _():
        for s in range(n_seq):
            for h in range(N_HEADS):
                s_scr[s * N_HEADS + h] = s0_ref[s, h] if has_s0 else jnp.zeros((HEAD_DIM, HEAD_DIM), F32)
            xpad[s, 0:pad, :] = jnp.zeros((pad, GD_QKV), F32)
            if has_s0:
                xpad[s, pad - (GD_CONV - 1):pad, :] = cb0_ref[s]

    @pl.when(t > 0)
    def _():
        for s in range(n_seq):
            xpad[s, 0:pad, :] = xpad[s, tb:tb + pad, :]

    for s in range(n_seq):
        for j in range(GD_QKV // LANES):
            cs = slice(j * LANES, (j + 1) * LANES)
            xpad[s, pad:pad + tb, cs] = main_ref[s * tb:(s + 1) * tb, cs]
    for s in range(n_seq):
        for j in range(GD_QKV // LANES):
            cs = slice(j * LANES, (j + 1) * LANES)
            conv = xpad[s, pad - 3:pad - 3 + tb, cs] * cw_ref[0:1, cs]
            for tap in range(1, GD_CONV):
                conv = conv + xpad[s, pad - 3 + tap:pad - 3 + tap + tb, cs] * cw_ref[tap:tap + 1, cs]
            act[s * tb:(s + 1) * tb, cs] = _silu(conv)

    @pl.when(t == pl.num_programs(1) - 1)
    def _():
        for s in range(n_seq):
            cb_ref[s] = xpad[s, pad + tb - (GD_CONV - 1):pad + tb, :]

    row = lax.broadcasted_iota(jnp.int32, (C, C), 0)
    col = lax.broadcasted_iota(jnp.int32, (C, C), 1)
    incl = row >= col
    strict = row > col
    eye = jnp.where(row == col, 1.0, 0.0)
    tril = incl.astype(F32)
    triu = (row <= col).astype(F32)
    ab = ab_ref[...]
    g_col = -jnp.exp(al_ref[...]) * _softplus(ab[:, 0:N_HEADS] + dtb_ref[...])
    beta_scr[...] = _sigmoid(ab[:, N_HEADS:2 * N_HEADS])
    for s in range(n_seq):
        g_row = -jnp.exp(alt_ref[...]) * _softplus(abt_ref[s][0:N_HEADS, :] + dtbt_ref[...])
        for c in range(n_chunks):
            r = row0(s, c)
            gcc_scr[r:r + C, :] = _dot_hi(tril, g_col[r:r + C, :])
            gcr_scr[s * n_chunks + c] = _dot_hi(g_row[:, c * C:(c + 1) * C], triu)

    scale = HEAD_DIM ** -0.5
    onorm = on_ref[...]

    for s, c, h in problems:
        p, r, lo = pid(s, c, h), row0(s, c), h * HEAD_DIM
        rows = slice(r, r + C)
        q = act[rows, lo:lo + HEAD_DIM]
        k = act[rows, W + lo:W + lo + HEAD_DIM]
        v = act[rows, 2 * W + lo:2 * W + lo + HEAD_DIM]
        q = q * lax.rsqrt(jnp.sum(q * q, axis=-1, keepdims=True) + EPS) * scale
        k = k * lax.rsqrt(jnp.sum(k * k, axis=-1, keepdims=True) + EPS)
        gc_c = gcc_scr[rows, h:h + 1]
        beta_c = beta_scr[rows, h:h + 1]
        k_beta = k * beta_c
        gram_scr[p] = _dot_nt(jnp.concatenate([k_beta, q], axis=0).astype(BF16), k.astype(BF16))
        e_gc = jnp.exp(gc_c)
        rhs_scr[p] = jnp.concatenate([v * beta_c, k_beta * e_gc], axis=1)
        wq_scr[p, C:2 * C, :] = q * e_gc
        kout_scr[p] = k * jnp.exp(gc_c[C - 1:C, :] - gc_c)

    for s, c, h in problems:
        p, r = pid(s, c, h), row0(s, c)
        gc_c = gcc_scr[r:r + C, h:h + 1]
        gc_r = gcr_scr[s * n_chunks + c, h:h + 1, :]
        decay = jnp.exp(jnp.where(incl, gc_c - gc_r, -jnp.inf))
        gram = gram_scr[p]
        lower = jnp.where(strict, gram[:C] * decay, 0.0)
        low_scr[p] = lower
        attn_scr[p] = gram[C:] * decay
        inv_scr[p] = eye - lower
        mb = (-lower).astype(BF16)
        pow_scr[p] = _dot(mb, mb)

    power = 2
    while power < C:
        last = 2 * power >= C
        for s, c, h in problems:
            p = pid(s, c, h)
            pw = pow_scr[p]
            inv = inv_scr[p]
            pb = pw.astype(BF16)
            if last:
                inv_scr[p] = inv + _dot(inv.astype(BF16), pb)
            else:
                both = _dot(jnp.concatenate([inv, pw], axis=0).astype(BF16), pb)
                inv_scr[p] = inv + both[:C]
                pow_scr[p] = both[C:]
        power *= 2

    for s, c, h in problems:
        p = pid(s, c, h)
        x0_scr[p] = _dot(inv_scr[p].astype(BF16), rhs_scr[p].astype(BF16))
    for s, c, h in problems:
        p = pid(s, c, h)
        x0 = x0_scr[p]
        x_hi, x_lo = _split_bf16(x0)
        l_hi, l_lo = _split_bf16(low_scr[p])
        lx = _dot(l_hi, jnp.concatenate([x_hi, x_lo], axis=1))
        lx = lx[:, :2 * HEAD_DIM] + (lx[:, 2 * HEAD_DIM:] + _dot(l_lo, x_hi))
        rhs_scr[p] = rhs_scr[p] - x0 - lx
    for s, c, h in problems:
        p = pid(s, c, h)
        sol = x0_scr[p] + _dot(inv_scr[p].astype(BF16), rhs_scr[p].astype(BF16))
        u_scr[p] = sol[:, :HEAD_DIM]
        wq_scr[p, 0:C, :] = sol[:, HEAD_DIM:]

    for c in range(n_chunks):
        for s in range(n_seq):
            for h in range(N_HEADS):
                sh = s * N_HEADS + h
                ws_scr[sh] = _dot(wq_scr[pid(s, c, h)].astype(BF16), s_scr[sh].astype(BF16))
        for s in range(n_seq):
            for h in range(N_HEADS):
                p, r, lo, sh = pid(s, c, h), row0(s, c), h * HEAD_DIM, s * N_HEADS + h
                rows = slice(r, r + C)
                ws = ws_scr[sh]
                vn16 = (u_scr[p] - ws[:C]).astype(BF16)
                o = ws[C:] + _dot(attn_scr[p].astype(BF16), vn16)
                g_last = gcc_scr[r + C - 1:r + C, h:h + 1]
                s_scr[sh] = s_scr[sh] * jnp.exp(g_last) + _dot_tn(kout_scr[p].astype(BF16), vn16)
                z = main_ref[rows, 3 * W + lo:3 * W + lo + HEAD_DIM]
                og_ref[rows, lo:lo + HEAD_DIM] = _rms(o, onorm) * _silu(z)

    @pl.when(t == pl.num_programs(1) - 1)
    def _():
        for s in range(n_seq):
            for h in range(N_HEADS):
                sf_ref[s, h] = s_scr[s * N_HEADS + h]


def _gdn_recurrence(main, ab, conv_w, a_log, dt_bias, out_norm, cb0, s0, *, batch, seq, n_chunks, n_seq):
    n = main.shape[0]
    chunk = min(MAX_CHUNK, seq)
    tb = chunk * n_chunks
    assert seq % tb == 0 and batch % n_seq == 0 and seq >= GD_CONV - 1
    assert n_seq == 1 or tb == seq, "several sequences per step only when a step covers whole sequences"
    nt = seq // tb
    rows = n_seq * tb
    n_prob = n_seq * n_chunks * N_HEADS
    has_s0 = s0 is not None
    abt = ab.reshape(batch, seq, 2 * N_HEADS).transpose(0, 2, 1)
    state_spec = pl.BlockSpec((n_seq, N_HEADS, HEAD_DIM, HEAD_DIM), lambda b, t: (b, 0, 0, 0))
    cb_spec = pl.BlockSpec((n_seq, GD_CONV - 1, GD_QKV), lambda b, t: (b, 0, 0))
    small = lambda shape: pl.BlockSpec(shape, lambda b, t: (0,) * len(shape))
    in_specs = [pl.BlockSpec((rows, main.shape[1]), lambda b, t: (b * nt + t, 0)),
                pl.BlockSpec((rows, 2 * N_HEADS), lambda b, t: (b * nt + t, 0)),
                pl.BlockSpec((n_seq, 2 * N_HEADS, tb), lambda b, t: (b, 0, t)),
                small((GD_CONV, GD_QKV)), small((1, N_HEADS)), small((1, N_HEADS)),
                small((N_HEADS, 1)), small((N_HEADS, 1)), small((1, HEAD_DIM))]
    args = [main, ab, abt, conv_w, a_log.reshape(1, N_HEADS), dt_bias.reshape(1, N_HEADS),
            a_log.reshape(N_HEADS, 1), dt_bias.reshape(N_HEADS, 1), out_norm.reshape(1, HEAD_DIM)]
    if has_s0:
        in_specs += [cb_spec, state_spec]
        args += [cb0, s0]
    vm = lambda *shape: pltpu.VMEM(shape, F32)
    return pl.pallas_call(
        functools.partial(_gdn_kernel, chunk=chunk, n_chunks=n_chunks, n_seq=n_seq, has_s0=has_s0),
        grid=(batch // n_seq, nt),
        in_specs=in_specs,
        out_specs=[pl.BlockSpec((rows, D_MODEL), lambda b, t: (b * nt + t, 0)), state_spec, cb_spec],
        out_shape=[jax.ShapeDtypeStruct((n, D_MODEL), F32),
                   jax.ShapeDtypeStruct((batch, N_HEADS, HEAD_DIM, HEAD_DIM), F32),
                   jax.ShapeDtypeStruct((batch, GD_CONV - 1, GD_QKV), F32)],
        scratch_shapes=[vm(n_seq * N_HEADS, HEAD_DIM, HEAD_DIM),
                        vm(n_seq, tb + SUBLANES, GD_QKV),
                        vm(rows, GD_QKV),
                        vm(rows, N_HEADS),
                        vm(n_seq * n_chunks, N_HEADS, chunk),
                        vm(rows, N_HEADS),
                        vm(n_prob, 2 * chunk, chunk),
                        vm(n_prob, chunk, 2 * HEAD_DIM),
                        vm(n_prob, 2 * chunk, HEAD_DIM),
                        vm(n_prob, chunk, HEAD_DIM),
                        vm(n_prob, chunk, chunk),
                        vm(n_prob, chunk, chunk),
                        vm(n_prob, chunk, chunk),
                        vm(n_prob, chunk, chunk),
                        vm(n_prob, chunk, 2 * HEAD_DIM),
                        vm(n_prob, chunk, HEAD_DIM),
                        vm(n_seq * N_HEADS, 2 * chunk, HEAD_DIM)],
        compiler_params=_params("parallel", "arbitrary"),
        name="gdn_recurrence",
    )(*args)


def _swiglu_rows(x_of, wg, wu, wd, n_sub, emit):
    pending = None
    for sb in range(n_sub + 1):
        if sb < n_sub:
            x = x_of(sb)
            nxt = (_dot(x, wg), _dot(x, wu))
        if pending is not None:
            a, u = pending
            emit(sb - 1, _dot((_silu(a) * u).astype(BF16), wd))
        pending = nxt


def _ffn_kernel(x_ref, og_ref, wo_ref, g_ref, wg_ref, wu_ref, wd_ref, o_ref, h_scr, *, n_sub):
    j = pl.program_id(1)
    sub = x_ref.shape[0] // n_sub

    @pl.when(j == 0)
    def _():
        x = x_ref[...] + _dot(og_ref[...].astype(BF16), wo_ref[...])
        h_scr[...] = _rms(x, g_ref[...]).astype(BF16)
        o_ref[...] = x

    def emit(sb, y):
        o_ref[sb * sub:(sb + 1) * sub, :] += y

    _swiglu_rows(lambda sb: h_scr[sb * sub:(sb + 1) * sub, :], wg_ref[...], wu_ref[...], wd_ref[...], n_sub, emit)


def _ffn(x, og, w_out, gain, wg, wu, wd, *, tm, tf, n_sub):
    n, d = x.shape
    f = wg.shape[1]
    rows = pl.BlockSpec((tm, d), lambda i, j: (i, 0))
    return pl.pallas_call(
        functools.partial(_ffn_kernel, n_sub=n_sub),
        grid=(n // tm, f // tf),
        in_specs=[rows, rows,
                  pl.BlockSpec((d, d), lambda i, j: (0, 0)),
                  pl.BlockSpec((1, d), lambda i, j: (0, 0)),
                  pl.BlockSpec((d, tf), lambda i, j: (0, j)),
                  pl.BlockSpec((d, tf), lambda i, j: (0, j)),
                  pl.BlockSpec((tf, d), lambda i, j: (j, 0))],
        out_specs=rows,
        out_shape=jax.ShapeDtypeStruct((n, d), F32),
        scratch_shapes=[pltpu.VMEM((tm, d), BF16)],
        compiler_params=_params("parallel", "arbitrary"),
        name="dense_ffn",
    )(x, og, w_out, gain.reshape(1, d), wg, wu, wd)


ROUTER_SLOTS = 128
PAD_ROWS = 256


RUN_ALIGN = 32


def _run_copies(n_rows, max_rows, make_copy):
    pieces = []
    off = jnp.int32(0)
    bit = max_rows
    while bit >= RUN_ALIGN:
        has = (n_rows & bit) != 0
        pieces.append((has, make_copy(off, bit)))
        off = off + (n_rows & bit)
        bit //= 2
    return pieces


def _router_kernel(x_ref, og_ref, wo_ref, g_ref, wr_ref, xo_ref, dest_ref, gate_ref, cnt_ref, xs_hbm,
                   cnt_scr, stage, zeros_buf, n_smem, base_smem, sem, *, tm, cap):
    i = pl.program_id(0)
    last = pl.num_programs(0) - 1

    @pl.when(i == 0)
    def _():
        cnt_scr[...] = jnp.zeros_like(cnt_scr)
        zeros_buf[...] = jnp.zeros_like(zeros_buf)
        for e in range(N_EXPERTS):
            n_smem[e] = 0
            base_smem[e] = e * cap

    x = x_ref[...] + _dot(og_ref[...].astype(BF16), wo_ref[...])
    xo_ref[...] = x
    h = _rms(x, g_ref[...])
    h16 = h.astype(BF16)
    h_lo = (h - h16.astype(F32)).astype(BF16)
    w_hi, w_lo = _split_bf16(wr_ref[...])
    logits = _dot(h16, w_hi) + (_dot(h16, w_lo) + _dot(h_lo, w_hi))
    lane = lax.broadcasted_iota(jnp.int32, logits.shape, 1)
    m1 = jnp.max(logits, axis=-1, keepdims=True)
    i1 = jnp.min(jnp.where(logits == m1, lane, N_EXPERTS), axis=-1, keepdims=True)
    rest = jnp.where(lane == i1, -jnp.inf, logits)
    m2 = jnp.max(rest, axis=-1, keepdims=True)
    i2 = jnp.min(jnp.where(rest == m2, lane, N_EXPERTS), axis=-1, keepdims=True)
    e2 = jnp.exp(m2 - m1)
    den = 1.0 + e2
    two = lax.broadcasted_iota(jnp.int32, (tm, TOP_K), 1)
    gate_ref[...] = jnp.where(two == 0, 1.0 / den, e2 / den)

    onehot = ((lane == i1) | (lane == i2)).astype(F32)
    row = lax.broadcasted_iota(jnp.int32, (tm, tm), 0)
    col = lax.broadcasted_iota(jnp.int32, (tm, tm), 1)
    local = _dot((row > col).astype(BF16), onehot.astype(BF16))
    rank = local + cnt_scr[...]
    r1 = jnp.sum(jnp.where(lane == i1, rank, 0.0), axis=-1, keepdims=True).astype(jnp.int32)
    r2 = jnp.sum(jnp.where(lane == i2, rank, 0.0), axis=-1, keepdims=True).astype(jnp.int32)
    dest_ref[...] = jnp.where(two == 0, i1 * cap + r1, i2 * cap + r2)
    cnt_tile = jnp.sum(onehot, axis=0, keepdims=True)
    cnt = cnt_scr[...] + cnt_tile
    cnt_scr[...] = cnt
    cnt_ref[...] = cnt.astype(jnp.int32)

    def wait_runs():
        for e in range(N_EXPERTS):
            for has, copy in _run_copies(n_smem[e], tm, lambda off, size, e=e: pltpu.make_async_copy(
                    stage.at[e, pl.ds(0, size * SUBLANES)], xs_hbm.at[pl.ds(0, size * SUBLANES)], sem)):
                @pl.when(has)
                def _():
                    copy.wait()

    wait_runs()

    code = (onehot * (local + 1.0)).astype(BF16)
    code_t = _dot_tn(code, (row == col).astype(BF16))
    slot_id = lax.broadcasted_iota(jnp.int32, (ROUTER_SLOTS, tm), 0)
    code_i = code_t.astype(jnp.int32)

    def compact(e, q):
        sel = jnp.where(code_i[e:e + 1, :] == slot_id + (q * ROUTER_SLOTS + 1), 1.0, 0.0).astype(BF16)
        _store_token_tiles(stage.at[e], q * ROUTER_SLOTS, _dot(sel, h16))

    for e in range(N_EXPERTS):
        compact(e, 0)
    for e in range(N_EXPERTS):
        n_e = cnt_tile[0, e].astype(jnp.int32)
        for q in range(1, tm // ROUTER_SLOTS):
            @pl.when(q * ROUTER_SLOTS < n_e)
            def _():
                compact(e, q)
        n_copy = jnp.bitwise_and(n_e + (RUN_ALIGN - 1), -RUN_ALIGN)
        base = base_smem[e]
        for has, copy in _run_copies(n_copy, tm, lambda off, size, e=e, base=base: pltpu.make_async_copy(
                stage.at[e, pl.ds(pl.multiple_of(off * SUBLANES, SUBLANES), size * SUBLANES)],
                xs_hbm.at[pl.ds(pl.multiple_of((base + off) * SUBLANES, SUBLANES), size * SUBLANES)], sem)):
            @pl.when(has)
            def _():
                copy.start()
        n_smem[e] = n_copy
        base_smem[e] = base + n_e

    @pl.when(i == last)
    def _():
        wait_runs()
        pads = [pltpu.make_async_copy(
            zeros_buf, xs_hbm.at[pl.ds(pl.multiple_of(base_smem[e] * SUBLANES, SUBLANES), PAD_ROWS * SUBLANES)], sem)
            for e in range(N_EXPERTS)]
        for pad in pads:
            pad.start()
        for pad in pads:
            pad.wait()


def _router(x, og, w_out, gain, w_router, *, tm, cap):
    n, d = x.shape
    two = pl.BlockSpec((tm, TOP_K), lambda i: (i, 0))
    rows = pl.BlockSpec((tm, d), lambda i: (i, 0))
    return pl.pallas_call(
        functools.partial(_router_kernel, tm=tm, cap=cap),
        grid=(n // tm,),
        in_specs=[rows, rows,
                  pl.BlockSpec((d, d), lambda i: (0, 0)),
                  pl.BlockSpec((1, d), lambda i: (0, 0)),
                  pl.BlockSpec((d, N_EXPERTS), lambda i: (0, 0))],
        out_specs=[rows, two, two, pl.BlockSpec((1, N_EXPERTS), lambda i: (0, 0)),
                   pl.BlockSpec(memory_space=pl.ANY)],
        out_shape=[jax.ShapeDtypeStruct((n, d), F32),
                   jax.ShapeDtypeStruct((n, TOP_K), jnp.int32),
                   jax.ShapeDtypeStruct((n, TOP_K), F32),
                   jax.ShapeDtypeStruct((1, N_EXPERTS), jnp.int32),
                   jax.ShapeDtypeStruct((N_EXPERTS * cap * SUBLANES, LANES), F32)],
        scratch_shapes=[pltpu.VMEM((1, N_EXPERTS), F32),
                        pltpu.VMEM((N_EXPERTS, tm * SUBLANES, LANES), F32),
                        pltpu.VMEM((PAD_ROWS * SUBLANES, LANES), F32),
                        pltpu.SMEM((N_EXPERTS,), jnp.int32), pltpu.SMEM((N_EXPERTS,), jnp.int32),
                        pltpu.SemaphoreType.DMA(())],
        compiler_params=_params("arbitrary"),
        name="moe_router",
    )(x, og, w_out, gain.reshape(1, d), w_router)


def _tile_gather_copy(src_hbm, dst, sem, src_row, dst_row):
    return pltpu.make_async_copy(src_hbm.at[pl.ds(pl.multiple_of(src_row * SUBLANES, SUBLANES), SUBLANES)],
                                 dst.at[pl.ds(pl.multiple_of(dst_row * SUBLANES, SUBLANES), SUBLANES)], sem)


def _load_token_tiles(ref, first_row, n_rows):
    return jnp.concatenate([ref[pl.ds(first_row * SUBLANES + s, n_rows, stride=SUBLANES), :]
                            for s in range(D_MODEL // LANES)], axis=1)


def _store_token_tiles(ref, first_row, x):
    for s in range(D_MODEL // LANES):
        ref[pl.ds(first_row * SUBLANES + s, x.shape[0], stride=SUBLANES), :] = x[:, s * LANES:(s + 1) * LANES]


def _expert_kernel(be_ref, xblk_ref, oblk_ref, nv_ref, nu_ref, xs_ref, wg_ref, wu_ref, wd_ref, yb_ref, xb, acc,
                   *, sub_rows, n_sub):
    i = pl.program_id(0)
    j = pl.program_id(1)
    n_valid = nv_ref[i]

    @pl.when(j == 0)
    def _():
        acc[...] = jnp.zeros_like(acc)
        for sb in range(n_sub):
            @pl.when(sb * sub_rows < n_valid)
            def _():
                xb[sb * sub_rows:(sb + 1) * sub_rows, :] = _load_token_tiles(
                    xs_ref, sb * sub_rows, sub_rows).astype(BF16)

    def emit(sb, y):
        acc[sb * sub_rows:(sb + 1) * sub_rows, :] += y

    def x_of(sb):
        return xb[sb * sub_rows:(sb + 1) * sub_rows, :]

    full = n_valid > (n_sub - 1) * sub_rows

    @pl.when(full)
    def _():
        _swiglu_rows(x_of, wg_ref[0], wu_ref[0], wd_ref[0], n_sub, emit)

    for sb in range(n_sub - 1):
        @pl.when(jnp.logical_and(jnp.logical_not(full), sb * sub_rows < n_valid))
        def _():
            _swiglu_rows(lambda _, sb=sb: x_of(sb), wg_ref[0], wu_ref[0], wd_ref[0], 1, lambda _, y, sb=sb: emit(sb, y))

    @pl.when(j == pl.num_programs(1) - 1)
    def _():
        for sb in range(n_sub):
            rows = slice(sb * sub_rows, (sb + 1) * sub_rows)

            @pl.when(sb * sub_rows < n_valid)
            def _():
                _store_token_tiles(yb_ref, sb * sub_rows, acc[rows, :])

            @pl.when(sb * sub_rows >= n_valid)
            def _():
                tiles = slice(sb * sub_rows * SUBLANES, (sb + 1) * sub_rows * SUBLANES)
                yb_ref[tiles, :] = jnp.zeros((sub_rows * SUBLANES, LANES), F32)


def _experts(xs, block_e, x_block, out_block, n_valid, n_used, wg, wu, wd, *, sub_rows, n_sub, tf, out_blocks):
    d = D_MODEL
    tmm = sub_rows * n_sub
    nb = block_e.shape[0]
    nf = D_FF // tf

    def wcol(i, j, be, xblk, oblk, nv, nu):
        return (be[i], 0, jnp.where(i < nu[0], j, nf - 1))

    def wrow(i, j, be, xblk, oblk, nv, nu):
        return (be[i], jnp.where(i < nu[0], j, nf - 1), 0)

    grid_spec = pltpu.PrefetchScalarGridSpec(
        num_scalar_prefetch=5,
        grid=(nb, nf),
        in_specs=[pl.BlockSpec((tmm * SUBLANES, LANES), lambda i, j, be, xblk, oblk, nv, nu: (xblk[i], 0)),
                  pl.BlockSpec((1, d, tf), wcol),
                  pl.BlockSpec((1, d, tf), wcol),
                  pl.BlockSpec((1, tf, d), wrow)],
        out_specs=pl.BlockSpec((tmm * SUBLANES, LANES), lambda i, j, be, xblk, oblk, nv, nu: (oblk[i], 0)),
        scratch_shapes=[pltpu.VMEM((tmm, d), BF16), pltpu.VMEM((tmm, d), F32)],
    )
    return pl.pallas_call(
        functools.partial(_expert_kernel, sub_rows=sub_rows, n_sub=n_sub),
        grid_spec=grid_spec,
        out_shape=jax.ShapeDtypeStruct((out_blocks * tmm * SUBLANES, LANES), F32),
        compiler_params=_params("arbitrary", "arbitrary"),
        name="moe_experts",
    )(block_e, x_block, out_block, n_valid, n_used, xs, wg, wu, wd)


def _combine_kernel(dest_ref, yb_hbm, x_ref, gate_ref, gain_ref, o_ref, r0, r1, sems, *, tm):
    i = pl.program_id(0)
    slot = lax.rem(i, 2)

    def start_gather(tile, dst_slot):
        def issue(r, carry):
            first = TOP_K * (tile * tm + r)
            _tile_gather_copy(yb_hbm, r0.at[dst_slot], sems.at[dst_slot, 0], dest_ref[first], r).start(priority=0)
            _tile_gather_copy(yb_hbm, r1.at[dst_slot], sems.at[dst_slot, 1], dest_ref[first + 1], r).start(priority=1)
            return carry

        lax.fori_loop(0, tm, issue, 0, unroll=8)

    @pl.when(i == 0)
    def _():
        start_gather(0, 0)

    @pl.when(i + 1 < pl.num_programs(0))
    def _():
        start_gather(i + 1, 1 - slot)

    whole = pl.ds(0, tm * SUBLANES)
    pltpu.make_async_copy(yb_hbm.at[whole], r0.at[slot], sems.at[slot, 0]).wait()
    pltpu.make_async_copy(yb_hbm.at[whole], r1.at[slot], sems.at[slot, 1]).wait()
    g = gate_ref[...]
    y = _load_token_tiles(r0.at[slot], 0, tm) * g[:, 0:1] + _load_token_tiles(r1.at[slot], 0, tm) * g[:, 1:2]
    o_ref[...] = _rms(x_ref[...] + y, gain_ref[...])


def _combine(yb, dest_flat, x, gates, gain, *, tm):
    n, d = x.shape
    grid_spec = pltpu.PrefetchScalarGridSpec(
        num_scalar_prefetch=1,
        grid=(n // tm,),
        in_specs=[pl.BlockSpec(memory_space=pl.ANY),
                  pl.BlockSpec((tm, d), lambda i, dest: (i, 0)),
                  pl.BlockSpec((tm, TOP_K), lambda i, dest: (i, 0)),
                  pl.BlockSpec((1, d), lambda i, dest: (0, 0))],
        out_specs=pl.BlockSpec((tm, d), lambda i, dest: (i, 0)),
        scratch_shapes=[pltpu.VMEM((2, tm * SUBLANES, LANES), F32), pltpu.VMEM((2, tm * SUBLANES, LANES), F32),
                        pltpu.SemaphoreType.DMA((2, TOP_K))],
    )
    return pl.pallas_call(
        functools.partial(_combine_kernel, tm=tm),
        grid_spec=grid_spec,
        out_shape=jax.ShapeDtypeStruct((n, d), F32),
        compiler_params=_params("arbitrary"),
        name="moe_combine",
    )(dest_flat, yb, x, gates, gain.reshape(1, d))


def _moe_and_final_norm(x, og, w_out, gain, w_router, wg, wu, wd, final_gain, *, tiles):
    n = x.shape[0]
    sub_rows, n_sub = tiles["moe_sub_rows"], tiles["moe_subs"]
    tmm = sub_rows * n_sub
    assert sub_rows == PAD_ROWS and n % tmm == 0
    cap = n + tmm
    x, dest, gates, counts, xs = _router(x, og, w_out, gain, w_router, tm=tiles["router_rows"], cap=cap)
    counts = counts[0]
    blocks_per_e = (counts + tmm - 1) // tmm
    blocks_end = jnp.cumsum(blocks_per_e)
    n_used = blocks_end[-1].astype(jnp.int32)
    nb = -(-(n * TOP_K + N_EXPERTS * (tmm - 1)) // tmm)
    blk = jnp.minimum(jnp.arange(nb, dtype=jnp.int32), n_used - 1)
    block_e = jnp.minimum(jnp.searchsorted(blocks_end, blk, side="right"), N_EXPERTS - 1).astype(jnp.int32)
    k_in_e = blk - (blocks_end - blocks_per_e)[block_e]
    x_block = (block_e * (cap // tmm) + k_in_e).astype(jnp.int32)
    used = jnp.arange(nb, dtype=jnp.int32) < n_used
    n_valid = jnp.where(used, jnp.clip(counts[block_e] - k_in_e * tmm, 0, tmm), 0).astype(jnp.int32)
    dump_block = N_EXPERTS * (cap // tmm)
    out_block = jnp.where(used, x_block, dump_block).astype(jnp.int32)
    yb = _experts(xs, block_e, x_block, out_block, n_valid, n_used.reshape(1), wg, wu, wd,
                  sub_rows=sub_rows, n_sub=n_sub, tf=tiles["moe_ff_cols"], out_blocks=dump_block + 1)
    return _combine(yb, dest.reshape(-1), x, gates, final_gain, tm=tiles["combine_rows"])


def _tiles(n_rows, seq):
    big = n_rows >= 8192
    return {
        "proj_rows": 1024 if n_rows % 1024 == 0 else n_rows,
        "proj_cols": 2048,
        "rec_chunks": min(4, seq // min(MAX_CHUNK, seq)),
        "rec_seqs": 1 if seq > MAX_CHUNK else 4,
        "ffn_rows": 1024 if n_rows % 1024 == 0 else n_rows,
        "ff_cols": 512,
        "router_rows": 256,
        "moe_sub_rows": 256,
        "moe_subs": 4 if big else 2,
        "moe_ff_cols": 1792,
        "ffn_subs": 4 if big else 2,
        "combine_rows": 256,
    }


def _trunk(x3, hg_state, gd_state, gd_conv, p):
    batch, seq, d = x3.shape
    n = batch * seq
    x = x3.reshape(n, d)
    tiles = _tiles(n, seq)
    tm, tn = tiles["proj_rows"], tiles["proj_cols"]

    proj = _norm_proj(x, p["norm_mix"][0], p["hgrn_w_in"], tm=tm, tn=tn)[0]
    og, hg_new = _hgrn_recurrence(proj, p["hgrn_lb"], p["hgrn_norm"], hg_state, batch=batch, seq=seq, layer=0,
                                  n_chunks=tiles["rec_chunks"], n_seq=tiles["rec_seqs"])
    x = _ffn(x, og, p["hgrn_w_out"], p["norm_ffn"][0], p["ffn_w_gate"], p["ffn_w_up"], p["ffn_w_down"],
             tm=tiles["ffn_rows"], tf=tiles["ff_cols"], n_sub=tiles["ffn_subs"])

    main, ab = _norm_proj(x, p["norm_mix"][1], p["gdn_w_main"], p["gdn_w_ab"], tm=tm, tn=tn)
    og, gd_new, cv_new = _gdn_recurrence(main, ab, p["gdn_conv"], p["gdn_a_log"], p["gdn_dt_bias"], p["gdn_norm"],
                                         gd_conv, gd_state, batch=batch, seq=seq, n_chunks=tiles["rec_chunks"],
                                         n_seq=tiles["rec_seqs"])
    y = _moe_and_final_norm(x, og, p["gdn_w_out"], p["norm_ffn"][1], p["moe_router"], p["moe_w_gate"], p["moe_w_up"], p["moe_w_down"],
                            p["norm_out"], tiles=tiles)
    return y.reshape(batch, seq, d), hg_new[None], gd_new[None], cv_new[None]


def kernel(x_prompt, x_sample, state_hgrn, state_gdn, state_gdn_conv, norm_mix, norm_ffn, norm_out, hgrn_w_in, hgrn_lb, hgrn_norm, hgrn_w_out, gdn_w_in, gdn_conv, gdn_a_log, gdn_dt_bias, gdn_norm, gdn_w_out, ffn_w_gate, ffn_w_up, ffn_w_down, moe_router, moe_w_gate, moe_w_up, moe_w_down):
    assert hgrn_w_in.shape[0] == 1 and gdn_w_in.shape[0] == 1, "one HGRN2 layer and one gated-DeltaNet layer"
    main_cols = GD_QKV + N_HEADS * HEAD_DIM
    p = {
        "norm_mix": norm_mix, "norm_ffn": norm_ffn, "norm_out": norm_out,
        "hgrn_w_in": hgrn_w_in[0].astype(BF16), "hgrn_lb": hgrn_lb, "hgrn_norm": hgrn_norm[0],
        "hgrn_w_out": hgrn_w_out[0].astype(BF16),
        "gdn_w_main": gdn_w_in[0, :, :main_cols].astype(BF16), "gdn_w_ab": gdn_w_in[0, :, main_cols:].astype(BF16),
        "gdn_conv": gdn_conv[0], "gdn_a_log": gdn_a_log[0], "gdn_dt_bias": gdn_dt_bias[0], "gdn_norm": gdn_norm[0],
        "gdn_w_out": gdn_w_out[0].astype(BF16),
        "ffn_w_gate": ffn_w_gate[0].astype(BF16), "ffn_w_up": ffn_w_up[0].astype(BF16),
        "ffn_w_down": ffn_w_down[0].astype(BF16),
        "moe_router": moe_router[0],
        "moe_w_gate": moe_w_gate[0].astype(BF16), "moe_w_up": moe_w_up[0].astype(BF16),
        "moe_w_down": moe_w_down[0].astype(BF16),
    }
    y_p, hg_p, gd_p, cv_p = _trunk(x_prompt, None, None, None, p)
    y_s, hg_s, gd_s, cv_s = _trunk(x_sample, state_hgrn[0], state_gdn[0], state_gdn_conv[0], p)
    return (y_p, y_s, hg_p, hg_s, gd_p, gd_s, cv_p, cv_s)
```

```python
import functools

import jax
import jax.numpy as jnp
from jax import lax
from jax.experimental import pallas as pl
from jax.experimental.pallas import tpu as pltpu

F32 = jnp.float32
BF16 = jnp.bfloat16
HIGHEST = lax.Precision.HIGHEST

D_MODEL = 1024
N_HEADS = 8
HEAD_DIM = 128
D_FF = 3584
N_EXPERTS = 8
TOP_K = 2
GD_QKV = 3 * D_MODEL
GD_CONV = 4
MAX_CHUNK = 64
EPS = 1e-6
LANES = 128
SUBLANES = 8
VMEM_LIMIT_BYTES = 52 * 1024 * 1024


def _params(*sem):
    return pltpu.CompilerParams(dimension_semantics=sem, vmem_limit_bytes=VMEM_LIMIT_BYTES)


def _rms(x, gain):
    return x * lax.rsqrt(jnp.mean(x * x, axis=-1, keepdims=True) + EPS) * gain


def _sigmoid(x):
    t = jnp.exp(-jnp.abs(x))
    r = 1.0 / (1.0 + t)
    return jnp.where(x >= 0, r, t * r)


def _silu(x):
    return x * (0.5 * jnp.tanh(0.5 * x) + 0.5)


def _softplus(x):
    return jnp.maximum(x, 0.0) + jnp.log1p(jnp.exp(-jnp.abs(x)))


def _dot(a, b):
    return jnp.dot(a, b, preferred_element_type=F32)


def _dot_nt(a, b):
    return lax.dot_general(a, b, (((1,), (1,)), ((), ())), preferred_element_type=F32)


def _dot_tn(a, b):
    return lax.dot_general(a, b, (((0,), (0,)), ((), ())), preferred_element_type=F32)


def _dot_hi(a, b):
    return jnp.dot(a, b, precision=HIGHEST, preferred_element_type=F32)


def _norm_proj_kernel(x_ref, g_ref, w_ref, *rest, has_small):
    if has_small:
        ws_ref, o_ref, os_ref, h_scr = rest
    else:
        o_ref, h_scr = rest

    @pl.when(pl.program_id(1) == 0)
    def _():
        hb = _rms(x_ref[...], g_ref[...]).astype(BF16)
        h_scr[...] = hb
        if has_small:
            os_ref[...] = _dot(hb, ws_ref[...])

    o_ref[...] = _dot(h_scr[...], w_ref[...])


def _norm_proj(x, gain, w, w_small=None, *, tm, tn):
    n, d = x.shape
    e = w.shape[1]
    has_small = w_small is not None
    in_specs = [pl.BlockSpec((tm, d), lambda i, j: (i, 0)),
                pl.BlockSpec((1, d), lambda i, j: (0, 0)),
                pl.BlockSpec((d, tn), lambda i, j: (0, j))]
    out_shape = [jax.ShapeDtypeStruct((n, e), F32)]
    out_specs = [pl.BlockSpec((tm, tn), lambda i, j: (i, j))]
    args = [x, gain.reshape(1, d), w]
    if has_small:
        es = w_small.shape[1]
        in_specs.append(pl.BlockSpec((d, es), lambda i, j: (0, 0)))
        out_shape.append(jax.ShapeDtypeStruct((n, es), F32))
        out_specs.append(pl.BlockSpec((tm, es), lambda i, j: (i, 0)))
        args.append(w_small)
    return pl.pallas_call(
        functools.partial(_norm_proj_kernel, has_small=has_small),
        grid=(n // tm, e // tn),
        in_specs=in_specs, out_specs=out_specs, out_shape=out_shape,
        scratch_shapes=[pltpu.VMEM((tm, d), BF16)],
        compiler_params=_params("parallel", "arbitrary"),
        name="norm_proj",
    )(*args)


def _hgrn_kernel(*refs, chunk, n_chunks, n_seq, layer, has_s0):
    if has_s0:
        proj_ref, lb_ref, on_ref, s0_ref, og_ref, sf_ref, *scr = refs
    else:
        proj_ref, lb_ref, on_ref, og_ref, sf_ref, *scr = refs
    st_scr, b_scr, q_scr, k_scr, a_scr, qin_scr, kout_scr, dec_scr, oi_scr = scr
    C = chunk
    W = N_HEADS * HEAD_DIM
    tb = C * n_chunks
    t = pl.program_id(1)
    problems = [(s, c, h) for s in range(n_seq) for c in range(n_chunks) for h in range(N_HEADS)]

    def pid(s, c, h):
        return (s * n_chunks + c) * N_HEADS + h

    def row0(s, c):
        return s * tb + c * C

    @pl.when(t == 0)
    def _():
        for s in range(n_seq):
            for h in range(N_HEADS):
                st_scr[s * N_HEADS + h] = s0_ref[s, h].T if has_s0 else jnp.zeros((HEAD_DIM, HEAD_DIM), F32)

    lbraw = lb_ref[...]
    ex = jnp.exp(lbraw - jnp.max(lbraw, axis=0, keepdims=True))
    sm = ex / jnp.sum(ex, axis=0, keepdims=True)
    lb = sm[0:1]
    for j in range(1, layer + 1):
        lb = lb + sm[j:j + 1]

    row = lax.broadcasted_iota(jnp.int32, (C, C), 0)
    col = lax.broadcasted_iota(jnp.int32, (C, C), 1)
    causal = row >= col
    tril = causal.astype(BF16)
    mid = C // 2
    scale = HEAD_DIM ** -0.5
    onorm = on_ref[...]

    for s in range(n_seq):
        for c in range(n_chunks):
            rows = slice(row0(s, c), row0(s, c) + C)
            f_raw = proj_ref[rows, W:2 * W]
            tt = jnp.exp(-jnp.abs(f_raw))
            rr = 1.0 / (1.0 + tt)
            pos = f_raw >= 0
            sig = jnp.where(pos, rr, tt * rr)
            sig_neg = jnp.where(pos, tt * rr, rr)
            log_f = jnp.log(lb + (1.0 - lb) * sig)
            k_scr[rows, :] = (1.0 - lb) * sig_neg
            q_scr[rows, :] = _silu(proj_ref[rows, 0:W]) * scale
            hi = log_f.astype(BF16)
            rest = log_f - hi.astype(F32)
            md = rest.astype(BF16)
            lw = (rest - md.astype(F32)).astype(BF16)
            b_scr[rows, :] = (_dot(tril, lw) + _dot(tril, md)) + _dot(tril, hi)

    for s, c, h in problems:
        p, lo = pid(s, c, h), h * HEAD_DIM
        rows = slice(row0(s, c), row0(s, c) + C)
        b = b_scr[rows, lo:lo + HEAD_DIM]
        q = q_scr[rows, lo:lo + HEAD_DIM]
        k = k_scr[rows, lo:lo + HEAD_DIM]
        b_mid = b[mid:mid + 1]
        b_last = b[C - 1:C]
        a_scr[p] = _dot_nt((q * jnp.exp(b - b_mid)).astype(BF16), (k * jnp.exp(b_mid - b)).astype(BF16))
        qin_scr[p] = q * jnp.exp(b)
        kout_scr[p] = k * jnp.exp(b_last - b)
        dec_scr[p] = jnp.exp(b_last)

    for s, c, h in problems:
        p, lo = pid(s, c, h), h * HEAD_DIM
        rows = slice(row0(s, c), row0(s, c) + C)
        a = jnp.where(causal, a_scr[p], 0.0)
        oi_scr[p] = _dot(a.astype(BF16), proj_ref[rows, 2 * W + lo:2 * W + lo + HEAD_DIM].astype(BF16))

    for c in range(n_chunks):
        for s in range(n_seq):
            for h in range(N_HEADS):
                p, lo, sh = pid(s, c, h), h * HEAD_DIM, s * N_HEADS + h
                rows = slice(row0(s, c), row0(s, c) + C)
                st = st_scr[sh]
                o = oi_scr[p] + _dot_nt(qin_scr[p].astype(BF16), st.astype(BF16))
                vb = proj_ref[rows, 2 * W + lo:2 * W + lo + HEAD_DIM].astype(BF16)
                st_scr[sh] = st * dec_scr[p] + _dot_tn(vb, kout_scr[p].astype(BF16))
                gate = proj_ref[rows, 3 * W + lo:3 * W + lo + HEAD_DIM]
                og_ref[rows, lo:lo + HEAD_DIM] = _rms(o, onorm) * _silu(gate)

    @pl.when(t == pl.num_programs(1) - 1)
    def _():
        for s in range(n_seq):
            for h in range(N_HEADS):
                sf_ref[s, h] = st_scr[s * N_HEADS + h].T


def _hgrn_recurrence(proj, lb_raw, out_norm, s0, *, batch, seq, layer, n_chunks, n_seq):
    n = proj.shape[0]
    chunk = min(MAX_CHUNK, seq)
    tb = chunk * n_chunks
    assert seq % tb == 0 and batch % n_seq == 0
    assert n_seq == 1 or tb == seq, "several sequences per step only when a step covers whole sequences"
    nt = seq // tb
    rows = n_seq * tb
    n_prob = n_seq * n_chunks * N_HEADS
    width = N_HEADS * HEAD_DIM
    has_s0 = s0 is not None
    state_spec = pl.BlockSpec((n_seq, N_HEADS, HEAD_DIM, HEAD_DIM), lambda b, t: (b, 0, 0, 0))
    in_specs = [pl.BlockSpec((rows, proj.shape[1]), lambda b, t: (b * nt + t, 0)),
                pl.BlockSpec(lb_raw.shape, lambda b, t: (0, 0)),
                pl.BlockSpec((1, HEAD_DIM), lambda b, t: (0, 0))]
    args = [proj, lb_raw, out_norm.reshape(1, HEAD_DIM)]
    if has_s0:
        in_specs.append(state_spec)
        args.append(s0)
    vm = lambda *shape: pltpu.VMEM(shape, F32)
    return pl.pallas_call(
        functools.partial(_hgrn_kernel, chunk=chunk, n_chunks=n_chunks, n_seq=n_seq, layer=layer, has_s0=has_s0),
        grid=(batch // n_seq, nt),
        in_specs=in_specs,
        out_specs=[pl.BlockSpec((rows, D_MODEL), lambda b, t: (b * nt + t, 0)), state_spec],
        out_shape=[jax.ShapeDtypeStruct((n, D_MODEL), F32),
                   jax.ShapeDtypeStruct((batch, N_HEADS, HEAD_DIM, HEAD_DIM), F32)],
        scratch_shapes=[vm(n_seq * N_HEADS, HEAD_DIM, HEAD_DIM),
                        vm(rows, width),
                        vm(rows, width),
                        vm(rows, width),
                        vm(n_prob, chunk, chunk),
                        vm(n_prob, chunk, HEAD_DIM),
                        vm(n_prob, chunk, HEAD_DIM),
                        vm(n_prob, 1, HEAD_DIM),
                        vm(n_prob, chunk, HEAD_DIM)],
        compiler_params=_params("parallel", "arbitrary"),
        name="hgrn_recurrence",
    )(*args)


def _split_bf16(x):
    hi = x.astype(BF16)
    return hi, (x - hi.astype(F32)).astype(BF16)


def _gdn_kernel(*refs, chunk, n_chunks, n_seq, has_s0):
    (main_ref, ab_ref, cw_ref, al_ref, dtb_ref, on_ref), rest = refs[:6], refs[6:]
    if has_s0:
        (cb0_ref, s0_ref), rest = rest[:2], rest[2:]
    (og_ref, sf_ref, cb_ref, s_scr, xpad, act, gcc_scr, gcr_scr, beta_scr, gram_scr, rhs_scr, wq_scr, kout_scr,
     inv_scr, pow_scr, low_scr, attn_scr, x0_scr, u_scr, ws_scr) = rest
    C = chunk
    W = N_HEADS * HEAD_DIM
    tb = C * n_chunks
    t = pl.program_id(1)
    pad = SUBLANES
    pairs = [(s, c, hp) for s in range(n_seq) for c in range(n_chunks) for hp in range(N_HEADS // 2)]

    def pid(s, c, h):
        return (s * n_chunks + c) * N_HEADS + h

    def ppid(s, c, hp):
        return (s * n_chunks + c) * (N_HEADS // 2) + hp

    def row0(s, c):
        return s * tb + c * C

    @pl.when(t == 0)
    def _():
        for s in range(n_seq):
            for h in range(N_HEADS):
                s_scr[s * N_HEADS + h] = s0_ref[s, h] if has_s0 else jnp.zeros((HEAD_DIM, HEAD_DIM), F32)
            xpad[s, 0:pad, :] = jnp.zeros((pad, GD_QKV), F32)
            if has_s0:
                xpad[s, pad - (GD_CONV - 1):pad, :] = cb0_ref[s]

    @pl.when(t > 0)
    def _():
        for s in range(n_seq):
            xpad[s, 0:pad, :] = xpad[s, tb:tb + pad, :]

    for s in range(n_seq):
        for j in range(GD_QKV // LANES):
            cs = slice(j * LANES, (j + 1) * LANES)
            xpad[s, pad:pad + tb, cs] = main_ref[s * tb:(s + 1) * tb, cs]
    for s in range(n_seq):
        for j in range(GD_QKV // LANES):
            cs = slice(j * LANES, (j + 1) * LANES)
            conv = xpad[s, pad - 3:pad - 3 + tb, cs] * cw_ref[0:1, cs]
            for tap in range(1, GD_CONV):
                conv = conv + xpad[s, pad - 3 + tap:pad - 3 + tap + tb, cs] * cw_ref[tap:tap + 1, cs]
            act[s * tb:(s + 1) * tb, cs] = _silu(conv)

    @pl.when(t == pl.num_programs(1) - 1)
    def _():
        for s in range(n_seq):
            cb_ref[s] = xpad[s, pad + tb - (GD_CONV - 1):pad + tb, :]

    C2 = 2 * C
    prow = lax.broadcasted_iota(jnp.int32, (C, C2), 0)
    pcol = lax.broadcasted_iota(jnp.int32, (C, C2), 1)
    half_b = pcol >= C
    pcol_in = jnp.where(half_b, pcol - C, pcol)
    incl = prow >= pcol_in
    strict = prow > pcol_in
    eye = jnp.where(prow == pcol_in, 1.0, 0.0)
    row = lax.broadcasted_iota(jnp.int32, (C, C), 0)
    col = lax.broadcasted_iota(jnp.int32, (C, C), 1)
    tril = (row >= col).astype(BF16)
    triu = (row <= col).astype(BF16)

    def first_head(m):
        return jnp.where(half_b, 0.0, m)

    def second_head(m):
        return jnp.where(half_b, m, 0.0)

    def block_diag(m):
        return jnp.concatenate([first_head(m), second_head(m)], axis=0)

    def split3(x):
        hi = x.astype(BF16)
        rest = x - hi.astype(F32)
        md = rest.astype(BF16)
        return hi, md, (rest - md.astype(F32)).astype(BF16)

    ab = ab_ref[...]
    g_col = -jnp.exp(al_ref[...]) * _softplus(ab[:, 0:N_HEADS] + dtb_ref[...])
    beta_scr[...] = _sigmoid(ab[:, N_HEADS:2 * N_HEADS])
    for s in range(n_seq):
        for c in range(n_chunks):
            r = row0(s, c)
            hi, md, lw = split3(g_col[r:r + C, :])
            gcc_scr[r:r + C, :] = (_dot(tril, lw) + _dot(tril, md)) + _dot(tril, hi)
            gcr_scr[s * n_chunks + c] = (_dot_tn(lw, triu) + _dot_tn(md, triu)) + _dot_tn(hi, triu)

    scale = HEAD_DIM ** -0.5
    onorm = on_ref[...]
    zeros_k = jnp.zeros((C, HEAD_DIM), F32)

    for s, c, hp in pairs:
        r = row0(s, c)
        rows = slice(r, r + C)
        kq, ks = [], []
        for h in (2 * hp, 2 * hp + 1):
            p, lo = pid(s, c, h), h * HEAD_DIM
            q = act[rows, lo:lo + HEAD_DIM]
            k = act[rows, W + lo:W + lo + HEAD_DIM]
            v = act[rows, 2 * W + lo:2 * W + lo + HEAD_DIM]
            q = q * lax.rsqrt(jnp.sum(q * q, axis=-1, keepdims=True) + EPS) * scale
            k = k * lax.rsqrt(jnp.sum(k * k, axis=-1, keepdims=True) + EPS)
            gc_c = gcc_scr[rows, h:h + 1]
            beta_c = beta_scr[rows, h:h + 1]
            k_beta = k * beta_c
            e_gc = jnp.exp(gc_c)
            rhs_scr[p] = jnp.concatenate([v * beta_c, k_beta * e_gc], axis=1)
            wq_scr[p, C:2 * C, :] = q * e_gc
            kout_scr[p] = k * jnp.exp(gc_c[C - 1:C, :] - gc_c)
            kq.append(jnp.concatenate([k_beta, q], axis=0))
            ks.append(k)
        keys = jnp.concatenate([jnp.concatenate([ks[0], zeros_k], axis=1),
                                jnp.concatenate([zeros_k, ks[1]], axis=1)], axis=0)
        gram_scr[ppid(s, c, hp)] = _dot_nt(jnp.concatenate(kq, axis=1).astype(BF16), keys.astype(BF16))

    for s, c, hp in pairs:
        pp, r, ha, hb = ppid(s, c, hp), row0(s, c), 2 * hp, 2 * hp + 1
        gc_c = jnp.where(half_b, gcc_scr[r:r + C, hb:hb + 1], gcc_scr[r:r + C, ha:ha + 1])
        gcr = gcr_scr[s * n_chunks + c]
        gc_r = jnp.concatenate([gcr[ha:ha + 1, :], gcr[hb:hb + 1, :]], axis=1)
        decay = jnp.exp(jnp.where(incl, gc_c - gc_r, -jnp.inf))
        gram = gram_scr[pp]
        lower = jnp.where(strict, gram[:C] * decay, 0.0)
        low_scr[pp] = lower
        attn_scr[pp] = gram[C:] * decay
        inv_scr[pp] = eye - lower
        pow_scr[pp] = _dot((-lower).astype(BF16), block_diag(-lower).astype(BF16))

    power = 2
    while power < C:
        last = 2 * power >= C
        for s, c, hp in pairs:
            pp = ppid(s, c, hp)
            pw = pow_scr[pp]
            inv = inv_scr[pp]
            pb = block_diag(pw).astype(BF16)
            if last:
                inv_scr[pp] = inv + _dot(inv.astype(BF16), pb)
            else:
                both = _dot(jnp.concatenate([inv, pw], axis=0).astype(BF16), pb)
                inv_scr[pp] = inv + both[:C]
                pow_scr[pp] = both[C:]
        power *= 2

    def stacked_rhs(s, c, hp):
        return jnp.concatenate([rhs_scr[pid(s, c, 2 * hp)], rhs_scr[pid(s, c, 2 * hp + 1)]], axis=0).astype(BF16)

    for s, c, hp in pairs:
        inv = inv_scr[ppid(s, c, hp)]
        rhs = stacked_rhs(s, c, hp)
        x0_scr[pid(s, c, 2 * hp)] = _dot(first_head(inv).astype(BF16), rhs)
        x0_scr[pid(s, c, 2 * hp + 1)] = _dot(second_head(inv).astype(BF16), rhs)
    for s, c, hp in pairs:
        pa, pb2 = pid(s, c, 2 * hp), pid(s, c, 2 * hp + 1)
        low = low_scr[ppid(s, c, hp)]
        xa_hi, xa_lo = _split_bf16(x0_scr[pa])
        xb_hi, xb_lo = _split_bf16(x0_scr[pb2])
        x_both = jnp.concatenate([jnp.concatenate([xa_hi, xa_lo], axis=1),
                                  jnp.concatenate([xb_hi, xb_lo], axis=1)], axis=0)
        x_hi = jnp.concatenate([xa_hi, xb_hi], axis=0)
        for p, part in ((pa, first_head(low)), (pb2, second_head(low))):
            l_hi, l_lo = _split_bf16(part)
            lx = _dot(l_hi, x_both)
            lx = lx[:, :2 * HEAD_DIM] + (lx[:, 2 * HEAD_DIM:] + _dot(l_lo, x_hi))
            rhs_scr[p] = rhs_scr[p] - x0_scr[p] - lx
    for s, c, hp in pairs:
        inv = inv_scr[ppid(s, c, hp)]
        resid = stacked_rhs(s, c, hp)
        for p, part in ((pid(s, c, 2 * hp), first_head(inv)), (pid(s, c, 2 * hp + 1), second_head(inv))):
            sol = x0_scr[p] + _dot(part.astype(BF16), resid)
            u_scr[p] = sol[:, :HEAD_DIM]
            wq_scr[p, 0:C, :] = sol[:, HEAD_DIM:]

    for c in range(n_chunks):
        for s in range(n_seq):
            for h in range(N_HEADS):
                sh = s * N_HEADS + h
                ws_scr[sh] = _dot(wq_scr[pid(s, c, h)].astype(BF16), s_scr[sh].astype(BF16))
        for s in range(n_seq):
            for hp in range(N_HEADS // 2):
                r = row0(s, c)
                rows = slice(r, r + C)
                heads = (2 * hp, 2 * hp + 1)
                vn = [(u_scr[pid(s, c, h)] - ws_scr[s * N_HEADS + h][:C]).astype(BF16) for h in heads]
                vn_both = jnp.concatenate(vn, axis=0)
                attn = attn_scr[ppid(s, c, hp)]
                for h, vn16, part in ((heads[0], vn[0], first_head(attn)), (heads[1], vn[1], second_head(attn))):
                    p, lo, sh = pid(s, c, h), h * HEAD_DIM, s * N_HEADS + h
                    o = ws_scr[sh][C:] + _dot(part.astype(BF16), vn_both)
                    g_last = gcc_scr[r + C - 1:r + C, h:h + 1]
                    s_scr[sh] = s_scr[sh] * jnp.exp(g_last) + _dot_tn(kout_scr[p].astype(BF16), vn16)
                    z = main_ref[rows, 3 * W + lo:3 * W + lo + HEAD_DIM]
                    og_ref[rows, lo:lo + HEAD_DIM] = _rms(o, onorm) * _silu(z)

    @pl.when(t == pl.num_programs(1) - 1)
    def _():
        for s in range(n_seq):
            for h in range(N_HEADS):
                sf_ref[s, h] = s_scr[s * N_HEADS + h]


def _gdn_recurrence(main, ab, conv_w, a_log, dt_bias, out_norm, cb0, s0, *, batch, seq, n_chunks, n_seq):
    n = main.shape[0]
    chunk = min(MAX_CHUNK, seq)
    tb = chunk * n_chunks
    assert seq % tb == 0 and batch % n_seq == 0 and seq >= GD_CONV - 1
    assert n_seq == 1 or tb == seq, "several sequences per step only when a step covers whole sequences"
    nt = seq // tb
    rows = n_seq * tb
    n_prob = n_seq * n_chunks * N_HEADS
    has_s0 = s0 is not None
    state_spec = pl.BlockSpec((n_seq, N_HEADS, HEAD_DIM, HEAD_DIM), lambda b, t: (b, 0, 0, 0))
    cb_spec = pl.BlockSpec((n_seq, GD_CONV - 1, GD_QKV), lambda b, t: (b, 0, 0))
    small = lambda shape: pl.BlockSpec(shape, lambda b, t: (0,) * len(shape))
    in_specs = [pl.BlockSpec((rows, main.shape[1]), lambda b, t: (b * nt + t, 0)),
                pl.BlockSpec((rows, 2 * N_HEADS), lambda b, t: (b * nt + t, 0)),
                small((GD_CONV, GD_QKV)), small((1, N_HEADS)), small((1, N_HEADS)), small((1, HEAD_DIM))]
    args = [main, ab, conv_w, a_log.reshape(1, N_HEADS), dt_bias.reshape(1, N_HEADS), out_norm.reshape(1, HEAD_DIM)]
    if has_s0:
        in_specs += [cb_spec, state_spec]
        args += [cb0, s0]
    vm = lambda *shape: pltpu.VMEM(shape, F32)
    return pl.pallas_call(
        functools.partial(_gdn_kernel, chunk=chunk, n_chunks=n_chunks, n_seq=n_seq, has_s0=has_s0),
        grid=(batch // n_seq, nt),
        in_specs=in_specs,
        out_specs=[pl.BlockSpec((rows, D_MODEL), lambda b, t: (b * nt + t, 0)), state_spec, cb_spec],
        out_shape=[jax.ShapeDtypeStruct((n, D_MODEL), F32),
                   jax.ShapeDtypeStruct((batch, N_HEADS, HEAD_DIM, HEAD_DIM), F32),
                   jax.ShapeDtypeStruct((batch, GD_CONV - 1, GD_QKV), F32)],
        scratch_shapes=[vm(n_seq * N_HEADS, HEAD_DIM, HEAD_DIM),
                        vm(n_seq, tb + SUBLANES, GD_QKV),
                        vm(rows, GD_QKV),
                        vm(rows, N_HEADS),
                        vm(n_seq * n_chunks, N_HEADS, chunk),
                        vm(rows, N_HEADS),
                        vm(n_prob // 2, 2 * chunk, 2 * chunk),
                        vm(n_prob, chunk, 2 * HEAD_DIM),
                        vm(n_prob, 2 * chunk, HEAD_DIM),
                        vm(n_prob, chunk, HEAD_DIM),
                        vm(n_prob // 2, chunk, 2 * chunk),
                        vm(n_prob // 2, chunk, 2 * chunk),
                        vm(n_prob // 2, chunk, 2 * chunk),
                        vm(n_prob // 2, chunk, 2 * chunk),
                        vm(n_prob, chunk, 2 * HEAD_DIM),
                        vm(n_prob, chunk, HEAD_DIM),
                        vm(n_seq * N_HEADS, 2 * chunk, HEAD_DIM)],
        compiler_params=_params("parallel", "arbitrary"),
        name="gdn_recurrence",
    )(*args)


def _swiglu_rows(x_of, wg, wu, wd, n_sub, emit):
    pending = None
    for sb in range(n_sub + 1):
        if sb < n_sub:
            x = x_of(sb)
            nxt = (_dot(x, wg), _dot(x, wu))
        if pending is not None:
            a, u = pending
            emit(sb - 1, _dot((_silu(a) * u).astype(BF16), wd))
        pending = nxt


def _ffn_kernel(x_ref, og_ref, wo_ref, g_ref, wg_ref, wu_ref, wd_ref, o_ref, h_scr, *, n_sub):
    j = pl.program_id(1)
    sub = x_ref.shape[0] // n_sub

    @pl.when(j == 0)
    def _():
        x = x_ref[...] + _dot(og_ref[...].astype(BF16), wo_ref[...])
        h_scr[...] = _rms(x, g_ref[...]).astype(BF16)
        o_ref[...] = x

    def emit(sb, y):
        o_ref[sb * sub:(sb + 1) * sub, :] += y

    _swiglu_rows(lambda sb: h_scr[sb * sub:(sb + 1) * sub, :], wg_ref[...], wu_ref[...], wd_ref[...], n_sub, emit)


def _ffn(x, og, w_out, gain, wg, wu, wd, *, tm, tf, n_sub):
    n, d = x.shape
    f = wg.shape[1]
    rows = pl.BlockSpec((tm, d), lambda i, j: (i, 0))
    return pl.pallas_call(
        functools.partial(_ffn_kernel, n_sub=n_sub),
        grid=(n // tm, f // tf),
        in_specs=[rows, rows,
                  pl.BlockSpec((d, d), lambda i, j: (0, 0)),
                  pl.BlockSpec((1, d), lambda i, j: (0, 0)),
                  pl.BlockSpec((d, tf), lambda i, j: (0, j)),
                  pl.BlockSpec((d, tf), lambda i, j: (0, j)),
                  pl.BlockSpec((tf, d), lambda i, j: (j, 0))],
        out_specs=rows,
        out_shape=jax.ShapeDtypeStruct((n, d), F32),
        scratch_shapes=[pltpu.VMEM((tm, d), BF16)],
        compiler_params=_params("parallel", "arbitrary"),
        name="dense_ffn",
    )(x, og, w_out, gain.reshape(1, d), wg, wu, wd)


ROUTER_SLOTS = 128
PAD_ROWS = 256


RUN_ALIGN = 32


def _run_copies(n_rows, max_rows, make_copy):
    pieces = []
    off = jnp.int32(0)
    bit = max_rows
    while bit >= RUN_ALIGN:
        has = (n_rows & bit) != 0
        pieces.append((has, make_copy(off, bit)))
        off = off + (n_rows & bit)
        bit //= 2
    return pieces


def _router_kernel(x_ref, og_ref, wo_ref, g_ref, wr_ref, xo_ref, dest_ref, gate_ref, cnt_ref, xs_hbm,
                   cnt_scr, stage, zeros_buf, n_smem, base_smem, sem, *, tm, cap):
    i = pl.program_id(0)
    last = pl.num_programs(0) - 1

    @pl.when(i == 0)
    def _():
        cnt_scr[...] = jnp.zeros_like(cnt_scr)
        zeros_buf[...] = jnp.zeros_like(zeros_buf)
        for e in range(N_EXPERTS):
            n_smem[e] = 0
            base_smem[e] = e * cap

    x = x_ref[...] + _dot(og_ref[...].astype(BF16), wo_ref[...])
    xo_ref[...] = x
    h = _rms(x, g_ref[...])
    h16 = h.astype(BF16)
    h_lo = (h - h16.astype(F32)).astype(BF16)
    w_hi, w_lo = _split_bf16(wr_ref[...])
    logits = _dot(h16, w_hi) + (_dot(h16, w_lo) + _dot(h_lo, w_hi))
    lane = lax.broadcasted_iota(jnp.int32, logits.shape, 1)
    m1 = jnp.max(logits, axis=-1, keepdims=True)
    i1 = jnp.min(jnp.where(logits == m1, lane, N_EXPERTS), axis=-1, keepdims=True)
    rest = jnp.where(lane == i1, -jnp.inf, logits)
    m2 = jnp.max(rest, axis=-1, keepdims=True)
    i2 = jnp.min(jnp.where(rest == m2, lane, N_EXPERTS), axis=-1, keepdims=True)
    e2 = jnp.exp(m2 - m1)
    den = 1.0 + e2
    two = lax.broadcasted_iota(jnp.int32, (tm, TOP_K), 1)
    gate_ref[...] = jnp.where(two == 0, 1.0 / den, e2 / den)

    onehot = ((lane == i1) | (lane == i2)).astype(F32)
    row = lax.broadcasted_iota(jnp.int32, (tm, tm), 0)
    col = lax.broadcasted_iota(jnp.int32, (tm, tm), 1)
    local = _dot((row > col).astype(BF16), onehot.astype(BF16))
    rank = local + cnt_scr[...]
    r1 = jnp.sum(jnp.where(lane == i1, rank, 0.0), axis=-1, keepdims=True).astype(jnp.int32)
    r2 = jnp.sum(jnp.where(lane == i2, rank, 0.0), axis=-1, keepdims=True).astype(jnp.int32)
    dest_ref[...] = jnp.where(two == 0, i1 * cap + r1, i2 * cap + r2)
    cnt_tile = jnp.sum(onehot, axis=0, keepdims=True)
    cnt = cnt_scr[...] + cnt_tile
    cnt_scr[...] = cnt
    cnt_ref[...] = cnt.astype(jnp.int32)

    def wait_runs():
        for e in range(N_EXPERTS):
            for has, copy in _run_copies(n_smem[e], tm, lambda off, size, e=e: pltpu.make_async_copy(
                    stage.at[e, pl.ds(0, size * SUBLANES)], xs_hbm.at[pl.ds(0, size * SUBLANES)], sem)):
                @pl.when(has)
                def _():
                    copy.wait()

    wait_runs()

    code = (onehot * (local + 1.0)).astype(BF16)
    code_t = _dot_tn(code, (row == col).astype(BF16))
    slot_id = lax.broadcasted_iota(jnp.int32, (ROUTER_SLOTS, tm), 0)
    code_i = code_t.astype(jnp.int32)

    def compact(e, q):
        sel = jnp.where(code_i[e:e + 1, :] == slot_id + (q * ROUTER_SLOTS + 1), 1.0, 0.0).astype(BF16)
        _store_token_tiles(stage.at[e], q * ROUTER_SLOTS, _dot(sel, h16))

    for e in range(N_EXPERTS):
        compact(e, 0)
    for e in range(N_EXPERTS):
        n_e = cnt_tile[0, e].astype(jnp.int32)
        for q in range(1, tm // ROUTER_SLOTS):
            @pl.when(q * ROUTER_SLOTS < n_e)
            def _():
                compact(e, q)
        n_copy = jnp.bitwise_and(n_e + (RUN_ALIGN - 1), -RUN_ALIGN)
        base = base_smem[e]
        for has, copy in _run_copies(n_copy, tm, lambda off, size, e=e, base=base: pltpu.make_async_copy(
                stage.at[e, pl.ds(pl.multiple_of(off * SUBLANES, SUBLANES), size * SUBLANES)],
                xs_hbm.at[pl.ds(pl.multiple_of((base + off) * SUBLANES, SUBLANES), size * SUBLANES)], sem)):
            @pl.when(has)
            def _():
                copy.start()
        n_smem[e] = n_copy
        base_smem[e] = base + n_e

    @pl.when(i == last)
    def _():
        wait_runs()
        pads = [pltpu.make_async_copy(
            zeros_buf, xs_hbm.at[pl.ds(pl.multiple_of(base_smem[e] * SUBLANES, SUBLANES), PAD_ROWS * SUBLANES)], sem)
            for e in range(N_EXPERTS)]
        for pad in pads:
            pad.start()
        for pad in pads:
            pad.wait()


def _router(x, og, w_out, gain, w_router, *, tm, cap):
    n, d = x.shape
    two = pl.BlockSpec((tm, TOP_K), lambda i: (i, 0))
    rows = pl.BlockSpec((tm, d), lambda i: (i, 0))
    return pl.pallas_call(
        functools.partial(_router_kernel, tm=tm, cap=cap),
        grid=(n // tm,),
        in_specs=[rows, rows,
                  pl.BlockSpec((d, d), lambda i: (0, 0)),
                  pl.BlockSpec((1, d), lambda i: (0, 0)),
                  pl.BlockSpec((d, N_EXPERTS), lambda i: (0, 0))],
        out_specs=[rows, two, two, pl.BlockSpec((1, N_EXPERTS), lambda i: (0, 0)),
                   pl.BlockSpec(memory_space=pl.ANY)],
        out_shape=[jax.ShapeDtypeStruct((n, d), F32),
                   jax.ShapeDtypeStruct((n, TOP_K), jnp.int32),
                   jax.ShapeDtypeStruct((n, TOP_K), F32),
                   jax.ShapeDtypeStruct((1, N_EXPERTS), jnp.int32),
                   jax.ShapeDtypeStruct((N_EXPERTS * cap * SUBLANES, LANES), F32)],
        scratch_shapes=[pltpu.VMEM((1, N_EXPERTS), F32),
                        pltpu.VMEM((N_EXPERTS, tm * SUBLANES, LANES), F32),
                        pltpu.VMEM((PAD_ROWS * SUBLANES, LANES), F32),
                        pltpu.SMEM((N_EXPERTS,), jnp.int32), pltpu.SMEM((N_EXPERTS,), jnp.int32),
                        pltpu.SemaphoreType.DMA(())],
        compiler_params=_params("arbitrary"),
        name="moe_router",
    )(x, og, w_out, gain.reshape(1, d), w_router)


def _tile_gather_copy(src_hbm, dst, sem, src_row, dst_row):
    return pltpu.make_async_copy(src_hbm.at[pl.ds(pl.multiple_of(src_row * SUBLANES, SUBLANES), SUBLANES)],
                                 dst.at[pl.ds(pl.multiple_of(dst_row * SUBLANES, SUBLANES), SUBLANES)], sem)


def _load_token_tiles(ref, first_row, n_rows):
    return jnp.concatenate([ref[pl.ds(first_row * SUBLANES + s, n_rows, stride=SUBLANES), :]
                            for s in range(D_MODEL // LANES)], axis=1)


def _store_token_tiles(ref, first_row, x):
    for s in range(D_MODEL // LANES):
        ref[pl.ds(first_row * SUBLANES + s, x.shape[0], stride=SUBLANES), :] = x[:, s * LANES:(s + 1) * LANES]


def _expert_kernel(be_ref, xblk_ref, oblk_ref, nv_ref, nu_ref, xs_ref, wg_ref, wu_ref, wd_ref, yb_ref, xb, acc,
                   *, sub_rows, n_sub):
    i = pl.program_id(0)
    j = pl.program_id(1)
    n_valid = nv_ref[i]

    @pl.when(j == 0)
    def _():
        acc[...] = jnp.zeros_like(acc)
        for sb in range(n_sub):
            @pl.when(sb * sub_rows < n_valid)
            def _():
                xb[sb * sub_rows:(sb + 1) * sub_rows, :] = _load_token_tiles(
                    xs_ref, sb * sub_rows, sub_rows).astype(BF16)

    def emit(sb, y):
        acc[sb * sub_rows:(sb + 1) * sub_rows, :] += y

    def x_of(sb):
        return xb[sb * sub_rows:(sb + 1) * sub_rows, :]

    full = n_valid > (n_sub - 1) * sub_rows

    @pl.when(full)
    def _():
        _swiglu_rows(x_of, wg_ref[0], wu_ref[0], wd_ref[0], n_sub, emit)

    for sb in range(n_sub - 1):
        @pl.when(jnp.logical_and(jnp.logical_not(full), sb * sub_rows < n_valid))
        def _():
            _swiglu_rows(lambda _, sb=sb: x_of(sb), wg_ref[0], wu_ref[0], wd_ref[0], 1, lambda _, y, sb=sb: emit(sb, y))

    @pl.when(j == pl.num_programs(1) - 1)
    def _():
        for sb in range(n_sub):
            rows = slice(sb * sub_rows, (sb + 1) * sub_rows)

            @pl.when(sb * sub_rows < n_valid)
            def _():
                _store_token_tiles(yb_ref, sb * sub_rows, acc[rows, :])

            @pl.when(sb * sub_rows >= n_valid)
            def _():
                tiles = slice(sb * sub_rows * SUBLANES, (sb + 1) * sub_rows * SUBLANES)
                yb_ref[tiles, :] = jnp.zeros((sub_rows * SUBLANES, LANES), F32)


def _experts(xs, block_e, x_block, out_block, n_valid, n_used, wg, wu, wd, *, sub_rows, n_sub, tf, out_blocks):
    d = D_MODEL
    tmm = sub_rows * n_sub
    nb = block_e.shape[0]
    nf = D_FF // tf

    def wcol(i, j, be, xblk, oblk, nv, nu):
        return (be[i], 0, jnp.where(i < nu[0], j, nf - 1))

    def wrow(i, j, be, xblk, oblk, nv, nu):
        return (be[i], jnp.where(i < nu[0], j, nf - 1), 0)

    grid_spec = pltpu.PrefetchScalarGridSpec(
        num_scalar_prefetch=5,
        grid=(nb, nf),
        in_specs=[pl.BlockSpec((tmm * SUBLANES, LANES), lambda i, j, be, xblk, oblk, nv, nu: (xblk[i], 0)),
                  pl.BlockSpec((1, d, tf), wcol),
                  pl.BlockSpec((1, d, tf), wcol),
                  pl.BlockSpec((1, tf, d), wrow)],
        out_specs=pl.BlockSpec((tmm * SUBLANES, LANES), lambda i, j, be, xblk, oblk, nv, nu: (oblk[i], 0)),
        scratch_shapes=[pltpu.VMEM((tmm, d), BF16), pltpu.VMEM((tmm, d), F32)],
    )
    return pl.pallas_call(
        functools.partial(_expert_kernel, sub_rows=sub_rows, n_sub=n_sub),
        grid_spec=grid_spec,
        out_shape=jax.ShapeDtypeStruct((out_blocks * tmm * SUBLANES, LANES), F32),
        compiler_params=_params("arbitrary", "arbitrary"),
        name="moe_experts",
    )(block_e, x_block, out_block, n_valid, n_used, xs, wg, wu, wd)


def _combine_kernel(dest_ref, yb_hbm, x_ref, gate_ref, gain_ref, o_ref, r0, r1, sems, *, tm):
    i = pl.program_id(0)
    slot = lax.rem(i, 2)

    def start_gather(tile, dst_slot):
        def issue(r, carry):
            first = TOP_K * (tile * tm + r)
            _tile_gather_copy(yb_hbm, r0.at[dst_slot], sems.at[dst_slot, 0], dest_ref[first], r).start(priority=0)
            _tile_gather_copy(yb_hbm, r1.at[dst_slot], sems.at[dst_slot, 1], dest_ref[first + 1], r).start(priority=1)
            return carry

        lax.fori_loop(0, tm, issue, 0, unroll=8)

    @pl.when(i == 0)
    def _():
        start_gather(0, 0)

    @pl.when(i + 1 < pl.num_programs(0))
    def _():
        start_gather(i + 1, 1 - slot)

    whole = pl.ds(0, tm * SUBLANES)
    pltpu.make_async_copy(yb_hbm.at[whole], r0.at[slot], sems.at[slot, 0]).wait()
    pltpu.make_async_copy(yb_hbm.at[whole], r1.at[slot], sems.at[slot, 1]).wait()
    g = gate_ref[...]
    y = _load_token_tiles(r0.at[slot], 0, tm) * g[:, 0:1] + _load_token_tiles(r1.at[slot], 0, tm) * g[:, 1:2]
    o_ref[...] = _rms(x_ref[...] + y, gain_ref[...])


def _combine(yb, dest_flat, x, gates, gain, *, tm):
    n, d = x.shape
    grid_spec = pltpu.PrefetchScalarGridSpec(
        num_scalar_prefetch=1,
        grid=(n // tm,),
        in_specs=[pl.BlockSpec(memory_space=pl.ANY),
                  pl.BlockSpec((tm, d), lambda i, dest: (i, 0)),
                  pl.BlockSpec((tm, TOP_K), lambda i, dest: (i, 0)),
                  pl.BlockSpec((1, d), lambda i, dest: (0, 0))],
        out_specs=pl.BlockSpec((tm, d), lambda i, dest: (i, 0)),
        scratch_shapes=[pltpu.VMEM((2, tm * SUBLANES, LANES), F32), pltpu.VMEM((2, tm * SUBLANES, LANES), F32),
                        pltpu.SemaphoreType.DMA((2, TOP_K))],
    )
    return pl.pallas_call(
        functools.partial(_combine_kernel, tm=tm),
        grid_spec=grid_spec,
        out_shape=jax.ShapeDtypeStruct((n, d), F32),
        compiler_params=_params("arbitrary"),
        name="moe_combine",
    )(dest_flat, yb, x, gates, gain.reshape(1, d))


def _moe_and_final_norm(x, og, w_out, gain, w_router, wg, wu, wd, final_gain, *, tiles):
    n = x.shape[0]
    sub_rows, n_sub = tiles["moe_sub_rows"], tiles["moe_subs"]
    tmm = sub_rows * n_sub
    assert sub_rows == PAD_ROWS and n % tmm == 0
    cap = n + tmm
    x, dest, gates, counts, xs = _router(x, og, w_out, gain, w_router, tm=tiles["router_rows"], cap=cap)
    counts = counts[0]
    blocks_per_e = (counts + tmm - 1) // tmm
    blocks_end = jnp.cumsum(blocks_per_e)
    n_used = blocks_end[-1].astype(jnp.int32)
    nb = -(-(n * TOP_K + N_EXPERTS * (tmm - 1)) // tmm)
    blk = jnp.minimum(jnp.arange(nb, dtype=jnp.int32), n_used - 1)
    block_e = jnp.minimum(jnp.sum(blk[:, None] >= blocks_end[None, :], axis=1), N_EXPERTS - 1).astype(jnp.int32)
    k_in_e = blk - (blocks_end - blocks_per_e)[block_e]
    x_block = (block_e * (cap // tmm) + k_in_e).astype(jnp.int32)
    used = jnp.arange(nb, dtype=jnp.int32) < n_used
    n_valid = jnp.where(used, jnp.clip(counts[block_e] - k_in_e * tmm, 0, tmm), 0).astype(jnp.int32)
    dump_block = N_EXPERTS * (cap // tmm)
    out_block = jnp.where(used, x_block, dump_block).astype(jnp.int32)
    yb = _experts(xs, block_e, x_block, out_block, n_valid, n_used.reshape(1), wg, wu, wd,
                  sub_rows=sub_rows, n_sub=n_sub, tf=tiles["moe_ff_cols"], out_blocks=dump_block + 1)
    return _combine(yb, dest.reshape(-1), x, gates, final_gain, tm=tiles["combine_rows"])


def _tiles(n_rows, seq):
    big = n_rows >= 8192
    return {
        "proj_rows": 1024 if n_rows % 1024 == 0 else n_rows,
        "proj_cols": 2048,
        "rec_chunks": min(4, seq // min(MAX_CHUNK, seq)),
        "rec_seqs": 1 if seq > MAX_CHUNK else 4,
        "ffn_rows": 1024 if n_rows % 1024 == 0 else n_rows,
        "ff_cols": 512,
        "router_rows": 256,
        "moe_sub_rows": 256,
        "moe_subs": 4 if big else 2,
        "moe_ff_cols": 1792,
        "ffn_subs": 4 if big else 2,
        "combine_rows": 256,
    }


def _trunk(x3, hg_state, gd_state, gd_conv, p):
    batch, seq, d = x3.shape
    n = batch * seq
    x = x3.reshape(n, d)
    tiles = _tiles(n, seq)
    tm, tn = tiles["proj_rows"], tiles["proj_cols"]

    proj = _norm_proj(x, p["norm_mix"][0], p["hgrn_w_in"], tm=tm, tn=tn)[0]
    og, hg_new = _hgrn_recurrence(proj, p["hgrn_lb"], p["hgrn_norm"], hg_state, batch=batch, seq=seq, layer=0,
                                  n_chunks=tiles["rec_chunks"], n_seq=tiles["rec_seqs"])
    x = _ffn(x, og, p["hgrn_w_out"], p["norm_ffn"][0], p["ffn_w_gate"], p["ffn_w_up"], p["ffn_w_down"],
             tm=tiles["ffn_rows"], tf=tiles["ff_cols"], n_sub=tiles["ffn_subs"])

    main, ab = _norm_proj(x, p["norm_mix"][1], p["gdn_w_main"], p["gdn_w_ab"], tm=tm, tn=tn)
    og, gd_new, cv_new = _gdn_recurrence(main, ab, p["gdn_conv"], p["gdn_a_log"], p["gdn_dt_bias"], p["gdn_norm"],
                                         gd_conv, gd_state, batch=batch, seq=seq, n_chunks=tiles["rec_chunks"],
                                         n_seq=tiles["rec_seqs"])
    y = _moe_and_final_norm(x, og, p["gdn_w_out"], p["norm_ffn"][1], p["moe_router"], p["moe_w_gate"], p["moe_w_up"], p["moe_w_down"],
                            p["norm_out"], tiles=tiles)
    return y.reshape(batch, seq, d), hg_new[None], gd_new[None], cv_new[None]


def kernel(x_prompt, x_sample, state_hgrn, state_gdn, state_gdn_conv, norm_mix, norm_ffn, norm_out, hgrn_w_in, hgrn_lb, hgrn_norm, hgrn_w_out, gdn_w_in, gdn_conv, gdn_a_log, gdn_dt_bias, gdn_norm, gdn_w_out, ffn_w_gate, ffn_w_up, ffn_w_down, moe_router, moe_w_gate, moe_w_up, moe_w_down):
    assert hgrn_w_in.shape[0] == 1 and gdn_w_in.shape[0] == 1, "one HGRN2 layer and one gated-DeltaNet layer"
    main_cols = GD_QKV + N_HEADS * HEAD_DIM
    p = {
        "norm_mix": norm_mix, "norm_ffn": norm_ffn, "norm_out": norm_out,
        "hgrn_w_in": hgrn_w_in[0].astype(BF16), "hgrn_lb": hgrn_lb, "hgrn_norm": hgrn_norm[0],
        "hgrn_w_out": hgrn_w_out[0].astype(BF16),
        "gdn_w_main": gdn_w_in[0, :, :main_cols].astype(BF16), "gdn_w_ab": gdn_w_in[0, :, main_cols:].astype(BF16),
        "gdn_conv": gdn_conv[0], "gdn_a_log": gdn_a_log[0], "gdn_dt_bias": gdn_dt_bias[0], "gdn_norm": gdn_norm[0],
        "gdn_w_out": gdn_w_out[0].astype(BF16),
        "ffn_w_gate": ffn_w_gate[0].astype(BF16), "ffn_w_up": ffn_w_up[0].astype(BF16),
        "ffn_w_down": ffn_w_down[0].astype(BF16),
        "moe_router": moe_router[0],
        "moe_w_gate": moe_w_gate[0].astype(BF16), "moe_w_up": moe_w_up[0].astype(BF16),
        "moe_w_down": moe_w_down[0].astype(BF16),
    }
    y_p, hg_p, gd_p, cv_p = _trunk(x_prompt, None, None, None, p)
    y_s, hg_s, gd_s, cv_s = _trunk(x_sample, state_hgrn[0], state_gdn[0], state_gdn_conv[0], p)
    return (y_p, y_s, hg_p, hg_s, gd_p, gd_s, cv_p, cv_s)
```

```python
import functools

import jax
import jax.numpy as jnp
from jax import lax
from jax.experimental import pallas as pl
from jax.experimental.pallas import tpu as pltpu

F32 = jnp.float32
BF16 = jnp.bfloat16
HIGHEST = lax.Precision.HIGHEST

D_MODEL = 1024
N_HEADS = 8
HEAD_DIM = 128
D_FF = 3584
N_EXPERTS = 8
TOP_K = 2
GD_QKV = 3 * D_MODEL
GD_CONV = 4
MAX_CHUNK = 64
EPS = 1e-6
LANES = 128
SUBLANES = 8
VMEM_LIMIT_BYTES = 52 * 1024 * 1024


def _params(*sem):
    return pltpu.CompilerParams(dimension_semantics=sem, vmem_limit_bytes=VMEM_LIMIT_BYTES)


def _rms(x, gain):
    return x * lax.rsqrt(jnp.mean(x * x, axis=-1, keepdims=True) + EPS) * gain


def _sigmoid(x):
    t = jnp.exp(-jnp.abs(x))
    r = 1.0 / (1.0 + t)
    return jnp.where(x >= 0, r, t * r)


def _silu(x):
    return x * (0.5 * jnp.tanh(0.5 * x) + 0.5)


def _softplus(x):
    return jnp.maximum(x, 0.0) + jnp.log1p(jnp.exp(-jnp.abs(x)))


def _dot(a, b):
    return jnp.dot(a, b, preferred_element_type=F32)


def _dot_nt(a, b):
    return lax.dot_general(a, b, (((1,), (1,)), ((), ())), preferred_element_type=F32)


def _dot_tn(a, b):
    return lax.dot_general(a, b, (((0,), (0,)), ((), ())), preferred_element_type=F32)


def _dot_hi(a, b):
    return jnp.dot(a, b, precision=HIGHEST, preferred_element_type=F32)


def _norm_proj_kernel(x_ref, g_ref, w_ref, *rest, has_small):
    if has_small:
        ws_ref, o_ref, os_ref, h_scr = rest
    else:
        o_ref, h_scr = rest

    @pl.when(pl.program_id(1) == 0)
    def _():
        hb = _rms(x_ref[...], g_ref[...]).astype(BF16)
        h_scr[...] = hb
        if has_small:
            os_ref[...] = _dot(hb, ws_ref[...])

    o_ref[...] = _dot(h_scr[...], w_ref[...])


def _norm_proj(x, gain, w, w_small=None, *, tm, tn):
    n, d = x.shape
    e = w.shape[1]
    has_small = w_small is not None
    in_specs = [pl.BlockSpec((tm, d), lambda i, j: (i, 0)),
                pl.BlockSpec((1, d), lambda i, j: (0, 0)),
                pl.BlockSpec((d, tn), lambda i, j: (0, j))]
    out_shape = [jax.ShapeDtypeStruct((n, e), F32)]
    out_specs = [pl.BlockSpec((tm, tn), lambda i, j: (i, j))]
    args = [x, gain.reshape(1, d), w]
    if has_small:
        es = w_small.shape[1]
        in_specs.append(pl.BlockSpec((d, es), lambda i, j: (0, 0)))
        out_shape.append(jax.ShapeDtypeStruct((n, es), F32))
        out_specs.append(pl.BlockSpec((tm, es), lambda i, j: (i, 0)))
        args.append(w_small)
    return pl.pallas_call(
        functools.partial(_norm_proj_kernel, has_small=has_small),
        grid=(n // tm, e // tn),
        in_specs=in_specs, out_specs=out_specs, out_shape=out_shape,
        scratch_shapes=[pltpu.VMEM((tm, d), BF16)],
        compiler_params=_params("parallel", "arbitrary"),
        name="norm_proj",
    )(*args)


def _project_ahead(x_cur_ref, x_next_ref, gain_ref, w_refs, dst_scrs):
    g = pl.program_id(0) * pl.num_programs(1) + pl.program_id(1)
    slot = lax.rem(g, 2)

    def project(x_ref, dst_slot):
        h16 = _rms(x_ref[...], gain_ref[...]).astype(BF16)
        for w_ref, dst in zip(w_refs, dst_scrs):
            dst[dst_slot] = _dot(h16, w_ref[...])

    @pl.when(g == 0)
    def _():
        project(x_cur_ref, 0)

    project(x_next_ref, 1 - slot)
    return slot


def _hgrn_kernel(*refs, chunk, n_chunks, n_seq, layer, has_s0, fused):
    if fused:
        (x_cur_ref, x_next_ref, gain_ref, w_ref), refs = refs[:4], refs[4:]
    else:
        proj_ref, refs = refs[0], refs[1:]
    if has_s0:
        lb_ref, on_ref, s0_ref, og_ref, sf_ref, *scr = refs
    else:
        lb_ref, on_ref, og_ref, sf_ref, *scr = refs
    st_scr, b_scr, q_scr, k_scr, a_scr, qin_scr, kout_scr, dec_scr, oi_scr = scr[:9]
    C = chunk
    W = N_HEADS * HEAD_DIM
    tb = C * n_chunks
    t = pl.program_id(1)
    problems = [(s, c, h) for s in range(n_seq) for c in range(n_chunks) for h in range(N_HEADS)]

    def pid(s, c, h):
        return (s * n_chunks + c) * N_HEADS + h

    def row0(s, c):
        return s * tb + c * C

    @pl.when(t == 0)
    def _():
        for s in range(n_seq):
            for h in range(N_HEADS):
                st_scr[s * N_HEADS + h] = s0_ref[s, h].T if has_s0 else jnp.zeros((HEAD_DIM, HEAD_DIM), F32)

    if fused:
        proj_scr = scr[9]
        proj_ref = proj_scr.at[_project_ahead(x_cur_ref, x_next_ref, gain_ref, [w_ref], [proj_scr])]

    lbraw = lb_ref[...]
    ex = jnp.exp(lbraw - jnp.max(lbraw, axis=0, keepdims=True))
    sm = ex / jnp.sum(ex, axis=0, keepdims=True)
    lb = sm[0:1]
    for j in range(1, layer + 1):
        lb = lb + sm[j:j + 1]

    row = lax.broadcasted_iota(jnp.int32, (C, C), 0)
    col = lax.broadcasted_iota(jnp.int32, (C, C), 1)
    causal = row >= col
    tril = causal.astype(BF16)
    mid = C // 2
    scale = HEAD_DIM ** -0.5
    onorm = on_ref[...]

    for s in range(n_seq):
        for c in range(n_chunks):
            rows = slice(row0(s, c), row0(s, c) + C)
            f_raw = proj_ref[rows, W:2 * W]
            tt = jnp.exp(-jnp.abs(f_raw))
            rr = 1.0 / (1.0 + tt)
            pos = f_raw >= 0
            sig = jnp.where(pos, rr, tt * rr)
            sig_neg = jnp.where(pos, tt * rr, rr)
            log_f = jnp.log(lb + (1.0 - lb) * sig)
            k_scr[rows, :] = (1.0 - lb) * sig_neg
            q_scr[rows, :] = _silu(proj_ref[rows, 0:W]) * scale
            hi = log_f.astype(BF16)
            rest = log_f - hi.astype(F32)
            md = rest.astype(BF16)
            lw = (rest - md.astype(F32)).astype(BF16)
            b_scr[rows, :] = (_dot(tril, lw) + _dot(tril, md)) + _dot(tril, hi)

    for s, c, h in problems:
        p, lo = pid(s, c, h), h * HEAD_DIM
        rows = slice(row0(s, c), row0(s, c) + C)
        b = b_scr[rows, lo:lo + HEAD_DIM]
        q = q_scr[rows, lo:lo + HEAD_DIM]
        k = k_scr[rows, lo:lo + HEAD_DIM]
        b_mid = b[mid:mid + 1]
        b_last = b[C - 1:C]
        a_scr[p] = _dot_nt((q * jnp.exp(b - b_mid)).astype(BF16), (k * jnp.exp(b_mid - b)).astype(BF16))
        qin_scr[p] = q * jnp.exp(b)
        kout_scr[p] = k * jnp.exp(b_last - b)
        dec_scr[p] = jnp.exp(b_last)

    for s, c, h in problems:
        p, lo = pid(s, c, h), h * HEAD_DIM
        rows = slice(row0(s, c), row0(s, c) + C)
        a = jnp.where(causal, a_scr[p], 0.0)
        oi_scr[p] = _dot(a.astype(BF16), proj_ref[rows, 2 * W + lo:2 * W + lo + HEAD_DIM].astype(BF16))

    for c in range(n_chunks):
        for s in range(n_seq):
            for h in range(N_HEADS):
                p, lo, sh = pid(s, c, h), h * HEAD_DIM, s * N_HEADS + h
                rows = slice(row0(s, c), row0(s, c) + C)
                st = st_scr[sh]
                o = oi_scr[p] + _dot_nt(qin_scr[p].astype(BF16), st.astype(BF16))
                vb = proj_ref[rows, 2 * W + lo:2 * W + lo + HEAD_DIM].astype(BF16)
                st_scr[sh] = st * dec_scr[p] + _dot_tn(vb, kout_scr[p].astype(BF16))
                gate = proj_ref[rows, 3 * W + lo:3 * W + lo + HEAD_DIM]
                og_ref[rows, lo:lo + HEAD_DIM] = _rms(o, onorm) * _silu(gate)

    @pl.when(t == pl.num_programs(1) - 1)
    def _():
        for s in range(n_seq):
            for h in range(N_HEADS):
                sf_ref[s, h] = st_scr[s * N_HEADS + h].T


def _ahead_specs(x, gain, weights, rows, n_steps, nt):
    d = x.shape[1]
    specs = [pl.BlockSpec((rows, d), lambda b, t: (b * nt + t, 0)),
             pl.BlockSpec((rows, d), lambda b, t: (jnp.minimum(b * nt + t + 1, n_steps - 1), 0)),
             pl.BlockSpec((1, d), lambda b, t: (0, 0))]
    specs += [pl.BlockSpec(w.shape, lambda b, t: (0, 0)) for w in weights]
    return specs, [x, x, gain.reshape(1, d)] + list(weights)


def _hgrn_recurrence(proj, lb_raw, out_norm, s0, *, batch, seq, layer, n_chunks, n_seq, fuse=None):
    n = batch * seq
    chunk = min(MAX_CHUNK, seq)
    tb = chunk * n_chunks
    assert seq % tb == 0 and batch % n_seq == 0
    assert n_seq == 1 or tb == seq, "several sequences per step only when a step covers whole sequences"
    nt = seq // tb
    rows = n_seq * tb
    n_prob = n_seq * n_chunks * N_HEADS
    width = N_HEADS * HEAD_DIM
    has_s0 = s0 is not None
    fused = fuse is not None
    state_spec = pl.BlockSpec((n_seq, N_HEADS, HEAD_DIM, HEAD_DIM), lambda b, t: (b, 0, 0, 0))
    if fused:
        x, gain, w = fuse
        in_specs, args = _ahead_specs(x, gain, [w], rows, (batch // n_seq) * nt, nt)
    else:
        in_specs, args = [pl.BlockSpec((rows, proj.shape[1]), lambda b, t: (b * nt + t, 0))], [proj]
    in_specs += [pl.BlockSpec(lb_raw.shape, lambda b, t: (0, 0)),
                 pl.BlockSpec((1, HEAD_DIM), lambda b, t: (0, 0))]
    args += [lb_raw, out_norm.reshape(1, HEAD_DIM)]
    if has_s0:
        in_specs.append(state_spec)
        args.append(s0)
    vm = lambda *shape: pltpu.VMEM(shape, F32)
    ahead_scratch = [vm(2, rows, 4 * width)] if fused else []
    return pl.pallas_call(
        functools.partial(_hgrn_kernel, chunk=chunk, n_chunks=n_chunks, n_seq=n_seq, layer=layer, has_s0=has_s0,
                          fused=fused),
        grid=(batch // n_seq, nt),
        in_specs=in_specs,
        out_specs=[pl.BlockSpec((rows, D_MODEL), lambda b, t: (b * nt + t, 0)), state_spec],
        out_shape=[jax.ShapeDtypeStruct((n, D_MODEL), F32),
                   jax.ShapeDtypeStruct((batch, N_HEADS, HEAD_DIM, HEAD_DIM), F32)],
        scratch_shapes=[vm(n_seq * N_HEADS, HEAD_DIM, HEAD_DIM),
                        vm(rows, width),
                        vm(rows, width),
                        vm(rows, width),
                        vm(n_prob, chunk, chunk),
                        vm(n_prob, chunk, HEAD_DIM),
                        vm(n_prob, chunk, HEAD_DIM),
                        vm(n_prob, 1, HEAD_DIM),
                        vm(n_prob, chunk, HEAD_DIM)] + ahead_scratch,
        compiler_params=_params("arbitrary", "arbitrary"),
        name="hgrn_recurrence",
    )(*args)


def _split_bf16(x):
    hi = x.astype(BF16)
    return hi, (x - hi.astype(F32)).astype(BF16)


def _gdn_kernel(*refs, chunk, n_chunks, n_seq, has_s0, fused):
    if fused:
        (x_cur_ref, x_next_ref, gain_ref, w_main_ref, w_ab_ref), refs = refs[:5], refs[5:]
    else:
        (main_ref, ab_ref), refs = refs[:2], refs[2:]
    (cw_ref, al_ref, dtb_ref, on_ref), rest = refs[:4], refs[4:]
    if has_s0:
        (cb0_ref, s0_ref), rest = rest[:2], rest[2:]
    (og_ref, sf_ref, cb_ref, s_scr, xpad, act, gcc_scr, gcr_scr, beta_scr, gram_scr, rhs_scr, wq_scr, kout_scr,
     inv_scr, pow_scr, low_scr, attn_scr, x0_scr, u_scr, ws_scr) = rest[:20]
    C = chunk
    W = N_HEADS * HEAD_DIM
    tb = C * n_chunks
    t = pl.program_id(1)
    pad = SUBLANES
    pairs = [(s, c, hp) for s in range(n_seq) for c in range(n_chunks) for hp in range(N_HEADS // 2)]

    def pid(s, c, h):
        return (s * n_chunks + c) * N_HEADS + h

    def ppid(s, c, hp):
        return (s * n_chunks + c) * (N_HEADS // 2) + hp

    def row0(s, c):
        return s * tb + c * C

    @pl.when(t == 0)
    def _():
        for s in range(n_seq):
            for h in range(N_HEADS):
                s_scr[s * N_HEADS + h] = s0_ref[s, h] if has_s0 else jnp.zeros((HEAD_DIM, HEAD_DIM), F32)
            xpad[s, 0:pad, :] = jnp.zeros((pad, GD_QKV), F32)
            if has_s0:
                xpad[s, pad - (GD_CONV - 1):pad, :] = cb0_ref[s]

    @pl.when(t > 0)
    def _():
        for s in range(n_seq):
            xpad[s, 0:pad, :] = xpad[s, tb:tb + pad, :]

    if fused:
        proj_scr, ab_scr = rest[20:22]
        slot = _project_ahead(x_cur_ref, x_next_ref, gain_ref, [w_main_ref, w_ab_ref], [proj_scr, ab_scr])
        main_ref, ab_ref = proj_scr.at[slot], ab_scr.at[slot]

    for s in range(n_seq):
        for j in range(GD_QKV // LANES):
            cs = slice(j * LANES, (j + 1) * LANES)
            xpad[s, pad:pad + tb, cs] = main_ref[s * tb:(s + 1) * tb, cs]
    for s in range(n_seq):
        for j in range(GD_QKV // LANES):
            cs = slice(j * LANES, (j + 1) * LANES)
            conv = xpad[s, pad - 3:pad - 3 + tb, cs] * cw_ref[0:1, cs]
            for tap in range(1, GD_CONV):
                conv = conv + xpad[s, pad - 3 + tap:pad - 3 + tap + tb, cs] * cw_ref[tap:tap + 1, cs]
            act[s * tb:(s + 1) * tb, cs] = _silu(conv)

    C2 = 2 * C
    prow = lax.broadcasted_iota(jnp.int32, (C, C2), 0)
    pcol = lax.broadcasted_iota(jnp.int32, (C, C2), 1)
    half_b = pcol >= C
    pcol_in = jnp.where(half_b, pcol - C, pcol)
    incl = prow >= pcol_in
    strict = prow > pcol_in
    eye = jnp.where(prow == pcol_in, 1.0, 0.0)
    row = lax.broadcasted_iota(jnp.int32, (C, C), 0)
    col = lax.broadcasted_iota(jnp.int32, (C, C), 1)
    tril = (row >= col).astype(BF16)
    triu = (row <= col).astype(BF16)

    def first_head(m):
        return jnp.where(half_b, 0.0, m)

    def second_head(m):
        return jnp.where(half_b, m, 0.0)

    def block_diag(m):
        return jnp.concatenate([first_head(m), second_head(m)], axis=0)

    def split3(x):
        hi = x.astype(BF16)
        rest = x - hi.astype(F32)
        md = rest.astype(BF16)
        return hi, md, (rest - md.astype(F32)).astype(BF16)

    ab = ab_ref[...]
    g_col = -jnp.exp(al_ref[...]) * _softplus(ab[:, 0:N_HEADS] + dtb_ref[...])
    beta_scr[...] = _sigmoid(ab[:, N_HEADS:2 * N_HEADS])
    for s in range(n_seq):
        for c in range(n_chunks):
            r = row0(s, c)
            hi, md, lw = split3(g_col[r:r + C, :])
            gcc_scr[r:r + C, :] = (_dot(tril, lw) + _dot(tril, md)) + _dot(tril, hi)
            gcr_scr[s * n_chunks + c] = (_dot_tn(lw, triu) + _dot_tn(md, triu)) + _dot_tn(hi, triu)

    scale = HEAD_DIM ** -0.5
    onorm = on_ref[...]
    zeros_k = jnp.zeros((C, HEAD_DIM), F32)

    for s, c, hp in pairs:
        r = row0(s, c)
        rows = slice(r, r + C)
        kq, ks = [], []
        for h in (2 * hp, 2 * hp + 1):
            p, lo = pid(s, c, h), h * HEAD_DIM
            q = act[rows, lo:lo + HEAD_DIM]
            k = act[rows, W + lo:W + lo + HEAD_DIM]
            v = act[rows, 2 * W + lo:2 * W + lo + HEAD_DIM]
            q = q * lax.rsqrt(jnp.sum(q * q, axis=-1, keepdims=True) + EPS) * scale
            k = k * lax.rsqrt(jnp.sum(k * k, axis=-1, keepdims=True) + EPS)
            gc_c = gcc_scr[rows, h:h + 1]
            beta_c = beta_scr[rows, h:h + 1]
            k_beta = k * beta_c
            e_gc = jnp.exp(gc_c)
            rhs_scr[p] = jnp.concatenate([v * beta_c, k_beta * e_gc], axis=1)
            wq_scr[p, C:2 * C, :] = q * e_gc
            kout_scr[p] = k * jnp.exp(gc_c[C - 1:C, :] - gc_c)
            kq.append(jnp.concatenate([k_beta, q], axis=0))
            ks.append(k)
        keys = jnp.concatenate([jnp.concatenate([ks[0], zeros_k], axis=1),
                                jnp.concatenate([zeros_k, ks[1]], axis=1)], axis=0)
        gram_scr[ppid(s, c, hp)] = _dot_nt(jnp.concatenate(kq, axis=1).astype(BF16), keys.astype(BF16))

    for s, c, hp in pairs:
        pp, r, ha, hb = ppid(s, c, hp), row0(s, c), 2 * hp, 2 * hp + 1
        gc_c = jnp.where(half_b, gcc_scr[r:r + C, hb:hb + 1], gcc_scr[r:r + C, ha:ha + 1])
        gcr = gcr_scr[s * n_chunks + c]
        gc_r = jnp.concatenate([gcr[ha:ha + 1, :], gcr[hb:hb + 1, :]], axis=1)
        decay = jnp.exp(jnp.where(incl, gc_c - gc_r, -jnp.inf))
        gram = gram_scr[pp]
        lower = jnp.where(strict, gram[:C] * decay, 0.0)
        low_scr[pp] = lower
        attn_scr[pp] = gram[C:] * decay
        inv_scr[pp] = eye - lower
        pow_scr[pp] = _dot((-lower).astype(BF16), block_diag(-lower).astype(BF16))

    power = 2
    while power < C:
        last = 2 * power >= C
        for s, c, hp in pairs:
            pp = ppid(s, c, hp)
            pw = pow_scr[pp]
            inv = inv_scr[pp]
            pb = block_diag(pw).astype(BF16)
            if last:
                inv_scr[pp] = inv + _dot(inv.astype(BF16), pb)
            else:
                both = _dot(jnp.concatenate([inv, pw], axis=0).astype(BF16), pb)
                inv_scr[pp] = inv + both[:C]
                pow_scr[pp] = both[C:]
        power *= 2

    def stacked_rhs(s, c, hp):
        return jnp.concatenate([rhs_scr[pid(s, c, 2 * hp)], rhs_scr[pid(s, c, 2 * hp + 1)]], axis=0).astype(BF16)

    for s, c, hp in pairs:
        inv = inv_scr[ppid(s, c, hp)]
        rhs = stacked_rhs(s, c, hp)
        x0_scr[pid(s, c, 2 * hp)] = _dot(first_head(inv).astype(BF16), rhs)
        x0_scr[pid(s, c, 2 * hp + 1)] = _dot(second_head(inv).astype(BF16), rhs)
    for s, c, hp in pairs:
        pa, pb2 = pid(s, c, 2 * hp), pid(s, c, 2 * hp + 1)
        low = low_scr[ppid(s, c, hp)]
        xa_hi, xa_lo = _split_bf16(x0_scr[pa])
        xb_hi, xb_lo = _split_bf16(x0_scr[pb2])
        x_both = jnp.concatenate([jnp.concatenate([xa_hi, xa_lo], axis=1),
                                  jnp.concatenate([xb_hi, xb_lo], axis=1)], axis=0)
        x_hi = jnp.concatenate([xa_hi, xb_hi], axis=0)
        for p, part in ((pa, first_head(low)), (pb2, second_head(low))):
            l_hi, l_lo = _split_bf16(part)
            lx = _dot(l_hi, x_both)
            lx = lx[:, :2 * HEAD_DIM] + (lx[:, 2 * HEAD_DIM:] + _dot(l_lo, x_hi))
            rhs_scr[p] = rhs_scr[p] - x0_scr[p] - lx
    for s, c, hp in pairs:
        inv = inv_scr[ppid(s, c, hp)]
        resid = stacked_rhs(s, c, hp)
        for p, part in ((pid(s, c, 2 * hp), first_head(inv)), (pid(s, c, 2 * hp + 1), second_head(inv))):
            sol = x0_scr[p] + _dot(part.astype(BF16), resid)
            u_scr[p] = sol[:, :HEAD_DIM]
            wq_scr[p, 0:C, :] = sol[:, HEAD_DIM:]

    for c in range(n_chunks):
        for s in range(n_seq):
            for h in range(N_HEADS):
                sh = s * N_HEADS + h
                ws_scr[sh] = _dot(wq_scr[pid(s, c, h)].astype(BF16), s_scr[sh].astype(BF16))
        for s in range(n_seq):
            for hp in range(N_HEADS // 2):
                r = row0(s, c)
                rows = slice(r, r + C)
                heads = (2 * hp, 2 * hp + 1)
                vn = [(u_scr[pid(s, c, h)] - ws_scr[s * N_HEADS + h][:C]).astype(BF16) for h in heads]
                vn_both = jnp.concatenate(vn, axis=0)
                attn = attn_scr[ppid(s, c, hp)]
                for h, vn16, part in ((heads[0], vn[0], first_head(attn)), (heads[1], vn[1], second_head(attn))):
                    p, lo, sh = pid(s, c, h), h * HEAD_DIM, s * N_HEADS + h
                    o = ws_scr[sh][C:] + _dot(part.astype(BF16), vn_both)
                    g_last = gcc_scr[r + C - 1:r + C, h:h + 1]
                    s_scr[sh] = s_scr[sh] * jnp.exp(g_last) + _dot_tn(kout_scr[p].astype(BF16), vn16)
                    z = main_ref[rows, 3 * W + lo:3 * W + lo + HEAD_DIM]
                    og_ref[rows, lo:lo + HEAD_DIM] = _rms(o, onorm) * _silu(z)

    @pl.when(t == pl.num_programs(1) - 1)
    def _():
        for s in range(n_seq):
            cb_ref[s] = xpad[s, pad + tb - (GD_CONV - 1):pad + tb, :]
            for h in range(N_HEADS):
                sf_ref[s, h] = s_scr[s * N_HEADS + h]


def _gdn_recurrence(main, ab, conv_w, a_log, dt_bias, out_norm, cb0, s0, *, batch, seq, n_chunks, n_seq, fuse=None):
    n = batch * seq
    fused = fuse is not None
    chunk = min(MAX_CHUNK, seq)
    tb = chunk * n_chunks
    assert seq % tb == 0 and batch % n_seq == 0 and seq >= GD_CONV - 1
    assert n_seq == 1 or tb == seq, "several sequences per step only when a step covers whole sequences"
    nt = seq // tb
    rows = n_seq * tb
    n_prob = n_seq * n_chunks * N_HEADS
    has_s0 = s0 is not None
    state_spec = pl.BlockSpec((n_seq, N_HEADS, HEAD_DIM, HEAD_DIM), lambda b, t: (b, 0, 0, 0))
    cb_spec = pl.BlockSpec((n_seq, GD_CONV - 1, GD_QKV), lambda b, t: (b, 0, 0))
    small = lambda shape: pl.BlockSpec(shape, lambda b, t: (0,) * len(shape))
    main_cols = GD_QKV + N_HEADS * HEAD_DIM
    if fused:
        x, gain, w_main, w_ab = fuse
        in_specs, args = _ahead_specs(x, gain, [w_main, w_ab], rows, (batch // n_seq) * nt, nt)
    else:
        in_specs = [pl.BlockSpec((rows, main_cols), lambda b, t: (b * nt + t, 0)),
                    pl.BlockSpec((rows, 2 * N_HEADS), lambda b, t: (b * nt + t, 0))]
        args = [main, ab]
    in_specs += [small((GD_CONV, GD_QKV)), small((1, N_HEADS)), small((1, N_HEADS)), small((1, HEAD_DIM))]
    args += [conv_w, a_log.reshape(1, N_HEADS), dt_bias.reshape(1, N_HEADS), out_norm.reshape(1, HEAD_DIM)]
    if has_s0:
        in_specs += [cb_spec, state_spec]
        args += [cb0, s0]
    vm = lambda *shape: pltpu.VMEM(shape, F32)
    return pl.pallas_call(
        functools.partial(_gdn_kernel, chunk=chunk, n_chunks=n_chunks, n_seq=n_seq, has_s0=has_s0, fused=fused),
        grid=(batch // n_seq, nt),
        in_specs=in_specs,
        out_specs=[pl.BlockSpec((rows, D_MODEL), lambda b, t: (b * nt + t, 0)), state_spec, cb_spec],
        out_shape=[jax.ShapeDtypeStruct((n, D_MODEL), F32),
                   jax.ShapeDtypeStruct((batch, N_HEADS, HEAD_DIM, HEAD_DIM), F32),
                   jax.ShapeDtypeStruct((batch, GD_CONV - 1, GD_QKV), F32)],
        scratch_shapes=[vm(n_seq * N_HEADS, HEAD_DIM, HEAD_DIM),
                        vm(n_seq, tb + SUBLANES, GD_QKV),
                        vm(rows, GD_QKV),
                        vm(rows, N_HEADS),
                        vm(n_seq * n_chunks, N_HEADS, chunk),
                        vm(rows, N_HEADS),
                        vm(n_prob // 2, 2 * chunk, 2 * chunk),
                        vm(n_prob, chunk, 2 * HEAD_DIM),
                        vm(n_prob, 2 * chunk, HEAD_DIM),
                        vm(n_prob, chunk, HEAD_DIM),
                        vm(n_prob // 2, chunk, 2 * chunk),
                        vm(n_prob // 2, chunk, 2 * chunk),
                        vm(n_prob // 2, chunk, 2 * chunk),
                        vm(n_prob // 2, chunk, 2 * chunk),
                        vm(n_prob, chunk, 2 * HEAD_DIM),
                        vm(n_prob, chunk, HEAD_DIM),
                        vm(n_seq * N_HEADS, 2 * chunk, HEAD_DIM)]
        + ([vm(2, rows, main_cols), vm(2, rows, 2 * N_HEADS)] if fused else []),
        compiler_params=_params("arbitrary", "arbitrary"),
        name="gdn_recurrence",
    )(*args)


def _swiglu_rows(x_of, wg, wu, wd, n_sub, emit):
    pending = None
    for sb in range(n_sub + 1):
        if sb < n_sub:
            x = x_of(sb)
            nxt = (_dot(x, wg), _dot(x, wu))
        if pending is not None:
            a, u = pending
            emit(sb - 1, _dot((_silu(a) * u).astype(BF16), wd))
        pending = nxt


def _ffn_kernel(x_ref, og_ref, wo_ref, g_ref, wg_ref, wu_ref, wd_ref, o_ref, h_scr, *, n_sub):
    j = pl.program_id(1)
    sub = x_ref.shape[0] // n_sub

    @pl.when(j == 0)
    def _():
        x = x_ref[...] + _dot(og_ref[...].astype(BF16), wo_ref[...])
        h_scr[...] = _rms(x, g_ref[...]).astype(BF16)
        o_ref[...] = x

    def emit(sb, y):
        o_ref[sb * sub:(sb + 1) * sub, :] += y

    _swiglu_rows(lambda sb: h_scr[sb * sub:(sb + 1) * sub, :], wg_ref[...], wu_ref[...], wd_ref[...], n_sub, emit)


def _ffn(x, og, w_out, gain, wg, wu, wd, *, tm, tf, n_sub):
    n, d = x.shape
    f = wg.shape[1]
    rows = pl.BlockSpec((tm, d), lambda i, j: (i, 0))
    return pl.pallas_call(
        functools.partial(_ffn_kernel, n_sub=n_sub),
        grid=(n // tm, f // tf),
        in_specs=[rows, rows,
                  pl.BlockSpec((d, d), lambda i, j: (0, 0)),
                  pl.BlockSpec((1, d), lambda i, j: (0, 0)),
                  pl.BlockSpec((d, tf), lambda i, j: (0, j)),
                  pl.BlockSpec((d, tf), lambda i, j: (0, j)),
                  pl.BlockSpec((tf, d), lambda i, j: (j, 0))],
        out_specs=rows,
        out_shape=jax.ShapeDtypeStruct((n, d), F32),
        scratch_shapes=[pltpu.VMEM((tm, d), BF16)],
        compiler_params=_params("parallel", "arbitrary"),
        name="dense_ffn",
    )(x, og, w_out, gain.reshape(1, d), wg, wu, wd)


ROUTER_SLOTS = 128
PAD_ROWS = 256


RUN_ALIGN = 32


def _run_copies(n_rows, max_rows, make_copy):
    pieces = []
    off = jnp.int32(0)
    bit = max_rows
    while bit >= RUN_ALIGN:
        has = (n_rows & bit) != 0
        pieces.append((has, make_copy(off, bit)))
        off = off + (n_rows & bit)
        bit //= 2
    return pieces


def _router_kernel(x_ref, og_ref, wo_ref, g_ref, wr_ref, xo_ref, dest_ref, gate_ref, cnt_ref, xs_hbm,
                   cnt_scr, stage, zeros_buf, n_smem, base_smem, sem, *, tm, cap):
    i = pl.program_id(0)
    last = pl.num_programs(0) - 1

    @pl.when(i == 0)
    def _():
        cnt_scr[...] = jnp.zeros_like(cnt_scr)
        zeros_buf[...] = jnp.zeros_like(zeros_buf)
        for e in range(N_EXPERTS):
            n_smem[e] = 0
            base_smem[e] = e * cap

    x = x_ref[...] + _dot(og_ref[...].astype(BF16), wo_ref[...])
    xo_ref[...] = x
    h = _rms(x, g_ref[...])
    h16 = h.astype(BF16)
    h_lo = (h - h16.astype(F32)).astype(BF16)
    w_hi, w_lo = _split_bf16(wr_ref[...])
    logits = _dot(h16, w_hi) + (_dot(h16, w_lo) + _dot(h_lo, w_hi))
    lane = lax.broadcasted_iota(jnp.int32, logits.shape, 1)
    m1 = jnp.max(logits, axis=-1, keepdims=True)
    i1 = jnp.min(jnp.where(logits == m1, lane, N_EXPERTS), axis=-1, keepdims=True)
    rest = jnp.where(lane == i1, -jnp.inf, logits)
    m2 = jnp.max(rest, axis=-1, keepdims=True)
    i2 = jnp.min(jnp.where(rest == m2, lane, N_EXPERTS), axis=-1, keepdims=True)
    e2 = jnp.exp(m2 - m1)
    den = 1.0 + e2
    two = lax.broadcasted_iota(jnp.int32, (tm, TOP_K), 1)
    gate_ref[...] = jnp.where(two == 0, 1.0 / den, e2 / den)

    onehot = ((lane == i1) | (lane == i2)).astype(F32)
    row = lax.broadcasted_iota(jnp.int32, (tm, tm), 0)
    col = lax.broadcasted_iota(jnp.int32, (tm, tm), 1)
    local = _dot((row > col).astype(BF16), onehot.astype(BF16))
    rank = local + cnt_scr[...]
    r1 = jnp.sum(jnp.where(lane == i1, rank, 0.0), axis=-1, keepdims=True).astype(jnp.int32)
    r2 = jnp.sum(jnp.where(lane == i2, rank, 0.0), axis=-1, keepdims=True).astype(jnp.int32)
    dest_ref[...] = jnp.where(two == 0, i1 * cap + r1, i2 * cap + r2)
    cnt_tile = jnp.sum(onehot, axis=0, keepdims=True)
    cnt = cnt_scr[...] + cnt_tile
    cnt_scr[...] = cnt
    cnt_ref[...] = cnt.astype(jnp.int32)

    def wait_runs():
        for e in range(N_EXPERTS):
            for has, copy in _run_copies(n_smem[e], tm, lambda off, size, e=e: pltpu.make_async_copy(
                    stage.at[e, pl.ds(0, size * SUBLANES)], xs_hbm.at[pl.ds(0, size * SUBLANES)], sem)):
                @pl.when(has)
                def _():
                    copy.wait()

    wait_runs()

    code = (onehot * (local + 1.0)).astype(BF16)
    code_t = _dot_tn(code, (row == col).astype(BF16))
    slot_id = lax.broadcasted_iota(jnp.int32, (ROUTER_SLOTS, tm), 0)
    code_i = code_t.astype(jnp.int32)

    def compact(e, q):
        sel = jnp.where(code_i[e:e + 1, :] == slot_id + (q * ROUTER_SLOTS + 1), 1.0, 0.0).astype(BF16)
        _store_token_tiles(stage.at[e], q * ROUTER_SLOTS, _dot(sel, h16))

    for e in range(N_EXPERTS):
        compact(e, 0)
    for e in range(N_EXPERTS):
        n_e = cnt_tile[0, e].astype(jnp.int32)
        for q in range(1, tm // ROUTER_SLOTS):
            @pl.when(q * ROUTER_SLOTS < n_e)
            def _():
                compact(e, q)
        n_copy = jnp.bitwise_and(n_e + (RUN_ALIGN - 1), -RUN_ALIGN)
        base = base_smem[e]
        for has, copy in _run_copies(n_copy, tm, lambda off, size, e=e, base=base: pltpu.make_async_copy(
                stage.at[e, pl.ds(pl.multiple_of(off * SUBLANES, SUBLANES), size * SUBLANES)],
                xs_hbm.at[pl.ds(pl.multiple_of((base + off) * SUBLANES, SUBLANES), size * SUBLANES)], sem)):
            @pl.when(has)
            def _():
                copy.start()
        n_smem[e] = n_copy
        base_smem[e] = base + n_e

    @pl.when(i == last)
    def _():
        wait_runs()
        pads = [pltpu.make_async_copy(
            zeros_buf, xs_hbm.at[pl.ds(pl.multiple_of(base_smem[e] * SUBLANES, SUBLANES), PAD_ROWS * SUBLANES)], sem)
            for e in range(N_EXPERTS)]
        for pad in pads:
            pad.start()
        for pad in pads:
            pad.wait()


def _router(x, og, w_out, gain, w_router, *, tm, cap):
    n, d = x.shape
    two = pl.BlockSpec((tm, TOP_K), lambda i: (i, 0))
    rows = pl.BlockSpec((tm, d), lambda i: (i, 0))
    return pl.pallas_call(
        functools.partial(_router_kernel, tm=tm, cap=cap),
        grid=(n // tm,),
        in_specs=[rows, rows,
                  pl.BlockSpec((d, d), lambda i: (0, 0)),
                  pl.BlockSpec((1, d), lambda i: (0, 0)),
                  pl.BlockSpec((d, N_EXPERTS), lambda i: (0, 0))],
        out_specs=[rows, two, two, pl.BlockSpec((1, N_EXPERTS), lambda i: (0, 0)),
                   pl.BlockSpec(memory_space=pl.ANY)],
        out_shape=[jax.ShapeDtypeStruct((n, d), F32),
                   jax.ShapeDtypeStruct((n, TOP_K), jnp.int32),
                   jax.ShapeDtypeStruct((n, TOP_K), F32),
                   jax.ShapeDtypeStruct((1, N_EXPERTS), jnp.int32),
                   jax.ShapeDtypeStruct((N_EXPERTS * cap * SUBLANES, LANES), F32)],
        scratch_shapes=[pltpu.VMEM((1, N_EXPERTS), F32),
                        pltpu.VMEM((N_EXPERTS, tm * SUBLANES, LANES), F32),
                        pltpu.VMEM((PAD_ROWS * SUBLANES, LANES), F32),
                        pltpu.SMEM((N_EXPERTS,), jnp.int32), pltpu.SMEM((N_EXPERTS,), jnp.int32),
                        pltpu.SemaphoreType.DMA(())],
        compiler_params=_params("arbitrary"),
        name="moe_router",
    )(x, og, w_out, gain.reshape(1, d), w_router)


def _tile_gather_copy(src_hbm, dst, sem, src_row, dst_row):
    return pltpu.make_async_copy(src_hbm.at[pl.ds(pl.multiple_of(src_row * SUBLANES, SUBLANES), SUBLANES)],
                                 dst.at[pl.ds(pl.multiple_of(dst_row * SUBLANES, SUBLANES), SUBLANES)], sem)


def _load_token_tiles(ref, first_row, n_rows):
    return jnp.concatenate([ref[pl.ds(first_row * SUBLANES + s, n_rows, stride=SUBLANES), :]
                            for s in range(D_MODEL // LANES)], axis=1)


def _store_token_tiles(ref, first_row, x):
    for s in range(D_MODEL // LANES):
        ref[pl.ds(first_row * SUBLANES + s, x.shape[0], stride=SUBLANES), :] = x[:, s * LANES:(s + 1) * LANES]


def _expert_kernel(be_ref, xblk_ref, oblk_ref, nv_ref, nu_ref, xs_ref, wg_ref, wu_ref, wd_ref, yb_ref, xb, acc,
                   *, sub_rows, n_sub):
    i = pl.program_id(0)
    j = pl.program_id(1)
    n_valid = nv_ref[i]

    @pl.when(j == 0)
    def _():
        acc[...] = jnp.zeros_like(acc)
        for sb in range(n_sub):
            @pl.when(sb * sub_rows < n_valid)
            def _():
                xb[sb * sub_rows:(sb + 1) * sub_rows, :] = _load_token_tiles(
                    xs_ref, sb * sub_rows, sub_rows).astype(BF16)

    def emit(sb, y):
        acc[sb * sub_rows:(sb + 1) * sub_rows, :] += y

    def x_of(sb):
        return xb[sb * sub_rows:(sb + 1) * sub_rows, :]

    full = n_valid > (n_sub - 1) * sub_rows

    @pl.when(full)
    def _():
        _swiglu_rows(x_of, wg_ref[0], wu_ref[0], wd_ref[0], n_sub, emit)

    for sb in range(n_sub - 1):
        @pl.when(jnp.logical_and(jnp.logical_not(full), sb * sub_rows < n_valid))
        def _():
            _swiglu_rows(lambda _, sb=sb: x_of(sb), wg_ref[0], wu_ref[0], wd_ref[0], 1, lambda _, y, sb=sb: emit(sb, y))

    @pl.when(j == pl.num_programs(1) - 1)
    def _():
        for sb in range(n_sub):
            rows = slice(sb * sub_rows, (sb + 1) * sub_rows)

            @pl.when(sb * sub_rows < n_valid)
            def _():
                _store_token_tiles(yb_ref, sb * sub_rows, acc[rows, :])

            @pl.when(sb * sub_rows >= n_valid)
            def _():
                tiles = slice(sb * sub_rows * SUBLANES, (sb + 1) * sub_rows * SUBLANES)
                yb_ref[tiles, :] = jnp.zeros((sub_rows * SUBLANES, LANES), F32)


def _experts(xs, block_e, x_block, out_block, n_valid, n_used, wg, wu, wd, *, sub_rows, n_sub, tf, out_blocks):
    d = D_MODEL
    tmm = sub_rows * n_sub
    nb = block_e.shape[0]
    nf = D_FF // tf

    def wcol(i, j, be, xblk, oblk, nv, nu):
        return (be[i], 0, jnp.where(i < nu[0], j, nf - 1))

    def wrow(i, j, be, xblk, oblk, nv, nu):
        return (be[i], jnp.where(i < nu[0], j, nf - 1), 0)

    grid_spec = pltpu.PrefetchScalarGridSpec(
        num_scalar_prefetch=5,
        grid=(nb, nf),
        in_specs=[pl.BlockSpec((tmm * SUBLANES, LANES), lambda i, j, be, xblk, oblk, nv, nu: (xblk[i], 0)),
                  pl.BlockSpec((1, d, tf), wcol),
                  pl.BlockSpec((1, d, tf), wcol),
                  pl.BlockSpec((1, tf, d), wrow)],
        out_specs=pl.BlockSpec((tmm * SUBLANES, LANES), lambda i, j, be, xblk, oblk, nv, nu: (oblk[i], 0)),
        scratch_shapes=[pltpu.VMEM((tmm, d), BF16), pltpu.VMEM((tmm, d), F32)],
    )
    return pl.pallas_call(
        functools.partial(_expert_kernel, sub_rows=sub_rows, n_sub=n_sub),
        grid_spec=grid_spec,
        out_shape=jax.ShapeDtypeStruct((out_blocks * tmm * SUBLANES, LANES), F32),
        compiler_params=_params("arbitrary", "arbitrary"),
        name="moe_experts",
    )(block_e, x_block, out_block, n_valid, n_used, xs, wg, wu, wd)


def _combine_kernel(dest_ref, yb_hbm, x_ref, gate_ref, gain_ref, o_ref, r0, r1, sems, *, tm):
    i = pl.program_id(0)
    slot = lax.rem(i, 2)

    def start_gather(tile, dst_slot):
        def issue(r, carry):
            first = TOP_K * (tile * tm + r)
            _tile_gather_copy(yb_hbm, r0.at[dst_slot], sems.at[dst_slot, 0], dest_ref[first], r).start(priority=0)
            _tile_gather_copy(yb_hbm, r1.at[dst_slot], sems.at[dst_slot, 1], dest_ref[first + 1], r).start(priority=1)
            return carry

        lax.fori_loop(0, tm, issue, 0, unroll=8)

    @pl.when(i == 0)
    def _():
        start_gather(0, 0)

    @pl.when(i + 1 < pl.num_programs(0))
    def _():
        start_gather(i + 1, 1 - slot)

    whole = pl.ds(0, tm * SUBLANES)
    pltpu.make_async_copy(yb_hbm.at[whole], r0.at[slot], sems.at[slot, 0]).wait()
    pltpu.make_async_copy(yb_hbm.at[whole], r1.at[slot], sems.at[slot, 1]).wait()
    g = gate_ref[...]
    y = _load_token_tiles(r0.at[slot], 0, tm) * g[:, 0:1] + _load_token_tiles(r1.at[slot], 0, tm) * g[:, 1:2]
    o_ref[...] = _rms(x_ref[...] + y, gain_ref[...])


def _combine(yb, dest_flat, x, gates, gain, *, tm):
    n, d = x.shape
    grid_spec = pltpu.PrefetchScalarGridSpec(
        num_scalar_prefetch=1,
        grid=(n // tm,),
        in_specs=[pl.BlockSpec(memory_space=pl.ANY),
                  pl.BlockSpec((tm, d), lambda i, dest: (i, 0)),
                  pl.BlockSpec((tm, TOP_K), lambda i, dest: (i, 0)),
                  pl.BlockSpec((1, d), lambda i, dest: (0, 0))],
        out_specs=pl.BlockSpec((tm, d), lambda i, dest: (i, 0)),
        scratch_shapes=[pltpu.VMEM((2, tm * SUBLANES, LANES), F32), pltpu.VMEM((2, tm * SUBLANES, LANES), F32),
                        pltpu.SemaphoreType.DMA((2, TOP_K))],
    )
    return pl.pallas_call(
        functools.partial(_combine_kernel, tm=tm),
        grid_spec=grid_spec,
        out_shape=jax.ShapeDtypeStruct((n, d), F32),
        compiler_params=_params("arbitrary"),
        name="moe_combine",
    )(dest_flat, yb, x, gates, gain.reshape(1, d))


def _moe_and_final_norm(x, og, w_out, gain, w_router, wg, wu, wd, final_gain, *, tiles):
    n = x.shape[0]
    sub_rows, n_sub = tiles["moe_sub_rows"], tiles["moe_subs"]
    tmm = sub_rows * n_sub
    assert sub_rows == PAD_ROWS and n % tmm == 0
    cap = n + tmm
    x, dest, gates, counts, xs = _router(x, og, w_out, gain, w_router, tm=tiles["router_rows"], cap=cap)
    counts = counts[0]
    blocks_per_e = (counts + tmm - 1) // tmm
    blocks_end = jnp.cumsum(blocks_per_e)
    n_used = blocks_end[-1].astype(jnp.int32)
    nb = -(-(n * TOP_K + N_EXPERTS * (tmm - 1)) // tmm)
    blk = jnp.minimum(jnp.arange(nb, dtype=jnp.int32), n_used - 1)
    block_e = jnp.minimum(jnp.sum(blk[:, None] >= blocks_end[None, :], axis=1), N_EXPERTS - 1).astype(jnp.int32)
    k_in_e = blk - (blocks_end - blocks_per_e)[block_e]
    x_block = (block_e * (cap // tmm) + k_in_e).astype(jnp.int32)
    used = jnp.arange(nb, dtype=jnp.int32) < n_used
    n_valid = jnp.where(used, jnp.clip(counts[block_e] - k_in_e * tmm, 0, tmm), 0).astype(jnp.int32)
    dump_block = N_EXPERTS * (cap // tmm)
    out_block = jnp.where(used, x_block, dump_block).astype(jnp.int32)
    yb = _experts(xs, block_e, x_block, out_block, n_valid, n_used.reshape(1), wg, wu, wd,
                  sub_rows=sub_rows, n_sub=n_sub, tf=tiles["moe_ff_cols"], out_blocks=dump_block + 1)
    return _combine(yb, dest.reshape(-1), x, gates, final_gain, tm=tiles["combine_rows"])


def _tiles(n_rows, seq):
    big = n_rows >= 8192
    return {
        "proj_rows": 1024 if n_rows % 1024 == 0 else n_rows,
        "proj_cols": 2048,
        "rec_chunks": min(4, seq // min(MAX_CHUNK, seq)),
        "rec_seqs": 1 if seq > MAX_CHUNK else 4,
        "proj_in_recurrence": seq > MAX_CHUNK,
        "ffn_rows": 1024 if n_rows % 1024 == 0 else n_rows,
        "ff_cols": 512,
        "router_rows": 256,
        "moe_sub_rows": 256,
        "moe_subs": 4 if big else 2,
        "moe_ff_cols": 1792,
        "ffn_subs": 4 if big else 2,
        "combine_rows": 256,
    }


def _trunk(x3, hg_state, gd_state, gd_conv, p):
    batch, seq, d = x3.shape
    n = batch * seq
    x = x3.reshape(n, d)
    tiles = _tiles(n, seq)
    tm, tn = tiles["proj_rows"], tiles["proj_cols"]

    rec = dict(batch=batch, seq=seq, n_chunks=tiles["rec_chunks"], n_seq=tiles["rec_seqs"])

    if tiles["proj_in_recurrence"]:
        proj, fuse = None, (x, p["norm_mix"][0], p["hgrn_w_in"])
    else:
        proj, fuse = _norm_proj(x, p["norm_mix"][0], p["hgrn_w_in"], tm=tm, tn=tn)[0], None
    og, hg_new = _hgrn_recurrence(proj, p["hgrn_lb"], p["hgrn_norm"], hg_state, layer=0, fuse=fuse, **rec)
    x = _ffn(x, og, p["hgrn_w_out"], p["norm_ffn"][0], p["ffn_w_gate"], p["ffn_w_up"], p["ffn_w_down"],
             tm=tiles["ffn_rows"], tf=tiles["ff_cols"], n_sub=tiles["ffn_subs"])

    if tiles["proj_in_recurrence"]:
        main, ab, fuse = None, None, (x, p["norm_mix"][1], p["gdn_w_main"], p["gdn_w_ab"])
    else:
        (main, ab), fuse = _norm_proj(x, p["norm_mix"][1], p["gdn_w_main"], p["gdn_w_ab"], tm=tm, tn=tn), None
    og, gd_new, cv_new = _gdn_recurrence(main, ab, p["gdn_conv"], p["gdn_a_log"], p["gdn_dt_bias"], p["gdn_norm"],
                                         gd_conv, gd_state, fuse=fuse, **rec)
    y = _moe_and_final_norm(x, og, p["gdn_w_out"], p["norm_ffn"][1], p["moe_router"], p["moe_w_gate"], p["moe_w_up"], p["moe_w_down"],
                            p["norm_out"], tiles=tiles)
    return y.reshape(batch, seq, d), hg_new[None], gd_new[None], cv_new[None]


def kernel(x_prompt, x_sample, state_hgrn, state_gdn, state_gdn_conv, norm_mix, norm_ffn, norm_out, hgrn_w_in, hgrn_lb, hgrn_norm, hgrn_w_out, gdn_w_in, gdn_conv, gdn_a_log, gdn_dt_bias, gdn_norm, gdn_w_out, ffn_w_gate, ffn_w_up, ffn_w_down, moe_router, moe_w_gate, moe_w_up, moe_w_down):
    assert hgrn_w_in.shape[0] == 1 and gdn_w_in.shape[0] == 1, "one HGRN2 layer and one gated-DeltaNet layer"
    main_cols = GD_QKV + N_HEADS * HEAD_DIM
    p = {
        "norm_mix": norm_mix, "norm_ffn": norm_ffn, "norm_out": norm_out,
        "hgrn_w_in": hgrn_w_in[0].astype(BF16), "hgrn_lb": hgrn_lb, "hgrn_norm": hgrn_norm[0],
        "hgrn_w_out": hgrn_w_out[0].astype(BF16),
        "gdn_w_main": gdn_w_in[0, :, :main_cols].astype(BF16), "gdn_w_ab": gdn_w_in[0, :, main_cols:].astype(BF16),
        "gdn_conv": gdn_conv[0], "gdn_a_log": gdn_a_log[0], "gdn_dt_bias": gdn_dt_bias[0], "gdn_norm": gdn_norm[0],
        "gdn_w_out": gdn_w_out[0].astype(BF16),
        "ffn_w_gate": ffn_w_gate[0].astype(BF16), "ffn_w_up": ffn_w_up[0].astype(BF16),
        "ffn_w_down": ffn_w_down[0].astype(BF16),
        "moe_router": moe_router[0],
        "moe_w_gate": moe_w_gate[0].astype(BF16), "moe_w_up": moe_w_up[0].astype(BF16),
        "moe_w_down": moe_w_down[0].astype(BF16),
    }
    y_p, hg_p, gd_p, cv_p = _trunk(x_prompt, None, None, None, p)
    y_s, hg_s, gd_s, cv_s = _trunk(x_sample, state_hgrn[0], state_gdn[0], state_gdn_conv[0], p)
    return (y_p, y_s, hg_p, hg_s, gd_p, gd_s, cv_p, cv_s)
```

```python
import functools

import jax
import jax.numpy as jnp
from jax import lax
from jax.experimental import pallas as pl
from jax.experimental.pallas import tpu as pltpu

F32 = jnp.float32
BF16 = jnp.bfloat16
HIGHEST = lax.Precision.HIGHEST

D_MODEL = 1024
N_HEADS = 8
HEAD_DIM = 128
D_FF = 3584
N_EXPERTS = 8
TOP_K = 2
GD_QKV = 3 * D_MODEL
GD_CONV = 4
MAX_CHUNK = 64
EPS = 1e-6
LANES = 128
SUBLANES = 8
VMEM_LIMIT_BYTES = 52 * 1024 * 1024


def _params(*sem):
    return pltpu.CompilerParams(dimension_semantics=sem, vmem_limit_bytes=VMEM_LIMIT_BYTES)


def _rms(x, gain):
    return x * lax.rsqrt(jnp.mean(x * x, axis=-1, keepdims=True) + EPS) * gain


def _sigmoid(x):
    t = jnp.exp(-jnp.abs(x))
    r = 1.0 / (1.0 + t)
    return jnp.where(x >= 0, r, t * r)


def _silu(x):
    return x * (0.5 * jnp.tanh(0.5 * x) + 0.5)


def _softplus(x):
    return jnp.maximum(x, 0.0) + jnp.log1p(jnp.exp(-jnp.abs(x)))


def _dot(a, b):
    return jnp.dot(a, b, preferred_element_type=F32)


def _dot_nt(a, b):
    return lax.dot_general(a, b, (((1,), (1,)), ((), ())), preferred_element_type=F32)


def _dot_tn(a, b):
    return lax.dot_general(a, b, (((0,), (0,)), ((), ())), preferred_element_type=F32)


def _dot_hi(a, b):
    return jnp.dot(a, b, precision=HIGHEST, preferred_element_type=F32)


def _norm_proj_kernel(x_ref, g_ref, w_ref, *rest, has_small):
    if has_small:
        ws_ref, o_ref, os_ref, h_scr = rest
    else:
        o_ref, h_scr = rest

    @pl.when(pl.program_id(1) == 0)
    def _():
        hb = _rms(x_ref[...], g_ref[...]).astype(BF16)
        h_scr[...] = hb
        if has_small:
            os_ref[...] = _dot(hb, ws_ref[...])

    o_ref[...] = _dot(h_scr[...], w_ref[...])


def _norm_proj(x, gain, w, w_small=None, *, tm, tn):
    n, d = x.shape
    e = w.shape[1]
    has_small = w_small is not None
    in_specs = [pl.BlockSpec((tm, d), lambda i, j: (i, 0)),
                pl.BlockSpec((1, d), lambda i, j: (0, 0)),
                pl.BlockSpec((d, tn), lambda i, j: (0, j))]
    out_shape = [jax.ShapeDtypeStruct((n, e), F32)]
    out_specs = [pl.BlockSpec((tm, tn), lambda i, j: (i, j))]
    args = [x, gain.reshape(1, d), w]
    if has_small:
        es = w_small.shape[1]
        in_specs.append(pl.BlockSpec((d, es), lambda i, j: (0, 0)))
        out_shape.append(jax.ShapeDtypeStruct((n, es), F32))
        out_specs.append(pl.BlockSpec((tm, es), lambda i, j: (i, 0)))
        args.append(w_small)
    return pl.pallas_call(
        functools.partial(_norm_proj_kernel, has_small=has_small),
        grid=(n // tm, e // tn),
        in_specs=in_specs, out_specs=out_specs, out_shape=out_shape,
        scratch_shapes=[pltpu.VMEM((tm, d), BF16)],
        compiler_params=_params("parallel", "arbitrary"),
        name="norm_proj",
    )(*args)


PROJ_PIECES = 8


class _Projection:
    def __init__(self, x_ref, gain_ref, w_refs, dst_scrs, h_scr, dst_slot):
        self.x_ref, self.gain_ref, self.w_refs, self.dst_scrs = x_ref, gain_ref, w_refs, dst_scrs
        self.h_scr, self.dst_slot, self.done = h_scr, dst_slot, 0

    def upto(self, k):
        for piece in range(self.done, min(k, PROJ_PIECES)):
            if piece == 0:
                self.h_scr[...] = _rms(self.x_ref[...], self.gain_ref[...]).astype(BF16)
            h16 = self.h_scr[...]
            for w_ref, dst in zip(self.w_refs, self.dst_scrs):
                width = w_ref.shape[1]
                if width % (PROJ_PIECES * LANES) == 0:
                    cols = slice(piece * width // PROJ_PIECES, (piece + 1) * width // PROJ_PIECES)
                    dst[self.dst_slot, :, cols] = _dot(h16, w_ref[:, cols])
                elif piece == 0:
                    dst[self.dst_slot] = _dot(h16, w_ref[...])
        self.done = max(self.done, k)


class _NoProjection:
    def upto(self, k):
        pass


def _with_projection_ahead(body, refs, n_weights, **kw):
    x_cur_ref, x_next_ref, gain_ref = refs[:3]
    w_refs = refs[3:3 + n_weights]
    dst_scrs = refs[len(refs) - n_weights - 1:len(refs) - 1]
    h_scr = refs[-1]
    g = pl.program_id(0) * pl.num_programs(1) + pl.program_id(1)

    @pl.when(g == 0)
    def _():
        _Projection(x_cur_ref, gain_ref, w_refs, dst_scrs, h_scr, 0).upto(PROJ_PIECES)

    for cur in range(2):
        @pl.when(lax.rem(g, 2) == cur)
        def _():
            ahead = _Projection(x_next_ref, gain_ref, w_refs, dst_scrs, h_scr, 1 - cur)
            body(*[scr.at[cur] for scr in dst_scrs], *refs[3 + n_weights:len(refs) - n_weights - 1], ahead=ahead,
                 **kw)
            ahead.upto(PROJ_PIECES)


def _hgrn_kernel(*refs, fused, **kw):
    if fused:
        _with_projection_ahead(_hgrn_step, refs, 1, **kw)
    else:
        _hgrn_step(*refs, **kw)


def _hgrn_step(proj_ref, *refs, chunk, n_chunks, n_seq, layer, has_s0, ahead=None):
    if has_s0:
        lb_ref, on_ref, s0_ref, og_ref, sf_ref, *scr = refs
    else:
        lb_ref, on_ref, og_ref, sf_ref, *scr = refs
    st_scr, b_scr, q_scr, k_scr, a_scr, qin_scr, kout_scr, dec_scr, oi_scr = scr[:9]
    C = chunk
    W = N_HEADS * HEAD_DIM
    tb = C * n_chunks
    t = pl.program_id(1)
    problems = [(s, c, h) for s in range(n_seq) for c in range(n_chunks) for h in range(N_HEADS)]

    def pid(s, c, h):
        return (s * n_chunks + c) * N_HEADS + h

    def row0(s, c):
        return s * tb + c * C

    @pl.when(t == 0)
    def _():
        for s in range(n_seq):
            for h in range(N_HEADS):
                st_scr[s * N_HEADS + h] = s0_ref[s, h].T if has_s0 else jnp.zeros((HEAD_DIM, HEAD_DIM), F32)

    ahead = ahead or _NoProjection()
    ahead.upto(1)

    lbraw = lb_ref[...]
    ex = jnp.exp(lbraw - jnp.max(lbraw, axis=0, keepdims=True))
    sm = ex / jnp.sum(ex, axis=0, keepdims=True)
    lb = sm[0:1]
    for j in range(1, layer + 1):
        lb = lb + sm[j:j + 1]

    row = lax.broadcasted_iota(jnp.int32, (C, C), 0)
    col = lax.broadcasted_iota(jnp.int32, (C, C), 1)
    causal = row >= col
    tril = causal.astype(BF16)
    mid = C // 2
    scale = HEAD_DIM ** -0.5
    onorm = on_ref[...]

    for s in range(n_seq):
        for c in range(n_chunks):
            rows = slice(row0(s, c), row0(s, c) + C)
            f_raw = proj_ref[rows, W:2 * W]
            tt = jnp.exp(-jnp.abs(f_raw))
            rr = 1.0 / (1.0 + tt)
            pos = f_raw >= 0
            sig = jnp.where(pos, rr, tt * rr)
            sig_neg = jnp.where(pos, tt * rr, rr)
            log_f = jnp.log(lb + (1.0 - lb) * sig)
            k_scr[rows, :] = (1.0 - lb) * sig_neg
            q_scr[rows, :] = _silu(proj_ref[rows, 0:W]) * scale
            hi = log_f.astype(BF16)
            rest = log_f - hi.astype(F32)
            md = rest.astype(BF16)
            lw = (rest - md.astype(F32)).astype(BF16)
            b_scr[rows, :] = (_dot(tril, lw) + _dot(tril, md)) + _dot(tril, hi)

    ahead.upto(3)
    for s, c, h in problems:
        p, lo = pid(s, c, h), h * HEAD_DIM
        rows = slice(row0(s, c), row0(s, c) + C)
        b = b_scr[rows, lo:lo + HEAD_DIM]
        q = q_scr[rows, lo:lo + HEAD_DIM]
        k = k_scr[rows, lo:lo + HEAD_DIM]
        b_mid = b[mid:mid + 1]
        b_last = b[C - 1:C]
        a_scr[p] = _dot_nt((q * jnp.exp(b - b_mid)).astype(BF16), (k * jnp.exp(b_mid - b)).astype(BF16))
        qin_scr[p] = q * jnp.exp(b)
        kout_scr[p] = k * jnp.exp(b_last - b)
        dec_scr[p] = jnp.exp(b_last)

    ahead.upto(5)
    for s, c, h in problems:
        p, lo = pid(s, c, h), h * HEAD_DIM
        rows = slice(row0(s, c), row0(s, c) + C)
        a = jnp.where(causal, a_scr[p], 0.0)
        oi_scr[p] = _dot(a.astype(BF16), proj_ref[rows, 2 * W + lo:2 * W + lo + HEAD_DIM].astype(BF16))

    for c in range(n_chunks):
        ahead.upto(6 + c)
        for s in range(n_seq):
            for h in range(N_HEADS):
                p, lo, sh = pid(s, c, h), h * HEAD_DIM, s * N_HEADS + h
                rows = slice(row0(s, c), row0(s, c) + C)
                st = st_scr[sh]
                o = oi_scr[p] + _dot_nt(qin_scr[p].astype(BF16), st.astype(BF16))
                vb = proj_ref[rows, 2 * W + lo:2 * W + lo + HEAD_DIM].astype(BF16)
                st_scr[sh] = st * dec_scr[p] + _dot_tn(vb, kout_scr[p].astype(BF16))
                gate = proj_ref[rows, 3 * W + lo:3 * W + lo + HEAD_DIM]
                og_ref[rows, lo:lo + HEAD_DIM] = _rms(o, onorm) * _silu(gate)

    @pl.when(t == pl.num_programs(1) - 1)
    def _():
        for s in range(n_seq):
            for h in range(N_HEADS):
                sf_ref[s, h] = st_scr[s * N_HEADS + h].T


def _ahead_specs(x, gain, weights, rows, n_steps, nt):
    d = x.shape[1]
    specs = [pl.BlockSpec((rows, d), lambda b, t: (b * nt + t, 0)),
             pl.BlockSpec((rows, d), lambda b, t: (jnp.minimum(b * nt + t + 1, n_steps - 1), 0)),
             pl.BlockSpec((1, d), lambda b, t: (0, 0))]
    specs += [pl.BlockSpec(w.shape, lambda b, t: (0, 0)) for w in weights]
    return specs, [x, x, gain.reshape(1, d)] + list(weights)


def _hgrn_recurrence(proj, lb_raw, out_norm, s0, *, batch, seq, layer, n_chunks, n_seq, fuse=None):
    n = batch * seq
    chunk = min(MAX_CHUNK, seq)
    tb = chunk * n_chunks
    assert seq % tb == 0 and batch % n_seq == 0
    assert n_seq == 1 or tb == seq, "several sequences per step only when a step covers whole sequences"
    nt = seq // tb
    rows = n_seq * tb
    n_prob = n_seq * n_chunks * N_HEADS
    width = N_HEADS * HEAD_DIM
    has_s0 = s0 is not None
    fused = fuse is not None
    state_spec = pl.BlockSpec((n_seq, N_HEADS, HEAD_DIM, HEAD_DIM), lambda b, t: (b, 0, 0, 0))
    if fused:
        x, gain, w = fuse
        in_specs, args = _ahead_specs(x, gain, [w], rows, (batch // n_seq) * nt, nt)
    else:
        in_specs, args = [pl.BlockSpec((rows, proj.shape[1]), lambda b, t: (b * nt + t, 0))], [proj]
    in_specs += [pl.BlockSpec(lb_raw.shape, lambda b, t: (0, 0)),
                 pl.BlockSpec((1, HEAD_DIM), lambda b, t: (0, 0))]
    args += [lb_raw, out_norm.reshape(1, HEAD_DIM)]
    if has_s0:
        in_specs.append(state_spec)
        args.append(s0)
    vm = lambda *shape: pltpu.VMEM(shape, F32)
    ahead_scratch = [vm(2, rows, 4 * width), pltpu.VMEM((rows, D_MODEL), BF16)] if fused else []
    return pl.pallas_call(
        functools.partial(_hgrn_kernel, chunk=chunk, n_chunks=n_chunks, n_seq=n_seq, layer=layer, has_s0=has_s0,
                          fused=fused),
        grid=(batch // n_seq, nt),
        in_specs=in_specs,
        out_specs=[pl.BlockSpec((rows, D_MODEL), lambda b, t: (b * nt + t, 0)), state_spec],
        out_shape=[jax.ShapeDtypeStruct((n, D_MODEL), F32),
                   jax.ShapeDtypeStruct((batch, N_HEADS, HEAD_DIM, HEAD_DIM), F32)],
        scratch_shapes=[vm(n_seq * N_HEADS, HEAD_DIM, HEAD_DIM),
                        vm(rows, width),
                        vm(rows, width),
                        vm(rows, width),
                        vm(n_prob, chunk, chunk),
                        vm(n_prob, chunk, HEAD_DIM),
                        vm(n_prob, chunk, HEAD_DIM),
                        vm(n_prob, 1, HEAD_DIM),
                        vm(n_prob, chunk, HEAD_DIM)] + ahead_scratch,
        compiler_params=_params("arbitrary", "arbitrary"),
        name="hgrn_recurrence",
    )(*args)


def _split_bf16(x):
    hi = x.astype(BF16)
    return hi, (x - hi.astype(F32)).astype(BF16)


def _gdn_kernel(*refs, fused, **kw):
    if fused:
        _with_projection_ahead(_gdn_step, refs, 2, **kw)
    else:
        _gdn_step(*refs, **kw)


def _gdn_step(main_ref, ab_ref, *refs, chunk, n_chunks, n_seq, has_s0, ahead=None):
    (cw_ref, al_ref, dtb_ref, on_ref), rest = refs[:4], refs[4:]
    if has_s0:
        (cb0_ref, s0_ref), rest = rest[:2], rest[2:]
    (og_ref, sf_ref, cb_ref, s_scr, xpad, act, gcc_scr, gcr_scr, beta_scr, gram_scr, rhs_scr, wq_scr, kout_scr,
     inv_scr, pow_scr, low_scr, attn_scr, x0_scr, u_scr, ws_scr) = rest[:20]
    C = chunk
    W = N_HEADS * HEAD_DIM
    tb = C * n_chunks
    t = pl.program_id(1)
    pad = SUBLANES
    pairs = [(s, c, hp) for s in range(n_seq) for c in range(n_chunks) for hp in range(N_HEADS // 2)]

    def pid(s, c, h):
        return (s * n_chunks + c) * N_HEADS + h

    def ppid(s, c, hp):
        return (s * n_chunks + c) * (N_HEADS // 2) + hp

    def row0(s, c):
        return s * tb + c * C

    @pl.when(t == 0)
    def _():
        for s in range(n_seq):
            for h in range(N_HEADS):
                s_scr[s * N_HEADS + h] = s0_ref[s, h] if has_s0 else jnp.zeros((HEAD_DIM, HEAD_DIM), F32)
            xpad[s, 0:pad, :] = jnp.zeros((pad, GD_QKV), F32)
            if has_s0:
                xpad[s, pad - (GD_CONV - 1):pad, :] = cb0_ref[s]

    @pl.when(t > 0)
    def _():
        for s in range(n_seq):
            xpad[s, 0:pad, :] = xpad[s, tb:tb + pad, :]

    ahead = ahead or _NoProjection()
    ahead.upto(3)

    for s in range(n_seq):
        for j in range(GD_QKV // LANES):
            cs = slice(j * LANES, (j + 1) * LANES)
            xpad[s, pad:pad + tb, cs] = main_ref[s * tb:(s + 1) * tb, cs]
    for s in range(n_seq):
        for j in range(GD_QKV // LANES):
            cs = slice(j * LANES, (j + 1) * LANES)
            conv = xpad[s, pad - 3:pad - 3 + tb, cs] * cw_ref[0:1, cs]
            for tap in range(1, GD_CONV):
                conv = conv + xpad[s, pad - 3 + tap:pad - 3 + tap + tb, cs] * cw_ref[tap:tap + 1, cs]
            act[s * tb:(s + 1) * tb, cs] = _silu(conv)

    C2 = 2 * C
    prow = lax.broadcasted_iota(jnp.int32, (C, C2), 0)
    pcol = lax.broadcasted_iota(jnp.int32, (C, C2), 1)
    half_b = pcol >= C
    pcol_in = jnp.where(half_b, pcol - C, pcol)
    incl = prow >= pcol_in
    strict = prow > pcol_in
    eye = jnp.where(prow == pcol_in, 1.0, 0.0)
    row = lax.broadcasted_iota(jnp.int32, (C, C), 0)
    col = lax.broadcasted_iota(jnp.int32, (C, C), 1)
    tril = (row >= col).astype(BF16)
    triu = (row <= col).astype(BF16)

    def first_head(m):
        return jnp.where(half_b, 0.0, m)

    def second_head(m):
        return jnp.where(half_b, m, 0.0)

    def block_diag(m):
        return jnp.concatenate([first_head(m), second_head(m)], axis=0)

    def split3(x):
        hi = x.astype(BF16)
        rest = x - hi.astype(F32)
        md = rest.astype(BF16)
        return hi, md, (rest - md.astype(F32)).astype(BF16)

    ahead.upto(4)
    ab = ab_ref[...]
    g_col = -jnp.exp(al_ref[...]) * _softplus(ab[:, 0:N_HEADS] + dtb_ref[...])
    beta_scr[...] = _sigmoid(ab[:, N_HEADS:2 * N_HEADS])
    for s in range(n_seq):
        for c in range(n_chunks):
            r = row0(s, c)
            hi, md, lw = split3(g_col[r:r + C, :])
            gcc_scr[r:r + C, :] = (_dot(tril, lw) + _dot(tril, md)) + _dot(tril, hi)
            gcr_scr[s * n_chunks + c] = (_dot_tn(lw, triu) + _dot_tn(md, triu)) + _dot_tn(hi, triu)

    scale = HEAD_DIM ** -0.5
    onorm = on_ref[...]
    zeros_k = jnp.zeros((C, HEAD_DIM), F32)

    for s, c, hp in pairs:
        r = row0(s, c)
        rows = slice(r, r + C)
        kq, ks = [], []
        for h in (2 * hp, 2 * hp + 1):
            p, lo = pid(s, c, h), h * HEAD_DIM
            q = act[rows, lo:lo + HEAD_DIM]
            k = act[rows, W + lo:W + lo + HEAD_DIM]
            v = act[rows, 2 * W + lo:2 * W + lo + HEAD_DIM]
            q = q * lax.rsqrt(jnp.sum(q * q, axis=-1, keepdims=True) + EPS) * scale
            k = k * lax.rsqrt(jnp.sum(k * k, axis=-1, keepdims=True) + EPS)
            gc_c = gcc_scr[rows, h:h + 1]
            beta_c = beta_scr[rows, h:h + 1]
            k_beta = k * beta_c
            e_gc = jnp.exp(gc_c)
            rhs_scr[p] = jnp.concatenate([v * beta_c, k_beta * e_gc], axis=1)
            wq_scr[p, C:2 * C, :] = q * e_gc
            kout_scr[p] = k * jnp.exp(gc_c[C - 1:C, :] - gc_c)
            kq.append(jnp.concatenate([k_beta, q], axis=0))
            ks.append(k)
        keys = jnp.concatenate([jnp.concatenate([ks[0], zeros_k], axis=1),
                                jnp.concatenate([zeros_k, ks[1]], axis=1)], axis=0)
        gram_scr[ppid(s, c, hp)] = _dot_nt(jnp.concatenate(kq, axis=1).astype(BF16), keys.astype(BF16))

    ahead.upto(5)
    for s, c, hp in pairs:
        pp, r, ha, hb = ppid(s, c, hp), row0(s, c), 2 * hp, 2 * hp + 1
        gc_c = jnp.where(half_b, gcc_scr[r:r + C, hb:hb + 1], gcc_scr[r:r + C, ha:ha + 1])
        gcr = gcr_scr[s * n_chunks + c]
        gc_r = jnp.concatenate([gcr[ha:ha + 1, :], gcr[hb:hb + 1, :]], axis=1)
        decay = jnp.exp(jnp.where(incl, gc_c - gc_r, -jnp.inf))
        gram = gram_scr[pp]
        lower = jnp.where(strict, gram[:C] * decay, 0.0)
        low_scr[pp] = lower
        attn_scr[pp] = gram[C:] * decay
        inv_scr[pp] = eye - lower
        pow_scr[pp] = _dot((-lower).astype(BF16), block_diag(-lower).astype(BF16))

    ahead.upto(6)
    power = 2
    while power < C:
        last = 2 * power >= C
        for s, c, hp in pairs:
            pp = ppid(s, c, hp)
            pw = pow_scr[pp]
            inv = inv_scr[pp]
            pb = block_diag(pw).astype(BF16)
            if last:
                inv_scr[pp] = inv + _dot(inv.astype(BF16), pb)
            else:
                both = _dot(jnp.concatenate([inv, pw], axis=0).astype(BF16), pb)
                inv_scr[pp] = inv + both[:C]
                pow_scr[pp] = both[C:]
        power *= 2

    ahead.upto(7)
    def stacked_rhs(s, c, hp):
        return jnp.concatenate([rhs_scr[pid(s, c, 2 * hp)], rhs_scr[pid(s, c, 2 * hp + 1)]], axis=0).astype(BF16)

    for s, c, hp in pairs:
        inv = inv_scr[ppid(s, c, hp)]
        rhs = stacked_rhs(s, c, hp)
        x0_scr[pid(s, c, 2 * hp)] = _dot(first_head(inv).astype(BF16), rhs)
        x0_scr[pid(s, c, 2 * hp + 1)] = _dot(second_head(inv).astype(BF16), rhs)
    for s, c, hp in pairs:
        pa, pb2 = pid(s, c, 2 * hp), pid(s, c, 2 * hp + 1)
        low = low_scr[ppid(s, c, hp)]
        xa_hi, xa_lo = _split_bf16(x0_scr[pa])
        xb_hi, xb_lo = _split_bf16(x0_scr[pb2])
        x_both = jnp.concatenate([jnp.concatenate([xa_hi, xa_lo], axis=1),
                                  jnp.concatenate([xb_hi, xb_lo], axis=1)], axis=0)
        x_hi = jnp.concatenate([xa_hi, xb_hi], axis=0)
        for p, part in ((pa, first_head(low)), (pb2, second_head(low))):
            l_hi, l_lo = _split_bf16(part)
            lx = _dot(l_hi, x_both)
            lx = lx[:, :2 * HEAD_DIM] + (lx[:, 2 * HEAD_DIM:] + _dot(l_lo, x_hi))
            rhs_scr[p] = rhs_scr[p] - x0_scr[p] - lx
    for s, c, hp in pairs:
        inv = inv_scr[ppid(s, c, hp)]
        resid = stacked_rhs(s, c, hp)
        for p, part in ((pid(s, c, 2 * hp), first_head(inv)), (pid(s, c, 2 * hp + 1), second_head(inv))):
            sol = x0_scr[p] + _dot(part.astype(BF16), resid)
            u_scr[p] = sol[:, :HEAD_DIM]
            wq_scr[p, 0:C, :] = sol[:, HEAD_DIM:]

    ahead.upto(8)
    for c in range(n_chunks):
        for s in range(n_seq):
            for h in range(N_HEADS):
                sh = s * N_HEADS + h
                ws_scr[sh] = _dot(wq_scr[pid(s, c, h)].astype(BF16), s_scr[sh].astype(BF16))
        for s in range(n_seq):
            for hp in range(N_HEADS // 2):
                r = row0(s, c)
                rows = slice(r, r + C)
                heads = (2 * hp, 2 * hp + 1)
                vn = [(u_scr[pid(s, c, h)] - ws_scr[s * N_HEADS + h][:C]).astype(BF16) for h in heads]
                vn_both = jnp.concatenate(vn, axis=0)
                attn = attn_scr[ppid(s, c, hp)]
                for h, vn16, part in ((heads[0], vn[0], first_head(attn)), (heads[1], vn[1], second_head(attn))):
                    p, lo, sh = pid(s, c, h), h * HEAD_DIM, s * N_HEADS + h
                    o = ws_scr[sh][C:] + _dot(part.astype(BF16), vn_both)
                    g_last = gcc_scr[r + C - 1:r + C, h:h + 1]
                    s_scr[sh] = s_scr[sh] * jnp.exp(g_last) + _dot_tn(kout_scr[p].astype(BF16), vn16)
                    z = main_ref[rows, 3 * W + lo:3 * W + lo + HEAD_DIM]
                    og_ref[rows, lo:lo + HEAD_DIM] = _rms(o, onorm) * _silu(z)

    @pl.when(t == pl.num_programs(1) - 1)
    def _():
        for s in range(n_seq):
            cb_ref[s] = xpad[s, pad + tb - (GD_CONV - 1):pad + tb, :]
            for h in range(N_HEADS):
                sf_ref[s, h] = s_scr[s * N_HEADS + h]


def _gdn_recurrence(main, ab, conv_w, a_log, dt_bias, out_norm, cb0, s0, *, batch, seq, n_chunks, n_seq, fuse=None):
    n = batch * seq
    fused = fuse is not None
    chunk = min(MAX_CHUNK, seq)
    tb = chunk * n_chunks
    assert seq % tb == 0 and batch % n_seq == 0 and seq >= GD_CONV - 1
    assert n_seq == 1 or tb == seq, "several sequences per step only when a step covers whole sequences"
    nt = seq // tb
    rows = n_seq * tb
    n_prob = n_seq * n_chunks * N_HEADS
    has_s0 = s0 is not None
    state_spec = pl.BlockSpec((n_seq, N_HEADS, HEAD_DIM, HEAD_DIM), lambda b, t: (b, 0, 0, 0))
    cb_spec = pl.BlockSpec((n_seq, GD_CONV - 1, GD_QKV), lambda b, t: (b, 0, 0))
    small = lambda shape: pl.BlockSpec(shape, lambda b, t: (0,) * len(shape))
    main_cols = GD_QKV + N_HEADS * HEAD_DIM
    if fused:
        x, gain, w_main, w_ab = fuse
        in_specs, args = _ahead_specs(x, gain, [w_main, w_ab], rows, (batch // n_seq) * nt, nt)
    else:
        in_specs = [pl.BlockSpec((rows, main_cols), lambda b, t: (b * nt + t, 0)),
                    pl.BlockSpec((rows, 2 * N_HEADS), lambda b, t: (b * nt + t, 0))]
        args = [main, ab]
    in_specs += [small((GD_CONV, GD_QKV)), small((1, N_HEADS)), small((1, N_HEADS)), small((1, HEAD_DIM))]
    args += [conv_w, a_log.reshape(1, N_HEADS), dt_bias.reshape(1, N_HEADS), out_norm.reshape(1, HEAD_DIM)]
    if has_s0:
        in_specs += [cb_spec, state_spec]
        args += [cb0, s0]
    vm = lambda *shape: pltpu.VMEM(shape, F32)
    return pl.pallas_call(
        functools.partial(_gdn_kernel, chunk=chunk, n_chunks=n_chunks, n_seq=n_seq, has_s0=has_s0, fused=fused),
        grid=(batch // n_seq, nt),
        in_specs=in_specs,
        out_specs=[pl.BlockSpec((rows, D_MODEL), lambda b, t: (b * nt + t, 0)), state_spec, cb_spec],
        out_shape=[jax.ShapeDtypeStruct((n, D_MODEL), F32),
                   jax.ShapeDtypeStruct((batch, N_HEADS, HEAD_DIM, HEAD_DIM), F32),
                   jax.ShapeDtypeStruct((batch, GD_CONV - 1, GD_QKV), F32)],
        scratch_shapes=[vm(n_seq * N_HEADS, HEAD_DIM, HEAD_DIM),
                        vm(n_seq, tb + SUBLANES, GD_QKV),
                        vm(rows, GD_QKV),
                        vm(rows, N_HEADS),
                        vm(n_seq * n_chunks, N_HEADS, chunk),
                        vm(rows, N_HEADS),
                        vm(n_prob // 2, 2 * chunk, 2 * chunk),
                        vm(n_prob, chunk, 2 * HEAD_DIM),
                        vm(n_prob, 2 * chunk, HEAD_DIM),
                        vm(n_prob, chunk, HEAD_DIM),
                        vm(n_prob // 2, chunk, 2 * chunk),
                        vm(n_prob // 2, chunk, 2 * chunk),
                        vm(n_prob // 2, chunk, 2 * chunk),
                        vm(n_prob // 2, chunk, 2 * chunk),
                        vm(n_prob, chunk, 2 * HEAD_DIM),
                        vm(n_prob, chunk, HEAD_DIM),
                        vm(n_seq * N_HEADS, 2 * chunk, HEAD_DIM)]
        + ([vm(2, rows, main_cols), vm(2, rows, 2 * N_HEADS), pltpu.VMEM((rows, D_MODEL), BF16)] if fused else []),
        compiler_params=_params("arbitrary", "arbitrary"),
        name="gdn_recurrence",
    )(*args)


def _swiglu_rows(x_of, wg, wu, wd, n_sub, emit):
    pending = None
    for sb in range(n_sub + 1):
        if sb < n_sub:
            x = x_of(sb)
            nxt = (_dot(x, wg), _dot(x, wu))
        if pending is not None:
            a, u = pending
            emit(sb - 1, _dot((_silu(a) * u).astype(BF16), wd))
        pending = nxt


def _ffn_kernel(x_ref, og_ref, wo_ref, g_ref, wg_ref, wu_ref, wd_ref, o_ref, h_scr, *, n_sub):
    j = pl.program_id(1)
    sub = x_ref.shape[0] // n_sub

    @pl.when(j == 0)
    def _():
        x = x_ref[...] + _dot(og_ref[...].astype(BF16), wo_ref[...])
        h_scr[...] = _rms(x, g_ref[...]).astype(BF16)
        o_ref[...] = x

    def emit(sb, y):
        o_ref[sb * sub:(sb + 1) * sub, :] += y

    _swiglu_rows(lambda sb: h_scr[sb * sub:(sb + 1) * sub, :], wg_ref[...], wu_ref[...], wd_ref[...], n_sub, emit)


def _ffn(x, og, w_out, gain, wg, wu, wd, *, tm, tf, n_sub):
    n, d = x.shape
    f = wg.shape[1]
    rows = pl.BlockSpec((tm, d), lambda i, j: (i, 0))
    return pl.pallas_call(
        functools.partial(_ffn_kernel, n_sub=n_sub),
        grid=(n // tm, f // tf),
        in_specs=[rows, rows,
                  pl.BlockSpec((d, d), lambda i, j: (0, 0)),
                  pl.BlockSpec((1, d), lambda i, j: (0, 0)),
                  pl.BlockSpec((d, tf), lambda i, j: (0, j)),
                  pl.BlockSpec((d, tf), lambda i, j: (0, j)),
                  pl.BlockSpec((tf, d), lambda i, j: (j, 0))],
        out_specs=rows,
        out_shape=jax.ShapeDtypeStruct((n, d), F32),
        scratch_shapes=[pltpu.VMEM((tm, d), BF16)],
        compiler_params=_params("parallel", "arbitrary"),
        name="dense_ffn",
    )(x, og, w_out, gain.reshape(1, d), wg, wu, wd)


ROUTER_SLOTS = 128
PAD_ROWS = 256


RUN_ALIGN = 32


def _run_copies(n_rows, max_rows, make_copy):
    pieces = []
    off = jnp.int32(0)
    bit = max_rows
    while bit >= RUN_ALIGN:
        has = (n_rows & bit) != 0
        pieces.append((has, make_copy(off, bit)))
        off = off + (n_rows & bit)
        bit //= 2
    return pieces


def _router_kernel(x_ref, og_ref, wo_ref, g_ref, wr_ref, xo_ref, dest_ref, gate_ref, cnt_ref, xs_hbm,
                   cnt_scr, stage, zeros_buf, n_smem, base_smem, sem, *, tm, cap):
    i = pl.program_id(0)
    last = pl.num_programs(0) - 1

    @pl.when(i == 0)
    def _():
        cnt_scr[...] = jnp.zeros_like(cnt_scr)
        zeros_buf[...] = jnp.zeros_like(zeros_buf)
        for e in range(N_EXPERTS):
            n_smem[e] = 0
            base_smem[e] = e * cap

    x = x_ref[...] + _dot(og_ref[...].astype(BF16), wo_ref[...])
    xo_ref[...] = x
    h = _rms(x, g_ref[...])
    h16 = h.astype(BF16)
    h_lo = (h - h16.astype(F32)).astype(BF16)
    w_hi, w_lo = _split_bf16(wr_ref[...])
    logits = _dot(h16, w_hi) + (_dot(h16, w_lo) + _dot(h_lo, w_hi))
    lane = lax.broadcasted_iota(jnp.int32, logits.shape, 1)
    m1 = jnp.max(logits, axis=-1, keepdims=True)
    i1 = jnp.min(jnp.where(logits == m1, lane, N_EXPERTS), axis=-1, keepdims=True)
    rest = jnp.where(lane == i1, -jnp.inf, logits)
    m2 = jnp.max(rest, axis=-1, keepdims=True)
    i2 = jnp.min(jnp.where(rest == m2, lane, N_EXPERTS), axis=-1, keepdims=True)
    e2 = jnp.exp(m2 - m1)
    den = 1.0 + e2
    two = lax.broadcasted_iota(jnp.int32, (tm, TOP_K), 1)
    gate_ref[...] = jnp.where(two == 0, 1.0 / den, e2 / den)

    onehot = ((lane == i1) | (lane == i2)).astype(F32)
    row = lax.broadcasted_iota(jnp.int32, (tm, tm), 0)
    col = lax.broadcasted_iota(jnp.int32, (tm, tm), 1)
    local = _dot((row > col).astype(BF16), onehot.astype(BF16))
    rank = local + cnt_scr[...]
    r1 = jnp.sum(jnp.where(lane == i1, rank, 0.0), axis=-1, keepdims=True).astype(jnp.int32)
    r2 = jnp.sum(jnp.where(lane == i2, rank, 0.0), axis=-1, keepdims=True).astype(jnp.int32)
    dest_ref[...] = jnp.where(two == 0, i1 * cap + r1, i2 * cap + r2)
    cnt_tile = jnp.sum(onehot, axis=0, keepdims=True)
    cnt = cnt_scr[...] + cnt_tile
    cnt_scr[...] = cnt
    cnt_ref[...] = cnt.astype(jnp.int32)

    def wait_runs():
        for e in range(N_EXPERTS):
            for has, copy in _run_copies(n_smem[e], tm, lambda off, size, e=e: pltpu.make_async_copy(
                    stage.at[e, pl.ds(0, size * SUBLANES)], xs_hbm.at[pl.ds(0, size * SUBLANES)], sem)):
                @pl.when(has)
                def _():
                    copy.wait()

    wait_runs()

    code = (onehot * (local + 1.0)).astype(BF16)
    code_t = _dot_tn(code, (row == col).astype(BF16))
    slot_id = lax.broadcasted_iota(jnp.int32, (ROUTER_SLOTS, tm), 0)
    code_i = code_t.astype(jnp.int32)

    def compact(e, q):
        sel = jnp.where(code_i[e:e + 1, :] == slot_id + (q * ROUTER_SLOTS + 1), 1.0, 0.0).astype(BF16)
        _store_token_tiles(stage.at[e], q * ROUTER_SLOTS, _dot(sel, h16))

    for e in range(N_EXPERTS):
        compact(e, 0)
    for e in range(N_EXPERTS):
        n_e = cnt_tile[0, e].astype(jnp.int32)
        for q in range(1, tm // ROUTER_SLOTS):
            @pl.when(q * ROUTER_SLOTS < n_e)
            def _():
                compact(e, q)
        n_copy = jnp.bitwise_and(n_e + (RUN_ALIGN - 1), -RUN_ALIGN)
        base = base_smem[e]
        for has, copy in _run_copies(n_copy, tm, lambda off, size, e=e, base=base: pltpu.make_async_copy(
                stage.at[e, pl.ds(pl.multiple_of(off * SUBLANES, SUBLANES), size * SUBLANES)],
                xs_hbm.at[pl.ds(pl.multiple_of((base + off) * SUBLANES, SUBLANES), size * SUBLANES)], sem)):
            @pl.when(has)
            def _():
                copy.start()
        n_smem[e] = n_copy
        base_smem[e] = base + n_e

    @pl.when(i == last)
    def _():
        wait_runs()
        pads = [pltpu.make_async_copy(
            zeros_buf, xs_hbm.at[pl.ds(pl.multiple_of(base_smem[e] * SUBLANES, SUBLANES), PAD_ROWS * SUBLANES)], sem)
            for e in range(N_EXPERTS)]
        for pad in pads:
            pad.start()
        for pad in pads:
            pad.wait()


def _router(x, og, w_out, gain, w_router, *, tm, cap):
    n, d = x.shape
    two = pl.BlockSpec((tm, TOP_K), lambda i: (i, 0))
    rows = pl.BlockSpec((tm, d), lambda i: (i, 0))
    return pl.pallas_call(
        functools.partial(_router_kernel, tm=tm, cap=cap),
        grid=(n // tm,),
        in_specs=[rows, rows,
                  pl.BlockSpec((d, d), lambda i: (0, 0)),
                  pl.BlockSpec((1, d), lambda i: (0, 0)),
                  pl.BlockSpec((d, N_EXPERTS), lambda i: (0, 0))],
        out_specs=[rows, two, two, pl.BlockSpec((1, N_EXPERTS), lambda i: (0, 0)),
                   pl.BlockSpec(memory_space=pl.ANY)],
        out_shape=[jax.ShapeDtypeStruct((n, d), F32),
                   jax.ShapeDtypeStruct((n, TOP_K), jnp.int32),
                   jax.ShapeDtypeStruct((n, TOP_K), F32),
                   jax.ShapeDtypeStruct((1, N_EXPERTS), jnp.int32),
                   jax.ShapeDtypeStruct((N_EXPERTS * cap * SUBLANES, LANES), F32)],
        scratch_shapes=[pltpu.VMEM((1, N_EXPERTS), F32),
                        pltpu.VMEM((N_EXPERTS, tm * SUBLANES, LANES), F32),
                        pltpu.VMEM((PAD_ROWS * SUBLANES, LANES), F32),
                        pltpu.SMEM((N_EXPERTS,), jnp.int32), pltpu.SMEM((N_EXPERTS,), jnp.int32),
                        pltpu.SemaphoreType.DMA(())],
        compiler_params=_params("arbitrary"),
        name="moe_router",
    )(x, og, w_out, gain.reshape(1, d), w_router)


def _tile_gather_copy(src_hbm, dst, sem, src_row, dst_row):
    return pltpu.make_async_copy(src_hbm.at[pl.ds(pl.multiple_of(src_row * SUBLANES, SUBLANES), SUBLANES)],
                                 dst.at[pl.ds(pl.multiple_of(dst_row * SUBLANES, SUBLANES), SUBLANES)], sem)


def _load_token_tiles(ref, first_row, n_rows):
    return jnp.concatenate([ref[pl.ds(first_row * SUBLANES + s, n_rows, stride=SUBLANES), :]
                            for s in range(D_MODEL // LANES)], axis=1)


def _store_token_tiles(ref, first_row, x):
    for s in range(D_MODEL // LANES):
        ref[pl.ds(first_row * SUBLANES + s, x.shape[0], stride=SUBLANES), :] = x[:, s * LANES:(s + 1) * LANES]


def _expert_kernel(be_ref, xblk_ref, oblk_ref, nv_ref, nu_ref, xs_ref, wg_ref, wu_ref, wd_ref, yb_ref, xb, acc,
                   *, sub_rows, n_sub):
    i = pl.program_id(0)
    j = pl.program_id(1)
    n_valid = nv_ref[i]

    @pl.when(j == 0)
    def _():
        acc[...] = jnp.zeros_like(acc)
        for sb in range(n_sub):
            @pl.when(sb * sub_rows < n_valid)
            def _():
                xb[sb * sub_rows:(sb + 1) * sub_rows, :] = _load_token_tiles(
                    xs_ref, sb * sub_rows, sub_rows).astype(BF16)

    def emit(sb, y):
        acc[sb * sub_rows:(sb + 1) * sub_rows, :] += y

    def x_of(sb):
        return xb[sb * sub_rows:(sb + 1) * sub_rows, :]

    full = n_valid > (n_sub - 1) * sub_rows

    @pl.when(full)
    def _():
        _swiglu_rows(x_of, wg_ref[0], wu_ref[0], wd_ref[0], n_sub, emit)

    for sb in range(n_sub - 1):
        @pl.when(jnp.logical_and(jnp.logical_not(full), sb * sub_rows < n_valid))
        def _():
            _swiglu_rows(lambda _, sb=sb: x_of(sb), wg_ref[0], wu_ref[0], wd_ref[0], 1, lambda _, y, sb=sb: emit(sb, y))

    @pl.when(j == pl.num_programs(1) - 1)
    def _():
        for sb in range(n_sub):
            rows = slice(sb * sub_rows, (sb + 1) * sub_rows)

            @pl.when(sb * sub_rows < n_valid)
            def _():
                _store_token_tiles(yb_ref, sb * sub_rows, acc[rows, :])

            @pl.when(sb * sub_rows >= n_valid)
            def _():
                tiles = slice(sb * sub_rows * SUBLANES, (sb + 1) * sub_rows * SUBLANES)
                yb_ref[tiles, :] = jnp.zeros((sub_rows * SUBLANES, LANES), F32)


def _experts(xs, block_e, x_block, out_block, n_valid, n_used, wg, wu, wd, *, sub_rows, n_sub, tf, out_blocks):
    d = D_MODEL
    tmm = sub_rows * n_sub
    nb = block_e.shape[0]
    nf = D_FF // tf

    def wcol(i, j, be, xblk, oblk, nv, nu):
        return (be[i], 0, jnp.where(i < nu[0], j, nf - 1))

    def wrow(i, j, be, xblk, oblk, nv, nu):
        return (be[i], jnp.where(i < nu[0], j, nf - 1), 0)

    grid_spec = pltpu.PrefetchScalarGridSpec(
        num_scalar_prefetch=5,
        grid=(nb, nf),
        in_specs=[pl.BlockSpec((tmm * SUBLANES, LANES), lambda i, j, be, xblk, oblk, nv, nu: (xblk[i], 0)),
                  pl.BlockSpec((1, d, tf), wcol),
                  pl.BlockSpec((1, d, tf), wcol),
                  pl.BlockSpec((1, tf, d), wrow)],
        out_specs=pl.BlockSpec((tmm * SUBLANES, LANES), lambda i, j, be, xblk, oblk, nv, nu: (oblk[i], 0)),
        scratch_shapes=[pltpu.VMEM((tmm, d), BF16), pltpu.VMEM((tmm, d), F32)],
    )
    return pl.pallas_call(
        functools.partial(_expert_kernel, sub_rows=sub_rows, n_sub=n_sub),
        grid_spec=grid_spec,
        out_shape=jax.ShapeDtypeStruct((out_blocks * tmm * SUBLANES, LANES), F32),
        compiler_params=_params("arbitrary", "arbitrary"),
        name="moe_experts",
    )(block_e, x_block, out_block, n_valid, n_used, xs, wg, wu, wd)


def _combine_kernel(dest_ref, yb_hbm, x_ref, gate_ref, gain_ref, o_ref, r0, r1, sems, *, tm):
    i = pl.program_id(0)
    slot = lax.rem(i, 2)

    def start_gather(tile, dst_slot):
        def issue(r, carry):
            first = TOP_K * (tile * tm + r)
            _tile_gather_copy(yb_hbm, r0.at[dst_slot], sems.at[dst_slot, 0], dest_ref[first], r).start(priority=0)
            _tile_gather_copy(yb_hbm, r1.at[dst_slot], sems.at[dst_slot, 1], dest_ref[first + 1], r).start(priority=1)
            return carry

        lax.fori_loop(0, tm, issue, 0, unroll=8)

    @pl.when(i == 0)
    def _():
        start_gather(0, 0)

    @pl.when(i + 1 < pl.num_programs(0))
    def _():
        start_gather(i + 1, 1 - slot)

    whole = pl.ds(0, tm * SUBLANES)
    pltpu.make_async_copy(yb_hbm.at[whole], r0.at[slot], sems.at[slot, 0]).wait()
    pltpu.make_async_copy(yb_hbm.at[whole], r1.at[slot], sems.at[slot, 1]).wait()
    g = gate_ref[...]
    y = _load_token_tiles(r0.at[slot], 0, tm) * g[:, 0:1] + _load_token_tiles(r1.at[slot], 0, tm) * g[:, 1:2]
    o_ref[...] = _rms(x_ref[...] + y, gain_ref[...])


def _combine(yb, dest_flat, x, gates, gain, *, tm):
    n, d = x.shape
    grid_spec = pltpu.PrefetchScalarGridSpec(
        num_scalar_prefetch=1,
        grid=(n // tm,),
        in_specs=[pl.BlockSpec(memory_space=pl.ANY),
                  pl.BlockSpec((tm, d), lambda i, dest: (i, 0)),
                  pl.BlockSpec((tm, TOP_K), lambda i, dest: (i, 0)),
                  pl.BlockSpec((1, d), lambda i, dest: (0, 0))],
        out_specs=pl.BlockSpec((tm, d), lambda i, dest: (i, 0)),
        scratch_shapes=[pltpu.VMEM((2, tm * SUBLANES, LANES), F32), pltpu.VMEM((2, tm * SUBLANES, LANES), F32),
                        pltpu.SemaphoreType.DMA((2, TOP_K))],
    )
    return pl.pallas_call(
        functools.partial(_combine_kernel, tm=tm),
        grid_spec=grid_spec,
        out_shape=jax.ShapeDtypeStruct((n, d), F32),
        compiler_params=_params("arbitrary"),
        name="moe_combine",
    )(dest_flat, yb, x, gates, gain.reshape(1, d))


def _moe_and_final_norm(x, og, w_out, gain, w_router, wg, wu, wd, final_gain, *, tiles):
    n = x.shape[0]
    sub_rows, n_sub = tiles["moe_sub_rows"], tiles["moe_subs"]
    tmm = sub_rows * n_sub
    assert sub_rows == PAD_ROWS and n % tmm == 0
    cap = n + tmm
    x, dest, gates, counts, xs = _router(x, og, w_out, gain, w_router, tm=tiles["router_rows"], cap=cap)
    counts = counts[0]
    blocks_per_e = (counts + tmm - 1) // tmm
    blocks_end = jnp.cumsum(blocks_per_e)
    n_used = blocks_end[-1].astype(jnp.int32)
    nb = -(-(n * TOP_K + N_EXPERTS * (tmm - 1)) // tmm)
    blk = jnp.minimum(jnp.arange(nb, dtype=jnp.int32), n_used - 1)
    block_e = jnp.minimum(jnp.sum(blk[:, None] >= blocks_end[None, :], axis=1), N_EXPERTS - 1).astype(jnp.int32)
    k_in_e = blk - (blocks_end - blocks_per_e)[block_e]
    x_block = (block_e * (cap // tmm) + k_in_e).astype(jnp.int32)
    used = jnp.arange(nb, dtype=jnp.int32) < n_used
    n_valid = jnp.where(used, jnp.clip(counts[block_e] - k_in_e * tmm, 0, tmm), 0).astype(jnp.int32)
    dump_block = N_EXPERTS * (cap // tmm)
    out_block = jnp.where(used, x_block, dump_block).astype(jnp.int32)
    yb = _experts(xs, block_e, x_block, out_block, n_valid, n_used.reshape(1), wg, wu, wd,
                  sub_rows=sub_rows, n_sub=n_sub, tf=tiles["moe_ff_cols"], out_blocks=dump_block + 1)
    return _combine(yb, dest.reshape(-1), x, gates, final_gain, tm=tiles["combine_rows"])


def _tiles(n_rows, seq):
    big = n_rows >= 8192
    return {
        "proj_rows": 1024 if n_rows % 1024 == 0 else n_rows,
        "proj_cols": 2048,
        "rec_chunks": min(4, seq // min(MAX_CHUNK, seq)),
        "rec_seqs": 1 if seq > MAX_CHUNK else 4,
        "proj_in_recurrence": seq > MAX_CHUNK,
        "ffn_rows": 1024 if n_rows % 1024 == 0 else n_rows,
        "ff_cols": 512,
        "router_rows": 256,
        "moe_sub_rows": 256,
        "moe_subs": 4 if big else 2,
        "moe_ff_cols": 1792,
        "ffn_subs": 4 if big else 2,
        "combine_rows": 256,
    }


def _trunk(x3, hg_state, gd_state, gd_conv, p):
    batch, seq, d = x3.shape
    n = batch * seq
    x = x3.reshape(n, d)
    tiles = _tiles(n, seq)
    tm, tn = tiles["proj_rows"], tiles["proj_cols"]

    rec = dict(batch=batch, seq=seq, n_chunks=tiles["rec_chunks"], n_seq=tiles["rec_seqs"])

    if tiles["proj_in_recurrence"]:
        proj, fuse = None, (x, p["norm_mix"][0], p["hgrn_w_in"])
    else:
        proj, fuse = _norm_proj(x, p["norm_mix"][0], p["hgrn_w_in"], tm=tm, tn=tn)[0], None
    og, hg_new = _hgrn_recurrence(proj, p["hgrn_lb"], p["hgrn_norm"], hg_state, layer=0, fuse=fuse, **rec)
    x = _ffn(x, og, p["hgrn_w_out"], p["norm_ffn"][0], p["ffn_w_gate"], p["ffn_w_up"], p["ffn_w_down"],
             tm=tiles["ffn_rows"], tf=tiles["ff_cols"], n_sub=tiles["ffn_subs"])

    if tiles["proj_in_recurrence"]:
        main, ab, fuse = None, None, (x, p["norm_mix"][1], p["gdn_w_main"], p["gdn_w_ab"])
    else:
        (main, ab), fuse = _norm_proj(x, p["norm_mix"][1], p["gdn_w_main"], p["gdn_w_ab"], tm=tm, tn=tn), None
    og, gd_new, cv_new = _gdn_recurrence(main, ab, p["gdn_conv"], p["gdn_a_log"], p["gdn_dt_bias"], p["gdn_norm"],
                                         gd_conv, gd_state, fuse=fuse, **rec)
    y = _moe_and_final_norm(x, og, p["gdn_w_out"], p["norm_ffn"][1], p["moe_router"], p["moe_w_gate"], p["moe_w_up"], p["moe_w_down"],
                            p["norm_out"], tiles=tiles)
    return y.reshape(batch, seq, d), hg_new[None], gd_new[None], cv_new[None]


def kernel(x_prompt, x_sample, state_hgrn, state_gdn, state_gdn_conv, norm_mix, norm_ffn, norm_out, hgrn_w_in, hgrn_lb, hgrn_norm, hgrn_w_out, gdn_w_in, gdn_conv, gdn_a_log, gdn_dt_bias, gdn_norm, gdn_w_out, ffn_w_gate, ffn_w_up, ffn_w_down, moe_router, moe_w_gate, moe_w_up, moe_w_down):
    assert hgrn_w_in.shape[0] == 1 and gdn_w_in.shape[0] == 1, "one HGRN2 layer and one gated-DeltaNet layer"
    main_cols = GD_QKV + N_HEADS * HEAD_DIM
    p = {
        "norm_mix": norm_mix, "norm_ffn": norm_ffn, "norm_out": norm_out,
        "hgrn_w_in": hgrn_w_in[0].astype(BF16), "hgrn_lb": hgrn_lb, "hgrn_norm": hgrn_norm[0],
        "hgrn_w_out": hgrn_w_out[0].astype(BF16),
        "gdn_w_main": gdn_w_in[0, :, :main_cols].astype(BF16), "gdn_w_ab": gdn_w_in[0, :, main_cols:].astype(BF16),
        "gdn_conv": gdn_conv[0], "gdn_a_log": gdn_a_log[0], "gdn_dt_bias": gdn_dt_bias[0], "gdn_norm": gdn_norm[0],
        "gdn_w_out": gdn_w_out[0].astype(BF16),
        "ffn_w_gate": ffn_w_gate[0].astype(BF16), "ffn_w_up": ffn_w_up[0].astype(BF16),
        "ffn_w_down": ffn_w_down[0].astype(BF16),
        "moe_router": moe_router[0],
        "moe_w_gate": moe_w_gate[0].astype(BF16), "moe_w_up": moe_w_up[0].astype(BF16),
        "moe_w_down": moe_w_down[0].astype(BF16),
    }
    y_p, hg_p, gd_p, cv_p = _trunk(x_prompt, None, None, None, p)
    y_s, hg_s, gd_s, cv_s = _trunk(x_sample, state_hgrn[0], state_gdn[0], state_gdn_conv[0], p)
    return (y_p, y_s, hg_p, hg_s, gd_p, gd_s, cv_p, cv_s)
```

```python
import functools

import jax
import jax.numpy as jnp
from jax import lax
from jax.experimental import pallas as pl
from jax.experimental.pallas import tpu as pltpu

F32 = jnp.float32
BF16 = jnp.bfloat16

D_MODEL = 1024
N_HEADS = 8
HEAD_DIM = 128
D_FF = 3584
N_EXPERTS = 8
TOP_K = 2
GD_QKV = 3 * D_MODEL
GD_CONV = 4
MAX_CHUNK = 64
EPS = 1e-6
LANES = 128
SUBLANES = 8
VMEM_LIMIT_BYTES = 52 * 1024 * 1024
PROJ_PIECES = 8
ROUTER_SLOTS = 128
PAD_ROWS = 256
RUN_ALIGN = 32


def _params(*sem):
    return pltpu.CompilerParams(dimension_semantics=sem, vmem_limit_bytes=VMEM_LIMIT_BYTES)


def _rms(x, gain):
    return x * lax.rsqrt(jnp.mean(x * x, axis=-1, keepdims=True) + EPS) * gain


def _sigmoid(x):
    t = jnp.exp(-jnp.abs(x))
    r = 1.0 / (1.0 + t)
    return jnp.where(x >= 0, r, t * r)


def _silu(x):
    return x * (0.5 * jnp.tanh(0.5 * x) + 0.5)


def _softplus(x):
    return jnp.maximum(x, 0.0) + jnp.log1p(jnp.exp(-jnp.abs(x)))


def _dot(a, b):
    return jnp.dot(a, b, preferred_element_type=F32)


def _dot_nt(a, b):
    return lax.dot_general(a, b, (((1,), (1,)), ((), ())), preferred_element_type=F32)


def _dot_tn(a, b):
    return lax.dot_general(a, b, (((0,), (0,)), ((), ())), preferred_element_type=F32)


def _norm_proj_kernel(x_ref, g_ref, w_ref, *rest, has_small):
    if has_small:
        ws_ref, o_ref, os_ref, h_scr = rest
    else:
        o_ref, h_scr = rest

    @pl.when(pl.program_id(1) == 0)
    def _():
        hb = _rms(x_ref[...], g_ref[...]).astype(BF16)
        h_scr[...] = hb
        if has_small:
            os_ref[...] = _dot(hb, ws_ref[...])

    o_ref[...] = _dot(h_scr[...], w_ref[...])


def _norm_proj(x, gain, w, w_small=None, *, tm, tn):
    n, d = x.shape
    e = w.shape[1]
    has_small = w_small is not None
    in_specs = [pl.BlockSpec((tm, d), lambda i, j: (i, 0)),
                pl.BlockSpec((1, d), lambda i, j: (0, 0)),
                pl.BlockSpec((d, tn), lambda i, j: (0, j))]
    out_shape = [jax.ShapeDtypeStruct((n, e), F32)]
    out_specs = [pl.BlockSpec((tm, tn), lambda i, j: (i, j))]
    args = [x, gain.reshape(1, d), w]
    if has_small:
        es = w_small.shape[1]
        in_specs.append(pl.BlockSpec((d, es), lambda i, j: (0, 0)))
        out_shape.append(jax.ShapeDtypeStruct((n, es), F32))
        out_specs.append(pl.BlockSpec((tm, es), lambda i, j: (i, 0)))
        args.append(w_small)
    return pl.pallas_call(
        functools.partial(_norm_proj_kernel, has_small=has_small),
        grid=(n // tm, e // tn),
        in_specs=in_specs, out_specs=out_specs, out_shape=out_shape,
        scratch_shapes=[pltpu.VMEM((tm, d), BF16)],
        compiler_params=_params("parallel", "arbitrary"),
        name="norm_proj",
    )(*args)


class _Projection:
    def __init__(self, x_ref, gain_ref, w_refs, dst_scrs, h_scr, dst_slot):
        self.x_ref, self.gain_ref, self.w_refs, self.dst_scrs = x_ref, gain_ref, w_refs, dst_scrs
        self.h_scr, self.dst_slot, self.done = h_scr, dst_slot, 0

    def upto(self, k):
        for piece in range(self.done, min(k, PROJ_PIECES)):
            if piece == 0:
                self.h_scr[...] = _rms(self.x_ref[...], self.gain_ref[...]).astype(BF16)
            h16 = self.h_scr[...]
            for w_ref, dst in zip(self.w_refs, self.dst_scrs):
                width = w_ref.shape[1]
                if width % (PROJ_PIECES * LANES) == 0:
                    cols = slice(piece * width // PROJ_PIECES, (piece + 1) * width // PROJ_PIECES)
                    dst[self.dst_slot, :, cols] = _dot(h16, w_ref[:, cols])
                elif piece == 0:
                    dst[self.dst_slot] = _dot(h16, w_ref[...])
        self.done = max(self.done, k)


class _NoProjection:
    def upto(self, k):
        pass


def _with_projection_ahead(body, refs, n_weights, **kw):
    x_cur_ref, x_next_ref, gain_ref = refs[:3]
    w_refs = refs[3:3 + n_weights]
    dst_scrs = refs[len(refs) - n_weights - 1:len(refs) - 1]
    h_scr = refs[-1]
    g = pl.program_id(0) * pl.num_programs(1) + pl.program_id(1)

    @pl.when(g == 0)
    def _():
        _Projection(x_cur_ref, gain_ref, w_refs, dst_scrs, h_scr, 0).upto(PROJ_PIECES)

    for cur in range(2):
        @pl.when(lax.rem(g, 2) == cur)
        def _():
            ahead = _Projection(x_next_ref, gain_ref, w_refs, dst_scrs, h_scr, 1 - cur)
            body(*[scr.at[cur] for scr in dst_scrs], *refs[3 + n_weights:len(refs) - n_weights - 1], ahead=ahead,
                 **kw)
            ahead.upto(PROJ_PIECES)


def _hgrn_kernel(*refs, fused, **kw):
    if fused:
        _with_projection_ahead(_hgrn_step, refs, 1, **kw)
    else:
        _hgrn_step(*refs, **kw)


def _hgrn_step(proj_ref, *refs, chunk, n_chunks, n_seq, layer, has_s0, ahead=None):
    if has_s0:
        lb_ref, on_ref, s0_ref, og_ref, sf_ref, *scr = refs
    else:
        lb_ref, on_ref, og_ref, sf_ref, *scr = refs
    st_scr, b_scr, q_scr, k_scr, a_scr, qin_scr, kout_scr, dec_scr, oi_scr = scr[:9]
    C = chunk
    W = N_HEADS * HEAD_DIM
    tb = C * n_chunks
    t = pl.program_id(1)
    problems = [(s, c, h) for s in range(n_seq) for c in range(n_chunks) for h in range(N_HEADS)]

    def pid(s, c, h):
        return (s * n_chunks + c) * N_HEADS + h

    def row0(s, c):
        return s * tb + c * C

    @pl.when(t == 0)
    def _():
        for s in range(n_seq):
            for h in range(N_HEADS):
                st_scr[s * N_HEADS + h] = s0_ref[s, h].T if has_s0 else jnp.zeros((HEAD_DIM, HEAD_DIM), F32)

    ahead = ahead or _NoProjection()
    ahead.upto(1)

    lbraw = lb_ref[...]
    ex = jnp.exp(lbraw - jnp.max(lbraw, axis=0, keepdims=True))
    sm = ex / jnp.sum(ex, axis=0, keepdims=True)
    lb = sm[0:1]
    for j in range(1, layer + 1):
        lb = lb + sm[j:j + 1]

    row = lax.broadcasted_iota(jnp.int32, (C, C), 0)
    col = lax.broadcasted_iota(jnp.int32, (C, C), 1)
    causal = row >= col
    tril = causal.astype(BF16)
    mid = C // 2
    scale = HEAD_DIM ** -0.5
    onorm = on_ref[...]

    for s in range(n_seq):
        for c in range(n_chunks):
            rows = slice(row0(s, c), row0(s, c) + C)
            f_raw = proj_ref[rows, W:2 * W]
            tt = jnp.exp(-jnp.abs(f_raw))
            rr = 1.0 / (1.0 + tt)
            pos = f_raw >= 0
            sig = jnp.where(pos, rr, tt * rr)
            sig_neg = jnp.where(pos, tt * rr, rr)
            log_f = jnp.log(lb + (1.0 - lb) * sig)
            k_scr[rows, :] = (1.0 - lb) * sig_neg
            q_scr[rows, :] = _silu(proj_ref[rows, 0:W]) * scale
            hi = log_f.astype(BF16)
            rest = log_f - hi.astype(F32)
            md = rest.astype(BF16)
            lw = (rest - md.astype(F32)).astype(BF16)
            b_scr[rows, :] = (_dot(tril, lw) + _dot(tril, md)) + _dot(tril, hi)

    ahead.upto(3)
    for s, c, h in problems:
        p, lo = pid(s, c, h), h * HEAD_DIM
        rows = slice(row0(s, c), row0(s, c) + C)
        b = b_scr[rows, lo:lo + HEAD_DIM]
        q = q_scr[rows, lo:lo + HEAD_DIM]
        k = k_scr[rows, lo:lo + HEAD_DIM]
        b_mid = b[mid:mid + 1]
        b_last = b[C - 1:C]
        a_scr[p] = _dot_nt((q * jnp.exp(b - b_mid)).astype(BF16), (k * jnp.exp(b_mid - b)).astype(BF16))
        qin_scr[p] = q * jnp.exp(b)
        kout_scr[p] = k * jnp.exp(b_last - b)
        dec_scr[p] = jnp.exp(b_last)

    ahead.upto(5)
    for s, c, h in problems:
        p, lo = pid(s, c, h), h * HEAD_DIM
        rows = slice(row0(s, c), row0(s, c) + C)
        a = jnp.where(causal, a_scr[p], 0.0)
        oi_scr[p] = _dot(a.astype(BF16), proj_ref[rows, 2 * W + lo:2 * W + lo + HEAD_DIM].astype(BF16))

    for c in range(n_chunks):
        ahead.upto(6 + c)
        for s in range(n_seq):
            for h in range(N_HEADS):
                p, lo, sh = pid(s, c, h), h * HEAD_DIM, s * N_HEADS + h
                rows = slice(row0(s, c), row0(s, c) + C)
                st = st_scr[sh]
                o = oi_scr[p] + _dot_nt(qin_scr[p].astype(BF16), st.astype(BF16))
                vb = proj_ref[rows, 2 * W + lo:2 * W + lo + HEAD_DIM].astype(BF16)
                st_scr[sh] = st * dec_scr[p] + _dot_tn(vb, kout_scr[p].astype(BF16))
                gate = proj_ref[rows, 3 * W + lo:3 * W + lo + HEAD_DIM]
                og_ref[rows, lo:lo + HEAD_DIM] = _rms(o, onorm) * _silu(gate)

    @pl.when(t == pl.num_programs(1) - 1)
    def _():
        for s in range(n_seq):
            for h in range(N_HEADS):
                sf_ref[s, h] = st_scr[s * N_HEADS + h].T


def _ahead_specs(x, gain, weights, rows, n_steps, nt):
    d = x.shape[1]
    specs = [pl.BlockSpec((rows, d), lambda b, t: (b * nt + t, 0)),
             pl.BlockSpec((rows, d), lambda b, t: (jnp.minimum(b * nt + t + 1, n_steps - 1), 0)),
             pl.BlockSpec((1, d), lambda b, t: (0, 0))]
    specs += [pl.BlockSpec(w.shape, lambda b, t: (0, 0)) for w in weights]
    return specs, [x, x, gain.reshape(1, d)] + list(weights)


def _hgrn_recurrence(proj, lb_raw, out_norm, s0, *, batch, seq, layer, n_chunks, n_seq, fuse=None):
    n = batch * seq
    chunk = min(MAX_CHUNK, seq)
    tb = chunk * n_chunks
    assert seq % tb == 0 and batch % n_seq == 0
    assert n_seq == 1 or tb == seq, "several sequences per step only when a step covers whole sequences"
    nt = seq // tb
    rows = n_seq * tb
    n_prob = n_seq * n_chunks * N_HEADS
    width = N_HEADS * HEAD_DIM
    has_s0 = s0 is not None
    fused = fuse is not None
    state_spec = pl.BlockSpec((n_seq, N_HEADS, HEAD_DIM, HEAD_DIM), lambda b, t: (b, 0, 0, 0))
    if fused:
        x, gain, w = fuse
        in_specs, args = _ahead_specs(x, gain, [w], rows, (batch // n_seq) * nt, nt)
    else:
        in_specs, args = [pl.BlockSpec((rows, proj.shape[1]), lambda b, t: (b * nt + t, 0))], [proj]
    in_specs += [pl.BlockSpec(lb_raw.shape, lambda b, t: (0, 0)),
                 pl.BlockSpec((1, HEAD_DIM), lambda b, t: (0, 0))]
    args += [lb_raw, out_norm.reshape(1, HEAD_DIM)]
    if has_s0:
        in_specs.append(state_spec)
        args.append(s0)
    vm = lambda *shape: pltpu.VMEM(shape, F32)
    ahead_scratch = [vm(2, rows, 4 * width), pltpu.VMEM((rows, D_MODEL), BF16)] if fused else []
    return pl.pallas_call(
        functools.partial(_hgrn_kernel, chunk=chunk, n_chunks=n_chunks, n_seq=n_seq, layer=layer, has_s0=has_s0,
                          fused=fused),
        grid=(batch // n_seq, nt),
        in_specs=in_specs,
        out_specs=[pl.BlockSpec((rows, D_MODEL), lambda b, t: (b * nt + t, 0)), state_spec],
        out_shape=[jax.ShapeDtypeStruct((n, D_MODEL), F32),
                   jax.ShapeDtypeStruct((batch, N_HEADS, HEAD_DIM, HEAD_DIM), F32)],
        scratch_shapes=[vm(n_seq * N_HEADS, HEAD_DIM, HEAD_DIM),
                        vm(rows, width),
                        vm(rows, width),
                        vm(rows, width),
                        vm(n_prob, chunk, chunk),
                        vm(n_prob, chunk, HEAD_DIM),
                        vm(n_prob, chunk, HEAD_DIM),
                        vm(n_prob, 1, HEAD_DIM),
                        vm(n_prob, chunk, HEAD_DIM)] + ahead_scratch,
        compiler_params=_params("arbitrary", "arbitrary"),
        name="hgrn_recurrence",
    )(*args)


def _split_bf16(x):
    hi = x.astype(BF16)
    return hi, (x - hi.astype(F32)).astype(BF16)


def _gdn_kernel(*refs, fused, **kw):
    if fused:
        _with_projection_ahead(_gdn_step, refs, 2, **kw)
    else:
        _gdn_step(*refs, **kw)


def _gdn_step(main_ref, ab_ref, *refs, chunk, n_chunks, n_seq, has_s0, ahead=None):
    (cw_ref, al_ref, dtb_ref, on_ref), rest = refs[:4], refs[4:]
    if has_s0:
        (cb0_ref, s0_ref), rest = rest[:2], rest[2:]
    (og_ref, sf_ref, cb_ref, s_scr, xpad, act, gcc_scr, gcr_scr, beta_scr, gram_scr, rhs_scr, wq_scr, kout_scr,
     inv_scr, pow_scr, low_scr, attn_scr, x0_scr, u_scr, ws_scr) = rest[:20]
    C = chunk
    W = N_HEADS * HEAD_DIM
    tb = C * n_chunks
    t = pl.program_id(1)
    pad = SUBLANES
    pairs = [(s, c, hp) for s in range(n_seq) for c in range(n_chunks) for hp in range(N_HEADS // 2)]

    def pid(s, c, h):
        return (s * n_chunks + c) * N_HEADS + h

    def ppid(s, c, hp):
        return (s * n_chunks + c) * (N_HEADS // 2) + hp

    def row0(s, c):
        return s * tb + c * C

    @pl.when(t == 0)
    def _():
        for s in range(n_seq):
            for h in range(N_HEADS):
                s_scr[s * N_HEADS + h] = s0_ref[s, h] if has_s0 else jnp.zeros((HEAD_DIM, HEAD_DIM), F32)
            xpad[s, 0:pad, :] = jnp.zeros((pad, GD_QKV), F32)
            if has_s0:
                xpad[s, pad - (GD_CONV - 1):pad, :] = cb0_ref[s]

    @pl.when(t > 0)
    def _():
        for s in range(n_seq):
            xpad[s, 0:pad, :] = xpad[s, tb:tb + pad, :]

    ahead = ahead or _NoProjection()
    ahead.upto(6)

    for s in range(n_seq):
        for j in range(GD_QKV // LANES):
            cs = slice(j * LANES, (j + 1) * LANES)
            xpad[s, pad:pad + tb, cs] = main_ref[s * tb:(s + 1) * tb, cs]
    for s in range(n_seq):
        for j in range(GD_QKV // LANES):
            cs = slice(j * LANES, (j + 1) * LANES)
            conv = xpad[s, pad - 3:pad - 3 + tb, cs] * cw_ref[0:1, cs]
            for tap in range(1, GD_CONV):
                conv = conv + xpad[s, pad - 3 + tap:pad - 3 + tap + tb, cs] * cw_ref[tap:tap + 1, cs]
            act[s * tb:(s + 1) * tb, cs] = _silu(conv)

    C2 = 2 * C
    prow = lax.broadcasted_iota(jnp.int32, (C, C2), 0)
    pcol = lax.broadcasted_iota(jnp.int32, (C, C2), 1)
    half_b = pcol >= C
    pcol_in = jnp.where(half_b, pcol - C, pcol)
    incl = prow >= pcol_in
    strict = prow > pcol_in
    eye = jnp.where(prow == pcol_in, 1.0, 0.0)
    row = lax.broadcasted_iota(jnp.int32, (C, C), 0)
    col = lax.broadcasted_iota(jnp.int32, (C, C), 1)
    tril = (row >= col).astype(BF16)
    triu = (row <= col).astype(BF16)

    def first_head(m):
        return jnp.where(half_b, 0.0, m)

    def second_head(m):
        return jnp.where(half_b, m, 0.0)

    def block_diag(m):
        return jnp.concatenate([first_head(m), second_head(m)], axis=0)

    def split3(x):
        hi = x.astype(BF16)
        rest = x - hi.astype(F32)
        md = rest.astype(BF16)
        return hi, md, (rest - md.astype(F32)).astype(BF16)

    ab = ab_ref[...]
    g_col = -jnp.exp(al_ref[...]) * _softplus(ab[:, 0:N_HEADS] + dtb_ref[...])
    beta_scr[...] = _sigmoid(ab[:, N_HEADS:2 * N_HEADS])
    for s in range(n_seq):
        for c in range(n_chunks):
            r = row0(s, c)
            hi, md, lw = split3(g_col[r:r + C, :])
            gcc_scr[r:r + C, :] = (_dot(tril, lw) + _dot(tril, md)) + _dot(tril, hi)
            gcr_scr[s * n_chunks + c] = (_dot_tn(lw, triu) + _dot_tn(md, triu)) + _dot_tn(hi, triu)

    scale = HEAD_DIM ** -0.5
    onorm = on_ref[...]
    zeros_k = jnp.zeros((C, HEAD_DIM), F32)

    for s, c, hp in pairs:
        r = row0(s, c)
        rows = slice(r, r + C)
        kq, ks = [], []
        for h in (2 * hp, 2 * hp + 1):
            p, lo = pid(s, c, h), h * HEAD_DIM
            q = act[rows, lo:lo + HEAD_DIM]
            k = act[rows, W + lo:W + lo + HEAD_DIM]
            v = act[rows, 2 * W + lo:2 * W + lo + HEAD_DIM]
            q = q * lax.rsqrt(jnp.sum(q * q, axis=-1, keepdims=True) + EPS) * scale
            k = k * lax.rsqrt(jnp.sum(k * k, axis=-1, keepdims=True) + EPS)
            gc_c = gcc_scr[rows, h:h + 1]
            beta_c = beta_scr[rows, h:h + 1]
            k_beta = k * beta_c
            e_gc = jnp.exp(gc_c)
            rhs_scr[p] = jnp.concatenate([v * beta_c, k_beta * e_gc], axis=1)
            wq_scr[p, C:2 * C, :] = q * e_gc
            kout_scr[p] = k * jnp.exp(gc_c[C - 1:C, :] - gc_c)
            kq.append(jnp.concatenate([k_beta, q], axis=0))
            ks.append(k)
        keys = jnp.concatenate([jnp.concatenate([ks[0], zeros_k], axis=1),
                                jnp.concatenate([zeros_k, ks[1]], axis=1)], axis=0)
        gram_scr[ppid(s, c, hp)] = _dot_nt(jnp.concatenate(kq, axis=1).astype(BF16), keys.astype(BF16))

    ahead.upto(7)
    for s, c, hp in pairs:
        pp, r, ha, hb = ppid(s, c, hp), row0(s, c), 2 * hp, 2 * hp + 1
        gc_c = jnp.where(half_b, gcc_scr[r:r + C, hb:hb + 1], gcc_scr[r:r + C, ha:ha + 1])
        gcr = gcr_scr[s * n_chunks + c]
        gc_r = jnp.concatenate([gcr[ha:ha + 1, :], gcr[hb:hb + 1, :]], axis=1)
        decay = jnp.exp(jnp.where(incl, gc_c - gc_r, -jnp.inf))
        gram = gram_scr[pp]
        lower = jnp.where(strict, gram[:C] * decay, 0.0)
        low_scr[pp] = lower
        attn_scr[pp] = gram[C:] * decay
        inv_scr[pp] = eye - lower
        pow_scr[pp] = _dot((-lower).astype(BF16), block_diag(-lower).astype(BF16))

    power = 2
    while power < C:
        last = 2 * power >= C
        for s, c, hp in pairs:
            pp = ppid(s, c, hp)
            pw = pow_scr[pp]
            inv = inv_scr[pp]
            pb = block_diag(pw).astype(BF16)
            if last:
                inv_scr[pp] = inv + _dot(inv.astype(BF16), pb)
            else:
                both = _dot(jnp.concatenate([inv, pw], axis=0).astype(BF16), pb)
                inv_scr[pp] = inv + both[:C]
                pow_scr[pp] = both[C:]
        power *= 2

    ahead.upto(8)
    def stacked_rhs(s, c, hp):
        return jnp.concatenate([rhs_scr[pid(s, c, 2 * hp)], rhs_scr[pid(s, c, 2 * hp + 1)]], axis=0).astype(BF16)

    for s, c, hp in pairs:
        inv = inv_scr[ppid(s, c, hp)]
        rhs = stacked_rhs(s, c, hp)
        x0_scr[pid(s, c, 2 * hp)] = _dot(first_head(inv).astype(BF16), rhs)
        x0_scr[pid(s, c, 2 * hp + 1)] = _dot(second_head(inv).astype(BF16), rhs)
    for s, c, hp in pairs:
        pa, pb2 = pid(s, c, 2 * hp), pid(s, c, 2 * hp + 1)
        low = low_scr[ppid(s, c, hp)]
        xa_hi, xa_lo = _split_bf16(x0_scr[pa])
        xb_hi, xb_lo = _split_bf16(x0_scr[pb2])
        x_both = jnp.concatenate([jnp.concatenate([xa_hi, xa_lo], axis=1),
                                  jnp.concatenate([xb_hi, xb_lo], axis=1)], axis=0)
        x_hi = jnp.concatenate([xa_hi, xb_hi], axis=0)
        for p, part in ((pa, first_head(low)), (pb2, second_head(low))):
            l_hi, l_lo = _split_bf16(part)
            lx = _dot(l_hi, x_both)
            lx = lx[:, :2 * HEAD_DIM] + (lx[:, 2 * HEAD_DIM:] + _dot(l_lo, x_hi))
            rhs_scr[p] = rhs_scr[p] - x0_scr[p] - lx
    for s, c, hp in pairs:
        inv = inv_scr[ppid(s, c, hp)]
        resid = stacked_rhs(s, c, hp)
        for p, part in ((pid(s, c, 2 * hp), first_head(inv)), (pid(s, c, 2 * hp + 1), second_head(inv))):
            sol = x0_scr[p] + _dot(part.astype(BF16), resid)
            u_scr[p] = sol[:, :HEAD_DIM]
            wq_scr[p, 0:C, :] = sol[:, HEAD_DIM:]

    for c in range(n_chunks):
        for s in range(n_seq):
            for h in range(N_HEADS):
                sh = s * N_HEADS + h
                ws_scr[sh] = _dot(wq_scr[pid(s, c, h)].astype(BF16), s_scr[sh].astype(BF16))
        for s in range(n_seq):
            for hp in range(N_HEADS // 2):
                r = row0(s, c)
                rows = slice(r, r + C)
                heads = (2 * hp, 2 * hp + 1)
                vn = [(u_scr[pid(s, c, h)] - ws_scr[s * N_HEADS + h][:C]).astype(BF16) for h in heads]
                vn_both = jnp.concatenate(vn, axis=0)
                attn = attn_scr[ppid(s, c, hp)]
                for h, vn16, part in ((heads[0], vn[0], first_head(attn)), (heads[1], vn[1], second_head(attn))):
                    p, lo, sh = pid(s, c, h), h * HEAD_DIM, s * N_HEADS + h
                    o = ws_scr[sh][C:] + _dot(part.astype(BF16), vn_both)
                    g_last = gcc_scr[r + C - 1:r + C, h:h + 1]
                    s_scr[sh] = s_scr[sh] * jnp.exp(g_last) + _dot_tn(kout_scr[p].astype(BF16), vn16)
                    z = main_ref[rows, 3 * W + lo:3 * W + lo + HEAD_DIM]
                    og_ref[rows, lo:lo + HEAD_DIM] = _rms(o, onorm) * _silu(z)

    @pl.when(t == pl.num_programs(1) - 1)
    def _():
        for s in range(n_seq):
            cb_ref[s] = xpad[s, pad + tb - (GD_CONV - 1):pad + tb, :]
            for h in range(N_HEADS):
                sf_ref[s, h] = s_scr[s * N_HEADS + h]


def _gdn_recurrence(main, ab, conv_w, a_log, dt_bias, out_norm, cb0, s0, *, batch, seq, n_chunks, n_seq, fuse=None):
    n = batch * seq
    fused = fuse is not None
    chunk = min(MAX_CHUNK, seq)
    tb = chunk * n_chunks
    assert seq % tb == 0 and batch % n_seq == 0 and seq >= GD_CONV - 1
    assert n_seq == 1 or tb == seq, "several sequences per step only when a step covers whole sequences"
    nt = seq // tb
    rows = n_seq * tb
    n_prob = n_seq * n_chunks * N_HEADS
    has_s0 = s0 is not None
    state_spec = pl.BlockSpec((n_seq, N_HEADS, HEAD_DIM, HEAD_DIM), lambda b, t: (b, 0, 0, 0))
    cb_spec = pl.BlockSpec((n_seq, GD_CONV - 1, GD_QKV), lambda b, t: (b, 0, 0))
    small = lambda shape: pl.BlockSpec(shape, lambda b, t: (0,) * len(shape))
    main_cols = GD_QKV + N_HEADS * HEAD_DIM
    if fused:
        x, gain, w_main, w_ab = fuse
        in_specs, args = _ahead_specs(x, gain, [w_main, w_ab], rows, (batch // n_seq) * nt, nt)
    else:
        in_specs = [pl.BlockSpec((rows, main_cols), lambda b, t: (b * nt + t, 0)),
                    pl.BlockSpec((rows, 2 * N_HEADS), lambda b, t: (b * nt + t, 0))]
        args = [main, ab]
    in_specs += [small((GD_CONV, GD_QKV)), small((1, N_HEADS)), small((1, N_HEADS)), small((1, HEAD_DIM))]
    args += [conv_w, a_log.reshape(1, N_HEADS), dt_bias.reshape(1, N_HEADS), out_norm.reshape(1, HEAD_DIM)]
    if has_s0:
        in_specs += [cb_spec, state_spec]
        args += [cb0, s0]
    vm = lambda *shape: pltpu.VMEM(shape, F32)
    return pl.pallas_call(
        functools.partial(_gdn_kernel, chunk=chunk, n_chunks=n_chunks, n_seq=n_seq, has_s0=has_s0, fused=fused),
        grid=(batch // n_seq, nt),
        in_specs=in_specs,
        out_specs=[pl.BlockSpec((rows, D_MODEL), lambda b, t: (b * nt + t, 0)), state_spec, cb_spec],
        out_shape=[jax.ShapeDtypeStruct((n, D_MODEL), F32),
                   jax.ShapeDtypeStruct((batch, N_HEADS, HEAD_DIM, HEAD_DIM), F32),
                   jax.ShapeDtypeStruct((batch, GD_CONV - 1, GD_QKV), F32)],
        scratch_shapes=[vm(n_seq * N_HEADS, HEAD_DIM, HEAD_DIM),
                        vm(n_seq, tb + SUBLANES, GD_QKV),
                        vm(rows, GD_QKV),
                        vm(rows, N_HEADS),
                        vm(n_seq * n_chunks, N_HEADS, chunk),
                        vm(rows, N_HEADS),
                        vm(n_prob // 2, 2 * chunk, 2 * chunk),
                        vm(n_prob, chunk, 2 * HEAD_DIM),
                        vm(n_prob, 2 * chunk, HEAD_DIM),
                        vm(n_prob, chunk, HEAD_DIM),
                        vm(n_prob // 2, chunk, 2 * chunk),
                        vm(n_prob // 2, chunk, 2 * chunk),
                        vm(n_prob // 2, chunk, 2 * chunk),
                        vm(n_prob // 2, chunk, 2 * chunk),
                        vm(n_prob, chunk, 2 * HEAD_DIM),
                        vm(n_prob, chunk, HEAD_DIM),
                        vm(n_seq * N_HEADS, 2 * chunk, HEAD_DIM)]
        + ([vm(2, rows, main_cols), vm(2, rows, 2 * N_HEADS), pltpu.VMEM((rows, D_MODEL), BF16)] if fused else []),
        compiler_params=_params("arbitrary", "arbitrary"),
        name="gdn_recurrence",
    )(*args)


def _swiglu_rows(x_of, wg, wu, wd, n_sub, emit):
    pending = None
    for sb in range(n_sub + 1):
        if sb < n_sub:
            x = x_of(sb)
            nxt = (_dot(x, wg), _dot(x, wu))
        if pending is not None:
            a, u = pending
            emit(sb - 1, _dot((_silu(a) * u).astype(BF16), wd))
        pending = nxt


def _ffn_kernel(x_ref, og_ref, wo_ref, g_ref, wg_ref, wu_ref, wd_ref, o_ref, h_scr, *, n_sub):
    j = pl.program_id(1)
    sub = x_ref.shape[0] // n_sub

    @pl.when(j == 0)
    def _():
        x = x_ref[...] + _dot(og_ref[...].astype(BF16), wo_ref[...])
        h_scr[...] = _rms(x, g_ref[...]).astype(BF16)
        o_ref[...] = x

    def emit(sb, y):
        o_ref[sb * sub:(sb + 1) * sub, :] += y

    _swiglu_rows(lambda sb: h_scr[sb * sub:(sb + 1) * sub, :], wg_ref[...], wu_ref[...], wd_ref[...], n_sub, emit)


def _ffn(x, og, w_out, gain, wg, wu, wd, *, tm, tf, n_sub):
    n, d = x.shape
    f = wg.shape[1]
    rows = pl.BlockSpec((tm, d), lambda i, j: (i, 0))
    return pl.pallas_call(
        functools.partial(_ffn_kernel, n_sub=n_sub),
        grid=(n // tm, f // tf),
        in_specs=[rows, rows,
                  pl.BlockSpec((d, d), lambda i, j: (0, 0)),
                  pl.BlockSpec((1, d), lambda i, j: (0, 0)),
                  pl.BlockSpec((d, tf), lambda i, j: (0, j)),
                  pl.BlockSpec((d, tf), lambda i, j: (0, j)),
                  pl.BlockSpec((tf, d), lambda i, j: (j, 0))],
        out_specs=rows,
        out_shape=jax.ShapeDtypeStruct((n, d), F32),
        scratch_shapes=[pltpu.VMEM((tm, d), BF16)],
        compiler_params=_params("parallel", "arbitrary"),
        name="dense_ffn",
    )(x, og, w_out, gain.reshape(1, d), wg, wu, wd)


def _run_copies(n_rows, max_rows, make_copy):
    pieces = []
    off = jnp.int32(0)
    bit = max_rows
    while bit >= RUN_ALIGN:
        has = (n_rows & bit) != 0
        pieces.append((has, make_copy(off, bit)))
        off = off + (n_rows & bit)
        bit //= 2
    return pieces


def _router_kernel(x_ref, og_ref, wo_ref, g_ref, wr_ref, xo_ref, dest_ref, gate_ref, cnt_ref, xs_hbm,
                   cnt_scr, stage, zeros_buf, n_smem, base_smem, sem, *, tm, cap):
    i = pl.program_id(0)
    last = pl.num_programs(0) - 1

    @pl.when(i == 0)
    def _():
        cnt_scr[...] = jnp.zeros_like(cnt_scr)
        zeros_buf[...] = jnp.zeros_like(zeros_buf)
        for e in range(N_EXPERTS):
            n_smem[e] = 0
            base_smem[e] = e * cap

    x = x_ref[...] + _dot(og_ref[...].astype(BF16), wo_ref[...])
    xo_ref[...] = x
    h = _rms(x, g_ref[...])
    h16 = h.astype(BF16)
    h_lo = (h - h16.astype(F32)).astype(BF16)
    w_hi, w_lo = _split_bf16(wr_ref[...])
    logits = _dot(h16, w_hi) + (_dot(h16, w_lo) + _dot(h_lo, w_hi))
    lane = lax.broadcasted_iota(jnp.int32, logits.shape, 1)
    m1 = jnp.max(logits, axis=-1, keepdims=True)
    i1 = jnp.min(jnp.where(logits == m1, lane, N_EXPERTS), axis=-1, keepdims=True)
    rest = jnp.where(lane == i1, -jnp.inf, logits)
    m2 = jnp.max(rest, axis=-1, keepdims=True)
    i2 = jnp.min(jnp.where(rest == m2, lane, N_EXPERTS), axis=-1, keepdims=True)
    e2 = jnp.exp(m2 - m1)
    den = 1.0 + e2
    two = lax.broadcasted_iota(jnp.int32, (tm, TOP_K), 1)
    gate_ref[...] = jnp.where(two == 0, 1.0 / den, e2 / den)

    onehot = ((lane == i1) | (lane == i2)).astype(F32)
    row = lax.broadcasted_iota(jnp.int32, (tm, tm), 0)
    col = lax.broadcasted_iota(jnp.int32, (tm, tm), 1)
    local = _dot((row > col).astype(BF16), onehot.astype(BF16))
    rank = local + cnt_scr[...]
    r1 = jnp.sum(jnp.where(lane == i1, rank, 0.0), axis=-1, keepdims=True).astype(jnp.int32)
    r2 = jnp.sum(jnp.where(lane == i2, rank, 0.0), axis=-1, keepdims=True).astype(jnp.int32)
    dest_ref[...] = jnp.where(two == 0, i1 * cap + r1, i2 * cap + r2)
    cnt_tile = jnp.sum(onehot, axis=0, keepdims=True)
    cnt = cnt_scr[...] + cnt_tile
    cnt_scr[...] = cnt
    cnt_ref[...] = cnt.astype(jnp.int32)

    def wait_runs():
        for e in range(N_EXPERTS):
            for has, copy in _run_copies(n_smem[e], tm, lambda off, size, e=e: pltpu.make_async_copy(
                    stage.at[e, pl.ds(0, size * SUBLANES)], xs_hbm.at[pl.ds(0, size * SUBLANES)], sem)):
                @pl.when(has)
                def _():
                    copy.wait()

    wait_runs()

    code = (onehot * (local + 1.0)).astype(BF16)
    code_t = _dot_tn(code, (row == col).astype(BF16))
    slot_id = lax.broadcasted_iota(jnp.int32, (ROUTER_SLOTS, tm), 0)
    code_i = code_t.astype(jnp.int32)

    def compact(e, q):
        sel = jnp.where(code_i[e:e + 1, :] == slot_id + (q * ROUTER_SLOTS + 1), 1.0, 0.0).astype(BF16)
        _store_token_tiles(stage.at[e], q * ROUTER_SLOTS, _dot(sel, h16))

    for e in range(N_EXPERTS):
        compact(e, 0)
    for e in range(N_EXPERTS):
        n_e = cnt_tile[0, e].astype(jnp.int32)
        for q in range(1, tm // ROUTER_SLOTS):
            @pl.when(q * ROUTER_SLOTS < n_e)
            def _():
                compact(e, q)
        n_copy = jnp.bitwise_and(n_e + (RUN_ALIGN - 1), -RUN_ALIGN)
        base = base_smem[e]
        for has, copy in _run_copies(n_copy, tm, lambda off, size, e=e, base=base: pltpu.make_async_copy(
                stage.at[e, pl.ds(pl.multiple_of(off * SUBLANES, SUBLANES), size * SUBLANES)],
                xs_hbm.at[pl.ds(pl.multiple_of((base + off) * SUBLANES, SUBLANES), size * SUBLANES)], sem)):
            @pl.when(has)
            def _():
                copy.start()
        n_smem[e] = n_copy
        base_smem[e] = base + n_e

    @pl.when(i == last)
    def _():
        wait_runs()
        pads = [pltpu.make_async_copy(
            zeros_buf, xs_hbm.at[pl.ds(pl.multiple_of(base_smem[e] * SUBLANES, SUBLANES), PAD_ROWS * SUBLANES)], sem)
            for e in range(N_EXPERTS)]
        for pad in pads:
            pad.start()
        for pad in pads:
            pad.wait()


def _router(x, og, w_out, gain, w_router, *, tm, cap):
    n, d = x.shape
    two = pl.BlockSpec((tm, TOP_K), lambda i: (i, 0))
    rows = pl.BlockSpec((tm, d), lambda i: (i, 0))
    return pl.pallas_call(
        functools.partial(_router_kernel, tm=tm, cap=cap),
        grid=(n // tm,),
        in_specs=[rows, rows,
                  pl.BlockSpec((d, d), lambda i: (0, 0)),
                  pl.BlockSpec((1, d), lambda i: (0, 0)),
                  pl.BlockSpec((d, N_EXPERTS), lambda i: (0, 0))],
        out_specs=[rows, two, two, pl.BlockSpec((1, N_EXPERTS), lambda i: (0, 0)),
                   pl.BlockSpec(memory_space=pl.ANY)],
        out_shape=[jax.ShapeDtypeStruct((n, d), F32),
                   jax.ShapeDtypeStruct((n, TOP_K), jnp.int32),
                   jax.ShapeDtypeStruct((n, TOP_K), F32),
                   jax.ShapeDtypeStruct((1, N_EXPERTS), jnp.int32),
                   jax.ShapeDtypeStruct((N_EXPERTS * cap * SUBLANES, LANES), F32)],
        scratch_shapes=[pltpu.VMEM((1, N_EXPERTS), F32),
                        pltpu.VMEM((N_EXPERTS, tm * SUBLANES, LANES), F32),
                        pltpu.VMEM((PAD_ROWS * SUBLANES, LANES), F32),
                        pltpu.SMEM((N_EXPERTS,), jnp.int32), pltpu.SMEM((N_EXPERTS,), jnp.int32),
                        pltpu.SemaphoreType.DMA(())],
        compiler_params=_params("arbitrary"),
        name="moe_router",
    )(x, og, w_out, gain.reshape(1, d), w_router)


def _tile_gather_copy(src_hbm, dst, sem, src_row, dst_row):
    return pltpu.make_async_copy(src_hbm.at[pl.ds(pl.multiple_of(src_row * SUBLANES, SUBLANES), SUBLANES)],
                                 dst.at[pl.ds(pl.multiple_of(dst_row * SUBLANES, SUBLANES), SUBLANES)], sem)


def _load_token_tiles(ref, first_row, n_rows):
    return jnp.concatenate([ref[pl.ds(first_row * SUBLANES + s, n_rows, stride=SUBLANES), :]
                            for s in range(D_MODEL // LANES)], axis=1)


def _store_token_tiles(ref, first_row, x):
    for s in range(D_MODEL // LANES):
        ref[pl.ds(first_row * SUBLANES + s, x.shape[0], stride=SUBLANES), :] = x[:, s * LANES:(s + 1) * LANES]


def _expert_kernel(be_ref, xblk_ref, oblk_ref, nv_ref, nu_ref, xs_ref, wg_ref, wu_ref, wd_ref, yb_ref, xb, acc,
                   *, sub_rows, n_sub):
    i = pl.program_id(0)
    j = pl.program_id(1)
    n_valid = nv_ref[i]

    @pl.when(j == 0)
    def _():
        acc[...] = jnp.zeros_like(acc)
        for sb in range(n_sub):
            @pl.when(sb * sub_rows < n_valid)
            def _():
                xb[sb * sub_rows:(sb + 1) * sub_rows, :] = _load_token_tiles(
                    xs_ref, sb * sub_rows, sub_rows).astype(BF16)

    def emit(sb, y):
        acc[sb * sub_rows:(sb + 1) * sub_rows, :] += y

    def x_of(sb):
        return xb[sb * sub_rows:(sb + 1) * sub_rows, :]

    full = n_valid > (n_sub - 1) * sub_rows

    @pl.when(full)
    def _():
        _swiglu_rows(x_of, wg_ref[0], wu_ref[0], wd_ref[0], n_sub, emit)

    for sb in range(n_sub - 1):
        @pl.when(jnp.logical_and(jnp.logical_not(full), sb * sub_rows < n_valid))
        def _():
            _swiglu_rows(lambda _, sb=sb: x_of(sb), wg_ref[0], wu_ref[0], wd_ref[0], 1, lambda _, y, sb=sb: emit(sb, y))

    @pl.when(j == pl.num_programs(1) - 1)
    def _():
        for sb in range(n_sub):
            rows = slice(sb * sub_rows, (sb + 1) * sub_rows)

            @pl.when(sb * sub_rows < n_valid)
            def _():
                _store_token_tiles(yb_ref, sb * sub_rows, acc[rows, :])

            @pl.when(sb * sub_rows >= n_valid)
            def _():
                tiles = slice(sb * sub_rows * SUBLANES, (sb + 1) * sub_rows * SUBLANES)
                yb_ref[tiles, :] = jnp.zeros((sub_rows * SUBLANES, LANES), F32)


def _experts(xs, block_e, x_block, out_block, n_valid, n_used, wg, wu, wd, *, sub_rows, n_sub, tf, out_blocks):
    d = D_MODEL
    tmm = sub_rows * n_sub
    nb = block_e.shape[0]
    nf = D_FF // tf

    def wcol(i, j, be, xblk, oblk, nv, nu):
        return (be[i], 0, jnp.where(i < nu[0], j, nf - 1))

    def wrow(i, j, be, xblk, oblk, nv, nu):
        return (be[i], jnp.where(i < nu[0], j, nf - 1), 0)

    grid_spec = pltpu.PrefetchScalarGridSpec(
        num_scalar_prefetch=5,
        grid=(nb, nf),
        in_specs=[pl.BlockSpec((tmm * SUBLANES, LANES), lambda i, j, be, xblk, oblk, nv, nu: (xblk[i], 0)),
                  pl.BlockSpec((1, d, tf), wcol),
                  pl.BlockSpec((1, d, tf), wcol),
                  pl.BlockSpec((1, tf, d), wrow)],
        out_specs=pl.BlockSpec((tmm * SUBLANES, LANES), lambda i, j, be, xblk, oblk, nv, nu: (oblk[i], 0)),
        scratch_shapes=[pltpu.VMEM((tmm, d), BF16), pltpu.VMEM((tmm, d), F32)],
    )
    return pl.pallas_call(
        functools.partial(_expert_kernel, sub_rows=sub_rows, n_sub=n_sub),
        grid_spec=grid_spec,
        out_shape=jax.ShapeDtypeStruct((out_blocks * tmm * SUBLANES, LANES), F32),
        compiler_params=_params("arbitrary", "arbitrary"),
        name="moe_experts",
    )(block_e, x_block, out_block, n_valid, n_used, xs, wg, wu, wd)


def _combine_kernel(dest_ref, yb_hbm, x_ref, gate_ref, gain_ref, o_ref, r0, r1, sems, *, tm):
    i = pl.program_id(0)
    slot = lax.rem(i, 2)

    def start_gather(tile, dst_slot):
        def issue(r, carry):
            first = TOP_K * (tile * tm + r)
            _tile_gather_copy(yb_hbm, r0.at[dst_slot], sems.at[dst_slot, 0], dest_ref[first], r).start(priority=0)
            _tile_gather_copy(yb_hbm, r1.at[dst_slot], sems.at[dst_slot, 1], dest_ref[first + 1], r).start(priority=1)
            return carry

        lax.fori_loop(0, tm, issue, 0, unroll=8)

    @pl.when(i == 0)
    def _():
        start_gather(0, 0)

    @pl.when(i + 1 < pl.num_programs(0))
    def _():
        start_gather(i + 1, 1 - slot)

    whole = pl.ds(0, tm * SUBLANES)
    pltpu.make_async_copy(yb_hbm.at[whole], r0.at[slot], sems.at[slot, 0]).wait()
    pltpu.make_async_copy(yb_hbm.at[whole], r1.at[slot], sems.at[slot, 1]).wait()
    g = gate_ref[...]
    y = _load_token_tiles(r0.at[slot], 0, tm) * g[:, 0:1] + _load_token_tiles(r1.at[slot], 0, tm) * g[:, 1:2]
    o_ref[...] = _rms(x_ref[...] + y, gain_ref[...])


def _combine(yb, dest_flat, x, gates, gain, *, tm):
    n, d = x.shape
    grid_spec = pltpu.PrefetchScalarGridSpec(
        num_scalar_prefetch=1,
        grid=(n // tm,),
        in_specs=[pl.BlockSpec(memory_space=pl.ANY),
                  pl.BlockSpec((tm, d), lambda i, dest: (i, 0)),
                  pl.BlockSpec((tm, TOP_K), lambda i, dest: (i, 0)),
                  pl.BlockSpec((1, d), lambda i, dest: (0, 0))],
        out_specs=pl.BlockSpec((tm, d), lambda i, dest: (i, 0)),
        scratch_shapes=[pltpu.VMEM((2, tm * SUBLANES, LANES), F32), pltpu.VMEM((2, tm * SUBLANES, LANES), F32),
                        pltpu.SemaphoreType.DMA((2, TOP_K))],
    )
    return pl.pallas_call(
        functools.partial(_combine_kernel, tm=tm),
        grid_spec=grid_spec,
        out_shape=jax.ShapeDtypeStruct((n, d), F32),
        compiler_params=_params("arbitrary"),
        name="moe_combine",
    )(dest_flat, yb, x, gates, gain.reshape(1, d))


def _moe_and_final_norm(x, og, w_out, gain, w_router, wg, wu, wd, final_gain, *, tiles):
    n = x.shape[0]
    sub_rows, n_sub = tiles["moe_sub_rows"], tiles["moe_subs"]
    tmm = sub_rows * n_sub
    assert sub_rows == PAD_ROWS and n % tmm == 0
    cap = n + tmm
    x, dest, gates, counts, xs = _router(x, og, w_out, gain, w_router, tm=tiles["router_rows"], cap=cap)
    counts = counts[0]
    blocks_per_e = (counts + tmm - 1) // tmm
    blocks_end = jnp.cumsum(blocks_per_e)
    n_used = blocks_end[-1].astype(jnp.int32)
    nb = -(-(n * TOP_K + N_EXPERTS * (tmm - 1)) // tmm)
    blk = jnp.minimum(jnp.arange(nb, dtype=jnp.int32), n_used - 1)
    block_e = jnp.minimum(jnp.sum(blk[:, None] >= blocks_end[None, :], axis=1), N_EXPERTS - 1).astype(jnp.int32)
    k_in_e = blk - (blocks_end - blocks_per_e)[block_e]
    x_block = (block_e * (cap // tmm) + k_in_e).astype(jnp.int32)
    used = jnp.arange(nb, dtype=jnp.int32) < n_used
    n_valid = jnp.where(used, jnp.clip(counts[block_e] - k_in_e * tmm, 0, tmm), 0).astype(jnp.int32)
    dump_block = N_EXPERTS * (cap // tmm)
    out_block = jnp.where(used, x_block, dump_block).astype(jnp.int32)
    yb = _experts(xs, block_e, x_block, out_block, n_valid, n_used.reshape(1), wg, wu, wd,
                  sub_rows=sub_rows, n_sub=n_sub, tf=tiles["moe_ff_cols"], out_blocks=dump_block + 1)
    return _combine(yb, dest.reshape(-1), x, gates, final_gain, tm=tiles["combine_rows"])


def _tiles(n_rows, seq):
    big = n_rows >= 8192
    return {
        "proj_rows": 1024 if n_rows % 1024 == 0 else n_rows,
        "proj_cols": 2048,
        "rec_chunks": min(4, seq // min(MAX_CHUNK, seq)),
        "rec_seqs": 1 if seq > MAX_CHUNK else 4,
        "proj_in_recurrence": seq > MAX_CHUNK,
        "ffn_rows": 1024 if n_rows % 1024 == 0 else n_rows,
        "ff_cols": 512,
        "router_rows": 256,
        "moe_sub_rows": 256,
        "moe_subs": 4 if big else 2,
        "moe_ff_cols": 1792,
        "ffn_subs": 4 if big else 2,
        "combine_rows": 256,
    }


def _trunk(x3, hg_state, gd_state, gd_conv, p):
    batch, seq, d = x3.shape
    n = batch * seq
    x = x3.reshape(n, d)
    tiles = _tiles(n, seq)
    tm, tn = tiles["proj_rows"], tiles["proj_cols"]

    rec = dict(batch=batch, seq=seq, n_chunks=tiles["rec_chunks"], n_seq=tiles["rec_seqs"])

    if tiles["proj_in_recurrence"]:
        proj, fuse = None, (x, p["norm_mix"][0], p["hgrn_w_in"])
    else:
        proj, fuse = _norm_proj(x, p["norm_mix"][0], p["hgrn_w_in"], tm=tm, tn=tn)[0], None
    og, hg_new = _hgrn_recurrence(proj, p["hgrn_lb"], p["hgrn_norm"], hg_state, layer=0, fuse=fuse, **rec)
    x = _ffn(x, og, p["hgrn_w_out"], p["norm_ffn"][0], p["ffn_w_gate"], p["ffn_w_up"], p["ffn_w_down"],
             tm=tiles["ffn_rows"], tf=tiles["ff_cols"], n_sub=tiles["ffn_subs"])

    if tiles["proj_in_recurrence"]:
        main, ab, fuse = None, None, (x, p["norm_mix"][1], p["gdn_w_main"], p["gdn_w_ab"])
    else:
        (main, ab), fuse = _norm_proj(x, p["norm_mix"][1], p["gdn_w_main"], p["gdn_w_ab"], tm=tm, tn=tn), None
    og, gd_new, cv_new = _gdn_recurrence(main, ab, p["gdn_conv"], p["gdn_a_log"], p["gdn_dt_bias"], p["gdn_norm"],
                                         gd_conv, gd_state, fuse=fuse, **rec)
    y = _moe_and_final_norm(x, og, p["gdn_w_out"], p["norm_ffn"][1], p["moe_router"], p["moe_w_gate"], p["moe_w_up"], p["moe_w_down"],
                            p["norm_out"], tiles=tiles)
    return y.reshape(batch, seq, d), hg_new[None], gd_new[None], cv_new[None]


def kernel(x_prompt, x_sample, state_hgrn, state_gdn, state_gdn_conv, norm_mix, norm_ffn, norm_out, hgrn_w_in, hgrn_lb, hgrn_norm, hgrn_w_out, gdn_w_in, gdn_conv, gdn_a_log, gdn_dt_bias, gdn_norm, gdn_w_out, ffn_w_gate, ffn_w_up, ffn_w_down, moe_router, moe_w_gate, moe_w_up, moe_w_down):
    assert hgrn_w_in.shape[0] == 1 and gdn_w_in.shape[0] == 1, "one HGRN2 layer and one gated-DeltaNet layer"
    main_cols = GD_QKV + N_HEADS * HEAD_DIM
    p = {
        "norm_mix": norm_mix, "norm_ffn": norm_ffn, "norm_out": norm_out,
        "hgrn_w_in": hgrn_w_in[0].astype(BF16), "hgrn_lb": hgrn_lb, "hgrn_norm": hgrn_norm[0],
        "hgrn_w_out": hgrn_w_out[0].astype(BF16),
        "gdn_w_main": gdn_w_in[0, :, :main_cols].astype(BF16), "gdn_w_ab": gdn_w_in[0, :, main_cols:].astype(BF16),
        "gdn_conv": gdn_conv[0], "gdn_a_log": gdn_a_log[0], "gdn_dt_bias": gdn_dt_bias[0], "gdn_norm": gdn_norm[0],
        "gdn_w_out": gdn_w_out[0].astype(BF16),
        "ffn_w_gate": ffn_w_gate[0].astype(BF16), "ffn_w_up": ffn_w_up[0].astype(BF16),
        "ffn_w_down": ffn_w_down[0].astype(BF16),
        "moe_router": moe_router[0],
        "moe_w_gate": moe_w_gate[0].astype(BF16), "moe_w_up": moe_w_up[0].astype(BF16),
        "moe_w_down": moe_w_down[0].astype(BF16),
    }
    y_p, hg_p, gd_p, cv_p = _trunk(x_prompt, None, None, None, p)
    y_s, hg_s, gd_s, cv_s = _trunk(x_sample, state_hgrn[0], state_gdn[0], state_gdn_conv[0], p)
    return (y_p, y_s, hg_p, hg_s, gd_p, gd_s, cv_p, cv_s)
```

```python
import functools

import jax
import jax.numpy as jnp
from jax import lax
from jax.experimental import pallas as pl
from jax.experimental.pallas import tpu as pltpu

F32 = jnp.float32
BF16 = jnp.bfloat16

D_MODEL = 1024
N_HEADS = 8
HEAD_DIM = 128
D_FF = 3584
N_EXPERTS = 8
TOP_K = 2
GD_QKV = 3 * D_MODEL
GD_CONV = 4
MAX_CHUNK = 64
EPS = 1e-6
LANES = 128
SUBLANES = 8
VMEM_LIMIT_BYTES = 52 * 1024 * 1024
PROJ_PIECES = 8
ROUTER_SLOTS = 128
PAD_ROWS = 256
RUN_ALIGN = 32


def _params(*sem):
    return pltpu.CompilerParams(dimension_semantics=sem, vmem_limit_bytes=VMEM_LIMIT_BYTES)


def _rms(x, gain):
    return x * lax.rsqrt(jnp.mean(x * x, axis=-1, keepdims=True) + EPS) * gain


def _sigmoid(x):
    t = jnp.exp(-jnp.abs(x))
    r = 1.0 / (1.0 + t)
    return jnp.where(x >= 0, r, t * r)


def _silu(x):
    return x * (0.5 * jnp.tanh(0.5 * x) + 0.5)


def _softplus(x):
    return jnp.maximum(x, 0.0) + jnp.log1p(jnp.exp(-jnp.abs(x)))


def _dot(a, b):
    return jnp.dot(a, b, preferred_element_type=F32)


def _dot_nt(a, b):
    return lax.dot_general(a, b, (((1,), (1,)), ((), ())), preferred_element_type=F32)


def _dot_tn(a, b):
    return lax.dot_general(a, b, (((0,), (0,)), ((), ())), preferred_element_type=F32)


def _norm_proj_kernel(x_ref, g_ref, w_ref, *rest, has_small):
    if has_small:
        ws_ref, o_ref, os_ref, h_scr = rest
    else:
        o_ref, h_scr = rest

    @pl.when(pl.program_id(1) == 0)
    def _():
        hb = _rms(x_ref[...], g_ref[...]).astype(BF16)
        h_scr[...] = hb
        if has_small:
            os_ref[...] = _dot(hb, ws_ref[...])

    o_ref[...] = _dot(h_scr[...], w_ref[...])


def _norm_proj(x, gain, w, w_small=None, *, tm, tn):
    n, d = x.shape
    e = w.shape[1]
    has_small = w_small is not None
    in_specs = [pl.BlockSpec((tm, d), lambda i, j: (i, 0)),
                pl.BlockSpec((1, d), lambda i, j: (0, 0)),
                pl.BlockSpec((d, tn), lambda i, j: (0, j))]
    out_shape = [jax.ShapeDtypeStruct((n, e), F32)]
    out_specs = [pl.BlockSpec((tm, tn), lambda i, j: (i, j))]
    args = [x, gain.reshape(1, d), w]
    if has_small:
        es = w_small.shape[1]
        in_specs.append(pl.BlockSpec((d, es), lambda i, j: (0, 0)))
        out_shape.append(jax.ShapeDtypeStruct((n, es), F32))
        out_specs.append(pl.BlockSpec((tm, es), lambda i, j: (i, 0)))
        args.append(w_small)
    return pl.pallas_call(
        functools.partial(_norm_proj_kernel, has_small=has_small),
        grid=(n // tm, e // tn),
        in_specs=in_specs, out_specs=out_specs, out_shape=out_shape,
        scratch_shapes=[pltpu.VMEM((tm, d), BF16)],
        compiler_params=_params("parallel", "arbitrary"),
        name="norm_proj",
    )(*args)


class _Projection:
    def __init__(self, x_ref, gain_ref, w_refs, dst_scrs, h_scr, dst_slot):
        self.x_ref, self.gain_ref, self.w_refs, self.dst_scrs = x_ref, gain_ref, w_refs, dst_scrs
        self.h_scr, self.dst_slot, self.done = h_scr, dst_slot, 0

    def upto(self, k):
        for piece in range(self.done, min(k, PROJ_PIECES)):
            if piece == 0:
                self.h_scr[...] = _rms(self.x_ref[...], self.gain_ref[...]).astype(BF16)
            h16 = self.h_scr[...]
            for w_ref, dst in zip(self.w_refs, self.dst_scrs):
                width = w_ref.shape[1]
                if width % (PROJ_PIECES * LANES) == 0:
                    cols = slice(piece * width // PROJ_PIECES, (piece + 1) * width // PROJ_PIECES)
                    dst[self.dst_slot, :, cols] = _dot(h16, w_ref[:, cols])
                elif piece == 0:
                    dst[self.dst_slot] = _dot(h16, w_ref[...])
        self.done = max(self.done, k)


class _NoProjection:
    def upto(self, k):
        pass


def _with_projection_ahead(body, refs, n_weights, **kw):
    x_cur_ref, x_next_ref, gain_ref = refs[:3]
    w_refs = refs[3:3 + n_weights]
    dst_scrs = refs[len(refs) - n_weights - 1:len(refs) - 1]
    h_scr = refs[-1]
    g = pl.program_id(0) * pl.num_programs(1) + pl.program_id(1)

    @pl.when(g == 0)
    def _():
        _Projection(x_cur_ref, gain_ref, w_refs, dst_scrs, h_scr, 0).upto(PROJ_PIECES)

    for cur in range(2):
        @pl.when(lax.rem(g, 2) == cur)
        def _():
            ahead = _Projection(x_next_ref, gain_ref, w_refs, dst_scrs, h_scr, 1 - cur)
            body(*[scr.at[cur] for scr in dst_scrs], *refs[3 + n_weights:len(refs) - n_weights - 1], ahead=ahead,
                 **kw)
            ahead.upto(PROJ_PIECES)


def _hgrn_kernel(*refs, fused, **kw):
    if fused:
        _with_projection_ahead(_hgrn_step, refs, 1, **kw)
    else:
        _hgrn_step(*refs, **kw)


def _hgrn_step(proj_ref, *refs, chunk, n_chunks, n_seq, layer, has_s0, ahead=None):
    if has_s0:
        lb_ref, on_ref, s0_ref, og_ref, sf_ref, *scr = refs
    else:
        lb_ref, on_ref, og_ref, sf_ref, *scr = refs
    st_scr, b_scr, q_scr, k_scr, a_scr, qin_scr, kout_scr, dec_scr, oi_scr = scr[:9]
    C = chunk
    W = N_HEADS * HEAD_DIM
    tb = C * n_chunks
    t = pl.program_id(1)
    problems = [(s, c, h) for s in range(n_seq) for c in range(n_chunks) for h in range(N_HEADS)]

    def pid(s, c, h):
        return (s * n_chunks + c) * N_HEADS + h

    def row0(s, c):
        return s * tb + c * C

    @pl.when(t == 0)
    def _():
        for s in range(n_seq):
            for h in range(N_HEADS):
                st_scr[s * N_HEADS + h] = s0_ref[s, h].T if has_s0 else jnp.zeros((HEAD_DIM, HEAD_DIM), F32)

    ahead = ahead or _NoProjection()
    ahead.upto(1)

    lbraw = lb_ref[...]
    ex = jnp.exp(lbraw - jnp.max(lbraw, axis=0, keepdims=True))
    sm = ex / jnp.sum(ex, axis=0, keepdims=True)
    lb = sm[0:1]
    for j in range(1, layer + 1):
        lb = lb + sm[j:j + 1]

    row = lax.broadcasted_iota(jnp.int32, (C, C), 0)
    col = lax.broadcasted_iota(jnp.int32, (C, C), 1)
    causal = row >= col
    tril = causal.astype(BF16)
    mid = C // 2
    scale = HEAD_DIM ** -0.5
    onorm = on_ref[...]

    for s in range(n_seq):
        for c in range(n_chunks):
            rows = slice(row0(s, c), row0(s, c) + C)
            f_raw = proj_ref[rows, W:2 * W]
            tt = jnp.exp(-jnp.abs(f_raw))
            rr = 1.0 / (1.0 + tt)
            pos = f_raw >= 0
            sig = jnp.where(pos, rr, tt * rr)
            sig_neg = jnp.where(pos, tt * rr, rr)
            log_f = jnp.log(lb + (1.0 - lb) * sig)
            k_scr[rows, :] = (1.0 - lb) * sig_neg
            q_scr[rows, :] = _silu(proj_ref[rows, 0:W]) * scale
            hi = log_f.astype(BF16)
            rest = log_f - hi.astype(F32)
            md = rest.astype(BF16)
            lw = (rest - md.astype(F32)).astype(BF16)
            b_scr[rows, :] = (_dot(tril, lw) + _dot(tril, md)) + _dot(tril, hi)

    ahead.upto(3)
    for s, c, h in problems:
        p, lo = pid(s, c, h), h * HEAD_DIM
        rows = slice(row0(s, c), row0(s, c) + C)
        b = b_scr[rows, lo:lo + HEAD_DIM]
        q = q_scr[rows, lo:lo + HEAD_DIM]
        k = k_scr[rows, lo:lo + HEAD_DIM]
        b_mid = b[mid:mid + 1]
        b_last = b[C - 1:C]
        a_scr[p] = _dot_nt((q * jnp.exp(b - b_mid)).astype(BF16), (k * jnp.exp(b_mid - b)).astype(BF16))
        qin_scr[p] = q * jnp.exp(b)
        kout_scr[p] = k * jnp.exp(b_last - b)
        dec_scr[p] = jnp.exp(b_last)

    ahead.upto(5)
    for s, c, h in problems:
        p, lo = pid(s, c, h), h * HEAD_DIM
        rows = slice(row0(s, c), row0(s, c) + C)
        a = jnp.where(causal, a_scr[p], 0.0)
        oi_scr[p] = _dot(a.astype(BF16), proj_ref[rows, 2 * W + lo:2 * W + lo + HEAD_DIM].astype(BF16))

    for c in range(n_chunks):
        ahead.upto(6 + c)
        for s in range(n_seq):
            for h in range(N_HEADS):
                p, lo, sh = pid(s, c, h), h * HEAD_DIM, s * N_HEADS + h
                rows = slice(row0(s, c), row0(s, c) + C)
                st = st_scr[sh]
                o = oi_scr[p] + _dot_nt(qin_scr[p].astype(BF16), st.astype(BF16))
                vb = proj_ref[rows, 2 * W + lo:2 * W + lo + HEAD_DIM].astype(BF16)
                st_scr[sh] = st * dec_scr[p] + _dot_tn(vb, kout_scr[p].astype(BF16))
                gate = proj_ref[rows, 3 * W + lo:3 * W + lo + HEAD_DIM]
                og_ref[rows, lo:lo + HEAD_DIM] = _rms(o, onorm) * _silu(gate)

    @pl.when(t == pl.num_programs(1) - 1)
    def _():
        for s in range(n_seq):
            for h in range(N_HEADS):
                sf_ref[s, h] = st_scr[s * N_HEADS + h].T


def _ahead_specs(x, gain, weights, rows, n_steps, nt):
    d = x.shape[1]
    specs = [pl.BlockSpec((rows, d), lambda b, t: (b * nt + t, 0)),
             pl.BlockSpec((rows, d), lambda b, t: (jnp.minimum(b * nt + t + 1, n_steps - 1), 0)),
             pl.BlockSpec((1, d), lambda b, t: (0, 0))]
    specs += [pl.BlockSpec(w.shape, lambda b, t: (0, 0)) for w in weights]
    return specs, [x, x, gain.reshape(1, d)] + list(weights)


def _hgrn_recurrence(proj, lb_raw, out_norm, s0, *, batch, seq, layer, n_chunks, n_seq, fuse=None):
    n = batch * seq
    chunk = min(MAX_CHUNK, seq)
    tb = chunk * n_chunks
    assert seq % tb == 0 and batch % n_seq == 0
    assert n_seq == 1 or tb == seq, "several sequences per step only when a step covers whole sequences"
    nt = seq // tb
    rows = n_seq * tb
    n_prob = n_seq * n_chunks * N_HEADS
    width = N_HEADS * HEAD_DIM
    has_s0 = s0 is not None
    fused = fuse is not None
    state_spec = pl.BlockSpec((n_seq, N_HEADS, HEAD_DIM, HEAD_DIM), lambda b, t: (b, 0, 0, 0))
    if fused:
        x, gain, w = fuse
        in_specs, args = _ahead_specs(x, gain, [w], rows, (batch // n_seq) * nt, nt)
    else:
        in_specs, args = [pl.BlockSpec((rows, proj.shape[1]), lambda b, t: (b * nt + t, 0))], [proj]
    in_specs += [pl.BlockSpec(lb_raw.shape, lambda b, t: (0, 0)),
                 pl.BlockSpec((1, HEAD_DIM), lambda b, t: (0, 0))]
    args += [lb_raw, out_norm.reshape(1, HEAD_DIM)]
    if has_s0:
        in_specs.append(state_spec)
        args.append(s0)
    vm = lambda *shape: pltpu.VMEM(shape, F32)
    ahead_scratch = [vm(2, rows, 4 * width), pltpu.VMEM((rows, D_MODEL), BF16)] if fused else []
    return pl.pallas_call(
        functools.partial(_hgrn_kernel, chunk=chunk, n_chunks=n_chunks, n_seq=n_seq, layer=layer, has_s0=has_s0,
                          fused=fused),
        grid=(batch // n_seq, nt),
        in_specs=in_specs,
        out_specs=[pl.BlockSpec((rows, D_MODEL), lambda b, t: (b * nt + t, 0)), state_spec],
        out_shape=[jax.ShapeDtypeStruct((n, D_MODEL), F32),
                   jax.ShapeDtypeStruct((batch, N_HEADS, HEAD_DIM, HEAD_DIM), F32)],
        scratch_shapes=[vm(n_seq * N_HEADS, HEAD_DIM, HEAD_DIM),
                        vm(rows, width),
                        vm(rows, width),
                        vm(rows, width),
                        vm(n_prob, chunk, chunk),
                        vm(n_prob, chunk, HEAD_DIM),
                        vm(n_prob, chunk, HEAD_DIM),
                        vm(n_prob, 1, HEAD_DIM),
                        vm(n_prob, chunk, HEAD_DIM)] + ahead_scratch,
        compiler_params=_params("arbitrary", "arbitrary"),
        name="hgrn_recurrence",
    )(*args)


def _split_bf16(x):
    hi = x.astype(BF16)
    return hi, (x - hi.astype(F32)).astype(BF16)


def _gdn_kernel(*refs, fused, **kw):
    if fused:
        _with_projection_ahead(_gdn_step, refs, 2, **kw)
    else:
        _gdn_step(*refs, **kw)


def _gdn_step(main_ref, ab_ref, *refs, chunk, n_chunks, n_seq, has_s0, ahead=None):
    (cw_ref, al_ref, dtb_ref, on_ref), rest = refs[:4], refs[4:]
    if has_s0:
        (cb0_ref, s0_ref), rest = rest[:2], rest[2:]
    (og_ref, sf_ref, cb_ref, s_scr, xpad, act, gcc_scr, gcr_scr, beta_scr, gram_scr, rhs_scr, wq_scr, kout_scr,
     inv_scr, pow_scr, low_scr, attn_scr, x0_scr, u_scr, ws_scr) = rest[:20]
    C = chunk
    W = N_HEADS * HEAD_DIM
    tb = C * n_chunks
    t = pl.program_id(1)
    pad = SUBLANES
    pairs = [(s, c, hp) for s in range(n_seq) for c in range(n_chunks) for hp in range(N_HEADS // 2)]

    def pid(s, c, h):
        return (s * n_chunks + c) * N_HEADS + h

    def ppid(s, c, hp):
        return (s * n_chunks + c) * (N_HEADS // 2) + hp

    def row0(s, c):
        return s * tb + c * C

    @pl.when(t == 0)
    def _():
        for s in range(n_seq):
            for h in range(N_HEADS):
                s_scr[s * N_HEADS + h] = s0_ref[s, h] if has_s0 else jnp.zeros((HEAD_DIM, HEAD_DIM), F32)
            xpad[s, 0:pad, :] = jnp.zeros((pad, GD_QKV), F32)
            if has_s0:
                xpad[s, pad - (GD_CONV - 1):pad, :] = cb0_ref[s]

    @pl.when(t > 0)
    def _():
        for s in range(n_seq):
            xpad[s, 0:pad, :] = xpad[s, tb:tb + pad, :]

    ahead = ahead or _NoProjection()
    ahead.upto(6)

    for s in range(n_seq):
        for j in range(GD_QKV // LANES):
            cs = slice(j * LANES, (j + 1) * LANES)
            xpad[s, pad:pad + tb, cs] = main_ref[s * tb:(s + 1) * tb, cs]
    for s in range(n_seq):
        for j in range(GD_QKV // LANES):
            cs = slice(j * LANES, (j + 1) * LANES)
            conv = xpad[s, pad - 3:pad - 3 + tb, cs] * cw_ref[0:1, cs]
            for tap in range(1, GD_CONV):
                conv = conv + xpad[s, pad - 3 + tap:pad - 3 + tap + tb, cs] * cw_ref[tap:tap + 1, cs]
            act[s * tb:(s + 1) * tb, cs] = _silu(conv)

    C2 = 2 * C
    prow = lax.broadcasted_iota(jnp.int32, (C, C2), 0)
    pcol = lax.broadcasted_iota(jnp.int32, (C, C2), 1)
    half_b = pcol >= C
    pcol_in = jnp.where(half_b, pcol - C, pcol)
    incl = prow >= pcol_in
    strict = prow > pcol_in
    eye = jnp.where(prow == pcol_in, 1.0, 0.0)
    row = lax.broadcasted_iota(jnp.int32, (C, C), 0)
    col = lax.broadcasted_iota(jnp.int32, (C, C), 1)
    tril = (row >= col).astype(BF16)
    triu = (row <= col).astype(BF16)

    def first_head(m):
        return jnp.where(half_b, 0.0, m)

    def second_head(m):
        return jnp.where(half_b, m, 0.0)

    def block_diag(m):
        return jnp.concatenate([first_head(m), second_head(m)], axis=0)

    def split3(x):
        hi = x.astype(BF16)
        rest = x - hi.astype(F32)
        md = rest.astype(BF16)
        return hi, md, (rest - md.astype(F32)).astype(BF16)

    ab = ab_ref[...]
    g_col = -jnp.exp(al_ref[...]) * _softplus(ab[:, 0:N_HEADS] + dtb_ref[...])
    beta_scr[...] = _sigmoid(ab[:, N_HEADS:2 * N_HEADS])
    for s in range(n_seq):
        for c in range(n_chunks):
            r = row0(s, c)
            hi, md, lw = split3(g_col[r:r + C, :])
            gcc_scr[r:r + C, :] = (_dot(tril, lw) + _dot(tril, md)) + _dot(tril, hi)
            gcr_scr[s * n_chunks + c] = (_dot_tn(lw, triu) + _dot_tn(md, triu)) + _dot_tn(hi, triu)

    scale = HEAD_DIM ** -0.5
    onorm = on_ref[...]
    zeros_k = jnp.zeros((C, HEAD_DIM), F32)

    for s, c, hp in pairs:
        r = row0(s, c)
        rows = slice(r, r + C)
        kq, ks = [], []
        for h in (2 * hp, 2 * hp + 1):
            p, lo = pid(s, c, h), h * HEAD_DIM
            q = act[rows, lo:lo + HEAD_DIM]
            k = act[rows, W + lo:W + lo + HEAD_DIM]
            v = act[rows, 2 * W + lo:2 * W + lo + HEAD_DIM]
            q = q * lax.rsqrt(jnp.sum(q * q, axis=-1, keepdims=True) + EPS) * scale
            k = k * lax.rsqrt(jnp.sum(k * k, axis=-1, keepdims=True) + EPS)
            gc_c = gcc_scr[rows, h:h + 1]
            beta_c = beta_scr[rows, h:h + 1]
            k_beta = k * beta_c
            e_gc = jnp.exp(gc_c)
            rhs_scr[p] = jnp.concatenate([v * beta_c, k_beta * e_gc], axis=1)
            wq_scr[p, C:2 * C, :] = q * e_gc
            kout_scr[p] = k * jnp.exp(gc_c[C - 1:C, :] - gc_c)
            kq.append(jnp.concatenate([k_beta, q], axis=0))
            ks.append(k)
        keys = jnp.concatenate([jnp.concatenate([ks[0], zeros_k], axis=1),
                                jnp.concatenate([zeros_k, ks[1]], axis=1)], axis=0)
        gram_scr[ppid(s, c, hp)] = _dot_nt(jnp.concatenate(kq, axis=1).astype(BF16), keys.astype(BF16))

    ahead.upto(7)
    for s, c, hp in pairs:
        pp, r, ha, hb = ppid(s, c, hp), row0(s, c), 2 * hp, 2 * hp + 1
        gc_c = jnp.where(half_b, gcc_scr[r:r + C, hb:hb + 1], gcc_scr[r:r + C, ha:ha + 1])
        gcr = gcr_scr[s * n_chunks + c]
        gc_r = jnp.concatenate([gcr[ha:ha + 1, :], gcr[hb:hb + 1, :]], axis=1)
        decay = jnp.exp(jnp.where(incl, gc_c - gc_r, -jnp.inf))
        gram = gram_scr[pp]
        lower = jnp.where(strict, gram[:C] * decay, 0.0)
        low_scr[pp] = lower
        attn_scr[pp] = gram[C:] * decay
        inv_scr[pp] = eye - lower
        pow_scr[pp] = _dot((-lower).astype(BF16), block_diag(-lower).astype(BF16))

    power = 2
    while power < C:
        last = 2 * power >= C
        for s, c, hp in pairs:
            pp = ppid(s, c, hp)
            pw = pow_scr[pp]
            inv = inv_scr[pp]
            pb = block_diag(pw).astype(BF16)
            if last:
                inv_scr[pp] = inv + _dot(inv.astype(BF16), pb)
            else:
                both = _dot(jnp.concatenate([inv, pw], axis=0).astype(BF16), pb)
                inv_scr[pp] = inv + both[:C]
                pow_scr[pp] = both[C:]
        power *= 2

    ahead.upto(8)
    def stacked_rhs(s, c, hp):
        return jnp.concatenate([rhs_scr[pid(s, c, 2 * hp)], rhs_scr[pid(s, c, 2 * hp + 1)]], axis=0).astype(BF16)

    for s, c, hp in pairs:
        inv = inv_scr[ppid(s, c, hp)]
        rhs = stacked_rhs(s, c, hp)
        x0_scr[pid(s, c, 2 * hp)] = _dot(first_head(inv).astype(BF16), rhs)
        x0_scr[pid(s, c, 2 * hp + 1)] = _dot(second_head(inv).astype(BF16), rhs)
    for s, c, hp in pairs:
        pa, pb2 = pid(s, c, 2 * hp), pid(s, c, 2 * hp + 1)
        low = low_scr[ppid(s, c, hp)]
        xa_hi, xa_lo = _split_bf16(x0_scr[pa])
        xb_hi, xb_lo = _split_bf16(x0_scr[pb2])
        x_both = jnp.concatenate([jnp.concatenate([xa_hi, xa_lo], axis=1),
                                  jnp.concatenate([xb_hi, xb_lo], axis=1)], axis=0)
        x_hi = jnp.concatenate([xa_hi, xb_hi], axis=0)
        for p, part in ((pa, first_head(low)), (pb2, second_head(low))):
            l_hi, l_lo = _split_bf16(part)
            lx = _dot(l_hi, x_both)
            lx = lx[:, :2 * HEAD_DIM] + (lx[:, 2 * HEAD_DIM:] + _dot(l_lo, x_hi))
            rhs_scr[p] = rhs_scr[p] - x0_scr[p] - lx
    for s, c, hp in pairs:
        inv = inv_scr[ppid(s, c, hp)]
        resid = stacked_rhs(s, c, hp)
        for p, part in ((pid(s, c, 2 * hp), first_head(inv)), (pid(s, c, 2 * hp + 1), second_head(inv))):
            sol = x0_scr[p] + _dot(part.astype(BF16), resid)
            u_scr[p] = sol[:, :HEAD_DIM]
            wq_scr[p, 0:C, :] = sol[:, HEAD_DIM:]

    for c in range(n_chunks):
        for s in range(n_seq):
            for h in range(N_HEADS):
                sh = s * N_HEADS + h
                ws_scr[sh] = _dot(wq_scr[pid(s, c, h)].astype(BF16), s_scr[sh].astype(BF16))
        for s in range(n_seq):
            for hp in range(N_HEADS // 2):
                r = row0(s, c)
                rows = slice(r, r + C)
                heads = (2 * hp, 2 * hp + 1)
                vn = [(u_scr[pid(s, c, h)] - ws_scr[s * N_HEADS + h][:C]).astype(BF16) for h in heads]
                vn_both = jnp.concatenate(vn, axis=0)
                attn = attn_scr[ppid(s, c, hp)]
                for h, vn16, part in ((heads[0], vn[0], first_head(attn)), (heads[1], vn[1], second_head(attn))):
                    p, lo, sh = pid(s, c, h), h * HEAD_DIM, s * N_HEADS + h
                    o = ws_scr[sh][C:] + _dot(part.astype(BF16), vn_both)
                    g_last = gcc_scr[r + C - 1:r + C, h:h + 1]
                    s_scr[sh] = s_scr[sh] * jnp.exp(g_last) + _dot_tn(kout_scr[p].astype(BF16), vn16)
                    z = main_ref[rows, 3 * W + lo:3 * W + lo + HEAD_DIM]
                    og_ref[rows, lo:lo + HEAD_DIM] = _rms(o, onorm) * _silu(z)

    @pl.when(t == pl.num_programs(1) - 1)
    def _():
        for s in range(n_seq):
            cb_ref[s] = xpad[s, pad + tb - (GD_CONV - 1):pad + tb, :]
            for h in range(N_HEADS):
                sf_ref[s, h] = s_scr[s * N_HEADS + h]


def _gdn_recurrence(main, ab, conv_w, a_log, dt_bias, out_norm, cb0, s0, *, batch, seq, n_chunks, n_seq, fuse=None):
    n = batch * seq
    fused = fuse is not None
    chunk = min(MAX_CHUNK, seq)
    tb = chunk * n_chunks
    assert seq % tb == 0 and batch % n_seq == 0 and seq >= GD_CONV - 1
    assert n_seq == 1 or tb == seq, "several sequences per step only when a step covers whole sequences"
    nt = seq // tb
    rows = n_seq * tb
    n_prob = n_seq * n_chunks * N_HEADS
    has_s0 = s0 is not None
    state_spec = pl.BlockSpec((n_seq, N_HEADS, HEAD_DIM, HEAD_DIM), lambda b, t: (b, 0, 0, 0))
    cb_spec = pl.BlockSpec((n_seq, GD_CONV - 1, GD_QKV), lambda b, t: (b, 0, 0))
    small = lambda shape: pl.BlockSpec(shape, lambda b, t: (0,) * len(shape))
    main_cols = GD_QKV + N_HEADS * HEAD_DIM
    if fused:
        x, gain, w_main, w_ab = fuse
        in_specs, args = _ahead_specs(x, gain, [w_main, w_ab], rows, (batch // n_seq) * nt, nt)
    else:
        in_specs = [pl.BlockSpec((rows, main_cols), lambda b, t: (b * nt + t, 0)),
                    pl.BlockSpec((rows, 2 * N_HEADS), lambda b, t: (b * nt + t, 0))]
        args = [main, ab]
    in_specs += [small((GD_CONV, GD_QKV)), small((1, N_HEADS)), small((1, N_HEADS)), small((1, HEAD_DIM))]
    args += [conv_w, a_log.reshape(1, N_HEADS), dt_bias.reshape(1, N_HEADS), out_norm.reshape(1, HEAD_DIM)]
    if has_s0:
        in_specs += [cb_spec, state_spec]
        args += [cb0, s0]
    vm = lambda *shape: pltpu.VMEM(shape, F32)
    return pl.pallas_call(
        functools.partial(_gdn_kernel, chunk=chunk, n_chunks=n_chunks, n_seq=n_seq, has_s0=has_s0, fused=fused),
        grid=(batch // n_seq, nt),
        in_specs=in_specs,
        out_specs=[pl.BlockSpec((rows, D_MODEL), lambda b, t: (b * nt + t, 0)), state_spec, cb_spec],
        out_shape=[jax.ShapeDtypeStruct((n, D_MODEL), F32),
                   jax.ShapeDtypeStruct((batch, N_HEADS, HEAD_DIM, HEAD_DIM), F32),
                   jax.ShapeDtypeStruct((batch, GD_CONV - 1, GD_QKV), F32)],
        scratch_shapes=[vm(n_seq * N_HEADS, HEAD_DIM, HEAD_DIM),
                        vm(n_seq, tb + SUBLANES, GD_QKV),
                        vm(rows, GD_QKV),
                        vm(rows, N_HEADS),
                        vm(n_seq * n_chunks, N_HEADS, chunk),
                        vm(rows, N_HEADS),
                        vm(n_prob // 2, 2 * chunk, 2 * chunk),
                        vm(n_prob, chunk, 2 * HEAD_DIM),
                        vm(n_prob, 2 * chunk, HEAD_DIM),
                        vm(n_prob, chunk, HEAD_DIM),
                        vm(n_prob // 2, chunk, 2 * chunk),
                        vm(n_prob // 2, chunk, 2 * chunk),
                        vm(n_prob // 2, chunk, 2 * chunk),
                        vm(n_prob // 2, chunk, 2 * chunk),
                        vm(n_prob, chunk, 2 * HEAD_DIM),
                        vm(n_prob, chunk, HEAD_DIM),
                        vm(n_seq * N_HEADS, 2 * chunk, HEAD_DIM)]
        + ([vm(2, rows, main_cols), vm(2, rows, 2 * N_HEADS), pltpu.VMEM((rows, D_MODEL), BF16)] if fused else []),
        compiler_params=_params("arbitrary", "arbitrary"),
        name="gdn_recurrence",
    )(*args)


def _swiglu_rows(x_of, wg, wu, wd, n_sub, emit):
    pending = None
    for sb in range(n_sub + 1):
        if sb < n_sub:
            x = x_of(sb)
            nxt = (_dot(x, wg), _dot(x, wu))
        if pending is not None:
            a, u = pending
            emit(sb - 1, _dot((_silu(a) * u).astype(BF16), wd))
        pending = nxt


def _ffn_kernel(x_ref, og_ref, wo_ref, g_ref, wg_ref, wu_ref, wd_ref, o_ref, h_scr, *, n_sub):
    j = pl.program_id(1)
    sub = x_ref.shape[0] // n_sub

    @pl.when(j == 0)
    def _():
        x = x_ref[...] + _dot(og_ref[...].astype(BF16), wo_ref[...])
        h_scr[...] = _rms(x, g_ref[...]).astype(BF16)
        o_ref[...] = x

    def emit(sb, y):
        o_ref[sb * sub:(sb + 1) * sub, :] += y

    _swiglu_rows(lambda sb: h_scr[sb * sub:(sb + 1) * sub, :], wg_ref[...], wu_ref[...], wd_ref[...], n_sub, emit)


def _ffn(x, og, w_out, gain, wg, wu, wd, *, tm, tf, n_sub):
    n, d = x.shape
    f = wg.shape[1]
    rows = pl.BlockSpec((tm, d), lambda i, j: (i, 0))
    return pl.pallas_call(
        functools.partial(_ffn_kernel, n_sub=n_sub),
        grid=(n // tm, f // tf),
        in_specs=[rows, rows,
                  pl.BlockSpec((d, d), lambda i, j: (0, 0)),
                  pl.BlockSpec((1, d), lambda i, j: (0, 0)),
                  pl.BlockSpec((d, tf), lambda i, j: (0, j)),
                  pl.BlockSpec((d, tf), lambda i, j: (0, j)),
                  pl.BlockSpec((tf, d), lambda i, j: (j, 0))],
        out_specs=rows,
        out_shape=jax.ShapeDtypeStruct((n, d), F32),
        scratch_shapes=[pltpu.VMEM((tm, d), BF16)],
        compiler_params=_params("parallel", "arbitrary"),
        name="dense_ffn",
    )(x, og, w_out, gain.reshape(1, d), wg, wu, wd)


def _run_copies(n_rows, max_rows, make_copy):
    pieces = []
    off = jnp.int32(0)
    bit = max_rows
    while bit >= RUN_ALIGN:
        has = (n_rows & bit) != 0
        pieces.append((has, make_copy(off, bit)))
        off = off + (n_rows & bit)
        bit //= 2
    return pieces


def _router_kernel(x_ref, og_ref, wo_ref, g_ref, wr_ref, xo_ref, dest_ref, gate_ref, cnt_ref, xs_hbm,
                   cnt_scr, stage, zeros_buf, h16_scr, code_scr, n_smem, ncopy_smem, base_smem, sem, *, tm, cap):
    i = pl.program_id(0)
    last = pl.num_programs(0) - 1

    @pl.when(i == 0)
    def _():
        cnt_scr[...] = jnp.zeros_like(cnt_scr)
        zeros_buf[...] = jnp.zeros_like(zeros_buf)
        h16_scr[1] = jnp.zeros((tm, D_MODEL), BF16)
        code_scr[1] = jnp.zeros((N_EXPERTS, tm), jnp.int32)
        for e in range(N_EXPERTS):
            n_smem[N_EXPERTS + e] = 0
            ncopy_smem[e] = 0
            base_smem[e] = e * cap

    slot_id = lax.broadcasted_iota(jnp.int32, (ROUTER_SLOTS, tm), 0)

    def wait_runs():
        for e in range(N_EXPERTS):
            for has, copy in _run_copies(ncopy_smem[e], tm, lambda off, size, e=e: pltpu.make_async_copy(
                    stage.at[e, pl.ds(0, size * SUBLANES)], xs_hbm.at[pl.ds(0, size * SUBLANES)], sem)):
                @pl.when(has)
                def _():
                    copy.wait()

    def compact(src, e, q):
        sel = jnp.where(code_scr[src, e:e + 1, :] == slot_id + (q * ROUTER_SLOTS + 1), 1.0, 0.0).astype(BF16)
        _store_token_tiles(stage.at[e], q * ROUTER_SLOTS, _dot(sel, h16_scr[src]))

    def compact_first(src):
        for e in range(N_EXPERTS):
            compact(src, e, 0)

    def finish_dispatch(src):
        for e in range(N_EXPERTS):
            n_e = n_smem[src * N_EXPERTS + e]
            for q in range(1, tm // ROUTER_SLOTS):
                @pl.when(q * ROUTER_SLOTS < n_e)
                def _():
                    compact(src, e, q)
            n_copy = jnp.bitwise_and(n_e + (RUN_ALIGN - 1), -RUN_ALIGN)
            base = base_smem[e]
            for has, copy in _run_copies(n_copy, tm, lambda off, size, e=e, base=base: pltpu.make_async_copy(
                    stage.at[e, pl.ds(pl.multiple_of(off * SUBLANES, SUBLANES), size * SUBLANES)],
                    xs_hbm.at[pl.ds(pl.multiple_of((base + off) * SUBLANES, SUBLANES), size * SUBLANES)], sem)):
                @pl.when(has)
                def _():
                    copy.start()
            ncopy_smem[e] = n_copy
            base_smem[e] = base + n_e

    wait_runs()
    for cur in range(2):
        @pl.when(lax.rem(i, 2) == cur)
        def _():
            compact_first(1 - cur)
            _route_tile(x_ref, og_ref, wo_ref, g_ref, wr_ref, xo_ref, dest_ref, gate_ref, cnt_ref, cnt_scr,
                        h16_scr, code_scr, n_smem, cur, tm=tm, cap=cap)
            finish_dispatch(1 - cur)

            @pl.when(i == last)
            def _():
                wait_runs()
                compact_first(cur)
                finish_dispatch(cur)
                wait_runs()
                pads = [pltpu.make_async_copy(
                    zeros_buf,
                    xs_hbm.at[pl.ds(pl.multiple_of(base_smem[e] * SUBLANES, SUBLANES), PAD_ROWS * SUBLANES)], sem)
                    for e in range(N_EXPERTS)]
                for pad in pads:
                    pad.start()
                for pad in pads:
                    pad.wait()


def _route_tile(x_ref, og_ref, wo_ref, g_ref, wr_ref, xo_ref, dest_ref, gate_ref, cnt_ref, cnt_scr,
                h16_scr, code_scr, n_smem, dst, *, tm, cap):
    x = x_ref[...] + _dot(og_ref[...].astype(BF16), wo_ref[...])
    xo_ref[...] = x
    h = _rms(x, g_ref[...])
    h16 = h.astype(BF16)
    h_lo = (h - h16.astype(F32)).astype(BF16)
    w_hi, w_lo = _split_bf16(wr_ref[...])
    logits = _dot(h16, w_hi) + (_dot(h16, w_lo) + _dot(h_lo, w_hi))
    lane = lax.broadcasted_iota(jnp.int32, logits.shape, 1)
    m1 = jnp.max(logits, axis=-1, keepdims=True)
    i1 = jnp.min(jnp.where(logits == m1, lane, N_EXPERTS), axis=-1, keepdims=True)
    rest = jnp.where(lane == i1, -jnp.inf, logits)
    m2 = jnp.max(rest, axis=-1, keepdims=True)
    i2 = jnp.min(jnp.where(rest == m2, lane, N_EXPERTS), axis=-1, keepdims=True)
    e2 = jnp.exp(m2 - m1)
    den = 1.0 + e2
    two = lax.broadcasted_iota(jnp.int32, (tm, TOP_K), 1)
    gate_ref[...] = jnp.where(two == 0, 1.0 / den, e2 / den)

    onehot = ((lane == i1) | (lane == i2)).astype(F32)
    row = lax.broadcasted_iota(jnp.int32, (tm, tm), 0)
    col = lax.broadcasted_iota(jnp.int32, (tm, tm), 1)
    local = _dot((row > col).astype(BF16), onehot.astype(BF16))
    rank = local + cnt_scr[...]
    r1 = jnp.sum(jnp.where(lane == i1, rank, 0.0), axis=-1, keepdims=True).astype(jnp.int32)
    r2 = jnp.sum(jnp.where(lane == i2, rank, 0.0), axis=-1, keepdims=True).astype(jnp.int32)
    dest_ref[...] = jnp.where(two == 0, i1 * cap + r1, i2 * cap + r2)
    cnt_tile = jnp.sum(onehot, axis=0, keepdims=True)
    cnt = cnt_scr[...] + cnt_tile
    cnt_scr[...] = cnt
    cnt_ref[...] = cnt.astype(jnp.int32)

    h16_scr[dst] = h16
    code = (onehot * (local + 1.0)).astype(BF16)
    code_scr[dst] = _dot_tn(code, (row == col).astype(BF16)).astype(jnp.int32)
    for e in range(N_EXPERTS):
        n_smem[dst * N_EXPERTS + e] = cnt_tile[0, e].astype(jnp.int32)


def _router(x, og, w_out, gain, w_router, *, tm, cap):
    n, d = x.shape
    two = pl.BlockSpec((tm, TOP_K), lambda i: (i, 0))
    rows = pl.BlockSpec((tm, d), lambda i: (i, 0))
    return pl.pallas_call(
        functools.partial(_router_kernel, tm=tm, cap=cap),
        grid=(n // tm,),
        in_specs=[rows, rows,
                  pl.BlockSpec((d, d), lambda i: (0, 0)),
                  pl.BlockSpec((1, d), lambda i: (0, 0)),
                  pl.BlockSpec((d, N_EXPERTS), lambda i: (0, 0))],
        out_specs=[rows, two, two, pl.BlockSpec((1, N_EXPERTS), lambda i: (0, 0)),
                   pl.BlockSpec(memory_space=pl.ANY)],
        out_shape=[jax.ShapeDtypeStruct((n, d), F32),
                   jax.ShapeDtypeStruct((n, TOP_K), jnp.int32),
                   jax.ShapeDtypeStruct((n, TOP_K), F32),
                   jax.ShapeDtypeStruct((1, N_EXPERTS), jnp.int32),
                   jax.ShapeDtypeStruct((N_EXPERTS * cap * SUBLANES, LANES), F32)],
        scratch_shapes=[pltpu.VMEM((1, N_EXPERTS), F32),
                        pltpu.VMEM((N_EXPERTS, tm * SUBLANES, LANES), F32),
                        pltpu.VMEM((PAD_ROWS * SUBLANES, LANES), F32),
                        pltpu.VMEM((2, tm, d), BF16),
                        pltpu.VMEM((2, N_EXPERTS, tm), jnp.int32),
                        pltpu.SMEM((2 * N_EXPERTS,), jnp.int32),
                        pltpu.SMEM((N_EXPERTS,), jnp.int32),
                        pltpu.SMEM((N_EXPERTS,), jnp.int32),
                        pltpu.SemaphoreType.DMA(())],
        compiler_params=_params("arbitrary"),
        name="moe_router",
    )(x, og, w_out, gain.reshape(1, d), w_router)


def _tile_gather_copy(src_hbm, dst, sem, src_row, dst_row):
    return pltpu.make_async_copy(src_hbm.at[pl.ds(pl.multiple_of(src_row * SUBLANES, SUBLANES), SUBLANES)],
                                 dst.at[pl.ds(pl.multiple_of(dst_row * SUBLANES, SUBLANES), SUBLANES)], sem)


def _load_token_tiles(ref, first_row, n_rows):
    return jnp.concatenate([ref[pl.ds(first_row * SUBLANES + s, n_rows, stride=SUBLANES), :]
                            for s in range(D_MODEL // LANES)], axis=1)


def _store_token_tiles(ref, first_row, x):
    for s in range(D_MODEL // LANES):
        ref[pl.ds(first_row * SUBLANES + s, x.shape[0], stride=SUBLANES), :] = x[:, s * LANES:(s + 1) * LANES]


def _expert_kernel(be_ref, xblk_ref, oblk_ref, nv_ref, nu_ref, xs_ref, wg_ref, wu_ref, wd_ref, yb_ref, xb, acc,
                   *, sub_rows, n_sub):
    i = pl.program_id(0)
    j = pl.program_id(1)
    n_valid = nv_ref[i]

    @pl.when(j == 0)
    def _():
        acc[...] = jnp.zeros_like(acc)
        for sb in range(n_sub):
            @pl.when(sb * sub_rows < n_valid)
            def _():
                xb[sb * sub_rows:(sb + 1) * sub_rows, :] = _load_token_tiles(
                    xs_ref, sb * sub_rows, sub_rows).astype(BF16)

    def emit(sb, y):
        acc[sb * sub_rows:(sb + 1) * sub_rows, :] += y

    def x_of(sb):
        return xb[sb * sub_rows:(sb + 1) * sub_rows, :]

    full = n_valid > (n_sub - 1) * sub_rows

    @pl.when(full)
    def _():
        _swiglu_rows(x_of, wg_ref[0], wu_ref[0], wd_ref[0], n_sub, emit)

    for sb in range(n_sub - 1):
        @pl.when(jnp.logical_and(jnp.logical_not(full), sb * sub_rows < n_valid))
        def _():
            _swiglu_rows(lambda _, sb=sb: x_of(sb), wg_ref[0], wu_ref[0], wd_ref[0], 1, lambda _, y, sb=sb: emit(sb, y))

    @pl.when(j == pl.num_programs(1) - 1)
    def _():
        for sb in range(n_sub):
            rows = slice(sb * sub_rows, (sb + 1) * sub_rows)

            @pl.when(sb * sub_rows < n_valid)
            def _():
                _store_token_tiles(yb_ref, sb * sub_rows, acc[rows, :])

            @pl.when(sb * sub_rows >= n_valid)
            def _():
                tiles = slice(sb * sub_rows * SUBLANES, (sb + 1) * sub_rows * SUBLANES)
                yb_ref[tiles, :] = jnp.zeros((sub_rows * SUBLANES, LANES), F32)


def _experts(xs, block_e, x_block, out_block, n_valid, n_used, wg, wu, wd, *, sub_rows, n_sub, tf, out_blocks):
    d = D_MODEL
    tmm = sub_rows * n_sub
    nb = block_e.shape[0]
    nf = D_FF // tf

    def wcol(i, j, be, xblk, oblk, nv, nu):
        return (be[i], 0, jnp.where(i < nu[0], j, nf - 1))

    def wrow(i, j, be, xblk, oblk, nv, nu):
        return (be[i], jnp.where(i < nu[0], j, nf - 1), 0)

    grid_spec = pltpu.PrefetchScalarGridSpec(
        num_scalar_prefetch=5,
        grid=(nb, nf),
        in_specs=[pl.BlockSpec((tmm * SUBLANES, LANES), lambda i, j, be, xblk, oblk, nv, nu: (xblk[i], 0)),
                  pl.BlockSpec((1, d, tf), wcol),
                  pl.BlockSpec((1, d, tf), wcol),
                  pl.BlockSpec((1, tf, d), wrow)],
        out_specs=pl.BlockSpec((tmm * SUBLANES, LANES), lambda i, j, be, xblk, oblk, nv, nu: (oblk[i], 0)),
        scratch_shapes=[pltpu.VMEM((tmm, d), BF16), pltpu.VMEM((tmm, d), F32)],
    )
    return pl.pallas_call(
        functools.partial(_expert_kernel, sub_rows=sub_rows, n_sub=n_sub),
        grid_spec=grid_spec,
        out_shape=jax.ShapeDtypeStruct((out_blocks * tmm * SUBLANES, LANES), F32),
        compiler_params=_params("arbitrary", "arbitrary"),
        name="moe_experts",
    )(block_e, x_block, out_block, n_valid, n_used, xs, wg, wu, wd)


def _combine_kernel(dest_ref, yb_hbm, x_ref, gate_ref, gain_ref, o_ref, r0, r1, sems, *, tm):
    i = pl.program_id(0)
    slot = lax.rem(i, 2)

    def start_gather(tile, dst_slot):
        def issue(r, carry):
            first = TOP_K * (tile * tm + r)
            _tile_gather_copy(yb_hbm, r0.at[dst_slot], sems.at[dst_slot, 0], dest_ref[first], r).start(priority=0)
            _tile_gather_copy(yb_hbm, r1.at[dst_slot], sems.at[dst_slot, 1], dest_ref[first + 1], r).start(priority=1)
            return carry

        lax.fori_loop(0, tm, issue, 0, unroll=8)

    @pl.when(i == 0)
    def _():
        start_gather(0, 0)

    @pl.when(i + 1 < pl.num_programs(0))
    def _():
        start_gather(i + 1, 1 - slot)

    whole = pl.ds(0, tm * SUBLANES)
    pltpu.make_async_copy(yb_hbm.at[whole], r0.at[slot], sems.at[slot, 0]).wait()
    pltpu.make_async_copy(yb_hbm.at[whole], r1.at[slot], sems.at[slot, 1]).wait()
    g = gate_ref[...]
    y = _load_token_tiles(r0.at[slot], 0, tm) * g[:, 0:1] + _load_token_tiles(r1.at[slot], 0, tm) * g[:, 1:2]
    o_ref[...] = _rms(x_ref[...] + y, gain_ref[...])


def _combine(yb, dest_flat, x, gates, gain, *, tm):
    n, d = x.shape
    grid_spec = pltpu.PrefetchScalarGridSpec(
        num_scalar_prefetch=1,
        grid=(n // tm,),
        in_specs=[pl.BlockSpec(memory_space=pl.ANY),
                  pl.BlockSpec((tm, d), lambda i, dest: (i, 0)),
                  pl.BlockSpec((tm, TOP_K), lambda i, dest: (i, 0)),
                  pl.BlockSpec((1, d), lambda i, dest: (0, 0))],
        out_specs=pl.BlockSpec((tm, d), lambda i, dest: (i, 0)),
        scratch_shapes=[pltpu.VMEM((2, tm * SUBLANES, LANES), F32), pltpu.VMEM((2, tm * SUBLANES, LANES), F32),
                        pltpu.SemaphoreType.DMA((2, TOP_K))],
    )
    return pl.pallas_call(
        functools.partial(_combine_kernel, tm=tm),
        grid_spec=grid_spec,
        out_shape=jax.ShapeDtypeStruct((n, d), F32),
        compiler_params=_params("arbitrary"),
        name="moe_combine",
    )(dest_flat, yb, x, gates, gain.reshape(1, d))


def _moe_and_final_norm(x, og, w_out, gain, w_router, wg, wu, wd, final_gain, *, tiles):
    n = x.shape[0]
    sub_rows, n_sub = tiles["moe_sub_rows"], tiles["moe_subs"]
    tmm = sub_rows * n_sub
    assert sub_rows == PAD_ROWS and n % tmm == 0
    cap = n + tmm
    x, dest, gates, counts, xs = _router(x, og, w_out, gain, w_router, tm=tiles["router_rows"], cap=cap)
    counts = counts[0]
    blocks_per_e = (counts + tmm - 1) // tmm
    blocks_end = jnp.cumsum(blocks_per_e)
    n_used = blocks_end[-1].astype(jnp.int32)
    nb = -(-(n * TOP_K + N_EXPERTS * (tmm - 1)) // tmm)
    blk = jnp.minimum(jnp.arange(nb, dtype=jnp.int32), n_used - 1)
    block_e = jnp.minimum(jnp.sum(blk[:, None] >= blocks_end[None, :], axis=1), N_EXPERTS - 1).astype(jnp.int32)
    k_in_e = blk - (blocks_end - blocks_per_e)[block_e]
    x_block = (block_e * (cap // tmm) + k_in_e).astype(jnp.int32)
    used = jnp.arange(nb, dtype=jnp.int32) < n_used
    n_valid = jnp.where(used, jnp.clip(counts[block_e] - k_in_e * tmm, 0, tmm), 0).astype(jnp.int32)
    dump_block = N_EXPERTS * (cap // tmm)
    out_block = jnp.where(used, x_block, dump_block).astype(jnp.int32)
    yb = _experts(xs, block_e, x_block, out_block, n_valid, n_used.reshape(1), wg, wu, wd,
                  sub_rows=sub_rows, n_sub=n_sub, tf=tiles["moe_ff_cols"], out_blocks=dump_block + 1)
    return _combine(yb, dest.reshape(-1), x, gates, final_gain, tm=tiles["combine_rows"])


def _tiles(n_rows, seq):
    big = n_rows >= 8192
    return {
        "proj_rows": 1024 if n_rows % 1024 == 0 else n_rows,
        "proj_cols": 2048,
        "rec_chunks": min(4, seq // min(MAX_CHUNK, seq)),
        "rec_seqs": 1 if seq > MAX_CHUNK else 4,
        "proj_in_recurrence": seq > MAX_CHUNK,
        "ffn_rows": 1024 if n_rows % 1024 == 0 else n_rows,
        "ff_cols": 512,
        "router_rows": 256,
        "moe_sub_rows": 256,
        "moe_subs": 4 if big else 2,
        "moe_ff_cols": 1792,
        "ffn_subs": 4 if big else 2,
        "combine_rows": 256,
    }


def _trunk(x3, hg_state, gd_state, gd_conv, p):
    batch, seq, d = x3.shape
    n = batch * seq
    x = x3.reshape(n, d)
    tiles = _tiles(n, seq)
    tm, tn = tiles["proj_rows"], tiles["proj_cols"]

    rec = dict(batch=batch, seq=seq, n_chunks=tiles["rec_chunks"], n_seq=tiles["rec_seqs"])

    if tiles["proj_in_recurrence"]:
        proj, fuse = None, (x, p["norm_mix"][0], p["hgrn_w_in"])
    else:
        proj, fuse = _norm_proj(x, p["norm_mix"][0], p["hgrn_w_in"], tm=tm, tn=tn)[0], None
    og, hg_new = _hgrn_recurrence(proj, p["hgrn_lb"], p["hgrn_norm"], hg_state, layer=0, fuse=fuse, **rec)
    x = _ffn(x, og, p["hgrn_w_out"], p["norm_ffn"][0], p["ffn_w_gate"], p["ffn_w_up"], p["ffn_w_down"],
             tm=tiles["ffn_rows"], tf=tiles["ff_cols"], n_sub=tiles["ffn_subs"])

    if tiles["proj_in_recurrence"]:
        main, ab, fuse = None, None, (x, p["norm_mix"][1], p["gdn_w_main"], p["gdn_w_ab"])
    else:
        (main, ab), fuse = _norm_proj(x, p["norm_mix"][1], p["gdn_w_main"], p["gdn_w_ab"], tm=tm, tn=tn), None
    og, gd_new, cv_new = _gdn_recurrence(main, ab, p["gdn_conv"], p["gdn_a_log"], p["gdn_dt_bias"], p["gdn_norm"],
                                         gd_conv, gd_state, fuse=fuse, **rec)
    y = _moe_and_final_norm(x, og, p["gdn_w_out"], p["norm_ffn"][1], p["moe_router"], p["moe_w_gate"], p["moe_w_up"], p["moe_w_down"],
                            p["norm_out"], tiles=tiles)
    return y.reshape(batch, seq, d), hg_new[None], gd_new[None], cv_new[None]


def kernel(x_prompt, x_sample, state_hgrn, state_gdn, state_gdn_conv, norm_mix, norm_ffn, norm_out, hgrn_w_in, hgrn_lb, hgrn_norm, hgrn_w_out, gdn_w_in, gdn_conv, gdn_a_log, gdn_dt_bias, gdn_norm, gdn_w_out, ffn_w_gate, ffn_w_up, ffn_w_down, moe_router, moe_w_gate, moe_w_up, moe_w_down):
    assert hgrn_w_in.shape[0] == 1 and gdn_w_in.shape[0] == 1, "one HGRN2 layer and one gated-DeltaNet layer"
    main_cols = GD_QKV + N_HEADS * HEAD_DIM
    p = {
        "norm_mix": norm_mix, "norm_ffn": norm_ffn, "norm_out": norm_out,
        "hgrn_w_in": hgrn_w_in[0].astype(BF16), "hgrn_lb": hgrn_lb, "hgrn_norm": hgrn_norm[0],
        "hgrn_w_out": hgrn_w_out[0].astype(BF16),
        "gdn_w_main": gdn_w_in[0, :, :main_cols].astype(BF16), "gdn_w_ab": gdn_w_in[0, :, main_cols:].astype(BF16),
        "gdn_conv": gdn_conv[0], "gdn_a_log": gdn_a_log[0], "gdn_dt_bias": gdn_dt_bias[0], "gdn_norm": gdn_norm[0],
        "gdn_w_out": gdn_w_out[0].astype(BF16),
        "ffn_w_gate": ffn_w_gate[0].astype(BF16), "ffn_w_up": ffn_w_up[0].astype(BF16),
        "ffn_w_down": ffn_w_down[0].astype(BF16),
        "moe_router": moe_router[0],
        "moe_w_gate": moe_w_gate[0].astype(BF16), "moe_w_up": moe_w_up[0].astype(BF16),
        "moe_w_down": moe_w_down[0].astype(BF16),
    }
    y_p, hg_p, gd_p, cv_p = _trunk(x_prompt, None, None, None, p)
    y_s, hg_s, gd_s, cv_s = _trunk(x_sample, state_hgrn[0], state_gdn[0], state_gdn_conv[0], p)
    return (y_p, y_s, hg_p, hg_s, gd_p, gd_s, cv_p, cv_s)
```

```python
import functools

import jax
import jax.numpy as jnp
from jax import lax
from jax.experimental import pallas as pl
from jax.experimental.pallas import tpu as pltpu

F32 = jnp.float32
BF16 = jnp.bfloat16

D_MODEL = 1024
N_HEADS = 8
HEAD_DIM = 128
D_FF = 3584
N_EXPERTS = 8
TOP_K = 2
GD_QKV = 3 * D_MODEL
GD_CONV = 4
MAX_CHUNK = 64
EPS = 1e-6
LANES = 128
SUBLANES = 8
VMEM_LIMIT_BYTES = 52 * 1024 * 1024
PROJ_PIECES = 8
ROUTER_SLOTS = 128
PAD_ROWS = 256
RUN_ALIGN = 32


def _params(*sem):
    return pltpu.CompilerParams(dimension_semantics=sem, vmem_limit_bytes=VMEM_LIMIT_BYTES)


def _rms(x, gain):
    return x * lax.rsqrt(jnp.mean(x * x, axis=-1, keepdims=True) + EPS) * gain


def _sigmoid(x):
    t = jnp.exp(-jnp.abs(x))
    r = 1.0 / (1.0 + t)
    return jnp.where(x >= 0, r, t * r)


def _silu(x):
    return x * (0.5 * jnp.tanh(0.5 * x) + 0.5)


def _softplus(x):
    return jnp.maximum(x, 0.0) + jnp.log1p(jnp.exp(-jnp.abs(x)))


def _dot(a, b):
    return jnp.dot(a, b, preferred_element_type=F32)


def _dot_nt(a, b):
    return lax.dot_general(a, b, (((1,), (1,)), ((), ())), preferred_element_type=F32)


def _dot_tn(a, b):
    return lax.dot_general(a, b, (((0,), (0,)), ((), ())), preferred_element_type=F32)


def _norm_proj_kernel(x_ref, g_ref, w_ref, *rest, has_small):
    if has_small:
        ws_ref, o_ref, os_ref, h_scr = rest
    else:
        o_ref, h_scr = rest

    @pl.when(pl.program_id(1) == 0)
    def _():
        hb = _rms(x_ref[...], g_ref[...]).astype(BF16)
        h_scr[...] = hb
        if has_small:
            os_ref[...] = _dot(hb, ws_ref[...])

    o_ref[...] = _dot(h_scr[...], w_ref[...])


def _norm_proj(x, gain, w, w_small=None, *, tm, tn):
    n, d = x.shape
    e = w.shape[1]
    has_small = w_small is not None
    in_specs = [pl.BlockSpec((tm, d), lambda i, j: (i, 0)),
                pl.BlockSpec((1, d), lambda i, j: (0, 0)),
                pl.BlockSpec((d, tn), lambda i, j: (0, j))]
    out_shape = [jax.ShapeDtypeStruct((n, e), F32)]
    out_specs = [pl.BlockSpec((tm, tn), lambda i, j: (i, j))]
    args = [x, gain.reshape(1, d), w]
    if has_small:
        es = w_small.shape[1]
        in_specs.append(pl.BlockSpec((d, es), lambda i, j: (0, 0)))
        out_shape.append(jax.ShapeDtypeStruct((n, es), F32))
        out_specs.append(pl.BlockSpec((tm, es), lambda i, j: (i, 0)))
        args.append(w_small)
    return pl.pallas_call(
        functools.partial(_norm_proj_kernel, has_small=has_small),
        grid=(n // tm, e // tn),
        in_specs=in_specs, out_specs=out_specs, out_shape=out_shape,
        scratch_shapes=[pltpu.VMEM((tm, d), BF16)],
        compiler_params=_params("parallel", "arbitrary"),
        name="norm_proj",
    )(*args)


class _Projection:
    def __init__(self, x_ref, gain_ref, w_refs, dst_scrs, h_scr, dst_slot):
        self.x_ref, self.gain_ref, self.w_refs, self.dst_scrs = x_ref, gain_ref, w_refs, dst_scrs
        self.h_scr, self.dst_slot, self.done = h_scr, dst_slot, 0

    def upto(self, k):
        for piece in range(self.done, min(k, PROJ_PIECES)):
            if piece == 0:
                self.h_scr[...] = _rms(self.x_ref[...], self.gain_ref[...]).astype(BF16)
            h16 = self.h_scr[...]
            for w_ref, dst in zip(self.w_refs, self.dst_scrs):
                width = w_ref.shape[1]
                if width % (PROJ_PIECES * LANES) == 0:
                    cols = slice(piece * width // PROJ_PIECES, (piece + 1) * width // PROJ_PIECES)
                    dst[self.dst_slot, :, cols] = _dot(h16, w_ref[:, cols])
                elif piece == 0:
                    dst[self.dst_slot] = _dot(h16, w_ref[...])
        self.done = max(self.done, k)


class _NoProjection:
    def upto(self, k):
        pass


def _with_projection_ahead(body, refs, n_weights, **kw):
    x_cur_ref, x_next_ref, gain_ref = refs[:3]
    w_refs = refs[3:3 + n_weights]
    dst_scrs = refs[len(refs) - n_weights - 1:len(refs) - 1]
    h_scr = refs[-1]
    g = pl.program_id(0) * pl.num_programs(1) + pl.program_id(1)

    @pl.when(g == 0)
    def _():
        _Projection(x_cur_ref, gain_ref, w_refs, dst_scrs, h_scr, 0).upto(PROJ_PIECES)

    for cur in range(2):
        @pl.when(lax.rem(g, 2) == cur)
        def _():
            ahead = _Projection(x_next_ref, gain_ref, w_refs, dst_scrs, h_scr, 1 - cur)
            body(*[scr.at[cur] for scr in dst_scrs], *refs[3 + n_weights:len(refs) - n_weights - 1], ahead=ahead,
                 **kw)
            ahead.upto(PROJ_PIECES)


def _hgrn_kernel(*refs, fused, **kw):
    if fused:
        _with_projection_ahead(_hgrn_step, refs, 1, **kw)
    else:
        _hgrn_step(*refs, **kw)


def _hgrn_step(proj_ref, *refs, chunk, n_chunks, n_seq, layer, has_s0, ahead=None):
    if has_s0:
        lb_ref, on_ref, s0_ref, og_ref, sf_ref, *scr = refs
    else:
        lb_ref, on_ref, og_ref, sf_ref, *scr = refs
    st_scr, b_scr, q_scr, k_scr, a_scr, qin_scr, kout_scr, dec_scr, oi_scr = scr[:9]
    C = chunk
    W = N_HEADS * HEAD_DIM
    tb = C * n_chunks
    t = pl.program_id(1)
    problems = [(s, c, h) for s in range(n_seq) for c in range(n_chunks) for h in range(N_HEADS)]

    def pid(s, c, h):
        return (s * n_chunks + c) * N_HEADS + h

    def row0(s, c):
        return s * tb + c * C

    @pl.when(t == 0)
    def _():
        for s in range(n_seq):
            for h in range(N_HEADS):
                st_scr[s * N_HEADS + h] = s0_ref[s, h].T if has_s0 else jnp.zeros((HEAD_DIM, HEAD_DIM), F32)

    ahead = ahead or _NoProjection()
    ahead.upto(1)

    lbraw = lb_ref[...]
    ex = jnp.exp(lbraw - jnp.max(lbraw, axis=0, keepdims=True))
    sm = ex / jnp.sum(ex, axis=0, keepdims=True)
    lb = sm[0:1]
    for j in range(1, layer + 1):
        lb = lb + sm[j:j + 1]

    row = lax.broadcasted_iota(jnp.int32, (C, C), 0)
    col = lax.broadcasted_iota(jnp.int32, (C, C), 1)
    causal = row >= col
    tril = causal.astype(BF16)
    mid = C // 2
    scale = HEAD_DIM ** -0.5
    onorm = on_ref[...]

    for s in range(n_seq):
        for c in range(n_chunks):
            rows = slice(row0(s, c), row0(s, c) + C)
            f_raw = proj_ref[rows, W:2 * W]
            tt = jnp.exp(-jnp.abs(f_raw))
            rr = 1.0 / (1.0 + tt)
            pos = f_raw >= 0
            sig = jnp.where(pos, rr, tt * rr)
            sig_neg = jnp.where(pos, tt * rr, rr)
            log_f = jnp.log(lb + (1.0 - lb) * sig)
            k_scr[rows, :] = (1.0 - lb) * sig_neg
            q_scr[rows, :] = _silu(proj_ref[rows, 0:W]) * scale
            hi = log_f.astype(BF16)
            rest = log_f - hi.astype(F32)
            md = rest.astype(BF16)
            lw = (rest - md.astype(F32)).astype(BF16)
            b_scr[rows, :] = (_dot(tril, lw) + _dot(tril, md)) + _dot(tril, hi)

    ahead.upto(3)
    for s, c, h in problems:
        p, lo = pid(s, c, h), h * HEAD_DIM
        rows = slice(row0(s, c), row0(s, c) + C)
        b = b_scr[rows, lo:lo + HEAD_DIM]
        q = q_scr[rows, lo:lo + HEAD_DIM]
        k = k_scr[rows, lo:lo + HEAD_DIM]
        b_mid = b[mid:mid + 1]
        b_last = b[C - 1:C]
        a_scr[p] = _dot_nt((q * jnp.exp(b - b_mid)).astype(BF16), (k * jnp.exp(b_mid - b)).astype(BF16))
        qin_scr[p] = q * jnp.exp(b)
        kout_scr[p] = k * jnp.exp(b_last - b)
        dec_scr[p] = jnp.exp(b_last)

    ahead.upto(5)
    for s, c, h in problems:
        p, lo = pid(s, c, h), h * HEAD_DIM
        rows = slice(row0(s, c), row0(s, c) + C)
        a = jnp.where(causal, a_scr[p], 0.0)
        oi_scr[p] = _dot(a.astype(BF16), proj_ref[rows, 2 * W + lo:2 * W + lo + HEAD_DIM].astype(BF16))

    for c in range(n_chunks):
        ahead.upto(6 + c)
        for s in range(n_seq):
            for h in range(N_HEADS):
                p, lo, sh = pid(s, c, h), h * HEAD_DIM, s * N_HEADS + h
                rows = slice(row0(s, c), row0(s, c) + C)
                st = st_scr[sh]
                o = oi_scr[p] + _dot_nt(qin_scr[p].astype(BF16), st.astype(BF16))
                vb = proj_ref[rows, 2 * W + lo:2 * W + lo + HEAD_DIM].astype(BF16)
                st_scr[sh] = st * dec_scr[p] + _dot_tn(vb, kout_scr[p].astype(BF16))
                gate = proj_ref[rows, 3 * W + lo:3 * W + lo + HEAD_DIM]
                og_ref[rows, lo:lo + HEAD_DIM] = _rms(o, onorm) * _silu(gate)

    @pl.when(t == pl.num_programs(1) - 1)
    def _():
        for s in range(n_seq):
            for h in range(N_HEADS):
                sf_ref[s, h] = st_scr[s * N_HEADS + h].T


def _ahead_specs(x, gain, weights, rows, n_steps, nt):
    d = x.shape[1]
    specs = [pl.BlockSpec((rows, d), lambda b, t: (b * nt + t, 0)),
             pl.BlockSpec((rows, d), lambda b, t: (jnp.minimum(b * nt + t + 1, n_steps - 1), 0)),
             pl.BlockSpec((1, d), lambda b, t: (0, 0))]
    specs += [pl.BlockSpec(w.shape, lambda b, t: (0, 0)) for w in weights]
    return specs, [x, x, gain.reshape(1, d)] + list(weights)


def _hgrn_recurrence(proj, lb_raw, out_norm, s0, *, batch, seq, layer, n_chunks, n_seq, fuse=None):
    n = batch * seq
    chunk = min(MAX_CHUNK, seq)
    tb = chunk * n_chunks
    assert seq % tb == 0 and batch % n_seq == 0
    assert n_seq == 1 or tb == seq, "several sequences per step only when a step covers whole sequences"
    nt = seq // tb
    rows = n_seq * tb
    n_prob = n_seq * n_chunks * N_HEADS
    width = N_HEADS * HEAD_DIM
    has_s0 = s0 is not None
    fused = fuse is not None
    state_spec = pl.BlockSpec((n_seq, N_HEADS, HEAD_DIM, HEAD_DIM), lambda b, t: (b, 0, 0, 0))
    if fused:
        x, gain, w = fuse
        in_specs, args = _ahead_specs(x, gain, [w], rows, (batch // n_seq) * nt, nt)
    else:
        in_specs, args = [pl.BlockSpec((rows, proj.shape[1]), lambda b, t: (b * nt + t, 0))], [proj]
    in_specs += [pl.BlockSpec(lb_raw.shape, lambda b, t: (0, 0)),
                 pl.BlockSpec((1, HEAD_DIM), lambda b, t: (0, 0))]
    args += [lb_raw, out_norm.reshape(1, HEAD_DIM)]
    if has_s0:
        in_specs.append(state_spec)
        args.append(s0)
    vm = lambda *shape: pltpu.VMEM(shape, F32)
    ahead_scratch = [vm(2, rows, 4 * width), pltpu.VMEM((rows, D_MODEL), BF16)] if fused else []
    return pl.pallas_call(
        functools.partial(_hgrn_kernel, chunk=chunk, n_chunks=n_chunks, n_seq=n_seq, layer=layer, has_s0=has_s0,
                          fused=fused),
        grid=(batch // n_seq, nt),
        in_specs=in_specs,
        out_specs=[pl.BlockSpec((rows, D_MODEL), lambda b, t: (b * nt + t, 0)), state_spec],
        out_shape=[jax.ShapeDtypeStruct((n, D_MODEL), F32),
                   jax.ShapeDtypeStruct((batch, N_HEADS, HEAD_DIM, HEAD_DIM), F32)],
        scratch_shapes=[vm(n_seq * N_HEADS, HEAD_DIM, HEAD_DIM),
                        vm(rows, width),
                        vm(rows, width),
                        vm(rows, width),
                        vm(n_prob, chunk, chunk),
                        vm(n_prob, chunk, HEAD_DIM),
                        vm(n_prob, chunk, HEAD_DIM),
                        vm(n_prob, 1, HEAD_DIM),
                        vm(n_prob, chunk, HEAD_DIM)] + ahead_scratch,
        compiler_params=_params("arbitrary", "arbitrary"),
        name="hgrn_recurrence",
    )(*args)


def _split_bf16(x):
    hi = x.astype(BF16)
    return hi, (x - hi.astype(F32)).astype(BF16)


def _gdn_kernel(*refs, fused, **kw):
    if fused:
        _with_projection_ahead(_gdn_step, refs, 2, **kw)
    else:
        _gdn_step(*refs, **kw)


def _gdn_step(main_ref, ab_ref, *refs, chunk, n_chunks, n_seq, has_s0, ahead=None):
    (cw_ref, al_ref, dtb_ref, on_ref), rest = refs[:4], refs[4:]
    if has_s0:
        (cb0_ref, s0_ref), rest = rest[:2], rest[2:]
    (og_ref, sf_ref, cb_ref, s_scr, xpad, act, gcc_scr, gcr_scr, beta_scr, gram_scr, rhs_scr, wq_scr, kout_scr,
     inv_scr, pow_scr, low_scr, attn_scr, x0_scr, u_scr, ws_scr) = rest[:20]
    C = chunk
    W = N_HEADS * HEAD_DIM
    tb = C * n_chunks
    t = pl.program_id(1)
    pad = SUBLANES
    pairs = [(s, c, hp) for s in range(n_seq) for c in range(n_chunks) for hp in range(N_HEADS // 2)]

    def pid(s, c, h):
        return (s * n_chunks + c) * N_HEADS + h

    def ppid(s, c, hp):
        return (s * n_chunks + c) * (N_HEADS // 2) + hp

    def row0(s, c):
        return s * tb + c * C

    @pl.when(t == 0)
    def _():
        for s in range(n_seq):
            for h in range(N_HEADS):
                s_scr[s * N_HEADS + h] = s0_ref[s, h] if has_s0 else jnp.zeros((HEAD_DIM, HEAD_DIM), F32)
            xpad[s, 0:pad, :] = jnp.zeros((pad, GD_QKV), F32)
            if has_s0:
                xpad[s, pad - (GD_CONV - 1):pad, :] = cb0_ref[s]

    @pl.when(t > 0)
    def _():
        for s in range(n_seq):
            xpad[s, 0:pad, :] = xpad[s, tb:tb + pad, :]

    ahead = ahead or _NoProjection()
    ahead.upto(6)

    for s in range(n_seq):
        for j in range(GD_QKV // LANES):
            cs = slice(j * LANES, (j + 1) * LANES)
            xpad[s, pad:pad + tb, cs] = main_ref[s * tb:(s + 1) * tb, cs]
    for s in range(n_seq):
        for j in range(GD_QKV // LANES):
            cs = slice(j * LANES, (j + 1) * LANES)
            conv = xpad[s, pad - 3:pad - 3 + tb, cs] * cw_ref[0:1, cs]
            for tap in range(1, GD_CONV):
                conv = conv + xpad[s, pad - 3 + tap:pad - 3 + tap + tb, cs] * cw_ref[tap:tap + 1, cs]
            act[s * tb:(s + 1) * tb, cs] = _silu(conv)

    C2 = 2 * C
    prow = lax.broadcasted_iota(jnp.int32, (C, C2), 0)
    pcol = lax.broadcasted_iota(jnp.int32, (C, C2), 1)
    half_b = pcol >= C
    pcol_in = jnp.where(half_b, pcol - C, pcol)
    incl = prow >= pcol_in
    strict = prow > pcol_in
    eye = jnp.where(prow == pcol_in, 1.0, 0.0)
    row = lax.broadcasted_iota(jnp.int32, (C, C), 0)
    col = lax.broadcasted_iota(jnp.int32, (C, C), 1)
    tril = (row >= col).astype(BF16)
    triu = (row <= col).astype(BF16)

    def first_head(m):
        return jnp.where(half_b, 0.0, m)

    def second_head(m):
        return jnp.where(half_b, m, 0.0)

    def block_diag(m):
        return jnp.concatenate([first_head(m), second_head(m)], axis=0)

    def split3(x):
        hi = x.astype(BF16)
        rest = x - hi.astype(F32)
        md = rest.astype(BF16)
        return hi, md, (rest - md.astype(F32)).astype(BF16)

    ab = ab_ref[...]
    g_col = -jnp.exp(al_ref[...]) * _softplus(ab[:, 0:N_HEADS] + dtb_ref[...])
    beta_scr[...] = _sigmoid(ab[:, N_HEADS:2 * N_HEADS])
    for s in range(n_seq):
        for c in range(n_chunks):
            r = row0(s, c)
            hi, md, lw = split3(g_col[r:r + C, :])
            gcc_scr[r:r + C, :] = (_dot(tril, lw) + _dot(tril, md)) + _dot(tril, hi)
            gcr_scr[s * n_chunks + c] = (_dot_tn(lw, triu) + _dot_tn(md, triu)) + _dot_tn(hi, triu)

    scale = HEAD_DIM ** -0.5
    onorm = on_ref[...]
    zeros_k = jnp.zeros((C, HEAD_DIM), F32)

    for s, c, hp in pairs:
        r = row0(s, c)
        rows = slice(r, r + C)
        kq, ks = [], []
        for h in (2 * hp, 2 * hp + 1):
            p, lo = pid(s, c, h), h * HEAD_DIM
            q = act[rows, lo:lo + HEAD_DIM]
            k = act[rows, W + lo:W + lo + HEAD_DIM]
            v = act[rows, 2 * W + lo:2 * W + lo + HEAD_DIM]
            q = q * lax.rsqrt(jnp.sum(q * q, axis=-1, keepdims=True) + EPS) * scale
            k = k * lax.rsqrt(jnp.sum(k * k, axis=-1, keepdims=True) + EPS)
            gc_c = gcc_scr[rows, h:h + 1]
            beta_c = beta_scr[rows, h:h + 1]
            k_beta = k * beta_c
            e_gc = jnp.exp(gc_c)
            rhs_scr[p] = jnp.concatenate([v * beta_c, k_beta * e_gc], axis=1)
            wq_scr[p, C:2 * C, :] = q * e_gc
            kout_scr[p] = k * jnp.exp(gc_c[C - 1:C, :] - gc_c)
            kq.append(jnp.concatenate([k_beta, q], axis=0))
            ks.append(k)
        keys = jnp.concatenate([jnp.concatenate([ks[0], zeros_k], axis=1),
                                jnp.concatenate([zeros_k, ks[1]], axis=1)], axis=0)
        gram_scr[ppid(s, c, hp)] = _dot_nt(jnp.concatenate(kq, axis=1).astype(BF16), keys.astype(BF16))

    ahead.upto(7)
    for s, c, hp in pairs:
        pp, r, ha, hb = ppid(s, c, hp), row0(s, c), 2 * hp, 2 * hp + 1
        gc_c = jnp.where(half_b, gcc_scr[r:r + C, hb:hb + 1], gcc_scr[r:r + C, ha:ha + 1])
        gcr = gcr_scr[s * n_chunks + c]
        gc_r = jnp.concatenate([gcr[ha:ha + 1, :], gcr[hb:hb + 1, :]], axis=1)
        decay = jnp.exp(jnp.where(incl, gc_c - gc_r, -jnp.inf))
        gram = gram_scr[pp]
        lower = jnp.where(strict, gram[:C] * decay, 0.0)
        low_scr[pp] = lower
        attn_scr[pp] = gram[C:] * decay
        inv_scr[pp] = eye - lower
        pow_scr[pp] = _dot((-lower).astype(BF16), block_diag(-lower).astype(BF16))

    power = 2
    while power < C:
        last = 2 * power >= C
        for s, c, hp in pairs:
            pp = ppid(s, c, hp)
            pw = pow_scr[pp]
            inv = inv_scr[pp]
            pb = block_diag(pw).astype(BF16)
            if last:
                inv_scr[pp] = inv + _dot(inv.astype(BF16), pb)
            else:
                both = _dot(jnp.concatenate([inv, pw], axis=0).astype(BF16), pb)
                inv_scr[pp] = inv + both[:C]
                pow_scr[pp] = both[C:]
        power *= 2

    ahead.upto(8)
    def stacked_rhs(s, c, hp):
        return jnp.concatenate([rhs_scr[pid(s, c, 2 * hp)], rhs_scr[pid(s, c, 2 * hp + 1)]], axis=0).astype(BF16)

    for s, c, hp in pairs:
        inv = inv_scr[ppid(s, c, hp)]
        rhs = stacked_rhs(s, c, hp)
        x0_scr[pid(s, c, 2 * hp)] = _dot(first_head(inv).astype(BF16), rhs)
        x0_scr[pid(s, c, 2 * hp + 1)] = _dot(second_head(inv).astype(BF16), rhs)
    for s, c, hp in pairs:
        pa, pb2 = pid(s, c, 2 * hp), pid(s, c, 2 * hp + 1)
        low = low_scr[ppid(s, c, hp)]
        xa_hi, xa_lo = _split_bf16(x0_scr[pa])
        xb_hi, xb_lo = _split_bf16(x0_scr[pb2])
        x_both = jnp.concatenate([jnp.concatenate([xa_hi, xa_lo], axis=1),
                                  jnp.concatenate([xb_hi, xb_lo], axis=1)], axis=0)
        x_hi = jnp.concatenate([xa_hi, xb_hi], axis=0)
        for p, part in ((pa, first_head(low)), (pb2, second_head(low))):
            l_hi, l_lo = _split_bf16(part)
            lx = _dot(l_hi, x_both)
            lx = lx[:, :2 * HEAD_DIM] + (lx[:, 2 * HEAD_DIM:] + _dot(l_lo, x_hi))
            rhs_scr[p] = rhs_scr[p] - x0_scr[p] - lx
    for s, c, hp in pairs:
        inv = inv_scr[ppid(s, c, hp)]
        resid = stacked_rhs(s, c, hp)
        for p, part in ((pid(s, c, 2 * hp), first_head(inv)), (pid(s, c, 2 * hp + 1), second_head(inv))):
            sol = x0_scr[p] + _dot(part.astype(BF16), resid)
            u_scr[p] = sol[:, :HEAD_DIM]
            wq_scr[p, 0:C, :] = sol[:, HEAD_DIM:]

    for c in range(n_chunks):
        for s in range(n_seq):
            for h in range(N_HEADS):
                sh = s * N_HEADS + h
                ws_scr[sh] = _dot(wq_scr[pid(s, c, h)].astype(BF16), s_scr[sh].astype(BF16))
        for s in range(n_seq):
            for hp in range(N_HEADS // 2):
                r = row0(s, c)
                rows = slice(r, r + C)
                heads = (2 * hp, 2 * hp + 1)
                vn = [(u_scr[pid(s, c, h)] - ws_scr[s * N_HEADS + h][:C]).astype(BF16) for h in heads]
                vn_both = jnp.concatenate(vn, axis=0)
                attn = attn_scr[ppid(s, c, hp)]
                for h, vn16, part in ((heads[0], vn[0], first_head(attn)), (heads[1], vn[1], second_head(attn))):
                    p, lo, sh = pid(s, c, h), h * HEAD_DIM, s * N_HEADS + h
                    o = ws_scr[sh][C:] + _dot(part.astype(BF16), vn_both)
                    g_last = gcc_scr[r + C - 1:r + C, h:h + 1]
                    s_scr[sh] = s_scr[sh] * jnp.exp(g_last) + _dot_tn(kout_scr[p].astype(BF16), vn16)
                    z = main_ref[rows, 3 * W + lo:3 * W + lo + HEAD_DIM]
                    og_ref[rows, lo:lo + HEAD_DIM] = _rms(o, onorm) * _silu(z)

    @pl.when(t == pl.num_programs(1) - 1)
    def _():
        for s in range(n_seq):
            cb_ref[s] = xpad[s, pad + tb - (GD_CONV - 1):pad + tb, :]
            for h in range(N_HEADS):
                sf_ref[s, h] = s_scr[s * N_HEADS + h]


def _gdn_recurrence(main, ab, conv_w, a_log, dt_bias, out_norm, cb0, s0, *, batch, seq, n_chunks, n_seq, fuse=None):
    n = batch * seq
    fused = fuse is not None
    chunk = min(MAX_CHUNK, seq)
    tb = chunk * n_chunks
    assert seq % tb == 0 and batch % n_seq == 0 and seq >= GD_CONV - 1
    assert n_seq == 1 or tb == seq, "several sequences per step only when a step covers whole sequences"
    nt = seq // tb
    rows = n_seq * tb
    n_prob = n_seq * n_chunks * N_HEADS
    has_s0 = s0 is not None
    state_spec = pl.BlockSpec((n_seq, N_HEADS, HEAD_DIM, HEAD_DIM), lambda b, t: (b, 0, 0, 0))
    cb_spec = pl.BlockSpec((n_seq, GD_CONV - 1, GD_QKV), lambda b, t: (b, 0, 0))
    small = lambda shape: pl.BlockSpec(shape, lambda b, t: (0,) * len(shape))
    main_cols = GD_QKV + N_HEADS * HEAD_DIM
    if fused:
        x, gain, w_main, w_ab = fuse
        in_specs, args = _ahead_specs(x, gain, [w_main, w_ab], rows, (batch // n_seq) * nt, nt)
    else:
        in_specs = [pl.BlockSpec((rows, main_cols), lambda b, t: (b * nt + t, 0)),
                    pl.BlockSpec((rows, 2 * N_HEADS), lambda b, t: (b * nt + t, 0))]
        args = [main, ab]
    in_specs += [small((GD_CONV, GD_QKV)), small((1, N_HEADS)), small((1, N_HEADS)), small((1, HEAD_DIM))]
    args += [conv_w, a_log.reshape(1, N_HEADS), dt_bias.reshape(1, N_HEADS), out_norm.reshape(1, HEAD_DIM)]
    if has_s0:
        in_specs += [cb_spec, state_spec]
        args += [cb0, s0]
    vm = lambda *shape: pltpu.VMEM(shape, F32)
    return pl.pallas_call(
        functools.partial(_gdn_kernel, chunk=chunk, n_chunks=n_chunks, n_seq=n_seq, has_s0=has_s0, fused=fused),
        grid=(batch // n_seq, nt),
        in_specs=in_specs,
        out_specs=[pl.BlockSpec((rows, D_MODEL), lambda b, t: (b * nt + t, 0)), state_spec, cb_spec],
        out_shape=[jax.ShapeDtypeStruct((n, D_MODEL), F32),
                   jax.ShapeDtypeStruct((batch, N_HEADS, HEAD_DIM, HEAD_DIM), F32),
                   jax.ShapeDtypeStruct((batch, GD_CONV - 1, GD_QKV), F32)],
        scratch_shapes=[vm(n_seq * N_HEADS, HEAD_DIM, HEAD_DIM),
                        vm(n_seq, tb + SUBLANES, GD_QKV),
                        vm(rows, GD_QKV),
                        vm(rows, N_HEADS),
                        vm(n_seq * n_chunks, N_HEADS, chunk),
                        vm(rows, N_HEADS),
                        vm(n_prob // 2, 2 * chunk, 2 * chunk),
                        vm(n_prob, chunk, 2 * HEAD_DIM),
                        vm(n_prob, 2 * chunk, HEAD_DIM),
                        vm(n_prob, chunk, HEAD_DIM),
                        vm(n_prob // 2, chunk, 2 * chunk),
                        vm(n_prob // 2, chunk, 2 * chunk),
                        vm(n_prob // 2, chunk, 2 * chunk),
                        vm(n_prob // 2, chunk, 2 * chunk),
                        vm(n_prob, chunk, 2 * HEAD_DIM),
                        vm(n_prob, chunk, HEAD_DIM),
                        vm(n_seq * N_HEADS, 2 * chunk, HEAD_DIM)]
        + ([vm(2, rows, main_cols), vm(2, rows, 2 * N_HEADS), pltpu.VMEM((rows, D_MODEL), BF16)] if fused else []),
        compiler_params=_params("arbitrary", "arbitrary"),
        name="gdn_recurrence",
    )(*args)


def _swiglu_rows(x_of, wg, wu, wd, n_sub, emit):
    pending = None
    for sb in range(n_sub + 1):
        if sb < n_sub:
            x = x_of(sb)
            nxt = (_dot(x, wg), _dot(x, wu))
        if pending is not None:
            a, u = pending
            emit(sb - 1, _dot((_silu(a) * u).astype(BF16), wd))
        pending = nxt


def _ffn_kernel(x_ref, og_ref, wo_ref, g_ref, wg_ref, wu_ref, wd_ref, o_ref, h_scr, *, n_sub):
    j = pl.program_id(1)
    sub = x_ref.shape[0] // n_sub

    @pl.when(j == 0)
    def _():
        x = x_ref[...] + _dot(og_ref[...].astype(BF16), wo_ref[...])
        h_scr[...] = _rms(x, g_ref[...]).astype(BF16)
        o_ref[...] = x

    def emit(sb, y):
        o_ref[sb * sub:(sb + 1) * sub, :] += y

    _swiglu_rows(lambda sb: h_scr[sb * sub:(sb + 1) * sub, :], wg_ref[...], wu_ref[...], wd_ref[...], n_sub, emit)


def _ffn(x, og, w_out, gain, wg, wu, wd, *, tm, tf, n_sub):
    n, d = x.shape
    f = wg.shape[1]
    rows = pl.BlockSpec((tm, d), lambda i, j: (i, 0))
    return pl.pallas_call(
        functools.partial(_ffn_kernel, n_sub=n_sub),
        grid=(n // tm, f // tf),
        in_specs=[rows, rows,
                  pl.BlockSpec((d, d), lambda i, j: (0, 0)),
                  pl.BlockSpec((1, d), lambda i, j: (0, 0)),
                  pl.BlockSpec((d, tf), lambda i, j: (0, j)),
                  pl.BlockSpec((d, tf), lambda i, j: (0, j)),
                  pl.BlockSpec((tf, d), lambda i, j: (j, 0))],
        out_specs=rows,
        out_shape=jax.ShapeDtypeStruct((n, d), F32),
        scratch_shapes=[pltpu.VMEM((tm, d), BF16)],
        compiler_params=_params("parallel", "arbitrary"),
        name="dense_ffn",
    )(x, og, w_out, gain.reshape(1, d), wg, wu, wd)


def _run_copies(n_rows, max_rows, make_copy):
    pieces = []
    off = jnp.int32(0)
    bit = max_rows
    while bit >= RUN_ALIGN:
        has = (n_rows & bit) != 0
        pieces.append((has, make_copy(off, bit)))
        off = off + (n_rows & bit)
        bit //= 2
    return pieces


def _router_kernel(x_ref, og_ref, wo_ref, g_ref, wr_ref, xo_ref, dest_ref, gate_ref, cnt_ref, carry_ref, xs_hbm,
                   cnt_scr, stage, zeros_buf, n_smem, base_smem, sem, *, tm, cap):
    i = pl.program_id(0)
    last = pl.num_programs(0) - 1

    @pl.when(i == 0)
    def _():
        cnt_scr[...] = jnp.zeros_like(cnt_scr)
        zeros_buf[...] = jnp.zeros_like(zeros_buf)
        for e in range(N_EXPERTS):
            n_smem[e] = 0
            base_smem[e] = e * cap

    x = x_ref[...] + _dot(og_ref[...].astype(BF16), wo_ref[...])
    xo_ref[...] = x
    h = _rms(x, g_ref[...])
    h16 = h.astype(BF16)
    h_lo = (h - h16.astype(F32)).astype(BF16)
    w_hi, w_lo = _split_bf16(wr_ref[...])
    logits = _dot(h16, w_hi) + (_dot(h16, w_lo) + _dot(h_lo, w_hi))
    lane = lax.broadcasted_iota(jnp.int32, logits.shape, 1)
    m1 = jnp.max(logits, axis=-1, keepdims=True)
    i1 = jnp.min(jnp.where(logits == m1, lane, N_EXPERTS), axis=-1, keepdims=True)
    rest = jnp.where(lane == i1, -jnp.inf, logits)
    m2 = jnp.max(rest, axis=-1, keepdims=True)
    i2 = jnp.min(jnp.where(rest == m2, lane, N_EXPERTS), axis=-1, keepdims=True)
    e2 = jnp.exp(m2 - m1)
    den = 1.0 + e2
    two = lax.broadcasted_iota(jnp.int32, (tm, TOP_K), 1)
    gate_ref[...] = jnp.where(two == 0, 1.0 / den, e2 / den)

    onehot = ((lane == i1) | (lane == i2)).astype(F32)
    row = lax.broadcasted_iota(jnp.int32, (tm, tm), 0)
    col = lax.broadcasted_iota(jnp.int32, (tm, tm), 1)
    local = _dot((row > col).astype(BF16), onehot.astype(BF16))
    r1 = jnp.sum(jnp.where(lane == i1, local, 0.0), axis=-1, keepdims=True).astype(jnp.int32)
    r2 = jnp.sum(jnp.where(lane == i2, local, 0.0), axis=-1, keepdims=True).astype(jnp.int32)
    dest_ref[...] = jnp.where(two == 0, i1 * tm + r1, i2 * tm + r2)
    carry_ref[0] = cnt_scr[...].astype(jnp.int32)
    cnt_tile = jnp.sum(onehot, axis=0, keepdims=True)
    cnt = cnt_scr[...] + cnt_tile
    cnt_scr[...] = cnt
    cnt_ref[...] = cnt.astype(jnp.int32)

    def wait_runs():
        for e in range(N_EXPERTS):
            for has, copy in _run_copies(n_smem[e], tm, lambda off, size, e=e: pltpu.make_async_copy(
                    stage.at[e, pl.ds(0, size * SUBLANES)], xs_hbm.at[pl.ds(0, size * SUBLANES)], sem)):
                @pl.when(has)
                def _():
                    copy.wait()

    wait_runs()

    code = (onehot * (local + 1.0)).astype(BF16)
    code_t = _dot_tn(code, (row == col).astype(BF16))
    slot_id = lax.broadcasted_iota(jnp.int32, (ROUTER_SLOTS, tm), 0)
    code_i = code_t.astype(jnp.int32)

    def compact(e, q):
        sel = jnp.where(code_i[e:e + 1, :] == slot_id + (q * ROUTER_SLOTS + 1), 1.0, 0.0).astype(BF16)
        _store_token_tiles(stage.at[e], q * ROUTER_SLOTS, _dot(sel, h16))

    for e in range(N_EXPERTS):
        compact(e, 0)
    for e in range(N_EXPERTS):
        n_e = cnt_tile[0, e].astype(jnp.int32)
        for q in range(1, tm // ROUTER_SLOTS):
            @pl.when(q * ROUTER_SLOTS < n_e)
            def _():
                compact(e, q)
        n_copy = jnp.bitwise_and(n_e + (RUN_ALIGN - 1), -RUN_ALIGN)
        base = base_smem[e]
        for has, copy in _run_copies(n_copy, tm, lambda off, size, e=e, base=base: pltpu.make_async_copy(
                stage.at[e, pl.ds(pl.multiple_of(off * SUBLANES, SUBLANES), size * SUBLANES)],
                xs_hbm.at[pl.ds(pl.multiple_of((base + off) * SUBLANES, SUBLANES), size * SUBLANES)], sem)):
            @pl.when(has)
            def _():
                copy.start()
        n_smem[e] = n_copy
        base_smem[e] = base + n_e

    @pl.when(i == last)
    def _():
        wait_runs()
        pads = [pltpu.make_async_copy(
            zeros_buf, xs_hbm.at[pl.ds(pl.multiple_of(base_smem[e] * SUBLANES, SUBLANES), PAD_ROWS * SUBLANES)], sem)
            for e in range(N_EXPERTS)]
        for pad in pads:
            pad.start()
        for pad in pads:
            pad.wait()


def _router(x, og, w_out, gain, w_router, *, tm, cap):
    n, d = x.shape
    two = pl.BlockSpec((tm, TOP_K), lambda i: (i, 0))
    rows = pl.BlockSpec((tm, d), lambda i: (i, 0))
    return pl.pallas_call(
        functools.partial(_router_kernel, tm=tm, cap=cap),
        grid=(n // tm,),
        in_specs=[rows, rows,
                  pl.BlockSpec((d, d), lambda i: (0, 0)),
                  pl.BlockSpec((1, d), lambda i: (0, 0)),
                  pl.BlockSpec((d, N_EXPERTS), lambda i: (0, 0))],
        out_specs=[rows, two, two, pl.BlockSpec((1, N_EXPERTS), lambda i: (0, 0)),
                   pl.BlockSpec((1, 1, N_EXPERTS), lambda i: (i, 0, 0)),
                   pl.BlockSpec(memory_space=pl.ANY)],
        out_shape=[jax.ShapeDtypeStruct((n, d), F32),
                   jax.ShapeDtypeStruct((n, TOP_K), jnp.int32),
                   jax.ShapeDtypeStruct((n, TOP_K), F32),
                   jax.ShapeDtypeStruct((1, N_EXPERTS), jnp.int32),
                   jax.ShapeDtypeStruct((n // tm, 1, N_EXPERTS), jnp.int32),
                   jax.ShapeDtypeStruct((N_EXPERTS * cap * SUBLANES, LANES), F32)],
        scratch_shapes=[pltpu.VMEM((1, N_EXPERTS), F32),
                        pltpu.VMEM((N_EXPERTS, tm * SUBLANES, LANES), F32),
                        pltpu.VMEM((PAD_ROWS * SUBLANES, LANES), F32),
                        pltpu.SMEM((N_EXPERTS,), jnp.int32), pltpu.SMEM((N_EXPERTS,), jnp.int32),
                        pltpu.SemaphoreType.DMA(())],
        compiler_params=_params("arbitrary"),
        name="moe_router",
    )(x, og, w_out, gain.reshape(1, d), w_router)


def _tile_gather_copy(src_hbm, dst, sem, src_row, dst_row):
    return pltpu.make_async_copy(src_hbm.at[pl.ds(pl.multiple_of(src_row * SUBLANES, SUBLANES), SUBLANES)],
                                 dst.at[pl.ds(pl.multiple_of(dst_row * SUBLANES, SUBLANES), SUBLANES)], sem)


def _load_token_tiles(ref, first_row, n_rows):
    return jnp.concatenate([ref[pl.ds(first_row * SUBLANES + s, n_rows, stride=SUBLANES), :]
                            for s in range(D_MODEL // LANES)], axis=1)


def _store_token_tiles(ref, first_row, x):
    for s in range(D_MODEL // LANES):
        ref[pl.ds(first_row * SUBLANES + s, x.shape[0], stride=SUBLANES), :] = x[:, s * LANES:(s + 1) * LANES]


def _expert_kernel(be_ref, xblk_ref, oblk_ref, nv_ref, nu_ref, xs_ref, wg_ref, wu_ref, wd_ref, yb_ref, xb, acc,
                   *, sub_rows, n_sub):
    i = pl.program_id(0)
    j = pl.program_id(1)
    n_valid = nv_ref[i]

    @pl.when(j == 0)
    def _():
        acc[...] = jnp.zeros_like(acc)
        for sb in range(n_sub):
            @pl.when(sb * sub_rows < n_valid)
            def _():
                xb[sb * sub_rows:(sb + 1) * sub_rows, :] = _load_token_tiles(
                    xs_ref, sb * sub_rows, sub_rows).astype(BF16)

    def emit(sb, y):
        acc[sb * sub_rows:(sb + 1) * sub_rows, :] += y

    def x_of(sb):
        return xb[sb * sub_rows:(sb + 1) * sub_rows, :]

    full = n_valid > (n_sub - 1) * sub_rows

    @pl.when(full)
    def _():
        _swiglu_rows(x_of, wg_ref[0], wu_ref[0], wd_ref[0], n_sub, emit)

    for sb in range(n_sub - 1):
        @pl.when(jnp.logical_and(jnp.logical_not(full), sb * sub_rows < n_valid))
        def _():
            _swiglu_rows(lambda _, sb=sb: x_of(sb), wg_ref[0], wu_ref[0], wd_ref[0], 1, lambda _, y, sb=sb: emit(sb, y))

    @pl.when(j == pl.num_programs(1) - 1)
    def _():
        for sb in range(n_sub):
            rows = slice(sb * sub_rows, (sb + 1) * sub_rows)

            @pl.when(sb * sub_rows < n_valid)
            def _():
                _store_token_tiles(yb_ref, sb * sub_rows, acc[rows, :])

            @pl.when(sb * sub_rows >= n_valid)
            def _():
                tiles = slice(sb * sub_rows * SUBLANES, (sb + 1) * sub_rows * SUBLANES)
                yb_ref[tiles, :] = jnp.zeros((sub_rows * SUBLANES, LANES), F32)


def _experts(xs, block_e, x_block, out_block, n_valid, n_used, wg, wu, wd, *, sub_rows, n_sub, tf, out_blocks):
    d = D_MODEL
    tmm = sub_rows * n_sub
    nb = block_e.shape[0]
    nf = D_FF // tf

    def wcol(i, j, be, xblk, oblk, nv, nu):
        return (be[i], 0, jnp.where(i < nu[0], j, nf - 1))

    def wrow(i, j, be, xblk, oblk, nv, nu):
        return (be[i], jnp.where(i < nu[0], j, nf - 1), 0)

    grid_spec = pltpu.PrefetchScalarGridSpec(
        num_scalar_prefetch=5,
        grid=(nb, nf),
        in_specs=[pl.BlockSpec((tmm * SUBLANES, LANES), lambda i, j, be, xblk, oblk, nv, nu: (xblk[i], 0)),
                  pl.BlockSpec((1, d, tf), wcol),
                  pl.BlockSpec((1, d, tf), wcol),
                  pl.BlockSpec((1, tf, d), wrow)],
        out_specs=pl.BlockSpec((tmm * SUBLANES, LANES), lambda i, j, be, xblk, oblk, nv, nu: (oblk[i], 0)),
        scratch_shapes=[pltpu.VMEM((tmm, d), BF16), pltpu.VMEM((tmm, d), F32)],
    )
    return pl.pallas_call(
        functools.partial(_expert_kernel, sub_rows=sub_rows, n_sub=n_sub),
        grid_spec=grid_spec,
        out_shape=jax.ShapeDtypeStruct((out_blocks * tmm * SUBLANES, LANES), F32),
        compiler_params=_params("arbitrary", "arbitrary"),
        name="moe_experts",
    )(block_e, x_block, out_block, n_valid, n_used, xs, wg, wu, wd)


def _combine_kernel(slot_ref, base_ref, len_ref, yb_hbm, x_ref, gate_ref, gain_ref, o_ref, stage, r0, r1, sems,
                    *, tm):
    i = pl.program_id(0)
    cur = lax.rem(i, 2)

    def run_pieces(tile, buf):
        pieces = []
        for e in range(N_EXPERTS):
            n_copy = jnp.bitwise_and(len_ref[tile * N_EXPERTS + e] + (RUN_ALIGN - 1), -RUN_ALIGN)
            base = base_ref[tile * N_EXPERTS + e]
            pieces += _run_copies(n_copy, tm, lambda off, size, e=e, base=base: pltpu.make_async_copy(
                yb_hbm.at[pl.ds(pl.multiple_of((base + off) * SUBLANES, SUBLANES), size * SUBLANES)],
                stage.at[buf, pl.ds(pl.multiple_of((e * tm + off) * SUBLANES, SUBLANES), size * SUBLANES)],
                sems.at[buf]))
        return pieces

    def start_runs(tile, buf):
        for has, copy in run_pieces(tile, buf):
            @pl.when(has)
            def _():
                copy.start()

    @pl.when(i == 0)
    def _():
        start_runs(0, 0)

    @pl.when(i + 1 < pl.num_programs(0))
    def _():
        start_runs(i + 1, 1 - cur)

    for has, copy in run_pieces(i, cur):
        @pl.when(has)
        def _():
            copy.wait()

    def place(t, carry):
        first = TOP_K * (i * tm + t)
        row = pl.ds(pl.multiple_of(t * SUBLANES, SUBLANES), SUBLANES)
        r0[row, :] = stage[cur, pl.ds(pl.multiple_of(slot_ref[first] * SUBLANES, SUBLANES), SUBLANES), :]
        r1[row, :] = stage[cur, pl.ds(pl.multiple_of(slot_ref[first + 1] * SUBLANES, SUBLANES), SUBLANES), :]
        return carry

    lax.fori_loop(0, tm, place, 0, unroll=8)
    g = gate_ref[...]
    y = _load_token_tiles(r0, 0, tm) * g[:, 0:1] + _load_token_tiles(r1, 0, tm) * g[:, 1:2]
    o_ref[...] = _rms(x_ref[...] + y, gain_ref[...])


def _combine(yb, slot_flat, run_base, run_len, x, gates, gain, *, tm):
    n, d = x.shape
    grid_spec = pltpu.PrefetchScalarGridSpec(
        num_scalar_prefetch=3,
        grid=(n // tm,),
        in_specs=[pl.BlockSpec(memory_space=pl.ANY),
                  pl.BlockSpec((tm, d), lambda i, *_: (i, 0)),
                  pl.BlockSpec((tm, TOP_K), lambda i, *_: (i, 0)),
                  pl.BlockSpec((1, d), lambda i, *_: (0, 0))],
        out_specs=pl.BlockSpec((tm, d), lambda i, *_: (i, 0)),
        scratch_shapes=[pltpu.VMEM((2, N_EXPERTS * tm * SUBLANES, LANES), F32),
                        pltpu.VMEM((tm * SUBLANES, LANES), F32), pltpu.VMEM((tm * SUBLANES, LANES), F32),
                        pltpu.SemaphoreType.DMA((2,))],
    )
    return pl.pallas_call(
        functools.partial(_combine_kernel, tm=tm),
        grid_spec=grid_spec,
        out_shape=jax.ShapeDtypeStruct((n, d), F32),
        compiler_params=_params("arbitrary"),
        name="moe_combine",
    )(slot_flat, run_base, run_len, yb, x, gates, gain.reshape(1, d))


def _moe_and_final_norm(x, og, w_out, gain, w_router, wg, wu, wd, final_gain, *, tiles):
    n = x.shape[0]
    sub_rows, n_sub = tiles["moe_sub_rows"], tiles["moe_subs"]
    tmm = sub_rows * n_sub
    assert sub_rows == PAD_ROWS and n % tmm == 0
    cap = n + tmm
    assert tiles["router_rows"] == tiles["combine_rows"]
    x, slots, gates, counts, carry, xs = _router(x, og, w_out, gain, w_router, tm=tiles["router_rows"], cap=cap)
    counts = counts[0]
    blocks_per_e = (counts + tmm - 1) // tmm
    blocks_end = jnp.cumsum(blocks_per_e)
    n_used = blocks_end[-1].astype(jnp.int32)
    nb = -(-(n * TOP_K + N_EXPERTS * (tmm - 1)) // tmm)
    blk = jnp.minimum(jnp.arange(nb, dtype=jnp.int32), n_used - 1)
    block_e = jnp.minimum(jnp.sum(blk[:, None] >= blocks_end[None, :], axis=1), N_EXPERTS - 1).astype(jnp.int32)
    k_in_e = blk - (blocks_end - blocks_per_e)[block_e]
    x_block = (block_e * (cap // tmm) + k_in_e).astype(jnp.int32)
    used = jnp.arange(nb, dtype=jnp.int32) < n_used
    n_valid = jnp.where(used, jnp.clip(counts[block_e] - k_in_e * tmm, 0, tmm), 0).astype(jnp.int32)
    dump_block = N_EXPERTS * (cap // tmm)
    out_block = jnp.where(used, x_block, dump_block).astype(jnp.int32)
    yb = _experts(xs, block_e, x_block, out_block, n_valid, n_used.reshape(1), wg, wu, wd,
                  sub_rows=sub_rows, n_sub=n_sub, tf=tiles["moe_ff_cols"], out_blocks=dump_block + 1)
    carry = carry[:, 0, :]
    run_base = (jnp.arange(N_EXPERTS, dtype=jnp.int32) * cap)[None, :] + carry
    run_len = jnp.concatenate([carry[1:], counts[None, :]], axis=0) - carry
    return _combine(yb, slots.reshape(-1), run_base.reshape(-1), run_len.reshape(-1), x, gates, final_gain,
                    tm=tiles["combine_rows"])


def _tiles(n_rows, seq):
    big = n_rows >= 8192
    return {
        "proj_rows": 1024 if n_rows % 1024 == 0 else n_rows,
        "proj_cols": 2048,
        "rec_chunks": min(4, seq // min(MAX_CHUNK, seq)),
        "rec_seqs": 1 if seq > MAX_CHUNK else 4,
        "proj_in_recurrence": seq > MAX_CHUNK,
        "ffn_rows": 1024 if n_rows % 1024 == 0 else n_rows,
        "ff_cols": 512,
        "router_rows": 256,
        "moe_sub_rows": 256,
        "moe_subs": 4 if big else 2,
        "moe_ff_cols": 1792,
        "ffn_subs": 4 if big else 2,
        "combine_rows": 256,
    }


def _trunk(x3, hg_state, gd_state, gd_conv, p):
    batch, seq, d = x3.shape
    n = batch * seq
    x = x3.reshape(n, d)
    tiles = _tiles(n, seq)
    tm, tn = tiles["proj_rows"], tiles["proj_cols"]

    rec = dict(batch=batch, seq=seq, n_chunks=tiles["rec_chunks"], n_seq=tiles["rec_seqs"])

    if tiles["proj_in_recurrence"]:
        proj, fuse = None, (x, p["norm_mix"][0], p["hgrn_w_in"])
    else:
        proj, fuse = _norm_proj(x, p["norm_mix"][0], p["hgrn_w_in"], tm=tm, tn=tn)[0], None
    og, hg_new = _hgrn_recurrence(proj, p["hgrn_lb"], p["hgrn_norm"], hg_state, layer=0, fuse=fuse, **rec)
    x = _ffn(x, og, p["hgrn_w_out"], p["norm_ffn"][0], p["ffn_w_gate"], p["ffn_w_up"], p["ffn_w_down"],
             tm=tiles["ffn_rows"], tf=tiles["ff_cols"], n_sub=tiles["ffn_subs"])

    if tiles["proj_in_recurrence"]:
        main, ab, fuse = None, None, (x, p["norm_mix"][1], p["gdn_w_main"], p["gdn_w_ab"])
    else:
        (main, ab), fuse = _norm_proj(x, p["norm_mix"][1], p["gdn_w_main"], p["gdn_w_ab"], tm=tm, tn=tn), None
    og, gd_new, cv_new = _gdn_recurrence(main, ab, p["gdn_conv"], p["gdn_a_log"], p["gdn_dt_bias"], p["gdn_norm"],
                                         gd_conv, gd_state, fuse=fuse, **rec)
    y = _moe_and_final_norm(x, og, p["gdn_w_out"], p["norm_ffn"][1], p["moe_router"], p["moe_w_gate"], p["moe_w_up"], p["moe_w_down"],
                            p["norm_out"], tiles=tiles)
    return y.reshape(batch, seq, d), hg_new[None], gd_new[None], cv_new[None]


def kernel(x_prompt, x_sample, state_hgrn, state_gdn, state_gdn_conv, norm_mix, norm_ffn, norm_out, hgrn_w_in, hgrn_lb, hgrn_norm, hgrn_w_out, gdn_w_in, gdn_conv, gdn_a_log, gdn_dt_bias, gdn_norm, gdn_w_out, ffn_w_gate, ffn_w_up, ffn_w_down, moe_router, moe_w_gate, moe_w_up, moe_w_down):
    assert hgrn_w_in.shape[0] == 1 and gdn_w_in.shape[0] == 1, "one HGRN2 layer and one gated-DeltaNet layer"
    main_cols = GD_QKV + N_HEADS * HEAD_DIM
    p = {
        "norm_mix": norm_mix, "norm_ffn": norm_ffn, "norm_out": norm_out,
        "hgrn_w_in": hgrn_w_in[0].astype(BF16), "hgrn_lb": hgrn_lb, "hgrn_norm": hgrn_norm[0],
        "hgrn_w_out": hgrn_w_out[0].astype(BF16),
        "gdn_w_main": gdn_w_in[0, :, :main_cols].astype(BF16), "gdn_w_ab": gdn_w_in[0, :, main_cols:].astype(BF16),
        "gdn_conv": gdn_conv[0], "gdn_a_log": gdn_a_log[0], "gdn_dt_bias": gdn_dt_bias[0], "gdn_norm": gdn_norm[0],
        "gdn_w_out": gdn_w_out[0].astype(BF16),
        "ffn_w_gate": ffn_w_gate[0].astype(BF16), "ffn_w_up": ffn_w_up[0].astype(BF16),
        "ffn_w_down": ffn_w_down[0].astype(BF16),
        "moe_router": moe_router[0],
        "moe_w_gate": moe_w_gate[0].astype(BF16), "moe_w_up": moe_w_up[0].astype(BF16),
        "moe_w_down": moe_w_down[0].astype(BF16),
    }
    y_p, hg_p, gd_p, cv_p = _trunk(x_prompt, None, None, None, p)
    y_s, hg_s, gd_s, cv_s = _trunk(x_sample, state_hgrn[0], state_gdn[0], state_gdn_conv[0], p)
    return (y_p, y_s, hg_p, hg_s, gd_p, gd_s, cv_p, cv_s)
```
